```python
import math
import jax, jax.numpy as jnp
from jax import lax
import numpy as np

D_MODEL = 1024
BATCH = 32
SEQ = 2048
DEPTH = 1

POOL_WINDOWS = (2, 4, 8, 16)
N_POOL_GROUPS = len(POOL_WINDOWS)
POOL_WIDTH = D_MODEL // 2
POOL_GROUP = POOL_WIDTH // N_POOL_GROUPS
N_HEADS = 4
HEAD_DIM = D_MODEL // 16
V_HEAD_DIM = 2 * HEAD_DIM
QK_WIDTH = N_HEADS * 2 * HEAD_DIM
ATTN_WIDTH = N_HEADS * V_HEAD_DIM
Q_BLOCK = 128
OFF_POOL = 0
OFF_Q = OFF_POOL + POOL_WIDTH
OFF_K = OFF_Q + QK_WIDTH
OFF_V = OFF_K + QK_WIDTH
OFF_GP = OFF_V + ATTN_WIDTH
OFF_GA = OFF_GP + D_MODEL
IN_WIDTH = OFF_GA + D_MODEL
N_GROUPS = 4
EXPERTS_PER_GROUP = 8
N_EXPERTS = N_GROUPS * EXPERTS_PER_GROUP
TOP_K = 2
D_EXPERT = D_MODEL // 2
ROW_BLOCK = 512
EPS = 1e-6

kernel_name = "hybrid_pool_diffattn_hmoe_block"


def rmsnorm(x, g):
    xf = x.astype(jnp.float32)
    y = xf * lax.rsqrt(jnp.mean(xf * xf, axis=-1, keepdims=True) + EPS)
    return (y * g.astype(jnp.float32)).astype(x.dtype)


def lambda_init(layer):
    return 0.8 - 0.6 * math.exp(-0.3 * layer)


def alibi_slopes():
    return jnp.asarray([2.0 ** (-8.0 * (h + 1) / N_HEADS) for h in range(N_HEADS)], jnp.float32)


def pool_mixer(u, w_grp, scale):
    B, S, C = u.shape
    uf = u.astype(jnp.float32)
    c0 = jnp.concatenate([jnp.zeros((B, 1, C), jnp.float32), jnp.cumsum(uf, axis=1)], axis=1)
    t = jnp.arange(S)
    outs = []
    for g, w in enumerate(POOL_WINDOWS):
        sl = slice(g * POOL_GROUP, (g + 1) * POOL_GROUP)
        cg = c0[:, :, sl]
        upper = cg[:, 1:]
        lower = jnp.pad(cg[:, :S + 1 - w], ((0, 0), (w - 1, 0), (0, 0)))
        cnt = jnp.minimum(t + 1, w).astype(jnp.float32)[None, :, None]
        outs.append((upper - lower) / cnt - uf[:, :, sl])
    d = jnp.stack(outs, axis=2).astype(u.dtype)
    y = jnp.einsum('bsgc,gcd->bsgd', d, w_grp).reshape(B, S, POOL_WIDTH)
    return y * scale


def diff_attention(q, k, v, q_g, k_g, lam, subln_g, lam_init):
    B, S = q.shape[0], q.shape[1]
    qf = rmsnorm(q.astype(jnp.float32), q_g) * (HEAD_DIM ** -0.5)
    kf = rmsnorm(k.astype(jnp.float32), k_g)
    slopes = alibi_slopes()
    nb = S // Q_BLOCK
    kpos = jnp.arange(S)
    qb = qf.reshape(B, nb, Q_BLOCK, N_HEADS, 2, HEAD_DIM).transpose(1, 0, 2, 3, 4, 5)

    def block(args):
        qblk, i = args
        qpos = i * Q_BLOCK + jnp.arange(Q_BLOCK)
        dist = (qpos[:, None] - kpos[None, :]).astype(jnp.float32)
        bias = jnp.where(dist[None] >= 0, -slopes[:, None, None] * dist[None], -jnp.inf)
        s = jnp.einsum('bqhmd,bkhmd->bhmqk', qblk, kf) + bias[None, :, None]
        p = jax.nn.softmax(s, axis=-1)
        a = p[:, :, 0] - lam * p[:, :, 1]
        return jnp.einsum('bhqk,bkhe->bqhe', a.astype(v.dtype), v)

    o = lax.map(block, (qb, jnp.arange(nb)))
    o = o.transpose(1, 0, 2, 3, 4).reshape(B, S, N_HEADS, V_HEAD_DIM)
    o = rmsnorm(o, subln_g) * (1.0 - lam_init)
    return o.reshape(B, S, ATTN_WIDTH)


def hier_moe(hn, w_rg, b_rg, w_re, b_re, w_gate, w_up, w_down):
    N, D = hn.shape
    pg = jax.nn.softmax((hn @ w_rg).astype(jnp.float32) + b_rg.astype(jnp.float32), axis=-1)
    p_top, g_idx = lax.top_k(pg, 1)
    le = ((hn @ w_re).astype(jnp.float32) + b_re.astype(jnp.float32)).reshape(N, N_GROUPS, EXPERTS_PER_GROUP)
    le_sel = jnp.take_along_axis(le, g_idx[:, :, None], axis=1)[:, 0]
    pe = jax.nn.softmax(le_sel, axis=-1)
    pv, e_idx = lax.top_k(pe, TOP_K)
    wts = p_top * pv / jnp.sum(pv, axis=-1, keepdims=True)
    eid = (g_idx * EXPERTS_PER_GROUP + e_idx).reshape(-1)
    tok = jnp.repeat(jnp.arange(N, dtype=jnp.int32), TOP_K)
    wflat = wts.reshape(-1)
    M = N * TOP_K
    order = jnp.argsort(eid)
    s_eid, s_tok, s_w = eid[order], tok[order], wflat[order]
    counts = jnp.bincount(eid, length=N_EXPERTS)
    padded = (counts + ROW_BLOCK - 1) // ROW_BLOCK * ROW_BLOCK
    pend = jnp.cumsum(padded)
    pstart = pend - padded
    start = jnp.cumsum(counts) - counts
    dest = pstart[s_eid] + jnp.arange(M) - start[s_eid]
    n_blocks = -(-M // ROW_BLOCK) + N_EXPERTS
    P = n_blocks * ROW_BLOCK
    slot_tok = jnp.zeros((P,), jnp.int32).at[dest].set(s_tok)
    slot_w = jnp.zeros((P,), jnp.float32).at[dest].set(s_w)
    block_eid = jnp.minimum(jnp.searchsorted(pend, jnp.arange(n_blocks) * ROW_BLOCK, side='right'), N_EXPERTS - 1)

    def expert_block(args):
        toks, wb, e = args
        xb = hn[toks]
        hdn = jax.nn.silu(xb @ w_gate[e]) * (xb @ w_up[e])
        return (hdn @ w_down[e]) * wb[:, None].astype(hn.dtype)

    yb = lax.map(expert_block, (slot_tok.reshape(n_blocks, ROW_BLOCK), slot_w.reshape(n_blocks, ROW_BLOCK), block_eid))
    return jnp.zeros((N, D), hn.dtype).at[slot_tok].add(yb.reshape(P, D))


def setup_inputs(seed: int = 0) -> dict:
    key = jax.random.key(seed)
    ks = jax.random.split(key, 24)
    f32 = jnp.float32
    nrm = lambda k, shape, fan: jax.random.normal(k, shape, f32) * (fan ** -0.5)
    gain = lambda k, shape: 1.0 + 0.02 * jax.random.normal(k, shape, f32)
    L = DEPTH
    return {
        "x": jax.random.normal(ks[0], (BATCH, SEQ, D_MODEL), f32),
        "norm1_g": gain(ks[1], (L, D_MODEL)),
        "w_in": nrm(ks[2], (L, D_MODEL, IN_WIDTH), D_MODEL),
        "pool_w": nrm(ks[3], (L, N_POOL_GROUPS, POOL_GROUP, POOL_GROUP), POOL_GROUP),
        "pool_scale": gain(ks[4], (L, POOL_WIDTH)),
        "w_pool_up": nrm(ks[5], (L, POOL_WIDTH, D_MODEL), POOL_WIDTH),
        "q_norm_g": gain(ks[6], (L, HEAD_DIM)),
        "k_norm_g": gain(ks[7], (L, HEAD_DIM)),
        "lambda_q1": 0.1 * jax.random.normal(ks[8], (L, HEAD_DIM), f32),
        "lambda_k1": 0.1 * jax.random.normal(ks[9], (L, HEAD_DIM), f32),
        "lambda_q2": 0.1 * jax.random.normal(ks[10], (L, HEAD_DIM), f32),
        "lambda_k2": 0.1 * jax.random.normal(ks[11], (L, HEAD_DIM), f32),
        "subln_g": gain(ks[12], (L, V_HEAD_DIM)),
        "w_attn_up": nrm(ks[13], (L, ATTN_WIDTH, D_MODEL), ATTN_WIDTH),
        "w_out": nrm(ks[14], (L, D_MODEL, D_MODEL), D_MODEL),
        "norm2_g": gain(ks[15], (L, D_MODEL)),
        "w_router_group": nrm(ks[16], (L, D_MODEL, N_GROUPS), D_MODEL),
        "b_router_group": 0.01 * jax.random.normal(ks[17], (L, N_GROUPS), f32),
        "w_router_expert": nrm(ks[18], (L, D_MODEL, N_EXPERTS), D_MODEL),
        "b_router_expert": 0.01 * jax.random.normal(ks[19], (L, N_EXPERTS), f32),
        "w_expert_gate": nrm(ks[20], (L, N_EXPERTS, D_MODEL, D_EXPERT), D_MODEL),
        "w_expert_up": nrm(ks[21], (L, N_EXPERTS, D_MODEL, D_EXPERT), D_MODEL),
        "w_expert_down": nrm(ks[22], (L, N_EXPERTS, D_EXPERT, D_MODEL), D_EXPERT),
    }


def reference(x, norm1_g, w_in, pool_w, pool_scale, w_pool_up, q_norm_g, k_norm_g, lambda_q1, lambda_k1,
              lambda_q2, lambda_k2, subln_g, w_attn_up, w_out, norm2_g, w_router_group, b_router_group,
              w_router_expert, b_router_expert, w_expert_gate, w_expert_up, w_expert_down):
    B, S, D = x.shape
    for l in range(DEPTH):
        lam_init = lambda_init(l)
        h = rmsnorm(x, norm1_g[l])
        proj = h @ w_in[l]
        u = proj[..., OFF_POOL:OFF_Q]
        q = proj[..., OFF_Q:OFF_K].reshape(B, S, N_HEADS, 2, HEAD_DIM)
        k = proj[..., OFF_K:OFF_V].reshape(B, S, N_HEADS, 2, HEAD_DIM)
        v = proj[..., OFF_V:OFF_GP].reshape(B, S, N_HEADS, V_HEAD_DIM)
        gate_p = jax.nn.sigmoid(proj[..., OFF_GP:OFF_GA])
        gate_a = jax.nn.sigmoid(proj[..., OFF_GA:IN_WIDTH])
        pool_out = pool_mixer(u, pool_w[l], pool_scale[l]) @ w_pool_up[l]
        lam = (jnp.exp(jnp.sum(lambda_q1[l].astype(jnp.float32) * lambda_k1[l].astype(jnp.float32)))
               - jnp.exp(jnp.sum(lambda_q2[l].astype(jnp.float32) * lambda_k2[l].astype(jnp.float32))) + lam_init)
        attn_out = diff_attention(q, k, v, q_norm_g[l], k_norm_g[l], lam, subln_g[l], lam_init) @ w_attn_up[l]
        x = x + (gate_p * pool_out + gate_a * attn_out) @ w_out[l]
        hn = rmsnorm(x, norm2_g[l]).reshape(B * S, D)
        x = x + hier_moe(hn, w_router_group[l], b_router_group[l], w_router_expert[l], b_router_expert[l],
                         w_expert_gate[l], w_expert_up[l], w_expert_down[l]).reshape(B, S, D)
    return x
```

```python
import functools
import math

import jax
import jax.numpy as jnp
from jax import lax
from jax.experimental import pallas as pl
from jax.experimental.pallas import tpu as pltpu

EPS = 1e-6
POOL_WINDOWS = (2, 4, 8, 16)
POOL_HALO = 16
N_HEADS = 4
N_GROUPS = 4
EXPERTS_PER_GROUP = 8
N_EXPERTS = N_GROUPS * EXPERTS_PER_GROUP
TOP_K = 2
LANES = 128
SUBLANES = 8
ROUTER_ROWS = 8 + N_EXPERTS

ROW_TILE = 512
ATTN_TILE = 256
EXPERT_ROWS = 512
COMBINE_TILE = 256
VMEM_LIMIT = 52 * 1024 * 1024

f32 = jnp.float32
bf16 = jnp.bfloat16


def _dot(a, b):
    return jnp.dot(a, b, preferred_element_type=f32)


def _dot_nt(a, b):
    return lax.dot_general(a, b, (((1,), (1,)), ((), ())), preferred_element_type=f32)


def _half_lane_rmsnorm(t, n_tiles, rows):
    lane = lax.broadcasted_iota(jnp.int32, (rows, LANES), 1)
    lo_mask = lane < (LANES // 2)
    outs = []
    for i in range(n_tiles):
        c = t[:, i * LANES:(i + 1) * LANES]
        sq = c * c
        lo = jnp.sum(jnp.where(lo_mask, sq, 0.0), axis=-1, keepdims=True)
        hi = jnp.sum(jnp.where(lo_mask, 0.0, sq), axis=-1, keepdims=True)
        ms = jnp.where(lo_mask, lo, hi) * (2.0 / LANES)
        outs.append(c * lax.rsqrt(ms + EPS))
    return jnp.concatenate(outs, axis=-1)


def _in_proj_kernel(x_ref, g1_ref, win_ref, poolw_ref, pscale_ref, wpu_ref, qg_ref, kg_ref,
                    pg_ref, ga_ref, qn_ref, kn_ref, v_ref, prev_ref, *, tm, d_model, pool_width, qk_width,
                    attn_width):
    j = pl.program_id(1)
    x = x_ref[...]
    ms = jnp.mean(x * x, axis=-1, keepdims=True)
    h = (x * lax.rsqrt(ms + EPS) * g1_ref[...]).astype(bf16)

    def proj(lo, width):
        return _dot(h, win_ref[:, lo:lo + width])

    off_q = pool_width
    off_k = off_q + qk_width
    off_v = off_k + qk_width
    off_gp = off_v + attn_width
    off_ga = off_gp + d_model

    u = proj(0, pool_width)

    @pl.when(j == 0)
    def _():
        prev_ref[...] = jnp.zeros_like(prev_ref)

    ext = jnp.concatenate([prev_ref[...], u], axis=0)
    prev_ref[...] = u[tm - POOL_HALO:, :]
    pos = j * tm + lax.broadcasted_iota(jnp.int32, (tm, 1), 0)
    group = pool_width // len(POOL_WINDOWS)
    ys = []
    for g, w in enumerate(POOL_WINDOWS):
        acc = ext[:, g * group:(g + 1) * group]
        shift = 1
        while shift < w:
            acc = acc + pltpu.roll(acc, shift, 0)
            shift *= 2
        wsum = acc[POOL_HALO:, :]
        cnt = jnp.minimum(pos + 1, w).astype(f32)
        d = wsum / cnt - u[:, g * group:(g + 1) * group]
        ys.append(_dot(d.astype(bf16), poolw_ref[g]))
    y = jnp.concatenate(ys, axis=-1) * pscale_ref[...]
    pool_out = _dot(y.astype(bf16), wpu_ref[...])

    half = d_model // 2
    for c in range(2):
        gp = jax.nn.sigmoid(proj(off_gp + c * half, half))
        pg_ref[:, c * half:(c + 1) * half] = (gp * pool_out[:, c * half:(c + 1) * half]).astype(bf16)
        ga_ref[:, c * half:(c + 1) * half] = jax.nn.sigmoid(proj(off_ga + c * half, half)).astype(bf16)

    n_tiles = qk_width // LANES
    qn_ref[...] = (_half_lane_rmsnorm(proj(off_q, qk_width), n_tiles, tm) * qg_ref[...]).astype(bf16)
    kn_ref[...] = (_half_lane_rmsnorm(proj(off_k, qk_width), n_tiles, tm) * kg_ref[...]).astype(bf16)
    v_ref[...] = proj(off_v, attn_width).astype(bf16)


def _in_proj(x, g1, w_in, pool_w, pool_scale, w_pool_up, qg, kg):
    B, S, D = x.shape
    tm = min(ROW_TILE, S)
    pool_width = w_pool_up.shape[0]
    qk_width = qg.shape[1]
    attn_width = qk_width
    in_width = w_in.shape[1]
    const2 = lambda b, j: (0, 0)
    row = lambda b, j: (b, j, 0)
    kern = functools.partial(_in_proj_kernel, tm=tm, d_model=D, pool_width=pool_width, qk_width=qk_width,
                             attn_width=attn_width)
    return pl.pallas_call(
        kern,
        grid=(B, S // tm),
        in_specs=[
            pl.BlockSpec((None, tm, D), row),
            pl.BlockSpec((1, D), const2),
            pl.BlockSpec((D, in_width), const2),
            pl.BlockSpec(pool_w.shape, lambda b, j: (0, 0, 0)),
            pl.BlockSpec((1, pool_width), const2),
            pl.BlockSpec((pool_width, D), const2),
            pl.BlockSpec((1, qk_width), const2),
            pl.BlockSpec((1, qk_width), const2),
        ],
        out_specs=[
            pl.BlockSpec((None, tm, D), row),
            pl.BlockSpec((None, tm, D), row),
            pl.BlockSpec((None, tm, qk_width), row),
            pl.BlockSpec((None, tm, qk_width), row),
            pl.BlockSpec((None, tm, attn_width), row),
        ],
        out_shape=[
            jax.ShapeDtypeStruct((B, S, D), bf16),
            jax.ShapeDtypeStruct((B, S, D), bf16),
            jax.ShapeDtypeStruct((B, S, qk_width), bf16),
            jax.ShapeDtypeStruct((B, S, qk_width), bf16),
            jax.ShapeDtypeStruct((B, S, attn_width), bf16),
        ],
        scratch_shapes=[pltpu.VMEM((POOL_HALO, pool_width), f32)],
        compiler_params=pltpu.CompilerParams(dimension_semantics=("arbitrary", "arbitrary"),
                                             vmem_limit_bytes=VMEM_LIMIT),
        name="in_proj",
    )(x, g1, w_in, pool_w, pool_scale, w_pool_up, qg, kg)


def _attn_kernel(slopes_ref, q_ref, k_ref, v_ref, bias_ref, lq1_ref, lk1_ref, lq2_ref, lk2_ref, sg_ref,
                 o_ref, m_ref, l_ref, acc_ref, *, tq, lam_init):
    h = pl.program_id(1)
    i = pl.program_id(2)
    slope = slopes_ref[h]
    q = q_ref[...]
    lane = lax.broadcasted_iota(jnp.int32, q.shape, 1)
    first = lane < (LANES // 2)
    zero = jnp.zeros_like(q)
    qs = jnp.concatenate([jnp.where(first, q, zero), jnp.where(first, zero, q)], axis=0)
    bias = bias_ref[...]

    def scores(j):
        kb = k_ref[pl.ds(pl.multiple_of(j * tq, tq), tq), :]
        return _dot_nt(qs, kb).reshape(2, tq, tq)

    def pv(p, j):
        vb = v_ref[pl.ds(pl.multiple_of(j * tq, tq), tq), :]
        return _dot(p.reshape(2 * tq, tq).astype(bf16), vb)

    row = lax.broadcasted_iota(jnp.int32, (tq, tq), 0)
    col = lax.broadcasted_iota(jnp.int32, (tq, tq), 1)
    s = scores(i) + jnp.where(row >= col, bias, -jnp.inf)[None]
    m0 = jnp.max(s, axis=-1, keepdims=True)
    p = jnp.exp(s - m0)
    m_ref[...] = m0
    l_ref[...] = jnp.sum(p, axis=-1, keepdims=True)
    acc_ref[...] = pv(p, i)

    def body(j, carry):
        c = -slope * ((i - j) * tq).astype(f32)
        s = scores(j) + bias[None]
        m_old = m_ref[...]
        m_new = jnp.maximum(m_old, jnp.max(s, axis=-1, keepdims=True) + c)
        p = jnp.exp(s - (m_new - c))
        alpha = jnp.exp(m_old - m_new)
        l_ref[...] = alpha * l_ref[...] + jnp.sum(p, axis=-1, keepdims=True)
        acc_ref[...] = alpha.reshape(2 * tq, 1) * acc_ref[...] + pv(p, j)
        m_ref[...] = m_new
        return carry

    lax.fori_loop(0, i, body, 0)

    lam = (jnp.exp(jnp.sum(lq1_ref[...] * lk1_ref[...], keepdims=True))
           - jnp.exp(jnp.sum(lq2_ref[...] * lk2_ref[...], keepdims=True)) + lam_init)
    o_all = acc_ref[...] / l_ref[...].reshape(2 * tq, 1)
    o = o_all[:tq] - lam * o_all[tq:]
    ms = jnp.mean(o * o, axis=-1, keepdims=True)
    o_ref[...] = (o * lax.rsqrt(ms + EPS) * sg_ref[...] * (1.0 - lam_init)).astype(bf16)


def _attention(qn, kn, v, bias, slopes, lq1, lk1, lq2, lk2, subln_g, lam_init):
    B, S, _ = qn.shape
    tq = min(ATTN_TILE, S)
    const2 = lambda b, h, i, *_: (0, 0)
    kern = functools.partial(_attn_kernel, tq=tq, lam_init=lam_init)
    hd = lq1.shape[1]
    return pl.pallas_call(
        kern,
        grid_spec=pltpu.PrefetchScalarGridSpec(
            num_scalar_prefetch=1,
            grid=(B, N_HEADS, S // tq),
            in_specs=[
                pl.BlockSpec((None, tq, LANES), lambda b, h, i, *_: (b, i, h)),
                pl.BlockSpec((None, S, LANES), lambda b, h, i, *_: (b, 0, h)),
                pl.BlockSpec((None, S, LANES), lambda b, h, i, *_: (b, 0, h)),
                pl.BlockSpec((None, tq, tq), lambda b, h, i, *_: (h, 0, 0)),
                pl.BlockSpec((1, hd), const2),
                pl.BlockSpec((1, hd), const2),
                pl.BlockSpec((1, hd), const2),
                pl.BlockSpec((1, hd), const2),
                pl.BlockSpec((1, LANES), const2),
            ],
            out_specs=pl.BlockSpec((None, tq, LANES), lambda b, h, i, *_: (b, i, h)),
            scratch_shapes=[pltpu.VMEM((2, tq, 1), f32), pltpu.VMEM((2, tq, 1), f32),
                            pltpu.VMEM((2 * tq, LANES), f32)],
        ),
        out_shape=jax.ShapeDtypeStruct((B, S, N_HEADS * LANES), bf16),
        compiler_params=pltpu.CompilerParams(dimension_semantics=("arbitrary", "arbitrary", "arbitrary"),
                                             vmem_limit_bytes=VMEM_LIMIT),
        name="diff_attn",
    )(slopes, qn, kn, v, bias, lq1, lk1, lq2, lk2, subln_g)


def _merge_kernel(pg_ref, ga_ref, o_ref, x_ref, wau_ref, wout_ref, g2_ref, wr_ref, br_ref,
                  x1_ref, hn_ref, eid_ref, wt_ref, rank_ref, cnt_ref, base_ref, *, tm, d_model):
    step = pl.program_id(0)

    @pl.when(step == 0)
    def _():
        base_ref[...] = jnp.zeros_like(base_ref)

    attn_out = _dot(o_ref[...], wau_ref[...])
    merged = pg_ref[...].astype(f32) + ga_ref[...].astype(f32) * attn_out
    x1 = x_ref[...] + _dot(merged.astype(bf16), wout_ref[...])
    x1_ref[...] = x1
    ms = jnp.mean(x1 * x1, axis=-1, keepdims=True)
    hn = x1 * lax.rsqrt(ms + EPS) * g2_ref[...]
    for s in range(d_model // LANES):
        hn_ref[pl.ds(s, tm, stride=SUBLANES), :] = hn[:, s * LANES:(s + 1) * LANES]

    logits = _dot_nt(wr_ref[...], hn.astype(bf16)) + br_ref[...]
    lg = logits[0:N_GROUPS]
    gmax = jnp.max(lg, axis=0, keepdims=True)
    p_top = 1.0 / jnp.sum(jnp.exp(lg - gmax), axis=0, keepdims=True)
    grow = lax.broadcasted_iota(jnp.int32, lg.shape, 0).astype(f32)
    g_idx = jnp.min(jnp.where(lg == gmax, grow, float(N_GROUPS)), axis=0, keepdims=True)

    sel = jnp.zeros((EXPERTS_PER_GROUP, tm), f32)
    for g in range(N_GROUPS):
        le_g = logits[SUBLANES + g * EXPERTS_PER_GROUP:SUBLANES + (g + 1) * EXPERTS_PER_GROUP]
        sel = jnp.where(g_idx == float(g), le_g, sel)
    erow = lax.broadcasted_iota(jnp.int32, sel.shape, 0).astype(f32)
    e1 = jnp.max(sel, axis=0, keepdims=True)
    i1 = jnp.min(jnp.where(sel == e1, erow, float(EXPERTS_PER_GROUP)), axis=0, keepdims=True)
    sel2 = jnp.where(erow == i1, -jnp.inf, sel)
    e2 = jnp.max(sel2, axis=0, keepdims=True)
    i2 = jnp.min(jnp.where(sel2 == e2, erow, float(EXPERTS_PER_GROUP)), axis=0, keepdims=True)
    r = jnp.exp(e2 - e1)
    w1 = p_top / (1.0 + r)
    w2 = p_top * r / (1.0 + r)
    eid1 = g_idx * float(EXPERTS_PER_GROUP) + i1
    eid2 = g_idx * float(EXPERTS_PER_GROUP) + i2
    eid_ref[...] = jnp.concatenate([eid1, eid2], axis=0).astype(jnp.int32)
    wt_ref[...] = jnp.concatenate([w1, w2], axis=0)

    xrow = lax.broadcasted_iota(jnp.int32, (N_EXPERTS, tm), 0).astype(f32)
    oh1 = jnp.where(xrow == eid1, 1.0, 0.0)
    oh2 = jnp.where(xrow == eid2, 1.0, 0.0)
    oh = oh1 + oh2
    a = lax.broadcasted_iota(jnp.int32, (tm, tm), 0)
    b = lax.broadcasted_iota(jnp.int32, (tm, tm), 1)
    upper = jnp.where(a <= b, 1.0, 0.0).astype(bf16)
    before = _dot(oh.astype(bf16), upper) + base_ref[...] - 1.0
    rank1 = jnp.sum(oh1 * before, axis=0, keepdims=True)
    rank2 = jnp.sum(oh2 * before, axis=0, keepdims=True)
    rank_ref[...] = jnp.concatenate([rank1, rank2], axis=0).astype(jnp.int32)
    base_ref[...] = base_ref[...] + jnp.sum(oh, axis=1, keepdims=True)
    cnt_ref[...] = jnp.broadcast_to(base_ref[...], cnt_ref.shape)


def _merge(pg, ga, o, x, w_attn_up, w_out, g2, wr, br):
    N, D = x.shape
    tm = min(ROW_TILE, N)
    aw = o.shape[1]
    const2 = lambda i: (0, 0)
    row = lambda i: (i, 0)
    colblk = lambda i: (0, i)
    kern = functools.partial(_merge_kernel, tm=tm, d_model=D)
    return pl.pallas_call(
        kern,
        grid=(N // tm,),
        in_specs=[
            pl.BlockSpec((tm, D), row),
            pl.BlockSpec((tm, D), row),
            pl.BlockSpec((tm, aw), row),
            pl.BlockSpec((tm, D), row),
            pl.BlockSpec((aw, D), const2),
            pl.BlockSpec((D, D), const2),
            pl.BlockSpec((1, D), const2),
            pl.BlockSpec((ROUTER_ROWS, D), const2),
            pl.BlockSpec((ROUTER_ROWS, 1), const2),
        ],
        out_specs=[
            pl.BlockSpec((tm, D), row),
            pl.BlockSpec((tm * SUBLANES, LANES), row),
            pl.BlockSpec((TOP_K, tm), colblk),
            pl.BlockSpec((TOP_K, tm), colblk),
            pl.BlockSpec((TOP_K, tm), colblk),
            pl.BlockSpec((N_EXPERTS, LANES), const2),
        ],
        out_shape=[
            jax.ShapeDtypeStruct((N, D), f32),
            jax.ShapeDtypeStruct((N * SUBLANES, LANES), f32),
            jax.ShapeDtypeStruct((TOP_K, N), jnp.int32),
            jax.ShapeDtypeStruct((TOP_K, N), f32),
            jax.ShapeDtypeStruct((TOP_K, N), jnp.int32),
            jax.ShapeDtypeStruct((N_EXPERTS, LANES), f32),
        ],
        scratch_shapes=[pltpu.VMEM((N_EXPERTS, 1), f32)],
        compiler_params=pltpu.CompilerParams(dimension_semantics=("arbitrary",), vmem_limit_bytes=VMEM_LIMIT),
        name="merge_router",
    )(pg, ga, o, x, w_attn_up, w_out, g2, wr, br)


def _start_row_gather(idx_ref, n_rows, src_hbm, dst, sem):
    def body(r, carry):
        t = idx_ref[0, r]
        pltpu.make_async_copy(src_hbm.at[pl.ds(pl.multiple_of(t * SUBLANES, SUBLANES), SUBLANES), :],
                              dst.at[pl.ds(pl.multiple_of(r * SUBLANES, SUBLANES), SUBLANES), :],
                              sem).start()
        return carry
    lax.fori_loop(0, n_rows, body, 0, unroll=8)


def _wait_row_gather(src_hbm, dst, sem):
    pltpu.make_async_copy(src_hbm.at[pl.ds(0, dst.shape[0]), :], dst, sem).wait()


def _rows_from_gather_layout(buf, n_rows, n_tiles):
    return jnp.concatenate([buf[pl.ds(s, n_rows, stride=SUBLANES), :] for s in range(n_tiles)], axis=-1)


def _expert_kernel(beid_ref, nbu_ref, tok_ref, tokn_ref, hn_hbm, wg_ref, wu_ref, wd_ref, ys_ref,
                   buf, sem, wg_bf, wu_bf, wd_bf, *, rows, d_model):
    b = pl.program_id(0)
    nbu = nbu_ref[0]
    slot = b % 2

    @pl.when(b == 0)
    def _():
        _start_row_gather(tok_ref, rows, hn_hbm, buf.at[0], sem.at[0])

    @pl.when(b + 1 < nbu)
    def _():
        _start_row_gather(tokn_ref, rows, hn_hbm, buf.at[1 - slot], sem.at[1 - slot])

    changed = jnp.logical_or(b == 0, beid_ref[b] != beid_ref[jnp.maximum(b - 1, 0)])

    @pl.when(changed)
    def _():
        wg_bf[...] = wg_ref[...].astype(bf16)
        wu_bf[...] = wu_ref[...].astype(bf16)
        wd_bf[...] = wd_ref[...].astype(bf16)

    n_tiles = d_model // LANES

    @pl.when(b < nbu)
    def _():
        _wait_row_gather(hn_hbm, buf.at[slot], sem.at[slot])
        xb = _rows_from_gather_layout(buf.at[slot], rows, n_tiles).astype(bf16)
        hdn = jax.nn.silu(_dot(xb, wg_bf[...])) * _dot(xb, wu_bf[...])
        y = _dot(hdn.astype(bf16), wd_bf[...])
        for s in range(n_tiles):
            ys_ref[pl.ds(s, rows, stride=SUBLANES), :] = y[:, s * LANES:(s + 1) * LANES]

    @pl.when(jnp.logical_and(b >= nbu, b > 0))
    def _():
        ys_ref[...] = jnp.zeros_like(ys_ref)


def _experts(block_eid, n_used, slot_tok, hn_g, w_gate, w_up, w_down):
    n_blocks = block_eid.shape[0]
    rows = slot_tok.shape[2]
    E, D, DE = w_gate.shape
    last = n_blocks - 1
    kern = functools.partial(_expert_kernel, rows=rows, d_model=D)
    return pl.pallas_call(
        kern,
        grid_spec=pltpu.PrefetchScalarGridSpec(
            num_scalar_prefetch=2,
            grid=(n_blocks,),
            in_specs=[
                pl.BlockSpec((None, 1, rows), lambda b, *_: (b, 0, 0), memory_space=pltpu.SMEM),
                pl.BlockSpec((None, 1, rows), lambda b, *_: (jnp.minimum(b + 1, last), 0, 0),
                             memory_space=pltpu.SMEM),
                pl.BlockSpec(memory_space=pl.ANY),
                pl.BlockSpec((None, D, DE), lambda b, eid, nbu: (eid[b], 0, 0)),
                pl.BlockSpec((None, D, DE), lambda b, eid, nbu: (eid[b], 0, 0)),
                pl.BlockSpec((None, DE, D), lambda b, eid, nbu: (eid[b], 0, 0)),
            ],
            out_specs=pl.BlockSpec((rows * SUBLANES, LANES), lambda b, *_: (b, 0)),
            scratch_shapes=[
                pltpu.VMEM((2, rows * SUBLANES, LANES), f32),
                pltpu.SemaphoreType.DMA((2,)),
                pltpu.VMEM((D, DE), bf16),
                pltpu.VMEM((D, DE), bf16),
                pltpu.VMEM((DE, D), bf16),
            ],
        ),
        out_shape=jax.ShapeDtypeStruct((n_blocks * rows * SUBLANES, LANES), f32),
        compiler_params=pltpu.CompilerParams(dimension_semantics=("arbitrary",), vmem_limit_bytes=VMEM_LIMIT),
        name="experts",
    )(block_eid, n_used, slot_tok, slot_tok, hn_g, w_gate, w_up, w_down)


def _combine_kernel(dst_ref, dstn_ref, x1_ref, wt_ref, ys_hbm, out_ref, buf, sem, *, te, d_model):
    i = pl.program_id(0)
    n = pl.num_programs(0)
    slot = i % 2

    @pl.when(i == 0)
    def _():
        _start_row_gather(dst_ref, TOP_K * te, ys_hbm, buf.at[0], sem.at[0])

    @pl.when(i + 1 < n)
    def _():
        _start_row_gather(dstn_ref, TOP_K * te, ys_hbm, buf.at[1 - slot], sem.at[1 - slot])

    _wait_row_gather(ys_hbm, buf.at[slot], sem.at[slot])
    n_tiles = d_model // LANES
    both = _rows_from_gather_layout(buf.at[slot], TOP_K * te, n_tiles)
    wt = wt_ref[...]
    out_ref[...] = x1_ref[...] + (wt[:, 0:1] * both[:te] + wt[:, 1:2] * both[te:])


def _combine(dest_blocks, x1, wt_cols, ys):
    N, D = x1.shape
    te = dest_blocks.shape[2] // TOP_K
    n = N // te
    kern = functools.partial(_combine_kernel, te=te, d_model=D)
    return pl.pallas_call(
        kern,
        grid=(n,),
        in_specs=[
            pl.BlockSpec((None, 1, TOP_K * te), lambda i: (i, 0, 0), memory_space=pltpu.SMEM),
            pl.BlockSpec((None, 1, TOP_K * te), lambda i: (jnp.minimum(i + 1, n - 1), 0, 0),
                         memory_space=pltpu.SMEM),
            pl.BlockSpec((te, D), lambda i: (i, 0)),
            pl.BlockSpec((te, TOP_K), lambda i: (i, 0)),
            pl.BlockSpec(memory_space=pl.ANY),
        ],
        out_specs=pl.BlockSpec((te, D), lambda i: (i, 0)),
        out_shape=jax.ShapeDtypeStruct((N, D), f32),
        scratch_shapes=[pltpu.VMEM((2, TOP_K * te * SUBLANES, LANES), f32), pltpu.SemaphoreType.DMA((2,))],
        compiler_params=pltpu.CompilerParams(dimension_semantics=("arbitrary",), vmem_limit_bytes=VMEM_LIMIT),
        name="combine",
    )(dest_blocks, dest_blocks, x1, wt_cols, ys)


def _layer(x, l, norm1_g, w_in, pool_w, pool_scale, w_pool_up, q_norm_g, k_norm_g, lambda_q1, lambda_k1,
           lambda_q2, lambda_k2, subln_g, w_attn_up, w_out, norm2_g, w_router_group, b_router_group,
           w_router_expert, b_router_expert, w_expert_gate, w_expert_up, w_expert_down):
    B, S, D = x.shape
    N = B * S
    head_dim = q_norm_g.shape[0]
    lam_init = 0.8 - 0.6 * math.exp(-0.3 * l)
    reps = (N_HEADS * 2 * head_dim) // head_dim

    qg = (jnp.tile(q_norm_g, reps) * (head_dim ** -0.5))[None, :]
    kg = jnp.tile(k_norm_g, reps)[None, :]
    pg, ga, qn, kn, v = _in_proj(x, norm1_g[None, :], w_in.astype(bf16), pool_w.astype(bf16),
                                 pool_scale[None, :], w_pool_up.astype(bf16), qg, kg)

    tq = min(ATTN_TILE, S)
    slopes = jnp.asarray([2.0 ** (-8.0 * (h + 1) / N_HEADS) for h in range(N_HEADS)], f32)
    dist = (jnp.arange(tq)[:, None] - jnp.arange(tq)[None, :]).astype(f32)
    bias = -slopes[:, None, None] * dist[None]
    o = _attention(qn, kn, v, bias, slopes, lambda_q1[None, :], lambda_k1[None, :], lambda_q2[None, :],
                   lambda_k2[None, :], subln_g[None, :], lam_init)

    wr = jnp.zeros((ROUTER_ROWS, D), f32)
    wr = wr.at[:N_GROUPS].set(w_router_group.T).at[SUBLANES:].set(w_router_expert.T).astype(bf16)
    br = jnp.zeros((ROUTER_ROWS, 1), f32)
    br = br.at[:N_GROUPS, 0].set(b_router_group).at[SUBLANES:, 0].set(b_router_expert)
    x1, hn_g, eid, wts, rank, cnt = _merge(pg.reshape(N, D), ga.reshape(N, D), o.reshape(N, -1),
                                           x.reshape(N, D), w_attn_up.astype(bf16), w_out.astype(bf16),
                                           norm2_g[None, :], wr, br)

    R = EXPERT_ROWS
    M = N * TOP_K
    counts = cnt[:, 0].astype(jnp.int32)
    padded = (counts + R - 1) // R * R
    pend = jnp.cumsum(padded)
    pstart = pend - padded
    dest = pstart[eid] + rank
    n_blocks = -(-M // R) + N_EXPERTS
    tok = jnp.broadcast_to(jnp.arange(N, dtype=jnp.int32)[None, :], (TOP_K, N))
    slot_tok = jnp.zeros((n_blocks * R,), jnp.int32).at[dest.reshape(-1)].set(tok.reshape(-1))
    block_eid = jnp.minimum(jnp.searchsorted(pend, jnp.arange(n_blocks, dtype=jnp.int32) * R, side='right'),
                            N_EXPERTS - 1).astype(jnp.int32)
    n_used = (pend[-1:] // R).astype(jnp.int32)

    ys = _experts(block_eid, n_used, slot_tok.reshape(n_blocks, 1, R), hn_g, w_expert_gate, w_expert_up,
                  w_expert_down)

    te = min(COMBINE_TILE, N)
    dest_blocks = dest.reshape(TOP_K, N // te, te).transpose(1, 0, 2).reshape(N // te, 1, TOP_K * te)
    out = _combine(dest_blocks, x1, wts.T, ys)
    return out.reshape(B, S, D)


def kernel(x, norm1_g, w_in, pool_w, pool_scale, w_pool_up, q_norm_g, k_norm_g, lambda_q1, lambda_k1, lambda_q2,
           lambda_k2, subln_g, w_attn_up, w_out, norm2_g, w_router_group, b_router_group, w_router_expert,
           b_router_expert, w_expert_gate, w_expert_up, w_expert_down):
    params = (norm1_g, w_in, pool_w, pool_scale, w_pool_up, q_norm_g, k_norm_g, lambda_q1, lambda_k1, lambda_q2,
              lambda_k2, subln_g, w_attn_up, w_out, norm2_g, w_router_group, b_router_group, w_router_expert,
              b_router_expert, w_expert_gate, w_expert_up, w_expert_down)
    for l in range(norm1_g.shape[0]):
        x = _layer(x, l, *(p[l] for p in params))
    return x
```

```python
import functools
import math

import jax
import jax.numpy as jnp
from jax import lax
from jax.experimental import pallas as pl
from jax.experimental.pallas import tpu as pltpu

EPS = 1e-6
POOL_WINDOWS = (2, 4, 8, 16)
POOL_HALO = 16
N_HEADS = 4
N_GROUPS = 4
EXPERTS_PER_GROUP = 8
N_EXPERTS = N_GROUPS * EXPERTS_PER_GROUP
TOP_K = 2
LANES = 128
SUBLANES = 8
ROUTER_ROWS = 8 + N_EXPERTS

ROW_TILE = 512
ATTN_TILE = 256
EXPERT_ROWS = 512
MOVE_TILE = 256
VMEM_LIMIT = 52 * 1024 * 1024

f32 = jnp.float32
bf16 = jnp.bfloat16


def _dot(a, b):
    return jnp.dot(a, b, preferred_element_type=f32)


def _dot_nt(a, b):
    return lax.dot_general(a, b, (((1,), (1,)), ((), ())), preferred_element_type=f32)


def _half_lane_rmsnorm(t, n_tiles, rows):
    lane = lax.broadcasted_iota(jnp.int32, (rows, LANES), 1)
    lo_mask = lane < (LANES // 2)
    outs = []
    for i in range(n_tiles):
        c = t[:, i * LANES:(i + 1) * LANES]
        sq = c * c
        lo = jnp.sum(jnp.where(lo_mask, sq, 0.0), axis=-1, keepdims=True)
        hi = jnp.sum(jnp.where(lo_mask, 0.0, sq), axis=-1, keepdims=True)
        ms = jnp.where(lo_mask, lo, hi) * (2.0 / LANES)
        outs.append(c * lax.rsqrt(ms + EPS))
    return jnp.concatenate(outs, axis=-1)


def _in_proj_kernel(x_ref, g1_ref, win_ref, poolw_ref, pscale_ref, wpu_ref, qg_ref, kg_ref,
                    pg_ref, ga_ref, qn_ref, kn_ref, vt_ref, prev_ref, *, tm, tk, d_model, pool_width, qk_width,
                    attn_width):
    j = pl.program_id(1)
    x = x_ref[...]
    ms = jnp.mean(x * x, axis=-1, keepdims=True)
    h = (x * lax.rsqrt(ms + EPS) * g1_ref[...]).astype(bf16)

    def proj(lo, width):
        return _dot(h, win_ref[:, lo:lo + width])

    off_q = pool_width
    off_k = off_q + qk_width
    off_v = off_k + qk_width
    off_gp = off_v + attn_width
    off_ga = off_gp + d_model

    u = proj(0, pool_width)

    @pl.when(j == 0)
    def _():
        prev_ref[...] = jnp.zeros_like(prev_ref)

    ext = jnp.concatenate([prev_ref[...], u], axis=0)
    prev_ref[...] = u[tm - POOL_HALO:, :]
    pos = j * tm + lax.broadcasted_iota(jnp.int32, (tm, 1), 0)
    group = pool_width // len(POOL_WINDOWS)
    ys = []
    for g, w in enumerate(POOL_WINDOWS):
        acc = ext[:, g * group:(g + 1) * group]
        shift = 1
        while shift < w:
            acc = acc + pltpu.roll(acc, shift, 0)
            shift *= 2
        wsum = acc[POOL_HALO:, :]
        cnt = jnp.minimum(pos + 1, w).astype(f32)
        d = wsum / cnt - u[:, g * group:(g + 1) * group]
        ys.append(_dot(d.astype(bf16), poolw_ref[g]))
    y = jnp.concatenate(ys, axis=-1) * pscale_ref[...]
    pool_out = _dot(y.astype(bf16), wpu_ref[...])

    half = d_model // 2
    for c in range(2):
        gp = jax.nn.sigmoid(proj(off_gp + c * half, half))
        pg_ref[:, c * half:(c + 1) * half] = (gp * pool_out[:, c * half:(c + 1) * half]).astype(bf16)
        ga_ref[:, c * half:(c + 1) * half] = jax.nn.sigmoid(proj(off_ga + c * half, half)).astype(bf16)

    n_tiles = qk_width // LANES
    qn_ref[...] = (_half_lane_rmsnorm(proj(off_q, qk_width), n_tiles, tm) * qg_ref[...]).astype(bf16)
    kn_ref[...] = (_half_lane_rmsnorm(proj(off_k, qk_width), n_tiles, tm) * kg_ref[...]).astype(bf16)
    v = proj(off_v, attn_width)
    for hh in range(attn_width // LANES):
        for c in range(tm // tk):
            vt_ref[hh, c] = v[c * tk:(c + 1) * tk, hh * LANES:(hh + 1) * LANES].T.astype(bf16)


def _in_proj(x, g1, w_in, pool_w, pool_scale, w_pool_up, qg, kg):
    B, S, D = x.shape
    tm = min(ROW_TILE, S)
    pool_width = w_pool_up.shape[0]
    qk_width = qg.shape[1]
    attn_width = qk_width
    in_width = w_in.shape[1]
    const2 = lambda b, j: (0, 0)
    row = lambda b, j: (b, j, 0)
    tk = min(ATTN_TILE, S)
    n_heads = attn_width // LANES
    kern = functools.partial(_in_proj_kernel, tm=tm, tk=tk, d_model=D, pool_width=pool_width, qk_width=qk_width,
                             attn_width=attn_width)
    return pl.pallas_call(
        kern,
        grid=(B, S // tm),
        in_specs=[
            pl.BlockSpec((None, tm, D), row),
            pl.BlockSpec((1, D), const2),
            pl.BlockSpec((D, in_width), const2),
            pl.BlockSpec(pool_w.shape, lambda b, j: (0, 0, 0)),
            pl.BlockSpec((1, pool_width), const2),
            pl.BlockSpec((pool_width, D), const2),
            pl.BlockSpec((1, qk_width), const2),
            pl.BlockSpec((1, qk_width), const2),
        ],
        out_specs=[
            pl.BlockSpec((None, tm, D), row),
            pl.BlockSpec((None, tm, D), row),
            pl.BlockSpec((None, tm, qk_width), row),
            pl.BlockSpec((None, tm, qk_width), row),
            pl.BlockSpec((None, n_heads, tm // tk, LANES, tk), lambda b, j: (b, 0, j, 0, 0)),
        ],
        out_shape=[
            jax.ShapeDtypeStruct((B, S, D), bf16),
            jax.ShapeDtypeStruct((B, S, D), bf16),
            jax.ShapeDtypeStruct((B, S, qk_width), bf16),
            jax.ShapeDtypeStruct((B, S, qk_width), bf16),
            jax.ShapeDtypeStruct((B, n_heads, S // tk, LANES, tk), bf16),
        ],
        scratch_shapes=[pltpu.VMEM((POOL_HALO, pool_width), f32)],
        compiler_params=pltpu.CompilerParams(dimension_semantics=("arbitrary", "arbitrary"),
                                             vmem_limit_bytes=VMEM_LIMIT),
        name="in_proj",
    )(x, g1, w_in, pool_w, pool_scale, w_pool_up, qg, kg)


def _attn_kernel(slopes_ref, q_ref, k_ref, vt_ref, bias_ref, lq1_ref, lk1_ref, lq2_ref, lk2_ref, sg_ref,
                 o_ref, m_ref, l_ref, acc_ref, *, tq, lam_init):
    h = pl.program_id(1)
    i = pl.program_id(2)
    slope = slopes_ref[h]
    q = q_ref[...]
    lane = lax.broadcasted_iota(jnp.int32, q.shape, 1)
    first = lane < (LANES // 2)
    zero = jnp.zeros_like(q)
    qs = jnp.concatenate([jnp.where(first, q, zero), jnp.where(first, zero, q)], axis=0)
    bias = bias_ref[...]

    def scores(j):
        kb = k_ref[pl.ds(pl.multiple_of(j * tq, tq), tq), :]
        return _dot_nt(kb, qs)

    m_ref[...] = jnp.full(m_ref.shape, -jnp.inf, f32)
    l_ref[...] = jnp.zeros_like(l_ref)
    acc_ref[...] = jnp.zeros_like(acc_ref)

    def update(s, j, c):
        m_old = m_ref[...]
        m_new = jnp.maximum(m_old, jnp.max(s, axis=0, keepdims=True) + c)
        p = jnp.exp(s - (m_new - c))
        alpha = jnp.exp(m_old - m_new)
        l_ref[...] = alpha * l_ref[...] + jnp.sum(p, axis=0, keepdims=True)
        acc_ref[...] = alpha * acc_ref[...] + _dot(vt_ref[j], p.astype(bf16))
        m_ref[...] = m_new

    def body(j, s):
        s_next = scores(j + 1)
        update(s + bias, j, -slope * ((i - j) * tq).astype(f32))
        return s_next

    s = lax.fori_loop(0, i, body, scores(0))
    kk = lax.broadcasted_iota(jnp.int32, bias.shape, 0)
    qq = lax.broadcasted_iota(jnp.int32, bias.shape, 1)
    qq = jnp.where(qq >= tq, qq - tq, qq)
    update(s + jnp.where(kk <= qq, bias, -jnp.inf), i, 0.0)

    lam = (jnp.exp(jnp.sum(lq1_ref[...] * lk1_ref[...], keepdims=True))
           - jnp.exp(jnp.sum(lq2_ref[...] * lk2_ref[...], keepdims=True)) + lam_init)
    o_all = acc_ref[...] / l_ref[...]
    o = o_all[:, :tq] - lam * o_all[:, tq:]
    ms = jnp.mean(o * o, axis=0, keepdims=True)
    on = o * lax.rsqrt(ms + EPS) * sg_ref[...] * (1.0 - lam_init)
    o_ref[...] = on.T.astype(bf16)


def _attention(qn, kn, vt, bias, slopes, lq1, lk1, lq2, lk2, subln_col, lam_init):
    B, S, _ = qn.shape
    tq = min(ATTN_TILE, S)
    nkv = S // tq
    const2 = lambda b, h, i, *_: (0, 0)
    kern = functools.partial(_attn_kernel, tq=tq, lam_init=lam_init)
    hd = lq1.shape[1]
    return pl.pallas_call(
        kern,
        grid_spec=pltpu.PrefetchScalarGridSpec(
            num_scalar_prefetch=1,
            grid=(B, N_HEADS, S // tq),
            in_specs=[
                pl.BlockSpec((None, tq, LANES), lambda b, h, i, *_: (b, i, h)),
                pl.BlockSpec((None, S, LANES), lambda b, h, i, *_: (b, 0, h)),
                pl.BlockSpec((None, None, nkv, LANES, tq), lambda b, h, i, *_: (b, h, 0, 0, 0)),
                pl.BlockSpec((None, tq, 2 * tq), lambda b, h, i, *_: (h, 0, 0)),
                pl.BlockSpec((1, hd), const2),
                pl.BlockSpec((1, hd), const2),
                pl.BlockSpec((1, hd), const2),
                pl.BlockSpec((1, hd), const2),
                pl.BlockSpec((LANES, 1), const2),
            ],
            out_specs=pl.BlockSpec((None, tq, LANES), lambda b, h, i, *_: (b, i, h)),
            scratch_shapes=[pltpu.VMEM((1, 2 * tq), f32), pltpu.VMEM((1, 2 * tq), f32),
                            pltpu.VMEM((LANES, 2 * tq), f32)],
        ),
        out_shape=jax.ShapeDtypeStruct((B, S, N_HEADS * LANES), bf16),
        compiler_params=pltpu.CompilerParams(dimension_semantics=("arbitrary", "arbitrary", "arbitrary"),
                                             vmem_limit_bytes=VMEM_LIMIT),
        name="diff_attn",
    )(slopes, qn, kn, vt, bias, lq1, lk1, lq2, lk2, subln_col)


def _merge_kernel(pg_ref, ga_ref, o_ref, x_ref, wau_ref, wout_ref, g2_ref, wr_ref, br_ref,
                  x1_ref, hn_ref, eid_ref, wt_ref, rank_ref, cnt_ref, base_ref, *, tm, d_model):
    step = pl.program_id(0)

    @pl.when(step == 0)
    def _():
        base_ref[...] = jnp.zeros_like(base_ref)

    attn_out = _dot(o_ref[...], wau_ref[...])
    merged = pg_ref[...].astype(f32) + ga_ref[...].astype(f32) * attn_out
    x1 = x_ref[...] + _dot(merged.astype(bf16), wout_ref[...])
    x1_ref[...] = x1
    ms = jnp.mean(x1 * x1, axis=-1, keepdims=True)
    hn = x1 * lax.rsqrt(ms + EPS) * g2_ref[...]
    for s in range(d_model // LANES):
        hn_ref[pl.ds(s, tm, stride=SUBLANES), :] = hn[:, s * LANES:(s + 1) * LANES]

    logits = _dot_nt(wr_ref[...], hn.astype(bf16)) + br_ref[...]
    lg = logits[0:N_GROUPS]
    gmax = jnp.max(lg, axis=0, keepdims=True)
    p_top = 1.0 / jnp.sum(jnp.exp(lg - gmax), axis=0, keepdims=True)
    grow = lax.broadcasted_iota(jnp.int32, lg.shape, 0).astype(f32)
    g_idx = jnp.min(jnp.where(lg == gmax, grow, float(N_GROUPS)), axis=0, keepdims=True)

    sel = jnp.zeros((EXPERTS_PER_GROUP, tm), f32)
    for g in range(N_GROUPS):
        le_g = logits[SUBLANES + g * EXPERTS_PER_GROUP:SUBLANES + (g + 1) * EXPERTS_PER_GROUP]
        sel = jnp.where(g_idx == float(g), le_g, sel)
    erow = lax.broadcasted_iota(jnp.int32, sel.shape, 0).astype(f32)
    e1 = jnp.max(sel, axis=0, keepdims=True)
    i1 = jnp.min(jnp.where(sel == e1, erow, float(EXPERTS_PER_GROUP)), axis=0, keepdims=True)
    sel2 = jnp.where(erow == i1, -jnp.inf, sel)
    e2 = jnp.max(sel2, axis=0, keepdims=True)
    i2 = jnp.min(jnp.where(sel2 == e2, erow, float(EXPERTS_PER_GROUP)), axis=0, keepdims=True)
    r = jnp.exp(e2 - e1)
    w1 = p_top / (1.0 + r)
    w2 = p_top * r / (1.0 + r)
    eid1 = g_idx * float(EXPERTS_PER_GROUP) + i1
    eid2 = g_idx * float(EXPERTS_PER_GROUP) + i2
    eid_ref[...] = jnp.concatenate([eid1, eid2], axis=0).astype(jnp.int32)
    wt_ref[...] = jnp.concatenate([w1, w2], axis=0)

    xrow = lax.broadcasted_iota(jnp.int32, (N_EXPERTS, tm), 0).astype(f32)
    oh1 = jnp.where(xrow == eid1, 1.0, 0.0)
    oh2 = jnp.where(xrow == eid2, 1.0, 0.0)
    oh = oh1 + oh2
    a = lax.broadcasted_iota(jnp.int32, (tm, tm), 0)
    b = lax.broadcasted_iota(jnp.int32, (tm, tm), 1)
    upper = jnp.where(a <= b, 1.0, 0.0).astype(bf16)
    before = _dot(oh.astype(bf16), upper) + base_ref[...] - 1.0
    rank1 = jnp.sum(oh1 * before, axis=0, keepdims=True)
    rank2 = jnp.sum(oh2 * before, axis=0, keepdims=True)
    rank_ref[...] = jnp.concatenate([rank1, rank2], axis=0).astype(jnp.int32)
    base_ref[...] = base_ref[...] + jnp.sum(oh, axis=1, keepdims=True)
    cnt_ref[...] = jnp.broadcast_to(base_ref[...], cnt_ref.shape)


def _merge(pg, ga, o, x, w_attn_up, w_out, g2, wr, br):
    N, D = x.shape
    tm = min(ROW_TILE, N)
    aw = o.shape[1]
    const2 = lambda i: (0, 0)
    row = lambda i: (i, 0)
    colblk = lambda i: (0, i)
    kern = functools.partial(_merge_kernel, tm=tm, d_model=D)
    return pl.pallas_call(
        kern,
        grid=(N // tm,),
        in_specs=[
            pl.BlockSpec((tm, D), row),
            pl.BlockSpec((tm, D), row),
            pl.BlockSpec((tm, aw), row),
            pl.BlockSpec((tm, D), row),
            pl.BlockSpec((aw, D), const2),
            pl.BlockSpec((D, D), const2),
            pl.BlockSpec((1, D), const2),
            pl.BlockSpec((ROUTER_ROWS, D), const2),
            pl.BlockSpec((ROUTER_ROWS, 1), const2),
        ],
        out_specs=[
            pl.BlockSpec((tm, D), row),
            pl.BlockSpec((tm * SUBLANES, LANES), row),
            pl.BlockSpec((TOP_K, tm), colblk),
            pl.BlockSpec((TOP_K, tm), colblk),
            pl.BlockSpec((TOP_K, tm), colblk),
            pl.BlockSpec((N_EXPERTS, LANES), const2),
        ],
        out_shape=[
            jax.ShapeDtypeStruct((N, D), f32),
            jax.ShapeDtypeStruct((N * SUBLANES, LANES), f32),
            jax.ShapeDtypeStruct((TOP_K, N), jnp.int32),
            jax.ShapeDtypeStruct((TOP_K, N), f32),
            jax.ShapeDtypeStruct((TOP_K, N), jnp.int32),
            jax.ShapeDtypeStruct((N_EXPERTS, LANES), f32),
        ],
        scratch_shapes=[pltpu.VMEM((N_EXPERTS, 1), f32)],
        compiler_params=pltpu.CompilerParams(dimension_semantics=("arbitrary",), vmem_limit_bytes=VMEM_LIMIT),
        name="merge_router",
    )(pg, ga, o, x, w_attn_up, w_out, g2, wr, br)


def _row(ref, r):
    return ref.at[pl.ds(pl.multiple_of(r * SUBLANES, SUBLANES), SUBLANES), :]


def _wait_bytes_of(ref_like, any_hbm, sem):
    n = ref_like.shape[0]
    pltpu.make_async_copy(any_hbm.at[pl.ds(0, n), :], any_hbm.at[pl.ds(0, n), :], sem).wait()


def _dispatch_kernel(pstart_ref, pend_ref, eid_ref, rank_ref, hn_hbm, xs_hbm, zero_ref, sem, zsem, *, tt, rows,
                     n_blocks):
    i = pl.program_id(0)
    n = pl.num_programs(0)
    slot = i % 2

    @pl.when(i == 0)
    def _():
        zero_ref[...] = jnp.zeros_like(zero_ref)
        for e in range(N_EXPERTS):
            @pl.when(pend_ref[e] > pstart_ref[e])
            def _():
                pltpu.make_async_copy(zero_ref, xs_hbm.at[pl.ds(pl.multiple_of((pend_ref[e] - rows) * SUBLANES,
                                                                               SUBLANES), rows * SUBLANES), :],
                                      zsem).start()
        def zero_block(b, carry):
            pltpu.make_async_copy(zero_ref, xs_hbm.at[pl.ds(pl.multiple_of(b * (rows * SUBLANES), SUBLANES),
                                                            rows * SUBLANES), :], zsem).start()
            return carry

        def wait_block(b, carry):
            pltpu.make_async_copy(zero_ref, xs_hbm.at[pl.ds(0, rows * SUBLANES), :], zsem).wait()
            return carry

        first_unused = pend_ref[N_EXPERTS - 1] // rows
        lax.fori_loop(first_unused, n_blocks, zero_block, 0)
        for e in range(N_EXPERTS):
            @pl.when(pend_ref[e] > pstart_ref[e])
            def _():
                wait_block(0, 0)
        lax.fori_loop(first_unused, n_blocks, wait_block, 0)

    def body(t, carry):
        tok = i * tt + t
        for k in range(TOP_K):
            dest = pstart_ref[eid_ref[0, k * tt + t]] + rank_ref[0, k * tt + t]
            pltpu.make_async_copy(_row(hn_hbm, tok), _row(xs_hbm, dest), sem.at[slot]).start()
        return carry
    lax.fori_loop(0, tt, body, 0, unroll=8)

    @pl.when(i > 0)
    def _():
        _wait_bytes_of(zero_ref.at[pl.ds(0, TOP_K * tt * SUBLANES), :], hn_hbm, sem.at[1 - slot])

    @pl.when(i == n - 1)
    def _():
        _wait_bytes_of(zero_ref.at[pl.ds(0, TOP_K * tt * SUBLANES), :], hn_hbm, sem.at[slot])


def _dispatch(pstart, pend, eid_blocks, rank_blocks, hn_g, n_slots):
    n, _, width = eid_blocks.shape
    tt = width // TOP_K
    rows = EXPERT_ROWS
    assert rows >= TOP_K * tt
    kern = functools.partial(_dispatch_kernel, tt=tt, rows=rows, n_blocks=n_slots // rows)
    smem_blk = pl.BlockSpec((None, 1, width), lambda i, *_: (i, 0, 0), memory_space=pltpu.SMEM)
    return pl.pallas_call(
        kern,
        grid_spec=pltpu.PrefetchScalarGridSpec(
            num_scalar_prefetch=2,
            grid=(n,),
            in_specs=[smem_blk, smem_blk, pl.BlockSpec(memory_space=pl.ANY)],
            out_specs=pl.BlockSpec(memory_space=pl.ANY),
            scratch_shapes=[pltpu.VMEM((rows * SUBLANES, LANES), f32), pltpu.SemaphoreType.DMA((2,)),
                            pltpu.SemaphoreType.DMA(())],
        ),
        out_shape=jax.ShapeDtypeStruct((n_slots * SUBLANES, LANES), f32),
        compiler_params=pltpu.CompilerParams(dimension_semantics=("arbitrary",), vmem_limit_bytes=VMEM_LIMIT,
                                             has_side_effects=True),
        name="dispatch",
    )(pstart, pend, eid_blocks, rank_blocks, hn_g)


def _rows_from_token_major(buf, n_rows, n_tiles):
    return jnp.concatenate([buf[pl.ds(s, n_rows, stride=SUBLANES), :] for s in range(n_tiles)], axis=-1)


def _expert_kernel(beid_ref, nbu_ref, xs_ref, wg_ref, wu_ref, wd_ref, ys_ref, wg_bf, wu_bf, wd_bf, *, rows,
                   d_model):
    b = pl.program_id(0)
    changed = jnp.logical_or(b == 0, beid_ref[b] != beid_ref[jnp.maximum(b - 1, 0)])

    @pl.when(changed)
    def _():
        wg_bf[...] = wg_ref[...].astype(bf16)
        wu_bf[...] = wu_ref[...].astype(bf16)
        wd_bf[...] = wd_ref[...].astype(bf16)

    n_tiles = d_model // LANES

    @pl.when(b < nbu_ref[0])
    def _():
        xb = _rows_from_token_major(xs_ref, rows, n_tiles).astype(bf16)
        hdn = jax.nn.silu(_dot(xb, wg_bf[...])) * _dot(xb, wu_bf[...])
        y = _dot(hdn.astype(bf16), wd_bf[...])
        for s in range(n_tiles):
            ys_ref[pl.ds(s, rows, stride=SUBLANES), :] = y[:, s * LANES:(s + 1) * LANES]

    @pl.when(b >= nbu_ref[0])
    def _():
        ys_ref[...] = jnp.zeros_like(ys_ref)


def _experts(block_eid, n_used, xs, w_gate, w_up, w_down):
    n_blocks = block_eid.shape[0]
    rows = EXPERT_ROWS
    E, D, DE = w_gate.shape
    kern = functools.partial(_expert_kernel, rows=rows, d_model=D)
    xs_idx = lambda b, eid, nbu: (jnp.minimum(b, nbu[0] - 1), 0)
    return pl.pallas_call(
        kern,
        grid_spec=pltpu.PrefetchScalarGridSpec(
            num_scalar_prefetch=2,
            grid=(n_blocks,),
            in_specs=[
                pl.BlockSpec((rows * SUBLANES, LANES), xs_idx),
                pl.BlockSpec((None, D, DE), lambda b, eid, nbu: (eid[b], 0, 0)),
                pl.BlockSpec((None, D, DE), lambda b, eid, nbu: (eid[b], 0, 0)),
                pl.BlockSpec((None, DE, D), lambda b, eid, nbu: (eid[b], 0, 0)),
            ],
            out_specs=pl.BlockSpec((rows * SUBLANES, LANES), lambda b, *_: (b, 0)),
            scratch_shapes=[
                pltpu.VMEM((D, DE), bf16),
                pltpu.VMEM((D, DE), bf16),
                pltpu.VMEM((DE, D), bf16),
            ],
        ),
        out_shape=jax.ShapeDtypeStruct((n_blocks * rows * SUBLANES, LANES), f32),
        compiler_params=pltpu.CompilerParams(dimension_semantics=("arbitrary",), vmem_limit_bytes=VMEM_LIMIT),
        name="experts",
    )(block_eid, n_used, xs, w_gate, w_up, w_down)


def _combine_kernel(pstart_ref, eid_ref, rank_ref, eidn_ref, rankn_ref, x1_ref, wt_ref, ys_hbm, out_ref,
                    buf, sem, *, te, d_model):
    i = pl.program_id(0)
    n = pl.num_programs(0)
    slot = i % 2

    def start_gather(e_ref, r_ref, dst, dsem):
        def body(r, carry):
            dest = pstart_ref[e_ref[0, r]] + r_ref[0, r]
            pltpu.make_async_copy(_row(ys_hbm, dest), _row(dst, r), dsem).start()
            return carry
        lax.fori_loop(0, TOP_K * te, body, 0, unroll=8)

    @pl.when(i == 0)
    def _():
        start_gather(eid_ref, rank_ref, buf.at[0], sem.at[0])

    @pl.when(i + 1 < n)
    def _():
        start_gather(eidn_ref, rankn_ref, buf.at[1 - slot], sem.at[1 - slot])

    _wait_bytes_of(buf.at[slot], ys_hbm, sem.at[slot])
    n_tiles = d_model // LANES
    both = _rows_from_token_major(buf.at[slot], TOP_K * te, n_tiles)
    wt = wt_ref[...]
    out_ref[...] = x1_ref[...] + (wt[:, 0:1] * both[:te] + wt[:, 1:2] * both[te:])


def _combine(pstart, eid_blocks, rank_blocks, x1, wt_cols, ys):
    N, D = x1.shape
    n, _, width = eid_blocks.shape
    te = width // TOP_K
    kern = functools.partial(_combine_kernel, te=te, d_model=D)
    cur = pl.BlockSpec((None, 1, width), lambda i, *_: (i, 0, 0), memory_space=pltpu.SMEM)
    nxt = pl.BlockSpec((None, 1, width), lambda i, *_: (jnp.minimum(i + 1, n - 1), 0, 0), memory_space=pltpu.SMEM)
    return pl.pallas_call(
        kern,
        grid_spec=pltpu.PrefetchScalarGridSpec(
            num_scalar_prefetch=1,
            grid=(n,),
            in_specs=[
                cur, cur, nxt, nxt,
                pl.BlockSpec((te, D), lambda i, *_: (i, 0)),
                pl.BlockSpec((te, TOP_K), lambda i, *_: (i, 0)),
                pl.BlockSpec(memory_space=pl.ANY),
            ],
            out_specs=pl.BlockSpec((te, D), lambda i, *_: (i, 0)),
            scratch_shapes=[pltpu.VMEM((2, TOP_K * te * SUBLANES, LANES), f32), pltpu.SemaphoreType.DMA((2,))],
        ),
        out_shape=jax.ShapeDtypeStruct((N, D), f32),
        compiler_params=pltpu.CompilerParams(dimension_semantics=("arbitrary",), vmem_limit_bytes=VMEM_LIMIT),
        name="combine",
    )(pstart, eid_blocks, rank_blocks, eid_blocks, rank_blocks, x1, wt_cols, ys)


def _token_blocks(a, tt):
    k, n = a.shape
    return a.reshape(k, n // tt, tt).transpose(1, 0, 2).reshape(n // tt, 1, k * tt)


def _layer(x, l, norm1_g, w_in, pool_w, pool_scale, w_pool_up, q_norm_g, k_norm_g, lambda_q1, lambda_k1,
           lambda_q2, lambda_k2, subln_g, w_attn_up, w_out, norm2_g, w_router_group, b_router_group,
           w_router_expert, b_router_expert, w_expert_gate, w_expert_up, w_expert_down):
    B, S, D = x.shape
    N = B * S
    head_dim = q_norm_g.shape[0]
    lam_init = 0.8 - 0.6 * math.exp(-0.3 * l)
    reps = (N_HEADS * 2 * head_dim) // head_dim

    qg = (jnp.tile(q_norm_g, reps) * (head_dim ** -0.5))[None, :]
    kg = jnp.tile(k_norm_g, reps)[None, :]
    pg, ga, qn, kn, vt = _in_proj(x, norm1_g[None, :], w_in.astype(bf16), pool_w.astype(bf16),
                                  pool_scale[None, :], w_pool_up.astype(bf16), qg, kg)

    tq = min(ATTN_TILE, S)
    slopes = jnp.asarray([2.0 ** (-8.0 * (h + 1) / N_HEADS) for h in range(N_HEADS)], f32)
    dist = (jnp.arange(tq)[None, :] - jnp.arange(tq)[:, None]).astype(f32)
    bias = -slopes[:, None, None] * jnp.concatenate([dist, dist], axis=1)[None]
    o = _attention(qn, kn, vt, bias, slopes, lambda_q1[None, :], lambda_k1[None, :], lambda_q2[None, :],
                   lambda_k2[None, :], subln_g[:, None], lam_init)

    wr = jnp.zeros((ROUTER_ROWS, D), f32)
    wr = wr.at[:N_GROUPS].set(w_router_group.T).at[SUBLANES:].set(w_router_expert.T).astype(bf16)
    br = jnp.zeros((ROUTER_ROWS, 1), f32)
    br = br.at[:N_GROUPS, 0].set(b_router_group).at[SUBLANES:, 0].set(b_router_expert)
    x1, hn_g, eid, wts, rank, cnt = _merge(pg.reshape(N, D), ga.reshape(N, D), o.reshape(N, -1),
                                           x.reshape(N, D), w_attn_up.astype(bf16), w_out.astype(bf16),
                                           norm2_g[None, :], wr, br)

    R = EXPERT_ROWS
    counts = cnt[:, 0].astype(jnp.int32)
    padded = (counts + R - 1) // R * R
    pend = jnp.cumsum(padded).astype(jnp.int32)
    pstart = pend - padded
    n_blocks = -(-(N * TOP_K) // R) + N_EXPERTS
    starts = jnp.arange(n_blocks, dtype=jnp.int32) * R
    block_eid = jnp.minimum(jnp.sum((pend[None, :] <= starts[:, None]).astype(jnp.int32), axis=1), N_EXPERTS - 1)
    n_used = pend[-1:] // R

    tt = min(MOVE_TILE, N)
    eid_blocks = _token_blocks(eid, tt)
    rank_blocks = _token_blocks(rank, tt)
    xs = _dispatch(pstart, pend, eid_blocks, rank_blocks, hn_g, n_blocks * R)
    ys = _experts(block_eid, n_used, xs, w_expert_gate, w_expert_up, w_expert_down)
    out = _combine(pstart, eid_blocks, rank_blocks, x1, wts.T, ys)
    return out.reshape(B, S, D)


def kernel(x, norm1_g, w_in, pool_w, pool_scale, w_pool_up, q_norm_g, k_norm_g, lambda_q1, lambda_k1, lambda_q2,
           lambda_k2, subln_g, w_attn_up, w_out, norm2_g, w_router_group, b_router_group, w_router_expert,
           b_router_expert, w_expert_gate, w_expert_up, w_expert_down):
    params = (norm1_g, w_in, pool_w, pool_scale, w_pool_up, q_norm_g, k_norm_g, lambda_q1, lambda_k1, lambda_q2,
              lambda_k2, subln_g, w_attn_up, w_out, norm2_g, w_router_group, b_router_group, w_router_expert,
              b_router_expert, w_expert_gate, w_expert_up, w_expert_down)
    for l in range(norm1_g.shape[0]):
        x = _layer(x, l, *(p[l] for p in params))
    return x
```

```python
import functools
import math

import jax
import jax.numpy as jnp
from jax import lax
from jax.experimental import pallas as pl
from jax.experimental.pallas import tpu as pltpu

EPS = 1e-6
POOL_WINDOWS = (2, 4, 8, 16)
POOL_HALO = 16
N_HEADS = 4
N_GROUPS = 4
EXPERTS_PER_GROUP = 8
N_EXPERTS = N_GROUPS * EXPERTS_PER_GROUP
TOP_K = 2
LANES = 128
SUBLANES = 8
ROUTER_ROWS = 8 + N_EXPERTS

ROW_TILE = 512
ATTN_TILE = 256
EXPERT_ROWS = 512
MOVE_TILE = 256
VMEM_LIMIT = 52 * 1024 * 1024

f32 = jnp.float32
bf16 = jnp.bfloat16


def _dot(a, b):
    return jnp.dot(a, b, preferred_element_type=f32)


def _dot_nt(a, b):
    return lax.dot_general(a, b, (((1,), (1,)), ((), ())), preferred_element_type=f32)


def _half_lane_rmsnorm(t, n_tiles, rows):
    lane = lax.broadcasted_iota(jnp.int32, (rows, LANES), 1)
    lo_mask = lane < (LANES // 2)
    outs = []
    for i in range(n_tiles):
        c = t[:, i * LANES:(i + 1) * LANES]
        sq = c * c
        lo = jnp.sum(jnp.where(lo_mask, sq, 0.0), axis=-1, keepdims=True)
        hi = jnp.sum(jnp.where(lo_mask, 0.0, sq), axis=-1, keepdims=True)
        ms = jnp.where(lo_mask, lo, hi) * (2.0 / LANES)
        outs.append(c * lax.rsqrt(ms + EPS))
    return jnp.concatenate(outs, axis=-1)


def _in_proj_kernel(x_ref, g1_ref, win_ref, poolw_ref, pscale_ref, wpu_ref, qg_ref, kg_ref,
                    pg_ref, ga_ref, qn_ref, kn_ref, vt_ref, prev_ref, *, tm, tk, d_model, pool_width, qk_width,
                    attn_width):
    j = pl.program_id(1)
    x = x_ref[...]
    ms = jnp.mean(x * x, axis=-1, keepdims=True)
    h = (x * lax.rsqrt(ms + EPS) * g1_ref[...]).astype(bf16)

    def proj(lo, width):
        return _dot(h, win_ref[:, lo:lo + width])

    off_q = pool_width
    off_k = off_q + qk_width
    off_v = off_k + qk_width
    off_gp = off_v + attn_width
    off_ga = off_gp + d_model

    u = proj(0, pool_width)

    @pl.when(j == 0)
    def _():
        prev_ref[...] = jnp.zeros_like(prev_ref)

    ext = jnp.concatenate([prev_ref[...], u], axis=0)
    prev_ref[...] = u[tm - POOL_HALO:, :]
    pos = j * tm + lax.broadcasted_iota(jnp.int32, (tm, 1), 0)
    group = pool_width // len(POOL_WINDOWS)
    ys = []
    for g, w in enumerate(POOL_WINDOWS):
        acc = ext[:, g * group:(g + 1) * group]
        shift = 1
        while shift < w:
            acc = acc + pltpu.roll(acc, shift, 0)
            shift *= 2
        wsum = acc[POOL_HALO:, :]
        cnt = jnp.minimum(pos + 1, w).astype(f32)
        d = wsum / cnt - u[:, g * group:(g + 1) * group]
        ys.append(_dot(d.astype(bf16), poolw_ref[g]))
    y = jnp.concatenate(ys, axis=-1) * pscale_ref[...]
    pool_out = _dot(y.astype(bf16), wpu_ref[...])

    half = d_model // 2
    for c in range(2):
        gp = jax.nn.sigmoid(proj(off_gp + c * half, half))
        pg_ref[:, c * half:(c + 1) * half] = (gp * pool_out[:, c * half:(c + 1) * half]).astype(bf16)
        ga_ref[:, c * half:(c + 1) * half] = jax.nn.sigmoid(proj(off_ga + c * half, half)).astype(bf16)

    n_tiles = qk_width // LANES
    qn_ref[...] = (_half_lane_rmsnorm(proj(off_q, qk_width), n_tiles, tm) * qg_ref[...]).astype(bf16)
    kn_ref[...] = (_half_lane_rmsnorm(proj(off_k, qk_width), n_tiles, tm) * kg_ref[...]).astype(bf16)
    v = proj(off_v, attn_width)
    for hh in range(attn_width // LANES):
        for c in range(tm // tk):
            vt_ref[hh, c] = v[c * tk:(c + 1) * tk, hh * LANES:(hh + 1) * LANES].T.astype(bf16)


def _in_proj(x, g1, w_in, pool_w, pool_scale, w_pool_up, qg, kg):
    B, S, D = x.shape
    tm = min(ROW_TILE, S)
    pool_width = w_pool_up.shape[0]
    qk_width = qg.shape[1]
    attn_width = qk_width
    in_width = w_in.shape[1]
    const2 = lambda b, j: (0, 0)
    row = lambda b, j: (b, j, 0)
    tk = min(ATTN_TILE, S)
    n_heads = attn_width // LANES
    kern = functools.partial(_in_proj_kernel, tm=tm, tk=tk, d_model=D, pool_width=pool_width, qk_width=qk_width,
                             attn_width=attn_width)
    return pl.pallas_call(
        kern,
        grid=(B, S // tm),
        in_specs=[
            pl.BlockSpec((None, tm, D), row),
            pl.BlockSpec((1, D), const2),
            pl.BlockSpec((D, in_width), const2),
            pl.BlockSpec(pool_w.shape, lambda b, j: (0, 0, 0)),
            pl.BlockSpec((1, pool_width), const2),
            pl.BlockSpec((pool_width, D), const2),
            pl.BlockSpec((1, qk_width), const2),
            pl.BlockSpec((1, qk_width), const2),
        ],
        out_specs=[
            pl.BlockSpec((None, tm, D), row),
            pl.BlockSpec((None, tm, D), row),
            pl.BlockSpec((None, tm, qk_width), row),
            pl.BlockSpec((None, tm, qk_width), row),
            pl.BlockSpec((None, n_heads, tm // tk, LANES, tk), lambda b, j: (b, 0, j, 0, 0)),
        ],
        out_shape=[
            jax.ShapeDtypeStruct((B, S, D), bf16),
            jax.ShapeDtypeStruct((B, S, D), bf16),
            jax.ShapeDtypeStruct((B, S, qk_width), bf16),
            jax.ShapeDtypeStruct((B, S, qk_width), bf16),
            jax.ShapeDtypeStruct((B, n_heads, S // tk, LANES, tk), bf16),
        ],
        scratch_shapes=[pltpu.VMEM((POOL_HALO, pool_width), f32)],
        compiler_params=pltpu.CompilerParams(dimension_semantics=("arbitrary", "arbitrary"),
                                             vmem_limit_bytes=VMEM_LIMIT),
        name="in_proj",
    )(x, g1, w_in, pool_w, pool_scale, w_pool_up, qg, kg)


def _attn_kernel(slopes_ref, q_ref, k_ref, vt_ref, bias_ref, lq1_ref, lk1_ref, lq2_ref, lk2_ref, sg_ref,
                 o_ref, m_ref, l_ref, acc_ref, *, tq, lam_init):
    h = pl.program_id(1)
    i = pl.program_id(2)
    slope = slopes_ref[h]
    q = q_ref[...]
    lane = lax.broadcasted_iota(jnp.int32, q.shape, 1)
    first = lane < (LANES // 2)
    zero = jnp.zeros_like(q)
    qs = jnp.concatenate([jnp.where(first, q, zero), jnp.where(first, zero, q)], axis=0)
    bias = bias_ref[...]

    def scores(j):
        kb = k_ref[pl.ds(pl.multiple_of(j * tq, tq), tq), :]
        return _dot_nt(kb, qs)

    m_ref[...] = jnp.full(m_ref.shape, -jnp.inf, f32)
    l_ref[...] = jnp.zeros_like(l_ref)
    acc_ref[...] = jnp.zeros_like(acc_ref)

    def update(s, j, c):
        m_old = m_ref[...]
        m_new = jnp.maximum(m_old, jnp.max(s, axis=0, keepdims=True) + c)
        p = jnp.exp(s - (m_new - c))
        alpha = jnp.exp(m_old - m_new)
        l_ref[...] = alpha * l_ref[...] + jnp.sum(p, axis=0, keepdims=True)
        acc_ref[...] = alpha * acc_ref[...] + _dot(vt_ref[j], p.astype(bf16))
        m_ref[...] = m_new

    def body(j, s):
        s_next = scores(j + 1)
        update(s + bias, j, -slope * ((i - j) * tq).astype(f32))
        return s_next

    s = lax.fori_loop(0, i, body, scores(0))
    kk = lax.broadcasted_iota(jnp.int32, bias.shape, 0)
    qq = lax.broadcasted_iota(jnp.int32, bias.shape, 1)
    qq = jnp.where(qq >= tq, qq - tq, qq)
    update(s + jnp.where(kk <= qq, bias, -jnp.inf), i, 0.0)

    lam = (jnp.exp(jnp.sum(lq1_ref[...] * lk1_ref[...], keepdims=True))
           - jnp.exp(jnp.sum(lq2_ref[...] * lk2_ref[...], keepdims=True)) + lam_init)
    o_all = acc_ref[...] / l_ref[...]
    o = o_all[:, :tq] - lam * o_all[:, tq:]
    ms = jnp.mean(o * o, axis=0, keepdims=True)
    on = o * lax.rsqrt(ms + EPS) * sg_ref[...] * (1.0 - lam_init)
    o_ref[...] = on.T.astype(bf16)


def _attention(qn, kn, vt, bias, slopes, lq1, lk1, lq2, lk2, subln_col, lam_init):
    B, S, _ = qn.shape
    tq = min(ATTN_TILE, S)
    nkv = S // tq
    const2 = lambda b, h, i, *_: (0, 0)
    kern = functools.partial(_attn_kernel, tq=tq, lam_init=lam_init)
    hd = lq1.shape[1]
    return pl.pallas_call(
        kern,
        grid_spec=pltpu.PrefetchScalarGridSpec(
            num_scalar_prefetch=1,
            grid=(B, N_HEADS, S // tq),
            in_specs=[
                pl.BlockSpec((None, tq, LANES), lambda b, h, i, *_: (b, i, h)),
                pl.BlockSpec((None, S, LANES), lambda b, h, i, *_: (b, 0, h)),
                pl.BlockSpec((None, None, nkv, LANES, tq), lambda b, h, i, *_: (b, h, 0, 0, 0)),
                pl.BlockSpec((None, tq, 2 * tq), lambda b, h, i, *_: (h, 0, 0)),
                pl.BlockSpec((1, hd), const2),
                pl.BlockSpec((1, hd), const2),
                pl.BlockSpec((1, hd), const2),
                pl.BlockSpec((1, hd), const2),
                pl.BlockSpec((LANES, 1), const2),
            ],
            out_specs=pl.BlockSpec((None, tq, LANES), lambda b, h, i, *_: (b, i, h)),
            scratch_shapes=[pltpu.VMEM((1, 2 * tq), f32), pltpu.VMEM((1, 2 * tq), f32),
                            pltpu.VMEM((LANES, 2 * tq), f32)],
        ),
        out_shape=jax.ShapeDtypeStruct((B, S, N_HEADS * LANES), bf16),
        compiler_params=pltpu.CompilerParams(dimension_semantics=("arbitrary", "arbitrary", "arbitrary"),
                                             vmem_limit_bytes=VMEM_LIMIT),
        name="diff_attn",
    )(slopes, qn, kn, vt, bias, lq1, lk1, lq2, lk2, subln_col)


def _merge_kernel(pg_ref, ga_ref, o_ref, x_ref, wau_ref, wout_ref, g2_ref, wr_ref, br_ref,
                  x1_ref, hn_ref, eid_ref, wt_ref, rank_ref, cnt_ref, base_ref, *, tm, d_model):
    step = pl.program_id(0)

    @pl.when(step == 0)
    def _():
        base_ref[...] = jnp.zeros_like(base_ref)

    attn_out = _dot(o_ref[...], wau_ref[...])
    merged = pg_ref[...].astype(f32) + ga_ref[...].astype(f32) * attn_out
    x1 = x_ref[...] + _dot(merged.astype(bf16), wout_ref[...])
    x1_ref[...] = x1
    ms = jnp.mean(x1 * x1, axis=-1, keepdims=True)
    hn = x1 * lax.rsqrt(ms + EPS) * g2_ref[...]
    for s in range(d_model // LANES):
        hn_ref[pl.ds(s, tm, stride=SUBLANES), :] = hn[:, s * LANES:(s + 1) * LANES]

    logits = _dot_nt(wr_ref[...], hn.astype(bf16)) + br_ref[...]
    lg = logits[0:N_GROUPS]
    gmax = jnp.max(lg, axis=0, keepdims=True)
    p_top = 1.0 / jnp.sum(jnp.exp(lg - gmax), axis=0, keepdims=True)
    grow = lax.broadcasted_iota(jnp.int32, lg.shape, 0).astype(f32)
    g_idx = jnp.min(jnp.where(lg == gmax, grow, float(N_GROUPS)), axis=0, keepdims=True)

    sel = jnp.zeros((EXPERTS_PER_GROUP, tm), f32)
    for g in range(N_GROUPS):
        le_g = logits[SUBLANES + g * EXPERTS_PER_GROUP:SUBLANES + (g + 1) * EXPERTS_PER_GROUP]
        sel = jnp.where(g_idx == float(g), le_g, sel)
    erow = lax.broadcasted_iota(jnp.int32, sel.shape, 0).astype(f32)
    e1 = jnp.max(sel, axis=0, keepdims=True)
    i1 = jnp.min(jnp.where(sel == e1, erow, float(EXPERTS_PER_GROUP)), axis=0, keepdims=True)
    sel2 = jnp.where(erow == i1, -jnp.inf, sel)
    e2 = jnp.max(sel2, axis=0, keepdims=True)
    i2 = jnp.min(jnp.where(sel2 == e2, erow, float(EXPERTS_PER_GROUP)), axis=0, keepdims=True)
    r = jnp.exp(e2 - e1)
    w1 = p_top / (1.0 + r)
    w2 = p_top * r / (1.0 + r)
    eid1 = g_idx * float(EXPERTS_PER_GROUP) + i1
    eid2 = g_idx * float(EXPERTS_PER_GROUP) + i2
    eid_ref[...] = jnp.concatenate([eid1, eid2], axis=0).astype(jnp.int32)
    wt_ref[...] = jnp.concatenate([w1, w2], axis=0)

    xrow = lax.broadcasted_iota(jnp.int32, (N_EXPERTS, tm), 0).astype(f32)
    oh1 = jnp.where(xrow == eid1, 1.0, 0.0)
    oh2 = jnp.where(xrow == eid2, 1.0, 0.0)
    oh = oh1 + oh2
    a = lax.broadcasted_iota(jnp.int32, (tm, tm), 0)
    b = lax.broadcasted_iota(jnp.int32, (tm, tm), 1)
    upper = jnp.where(a <= b, 1.0, 0.0).astype(bf16)
    before = _dot(oh.astype(bf16), upper) + base_ref[...] - 1.0
    rank1 = jnp.sum(oh1 * before, axis=0, keepdims=True)
    rank2 = jnp.sum(oh2 * before, axis=0, keepdims=True)
    rank_ref[...] = jnp.concatenate([rank1, rank2], axis=0).astype(jnp.int32)
    base_ref[...] = base_ref[...] + jnp.sum(oh, axis=1, keepdims=True)
    cnt_ref[...] = jnp.broadcast_to(base_ref[...], cnt_ref.shape)


def _merge(pg, ga, o, x, w_attn_up, w_out, g2, wr, br):
    N, D = x.shape
    tm = min(ROW_TILE, N)
    aw = o.shape[1]
    const2 = lambda i: (0, 0)
    row = lambda i: (i, 0)
    colblk = lambda i: (0, i)
    kern = functools.partial(_merge_kernel, tm=tm, d_model=D)
    return pl.pallas_call(
        kern,
        grid=(N // tm,),
        in_specs=[
            pl.BlockSpec((tm, D), row),
            pl.BlockSpec((tm, D), row),
            pl.BlockSpec((tm, aw), row),
            pl.BlockSpec((tm, D), row),
            pl.BlockSpec((aw, D), const2),
            pl.BlockSpec((D, D), const2),
            pl.BlockSpec((1, D), const2),
            pl.BlockSpec((ROUTER_ROWS, D), const2),
            pl.BlockSpec((ROUTER_ROWS, 1), const2),
        ],
        out_specs=[
            pl.BlockSpec((tm, D), row),
            pl.BlockSpec((tm * SUBLANES, LANES), row),
            pl.BlockSpec((TOP_K, tm), colblk),
            pl.BlockSpec((TOP_K, tm), colblk),
            pl.BlockSpec((TOP_K, tm), colblk),
            pl.BlockSpec((N_EXPERTS, LANES), const2),
        ],
        out_shape=[
            jax.ShapeDtypeStruct((N, D), f32),
            jax.ShapeDtypeStruct((N * SUBLANES, LANES), f32),
            jax.ShapeDtypeStruct((TOP_K, N), jnp.int32),
            jax.ShapeDtypeStruct((TOP_K, N), f32),
            jax.ShapeDtypeStruct((TOP_K, N), jnp.int32),
            jax.ShapeDtypeStruct((N_EXPERTS, LANES), f32),
        ],
        scratch_shapes=[pltpu.VMEM((N_EXPERTS, 1), f32)],
        compiler_params=pltpu.CompilerParams(dimension_semantics=("arbitrary",), vmem_limit_bytes=VMEM_LIMIT),
        name="merge_router",
    )(pg, ga, o, x, w_attn_up, w_out, g2, wr, br)


def _row(ref, r):
    return ref.at[pl.ds(pl.multiple_of(r * SUBLANES, SUBLANES), SUBLANES), :]


def _wait_bytes_of(ref_like, any_hbm, sem):
    n = ref_like.shape[0]
    pltpu.make_async_copy(any_hbm.at[pl.ds(0, n), :], any_hbm.at[pl.ds(0, n), :], sem).wait()


STAGES = 3


def _dispatch_kernel(pstart_ref, pend_ref, eid_ref, rank_ref, hn_hbm, xs_hbm, zero_ref, stage, in_sem, out_sem,
                     zsem, *, tt, rows, n_blocks):
    i = pl.program_id(0)
    n = pl.num_programs(0)

    @pl.when(i == 0)
    def _():
        zero_ref[...] = jnp.zeros_like(zero_ref)
        for e in range(N_EXPERTS):
            @pl.when(pend_ref[e] > pstart_ref[e])
            def _():
                pltpu.make_async_copy(zero_ref, xs_hbm.at[pl.ds(pl.multiple_of((pend_ref[e] - rows) * SUBLANES,
                                                                               SUBLANES), rows * SUBLANES), :],
                                      zsem).start()
        def zero_block(b, carry):
            pltpu.make_async_copy(zero_ref, xs_hbm.at[pl.ds(pl.multiple_of(b * (rows * SUBLANES), SUBLANES),
                                                            rows * SUBLANES), :], zsem).start()
            return carry

        def wait_block(b, carry):
            pltpu.make_async_copy(zero_ref, xs_hbm.at[pl.ds(0, rows * SUBLANES), :], zsem).wait()
            return carry

        first_unused = pend_ref[N_EXPERTS - 1] // rows
        lax.fori_loop(first_unused, n_blocks, zero_block, 0)
        for e in range(N_EXPERTS):
            @pl.when(pend_ref[e] > pstart_ref[e])
            def _():
                wait_block(0, 0)
        lax.fori_loop(first_unused, n_blocks, wait_block, 0)

    def stage_copy(step):
        s = step % STAGES
        src = hn_hbm.at[pl.ds(pl.multiple_of(step * (tt * SUBLANES), SUBLANES), tt * SUBLANES), :]
        return pltpu.make_async_copy(src, stage.at[s], in_sem.at[s])

    def wait_rows_of(step):
        _wait_bytes_of(zero_ref.at[pl.ds(0, TOP_K * tt * SUBLANES), :], hn_hbm, out_sem.at[step % STAGES])

    @pl.when(i == 0)
    def _():
        stage_copy(i).start()

    @pl.when(i >= STAGES - 1)
    def _():
        wait_rows_of(i - (STAGES - 1))

    @pl.when(i + 1 < n)
    def _():
        stage_copy(i + 1).start()

    stage_copy(i).wait()
    slot = i % STAGES

    def body(t, carry):
        for k in range(TOP_K):
            dest = pstart_ref[eid_ref[0, k * tt + t]] + rank_ref[0, k * tt + t]
            pltpu.make_async_copy(_row(stage.at[slot], t), _row(xs_hbm, dest), out_sem.at[slot]).start()
        return carry
    lax.fori_loop(0, tt, body, 0, unroll=8)

    @pl.when(i == n - 1)
    def _():
        for back in range(STAGES - 2, -1, -1):
            @pl.when(i - back >= 0)
            def _():
                wait_rows_of(i - back)


def _dispatch(pstart, pend, eid_blocks, rank_blocks, hn_g, n_slots):
    n, _, width = eid_blocks.shape
    tt = width // TOP_K
    rows = EXPERT_ROWS
    assert rows >= TOP_K * tt
    kern = functools.partial(_dispatch_kernel, tt=tt, rows=rows, n_blocks=n_slots // rows)
    smem_blk = pl.BlockSpec((None, 1, width), lambda i, *_: (i, 0, 0), memory_space=pltpu.SMEM)
    return pl.pallas_call(
        kern,
        grid_spec=pltpu.PrefetchScalarGridSpec(
            num_scalar_prefetch=2,
            grid=(n,),
            in_specs=[smem_blk, smem_blk, pl.BlockSpec(memory_space=pl.ANY)],
            out_specs=pl.BlockSpec(memory_space=pl.ANY),
            scratch_shapes=[pltpu.VMEM((rows * SUBLANES, LANES), f32),
                            pltpu.VMEM((STAGES, tt * SUBLANES, LANES), f32),
                            pltpu.SemaphoreType.DMA((STAGES,)), pltpu.SemaphoreType.DMA((STAGES,)),
                            pltpu.SemaphoreType.DMA(())],
        ),
        out_shape=jax.ShapeDtypeStruct((n_slots * SUBLANES, LANES), f32),
        compiler_params=pltpu.CompilerParams(dimension_semantics=("arbitrary",), vmem_limit_bytes=VMEM_LIMIT,
                                             has_side_effects=True),
        name="dispatch",
    )(pstart, pend, eid_blocks, rank_blocks, hn_g)


def _rows_from_token_major(buf, n_rows, n_tiles):
    return jnp.concatenate([buf[pl.ds(s, n_rows, stride=SUBLANES), :] for s in range(n_tiles)], axis=-1)


def _expert_kernel(beid_ref, nbu_ref, xs_ref, wg_ref, wu_ref, wd_ref, ys_ref, wg_bf, wu_bf, wd_bf, *, rows,
                   d_model):
    b = pl.program_id(0)
    changed = jnp.logical_or(b == 0, beid_ref[b] != beid_ref[jnp.maximum(b - 1, 0)])

    @pl.when(changed)
    def _():
        wg_bf[...] = wg_ref[...].astype(bf16)
        wu_bf[...] = wu_ref[...].astype(bf16)
        wd_bf[...] = wd_ref[...].astype(bf16)

    n_tiles = d_model // LANES

    @pl.when(b < nbu_ref[0])
    def _():
        xb = _rows_from_token_major(xs_ref, rows, n_tiles).astype(bf16)
        hdn = jax.nn.silu(_dot(xb, wg_bf[...])) * _dot(xb, wu_bf[...])
        y = _dot(hdn.astype(bf16), wd_bf[...])
        for s in range(n_tiles):
            ys_ref[pl.ds(s, rows, stride=SUBLANES), :] = y[:, s * LANES:(s + 1) * LANES]

    @pl.when(b >= nbu_ref[0])
    def _():
        ys_ref[...] = jnp.zeros_like(ys_ref)


def _experts(block_eid, n_used, xs, w_gate, w_up, w_down):
    n_blocks = block_eid.shape[0]
    rows = EXPERT_ROWS
    E, D, DE = w_gate.shape
    kern = functools.partial(_expert_kernel, rows=rows, d_model=D)
    xs_idx = lambda b, eid, nbu: (jnp.minimum(b, nbu[0] - 1), 0)
    return pl.pallas_call(
        kern,
        grid_spec=pltpu.PrefetchScalarGridSpec(
            num_scalar_prefetch=2,
            grid=(n_blocks,),
            in_specs=[
                pl.BlockSpec((rows * SUBLANES, LANES), xs_idx),
                pl.BlockSpec((None, D, DE), lambda b, eid, nbu: (eid[b], 0, 0)),
                pl.BlockSpec((None, D, DE), lambda b, eid, nbu: (eid[b], 0, 0)),
                pl.BlockSpec((None, DE, D), lambda b, eid, nbu: (eid[b], 0, 0)),
            ],
            out_specs=pl.BlockSpec((rows * SUBLANES, LANES), lambda b, *_: (b, 0)),
            scratch_shapes=[
                pltpu.VMEM((D, DE), bf16),
                pltpu.VMEM((D, DE), bf16),
                pltpu.VMEM((DE, D), bf16),
            ],
        ),
        out_shape=jax.ShapeDtypeStruct((n_blocks * rows * SUBLANES, LANES), f32),
        compiler_params=pltpu.CompilerParams(dimension_semantics=("arbitrary",), vmem_limit_bytes=VMEM_LIMIT),
        name="experts",
    )(block_eid, n_used, xs, w_gate, w_up, w_down)


def _combine_kernel(pstart_ref, eid_ref, rank_ref, eidn_ref, rankn_ref, x1_ref, wt_ref, ys_hbm, out_ref,
                    buf, sem, *, te, d_model):
    i = pl.program_id(0)
    n = pl.num_programs(0)
    slot = i % 2

    def start_gather(e_ref, r_ref, dst, dsem):
        def body(r, carry):
            dest = pstart_ref[e_ref[0, r]] + r_ref[0, r]
            pltpu.make_async_copy(_row(ys_hbm, dest), _row(dst, r), dsem).start()
            return carry
        lax.fori_loop(0, TOP_K * te, body, 0, unroll=8)

    @pl.when(i == 0)
    def _():
        start_gather(eid_ref, rank_ref, buf.at[0], sem.at[0])

    @pl.when(i + 1 < n)
    def _():
        start_gather(eidn_ref, rankn_ref, buf.at[1 - slot], sem.at[1 - slot])

    _wait_bytes_of(buf.at[slot], ys_hbm, sem.at[slot])
    n_tiles = d_model // LANES
    both = _rows_from_token_major(buf.at[slot], TOP_K * te, n_tiles)
    wt = wt_ref[...]
    out_ref[...] = x1_ref[...] + (wt[:, 0:1] * both[:te] + wt[:, 1:2] * both[te:])


def _combine(pstart, eid_blocks, rank_blocks, x1, wt_cols, ys):
    N, D = x1.shape
    n, _, width = eid_blocks.shape
    te = width // TOP_K
    kern = functools.partial(_combine_kernel, te=te, d_model=D)
    cur = pl.BlockSpec((None, 1, width), lambda i, *_: (i, 0, 0), memory_space=pltpu.SMEM)
    nxt = pl.BlockSpec((None, 1, width), lambda i, *_: (jnp.minimum(i + 1, n - 1), 0, 0), memory_space=pltpu.SMEM)
    return pl.pallas_call(
        kern,
        grid_spec=pltpu.PrefetchScalarGridSpec(
            num_scalar_prefetch=1,
            grid=(n,),
            in_specs=[
                cur, cur, nxt, nxt,
                pl.BlockSpec((te, D), lambda i, *_: (i, 0)),
                pl.BlockSpec((te, TOP_K), lambda i, *_: (i, 0)),
                pl.BlockSpec(memory_space=pl.ANY),
            ],
            out_specs=pl.BlockSpec((te, D), lambda i, *_: (i, 0)),
            scratch_shapes=[pltpu.VMEM((2, TOP_K * te * SUBLANES, LANES), f32), pltpu.SemaphoreType.DMA((2,))],
        ),
        out_shape=jax.ShapeDtypeStruct((N, D), f32),
        compiler_params=pltpu.CompilerParams(dimension_semantics=("arbitrary",), vmem_limit_bytes=VMEM_LIMIT),
        name="combine",
    )(pstart, eid_blocks, rank_blocks, eid_blocks, rank_blocks, x1, wt_cols, ys)


def _token_blocks(a, tt):
    k, n = a.shape
    return a.reshape(k, n // tt, tt).transpose(1, 0, 2).reshape(n // tt, 1, k * tt)


def _layer(x, l, norm1_g, w_in, pool_w, pool_scale, w_pool_up, q_norm_g, k_norm_g, lambda_q1, lambda_k1,
           lambda_q2, lambda_k2, subln_g, w_attn_up, w_out, norm2_g, w_router_group, b_router_group,
           w_router_expert, b_router_expert, w_expert_gate, w_expert_up, w_expert_down):
    B, S, D = x.shape
    N = B * S
    head_dim = q_norm_g.shape[0]
    lam_init = 0.8 - 0.6 * math.exp(-0.3 * l)
    reps = (N_HEADS * 2 * head_dim) // head_dim

    qg = (jnp.tile(q_norm_g, reps) * (head_dim ** -0.5))[None, :]
    kg = jnp.tile(k_norm_g, reps)[None, :]
    pg, ga, qn, kn, vt = _in_proj(x, norm1_g[None, :], w_in.astype(bf16), pool_w.astype(bf16),
                                  pool_scale[None, :], w_pool_up.astype(bf16), qg, kg)

    tq = min(ATTN_TILE, S)
    slopes = jnp.asarray([2.0 ** (-8.0 * (h + 1) / N_HEADS) for h in range(N_HEADS)], f32)
    dist = (jnp.arange(tq)[None, :] - jnp.arange(tq)[:, None]).astype(f32)
    bias = -slopes[:, None, None] * jnp.concatenate([dist, dist], axis=1)[None]
    o = _attention(qn, kn, vt, bias, slopes, lambda_q1[None, :], lambda_k1[None, :], lambda_q2[None, :],
                   lambda_k2[None, :], subln_g[:, None], lam_init)

    wr = jnp.zeros((ROUTER_ROWS, D), f32)
    wr = wr.at[:N_GROUPS].set(w_router_group.T).at[SUBLANES:].set(w_router_expert.T).astype(bf16)
    br = jnp.zeros((ROUTER_ROWS, 1), f32)
    br = br.at[:N_GROUPS, 0].set(b_router_group).at[SUBLANES:, 0].set(b_router_expert)
    x1, hn_g, eid, wts, rank, cnt = _merge(pg.reshape(N, D), ga.reshape(N, D), o.reshape(N, -1),
                                           x.reshape(N, D), w_attn_up.astype(bf16), w_out.astype(bf16),
                                           norm2_g[None, :], wr, br)

    R = EXPERT_ROWS
    counts = cnt[:, 0].astype(jnp.int32)
    padded = (counts + R - 1) // R * R
    pend = jnp.cumsum(padded).astype(jnp.int32)
    pstart = pend - padded
    n_blocks = -(-(N * TOP_K) // R) + N_EXPERTS
    starts = jnp.arange(n_blocks, dtype=jnp.int32) * R
    block_eid = jnp.minimum(jnp.sum((pend[None, :] <= starts[:, None]).astype(jnp.int32), axis=1), N_EXPERTS - 1)
    n_used = pend[-1:] // R

    tt = min(MOVE_TILE, N)
    eid_blocks = _token_blocks(eid, tt)
    rank_blocks = _token_blocks(rank, tt)
    xs = _dispatch(pstart, pend, eid_blocks, rank_blocks, hn_g, n_blocks * R)
    ys = _experts(block_eid, n_used, xs, w_expert_gate, w_expert_up, w_expert_down)
    out = _combine(pstart, eid_blocks, rank_blocks, x1, wts.T, ys)
    return out.reshape(B, S, D)


def kernel(x, norm1_g, w_in, pool_w, pool_scale, w_pool_up, q_norm_g, k_norm_g, lambda_q1, lambda_k1, lambda_q2,
           lambda_k2, subln_g, w_attn_up, w_out, norm2_g, w_router_group, b_router_group, w_router_expert,
           b_router_expert, w_expert_gate, w_expert_up, w_expert_down):
    params = (norm1_g, w_in, pool_w, pool_scale, w_pool_up, q_norm_g, k_norm_g, lambda_q1, lambda_k1, lambda_q2,
              lambda_k2, subln_g, w_attn_up, w_out, norm2_g, w_router_group, b_router_group, w_router_expert,
              b_router_expert, w_expert_gate, w_expert_up, w_expert_down)
    for l in range(norm1_g.shape[0]):
        x = _layer(x, l, *(p[l] for p in params))
    return x
```

```python
import functools
import math

import jax
import jax.numpy as jnp
from jax import lax
from jax.experimental import pallas as pl
from jax.experimental.pallas import tpu as pltpu

EPS = 1e-6
POOL_WINDOWS = (2, 4, 8, 16)
POOL_HALO = 16
N_HEADS = 4
N_GROUPS = 4
EXPERTS_PER_GROUP = 8
N_EXPERTS = N_GROUPS * EXPERTS_PER_GROUP
TOP_K = 2
LANES = 128
SUBLANES = 8
ROUTER_ROWS = 8 + N_EXPERTS
VT_ROWS = LANES + 16
LOG2E = 1.4426950408889634

ROW_TILE = 512
ATTN_TILE = 256
EXPERT_ROWS = 512
MOVE_TILE = 256
VMEM_LIMIT = 52 * 1024 * 1024

f32 = jnp.float32
bf16 = jnp.bfloat16


def _dot(a, b):
    return jnp.dot(a, b, preferred_element_type=f32)


def _dot_nt(a, b):
    return lax.dot_general(a, b, (((1,), (1,)), ((), ())), preferred_element_type=f32)


def _half_lane_rmsnorm(t, n_tiles, rows):
    lane = lax.broadcasted_iota(jnp.int32, (rows, LANES), 1)
    lo_mask = lane < (LANES // 2)
    outs = []
    for i in range(n_tiles):
        c = t[:, i * LANES:(i + 1) * LANES]
        sq = c * c
        lo = jnp.sum(jnp.where(lo_mask, sq, 0.0), axis=-1, keepdims=True)
        hi = jnp.sum(jnp.where(lo_mask, 0.0, sq), axis=-1, keepdims=True)
        ms = jnp.where(lo_mask, lo, hi) * (2.0 / LANES)
        outs.append(c * lax.rsqrt(ms + EPS))
    return jnp.concatenate(outs, axis=-1)


def _in_proj_kernel(x_ref, g1_ref, win_ref, poolw_ref, pscale_ref, wpu_ref, qg_ref, kg_ref,
                    pg_ref, ga_ref, qn_ref, kn_ref, vt_ref, prev_ref, *, tm, tk, d_model, pool_width, qk_width,
                    attn_width):
    j = pl.program_id(1)
    x = x_ref[...]
    ms = jnp.mean(x * x, axis=-1, keepdims=True)
    h = (x * lax.rsqrt(ms + EPS) * g1_ref[...]).astype(bf16)

    def proj(lo, width):
        return _dot(h, win_ref[:, lo:lo + width])

    off_q = pool_width
    off_k = off_q + qk_width
    off_v = off_k + qk_width
    off_gp = off_v + attn_width
    off_ga = off_gp + d_model

    u = proj(0, pool_width)

    @pl.when(j == 0)
    def _():
        prev_ref[...] = jnp.zeros_like(prev_ref)

    ext = jnp.concatenate([prev_ref[...], u], axis=0)
    prev_ref[...] = u[tm - POOL_HALO:, :]
    pos = j * tm + lax.broadcasted_iota(jnp.int32, (tm, 1), 0)
    group = pool_width // len(POOL_WINDOWS)
    ys = []
    for g, w in enumerate(POOL_WINDOWS):
        acc = ext[:, g * group:(g + 1) * group]
        shift = 1
        while shift < w:
            acc = acc + pltpu.roll(acc, shift, 0)
            shift *= 2
        wsum = acc[POOL_HALO:, :]
        cnt = jnp.minimum(pos + 1, w).astype(f32)
        d = wsum / cnt - u[:, g * group:(g + 1) * group]
        ys.append(_dot(d.astype(bf16), poolw_ref[g]))
    y = jnp.concatenate(ys, axis=-1) * pscale_ref[...]
    pool_out = _dot(y.astype(bf16), wpu_ref[...])

    half = d_model // 2
    for c in range(2):
        gp = jax.nn.sigmoid(proj(off_gp + c * half, half))
        pg_ref[:, c * half:(c + 1) * half] = (gp * pool_out[:, c * half:(c + 1) * half]).astype(bf16)
        ga_ref[:, c * half:(c + 1) * half] = jax.nn.sigmoid(proj(off_ga + c * half, half)).astype(bf16)

    n_tiles = qk_width // LANES
    qn_ref[...] = (_half_lane_rmsnorm(proj(off_q, qk_width), n_tiles, tm) * qg_ref[...]).astype(bf16)
    kn_ref[...] = (_half_lane_rmsnorm(proj(off_k, qk_width), n_tiles, tm) * kg_ref[...]).astype(bf16)
    v = proj(off_v, attn_width)
    extra = lax.broadcasted_iota(jnp.int32, (VT_ROWS - LANES, tk), 0)
    ones_rows = jnp.where(extra == 0, 1.0, 0.0).astype(bf16)
    for hh in range(attn_width // LANES):
        for c in range(tm // tk):
            vt_ref[hh, c, 0:LANES, :] = v[c * tk:(c + 1) * tk, hh * LANES:(hh + 1) * LANES].T.astype(bf16)
            vt_ref[hh, c, LANES:VT_ROWS, :] = ones_rows


def _in_proj(x, g1, w_in, pool_w, pool_scale, w_pool_up, qg, kg):
    B, S, D = x.shape
    tm = min(ROW_TILE, S)
    pool_width = w_pool_up.shape[0]
    qk_width = qg.shape[1]
    attn_width = qk_width
    in_width = w_in.shape[1]
    const2 = lambda b, j: (0, 0)
    row = lambda b, j: (b, j, 0)
    tk = min(ATTN_TILE, S)
    n_heads = attn_width // LANES
    kern = functools.partial(_in_proj_kernel, tm=tm, tk=tk, d_model=D, pool_width=pool_width, qk_width=qk_width,
                             attn_width=attn_width)
    return pl.pallas_call(
        kern,
        grid=(B, S // tm),
        in_specs=[
            pl.BlockSpec((None, tm, D), row),
            pl.BlockSpec((1, D), const2),
            pl.BlockSpec((D, in_width), const2),
            pl.BlockSpec(pool_w.shape, lambda b, j: (0, 0, 0)),
            pl.BlockSpec((1, pool_width), const2),
            pl.BlockSpec((pool_width, D), const2),
            pl.BlockSpec((1, qk_width), const2),
            pl.BlockSpec((1, qk_width), const2),
        ],
        out_specs=[
            pl.BlockSpec((None, tm, D), row),
            pl.BlockSpec((None, tm, D), row),
            pl.BlockSpec((None, tm, qk_width), row),
            pl.BlockSpec((None, tm, qk_width), row),
            pl.BlockSpec((None, n_heads, tm // tk, VT_ROWS, tk), lambda b, j: (b, 0, j, 0, 0)),
        ],
        out_shape=[
            jax.ShapeDtypeStruct((B, S, D), bf16),
            jax.ShapeDtypeStruct((B, S, D), bf16),
            jax.ShapeDtypeStruct((B, S, qk_width), bf16),
            jax.ShapeDtypeStruct((B, S, qk_width), bf16),
            jax.ShapeDtypeStruct((B, n_heads, S // tk, VT_ROWS, tk), bf16),
        ],
        scratch_shapes=[pltpu.VMEM((POOL_HALO, pool_width), f32)],
        compiler_params=pltpu.CompilerParams(dimension_semantics=("arbitrary", "arbitrary"),
                                             vmem_limit_bytes=VMEM_LIMIT),
        name="in_proj",
    )(x, g1, w_in, pool_w, pool_scale, w_pool_up, qg, kg)


def _attn_kernel(slopes_ref, q_ref, k_ref, vt_ref, fq_ref, fk_ref, lq1_ref, lk1_ref, lq2_ref, lk2_ref, sg_ref,
                 o_ref, sa_ref, sb_ref, pa_ref, pb_ref, m_ref, alpha_ref, acc_ref, *, tq, tk, lam_init):
    h = pl.program_id(1)
    i = pl.program_id(2)
    slope2 = slopes_ref[h]
    q = q_ref[...]
    lane = lax.broadcasted_iota(jnp.int32, q.shape, 1)
    first = lane < (LANES // 2)
    zero = jnp.zeros_like(q)
    qs = jnp.concatenate([jnp.where(first, q, zero), jnp.where(first, zero, q)], axis=0)
    qa = jnp.concatenate([qs, fq_ref[...]], axis=1)
    fk = fk_ref[...]

    def scores(n, s_out):
        kb = k_ref[pl.ds(pl.multiple_of(n * tk, tk), tk), :]
        s_out[...] = _dot_nt(jnp.concatenate([kb, fk], axis=1), qa)

    def softmax(n, s_in, p_out, key_offset=None):
        c = -slope2 * (i * tq - n * tk).astype(f32)
        s = s_in[...]
        if key_offset is not None:
            kk = lax.broadcasted_iota(jnp.int32, s.shape, 0) + key_offset
            qq = lax.broadcasted_iota(jnp.int32, s.shape, 1)
            qq = jnp.where(qq >= tq, qq - tq, qq)
            s = jnp.where(kk <= qq, s, -jnp.inf)
        m_old = m_ref[...]
        m_new = jnp.maximum(m_old, jnp.max(s, axis=0, keepdims=True) + c)
        p_out[...] = jnp.exp2(s - (m_new - c)).astype(bf16)
        m_ref[...] = m_new
        return jnp.exp2(m_old - m_new)

    def pv(n, p_in, alpha):
        acc_ref[...] = alpha * acc_ref[...] + _dot(vt_ref[jnp.maximum(n, 0)], p_in[...])

    m_ref[...] = jnp.full(m_ref.shape, -jnp.inf, f32)
    alpha_ref[...] = jnp.ones_like(alpha_ref)
    acc_ref[...] = jnp.zeros_like(acc_ref)
    pb_ref[...] = jnp.zeros_like(pb_ref)
    scores(0, sa_ref)

    def pair(t, carry):
        n = 2 * t
        scores(n + 1, sb_ref)
        a_even = softmax(n, sa_ref, pa_ref)
        pv(n - 1, pb_ref, alpha_ref[...])
        scores(n + 2, sa_ref)
        a_odd = softmax(n + 1, sb_ref, pb_ref)
        pv(n, pa_ref, a_even)
        alpha_ref[...] = a_odd
        return carry

    lax.fori_loop(0, i, pair, 0)
    n = 2 * i
    scores(n + 1, sb_ref)
    a_even = softmax(n, sa_ref, pa_ref, key_offset=0)
    pv(n - 1, pb_ref, alpha_ref[...])
    a_odd = softmax(n + 1, sb_ref, pb_ref, key_offset=tk)
    pv(n, pa_ref, a_even)
    pv(n + 1, pb_ref, a_odd)

    lam =(jnp.exp(jnp.sum(lq1_ref[...] * lk1_ref[...], keepdims=True))
           - jnp.exp(jnp.sum(lq2_ref[...] * lk2_ref[...], keepdims=True)) + lam_init)
    acc = acc_ref[...]
    o_all = acc[0:LANES] / acc[LANES:LANES + 1]
    o = o_all[:, :tq] - lam * o_all[:, tq:]
    ms = jnp.mean(o * o, axis=0, keepdims=True)
    on = o * lax.rsqrt(ms + EPS) * sg_ref[...] * (1.0 - lam_init)
    o_ref[...] = on.T.astype(bf16)


def _split_bf16(x, pieces=3):
    out = []
    for _ in range(pieces):
        p = x.astype(bf16)
        out.append(p)
        x = x - p.astype(f32)
    return out


def _alibi_features(slopes2, tq, tk):
    assert tk <= 256
    ones = jnp.ones((tk,), bf16)
    krel = jnp.arange(tk, dtype=f32).astype(bf16)
    fk = jnp.zeros((tk, LANES), bf16)
    for c in range(3):
        fk = fk.at[:, c].set(krel).at[:, 3 + c].set(ones)
    qrel = jnp.arange(tq, dtype=f32)
    fq = jnp.zeros((slopes2.shape[0], 2 * tq, LANES), bf16)
    a = _split_bf16(slopes2)
    b = _split_bf16(-slopes2[:, None] * jnp.concatenate([qrel, qrel])[None, :])
    for c in range(3):
        fq = fq.at[:, :, c].set(a[c][:, None]).at[:, :, 3 + c].set(b[c])
    return fq, fk


def _attention(qn, kn, vt, fq, fk, slopes, lq1, lk1, lq2, lk2, subln_col, lam_init):
    B, S, _ = qn.shape
    tk = fk.shape[0]
    tq = fq.shape[1] // 2
    assert tq == 2 * tk
    nkv = S // tk
    const2 = lambda b, h, i, *_: (0, 0)
    kern = functools.partial(_attn_kernel, tq=tq, tk=tk, lam_init=lam_init)
    hd = lq1.shape[1]
    return pl.pallas_call(
        kern,
        grid_spec=pltpu.PrefetchScalarGridSpec(
            num_scalar_prefetch=1,
            grid=(B, N_HEADS, S // tq),
            in_specs=[
                pl.BlockSpec((None, tq, LANES), lambda b, h, i, *_: (b, i, h)),
                pl.BlockSpec((None, S, LANES), lambda b, h, i, *_: (b, 0, h)),
                pl.BlockSpec((None, None, nkv, VT_ROWS, tk), lambda b, h, i, *_: (b, h, 0, 0, 0)),
                pl.BlockSpec((None, 2 * tq, LANES), lambda b, h, i, *_: (h, 0, 0)),
                pl.BlockSpec((tk, LANES), const2),
                pl.BlockSpec((1, hd), const2),
                pl.BlockSpec((1, hd), const2),
                pl.BlockSpec((1, hd), const2),
                pl.BlockSpec((1, hd), const2),
                pl.BlockSpec((LANES, 1), const2),
            ],
            out_specs=pl.BlockSpec((None, tq, LANES), lambda b, h, i, *_: (b, i, h)),
            scratch_shapes=[pltpu.VMEM((tk, 2 * tq), f32), pltpu.VMEM((tk, 2 * tq), f32),
                            pltpu.VMEM((tk, 2 * tq), bf16), pltpu.VMEM((tk, 2 * tq), bf16),
                            pltpu.VMEM((1, 2 * tq), f32), pltpu.VMEM((1, 2 * tq), f32),
                            pltpu.VMEM((VT_ROWS, 2 * tq), f32)],
        ),
        out_shape=jax.ShapeDtypeStruct((B, S, N_HEADS * LANES), bf16),
        compiler_params=pltpu.CompilerParams(dimension_semantics=("arbitrary", "arbitrary", "arbitrary"),
                                             vmem_limit_bytes=VMEM_LIMIT),
        name="diff_attn",
    )(slopes, qn, kn, vt, fq, fk, lq1, lk1, lq2, lk2, subln_col)


def _merge_kernel(pg_ref, ga_ref, o_ref, x_ref, wau_ref, wout_ref, g2_ref, wr_ref, br_ref,
                  x1_ref, hn_ref, eid_ref, wt_ref, rank_ref, cnt_ref, base_ref, *, tm, d_model):
    step = pl.program_id(0)

    @pl.when(step == 0)
    def _():
        base_ref[...] = jnp.zeros_like(base_ref)

    attn_out = _dot(o_ref[...], wau_ref[...])
    merged = pg_ref[...].astype(f32) + ga_ref[...].astype(f32) * attn_out
    x1 = x_ref[...] + _dot(merged.astype(bf16), wout_ref[...])
    x1_ref[...] = x1
    ms = jnp.mean(x1 * x1, axis=-1, keepdims=True)
    hn = x1 * lax.rsqrt(ms + EPS) * g2_ref[...]
    for s in range(d_model // LANES):
        hn_ref[pl.ds(s, tm, stride=SUBLANES), :] = hn[:, s * LANES:(s + 1) * LANES]

    logits = _dot_nt(wr_ref[...], hn.astype(bf16)) + br_ref[...]
    lg = logits[0:N_GROUPS]
    gmax = jnp.max(lg, axis=0, keepdims=True)
    p_top = 1.0 / jnp.sum(jnp.exp(lg - gmax), axis=0, keepdims=True)
    grow = lax.broadcasted_iota(jnp.int32, lg.shape, 0).astype(f32)
    g_idx = jnp.min(jnp.where(lg == gmax, grow, float(N_GROUPS)), axis=0, keepdims=True)

    sel = jnp.zeros((EXPERTS_PER_GROUP, tm), f32)
    for g in range(N_GROUPS):
        le_g = logits[SUBLANES + g * EXPERTS_PER_GROUP:SUBLANES + (g + 1) * EXPERTS_PER_GROUP]
        sel = jnp.where(g_idx == float(g), le_g, sel)
    erow = lax.broadcasted_iota(jnp.int32, sel.shape, 0).astype(f32)
    e1 = jnp.max(sel, axis=0, keepdims=True)
    i1 = jnp.min(jnp.where(sel == e1, erow, float(EXPERTS_PER_GROUP)), axis=0, keepdims=True)
    sel2 = jnp.where(erow == i1, -jnp.inf, sel)
    e2 = jnp.max(sel2, axis=0, keepdims=True)
    i2 = jnp.min(jnp.where(sel2 == e2, erow, float(EXPERTS_PER_GROUP)), axis=0, keepdims=True)
    r = jnp.exp(e2 - e1)
    w1 = p_top / (1.0 + r)
    w2 = p_top * r / (1.0 + r)
    eid1 = g_idx * float(EXPERTS_PER_GROUP) + i1
    eid2 = g_idx * float(EXPERTS_PER_GROUP) + i2
    eid_ref[...] = jnp.concatenate([eid1, eid2], axis=0).astype(jnp.int32)
    wt_ref[...] = jnp.concatenate([w1, w2], axis=0)

    xrow = lax.broadcasted_iota(jnp.int32, (N_EXPERTS, tm), 0).astype(f32)
    oh1 = jnp.where(xrow == eid1, 1.0, 0.0)
    oh2 = jnp.where(xrow == eid2, 1.0, 0.0)
    oh = oh1 + oh2
    a = lax.broadcasted_iota(jnp.int32, (tm, tm), 0)
    b = lax.broadcasted_iota(jnp.int32, (tm, tm), 1)
    upper = jnp.where(a <= b, 1.0, 0.0).astype(bf16)
    before = _dot(oh.astype(bf16), upper) + base_ref[...] - 1.0
    rank1 = jnp.sum(oh1 * before, axis=0, keepdims=True)
    rank2 = jnp.sum(oh2 * before, axis=0, keepdims=True)
    rank_ref[...] = jnp.concatenate([rank1, rank2], axis=0).astype(jnp.int32)
    base_ref[...] = base_ref[...] + jnp.sum(oh, axis=1, keepdims=True)
    cnt_ref[...] = jnp.broadcast_to(base_ref[...], cnt_ref.shape)


def _merge(pg, ga, o, x, w_attn_up, w_out, g2, wr, br):
    N, D = x.shape
    tm = min(ROW_TILE, N)
    aw = o.shape[1]
    const2 = lambda i: (0, 0)
    row = lambda i: (i, 0)
    colblk = lambda i: (0, i)
    kern = functools.partial(_merge_kernel, tm=tm, d_model=D)
    return pl.pallas_call(
        kern,
        grid=(N // tm,),
        in_specs=[
            pl.BlockSpec((tm, D), row),
            pl.BlockSpec((tm, D), row),
            pl.BlockSpec((tm, aw), row),
            pl.BlockSpec((tm, D), row),
            pl.BlockSpec((aw, D), const2),
            pl.BlockSpec((D, D), const2),
            pl.BlockSpec((1, D), const2),
            pl.BlockSpec((ROUTER_ROWS, D), const2),
            pl.BlockSpec((ROUTER_ROWS, 1), const2),
        ],
        out_specs=[
            pl.BlockSpec((tm, D), row),
            pl.BlockSpec((tm * SUBLANES, LANES), row),
            pl.BlockSpec((TOP_K, tm), colblk),
            pl.BlockSpec((TOP_K, tm), colblk),
            pl.BlockSpec((TOP_K, tm), colblk),
            pl.BlockSpec((N_EXPERTS, LANES), const2),
        ],
        out_shape=[
            jax.ShapeDtypeStruct((N, D), f32),
            jax.ShapeDtypeStruct((N * SUBLANES, LANES), f32),
            jax.ShapeDtypeStruct((TOP_K, N), jnp.int32),
            jax.ShapeDtypeStruct((TOP_K, N), f32),
            jax.ShapeDtypeStruct((TOP_K, N), jnp.int32),
            jax.ShapeDtypeStruct((N_EXPERTS, LANES), f32),
        ],
        scratch_shapes=[pltpu.VMEM((N_EXPERTS, 1), f32)],
        compiler_params=pltpu.CompilerParams(dimension_semantics=("arbitrary",), vmem_limit_bytes=VMEM_LIMIT),
        name="merge_router",
    )(pg, ga, o, x, w_attn_up, w_out, g2, wr, br)


def _row(ref, r):
    return ref.at[pl.ds(pl.multiple_of(r * SUBLANES, SUBLANES), SUBLANES), :]


def _wait_bytes_of(ref_like, any_hbm, sem):
    n = ref_like.shape[0]
    pltpu.make_async_copy(any_hbm.at[pl.ds(0, n), :], any_hbm.at[pl.ds(0, n), :], sem).wait()


STAGES = 3


def _dispatch_kernel(pstart_ref, pend_ref, eid_ref, rank_ref, hn_hbm, xs_hbm, zero_ref, stage, in_sem, out_sem,
                     zsem, *, tt, rows, n_blocks):
    i = pl.program_id(0)
    n = pl.num_programs(0)

    @pl.when(i == 0)
    def _():
        zero_ref[...] = jnp.zeros_like(zero_ref)
        for e in range(N_EXPERTS):
            @pl.when(pend_ref[e] > pstart_ref[e])
            def _():
                pltpu.make_async_copy(zero_ref, xs_hbm.at[pl.ds(pl.multiple_of((pend_ref[e] - rows) * SUBLANES,
                                                                               SUBLANES), rows * SUBLANES), :],
                                      zsem).start()
        def zero_block(b, carry):
            pltpu.make_async_copy(zero_ref, xs_hbm.at[pl.ds(pl.multiple_of(b * (rows * SUBLANES), SUBLANES),
                                                            rows * SUBLANES), :], zsem).start()
            return carry

        def wait_block(b, carry):
            pltpu.make_async_copy(zero_ref, xs_hbm.at[pl.ds(0, rows * SUBLANES), :], zsem).wait()
            return carry

        first_unused = pend_ref[N_EXPERTS - 1] // rows
        lax.fori_loop(first_unused, n_blocks, zero_block, 0)
        for e in range(N_EXPERTS):
            @pl.when(pend_ref[e] > pstart_ref[e])
            def _():
                wait_block(0, 0)
        lax.fori_loop(first_unused, n_blocks, wait_block, 0)

    def stage_copy(step):
        s = step % STAGES
        src = hn_hbm.at[pl.ds(pl.multiple_of(step * (tt * SUBLANES), SUBLANES), tt * SUBLANES), :]
        return pltpu.make_async_copy(src, stage.at[s], in_sem.at[s])

    def wait_rows_of(step):
        _wait_bytes_of(zero_ref.at[pl.ds(0, TOP_K * tt * SUBLANES), :], hn_hbm, out_sem.at[step % STAGES])

    @pl.when(i == 0)
    def _():
        stage_copy(i).start()

    @pl.when(i >= STAGES - 1)
    def _():
        wait_rows_of(i - (STAGES - 1))

    @pl.when(i + 1 < n)
    def _():
        stage_copy(i + 1).start()

    stage_copy(i).wait()
    slot = i % STAGES

    def body(t, carry):
        for k in range(TOP_K):
            dest = pstart_ref[eid_ref[0, k * tt + t]] + rank_ref[0, k * tt + t]
            pltpu.make_async_copy(_row(stage.at[slot], t), _row(xs_hbm, dest), out_sem.at[slot]).start()
        return carry
    lax.fori_loop(0, tt, body, 0, unroll=8)

    @pl.when(i == n - 1)
    def _():
        for back in range(STAGES - 2, -1, -1):
            @pl.when(i - back >= 0)
            def _():
                wait_rows_of(i - back)


def _dispatch(pstart, pend, eid_blocks, rank_blocks, hn_g, n_slots):
    n, _, width = eid_blocks.shape
    tt = width // TOP_K
    rows = EXPERT_ROWS
    assert rows >= TOP_K * tt
    kern = functools.partial(_dispatch_kernel, tt=tt, rows=rows, n_blocks=n_slots // rows)
    smem_blk = pl.BlockSpec((None, 1, width), lambda i, *_: (i, 0, 0), memory_space=pltpu.SMEM)
    return pl.pallas_call(
        kern,
        grid_spec=pltpu.PrefetchScalarGridSpec(
            num_scalar_prefetch=2,
            grid=(n,),
            in_specs=[smem_blk, smem_blk, pl.BlockSpec(memory_space=pl.ANY)],
            out_specs=pl.BlockSpec(memory_space=pl.ANY),
            scratch_shapes=[pltpu.VMEM((rows * SUBLANES, LANES), f32),
                            pltpu.VMEM((STAGES, tt * SUBLANES, LANES), f32),
                            pltpu.SemaphoreType.DMA((STAGES,)), pltpu.SemaphoreType.DMA((STAGES,)),
                            pltpu.SemaphoreType.DMA(())],
        ),
        out_shape=jax.ShapeDtypeStruct((n_slots * SUBLANES, LANES), f32),
        compiler_params=pltpu.CompilerParams(dimension_semantics=("arbitrary",), vmem_limit_bytes=VMEM_LIMIT,
                                             has_side_effects=True),
        name="dispatch",
    )(pstart, pend, eid_blocks, rank_blocks, hn_g)


def _rows_from_token_major(buf, n_rows, n_tiles):
    return jnp.concatenate([buf[pl.ds(s, n_rows, stride=SUBLANES), :] for s in range(n_tiles)], axis=-1)


def _expert_kernel(beid_ref, nbu_ref, xs_ref, wg_ref, wu_ref, wd_ref, ys_ref, wg_bf, wu_bf, wd_bf, *, rows,
                   d_model):
    b = pl.program_id(0)
    changed = jnp.logical_or(b == 0, beid_ref[b] != beid_ref[jnp.maximum(b - 1, 0)])

    @pl.when(changed)
    def _():
        wg_bf[...] = wg_ref[...].astype(bf16)
        wu_bf[...] = wu_ref[...].astype(bf16)
        wd_bf[...] = wd_ref[...].astype(bf16)

    n_tiles = d_model // LANES

    @pl.when(b < nbu_ref[0])
    def _():
        xb = _rows_from_token_major(xs_ref, rows, n_tiles).astype(bf16)
        hdn = jax.nn.silu(_dot(xb, wg_bf[...])) * _dot(xb, wu_bf[...])
        y = _dot(hdn.astype(bf16), wd_bf[...])
        for s in range(n_tiles):
            ys_ref[pl.ds(s, rows, stride=SUBLANES), :] = y[:, s * LANES:(s + 1) * LANES]

    @pl.when(b >= nbu_ref[0])
    def _():
        ys_ref[...] = jnp.zeros_like(ys_ref)


def _experts(block_eid, n_used, xs, w_gate, w_up, w_down):
    n_blocks = block_eid.shape[0]
    rows = EXPERT_ROWS
    E, D, DE = w_gate.shape
    kern = functools.partial(_expert_kernel, rows=rows, d_model=D)
    xs_idx = lambda b, eid, nbu: (jnp.minimum(b, nbu[0] - 1), 0)
    return pl.pallas_call(
        kern,
        grid_spec=pltpu.PrefetchScalarGridSpec(
            num_scalar_prefetch=2,
            grid=(n_blocks,),
            in_specs=[
                pl.BlockSpec((rows * SUBLANES, LANES), xs_idx),
                pl.BlockSpec((None, D, DE), lambda b, eid, nbu: (eid[b], 0, 0)),
                pl.BlockSpec((None, D, DE), lambda b, eid, nbu: (eid[b], 0, 0)),
                pl.BlockSpec((None, DE, D), lambda b, eid, nbu: (eid[b], 0, 0)),
            ],
            out_specs=pl.BlockSpec((rows * SUBLANES, LANES), lambda b, *_: (b, 0)),
            scratch_shapes=[
                pltpu.VMEM((D, DE), bf16),
                pltpu.VMEM((D, DE), bf16),
                pltpu.VMEM((DE, D), bf16),
            ],
        ),
        out_shape=jax.ShapeDtypeStruct((n_blocks * rows * SUBLANES, LANES), f32),
        compiler_params=pltpu.CompilerParams(dimension_semantics=("arbitrary",), vmem_limit_bytes=VMEM_LIMIT),
        name="experts",
    )(block_eid, n_used, xs, w_gate, w_up, w_down)


def _combine_kernel(pstart_ref, eid_ref, rank_ref, eidn_ref, rankn_ref, x1_ref, wt_ref, ys_hbm, out_ref,
                    buf, sem, *, te, d_model):
    i = pl.program_id(0)
    n = pl.num_programs(0)
    slot = i % 2

    def start_gather(e_ref, r_ref, dst, dsem):
        def body(r, carry):
            dest = pstart_ref[e_ref[0, r]] + r_ref[0, r]
            pltpu.make_async_copy(_row(ys_hbm, dest), _row(dst, r), dsem).start()
            return carry
        lax.fori_loop(0, TOP_K * te, body, 0, unroll=8)

    @pl.when(i == 0)
    def _():
        start_gather(eid_ref, rank_ref, buf.at[0], sem.at[0])

    @pl.when(i + 1 < n)
    def _():
        start_gather(eidn_ref, rankn_ref, buf.at[1 - slot], sem.at[1 - slot])

    _wait_bytes_of(buf.at[slot], ys_hbm, sem.at[slot])
    n_tiles = d_model // LANES
    both = _rows_from_token_major(buf.at[slot], TOP_K * te, n_tiles)
    wt = wt_ref[...]
    out_ref[...] = x1_ref[...] + (wt[:, 0:1] * both[:te] + wt[:, 1:2] * both[te:])


def _combine(pstart, eid_blocks, rank_blocks, x1, wt_cols, ys):
    N, D = x1.shape
    n, _, width = eid_blocks.shape
    te = width // TOP_K
    kern = functools.partial(_combine_kernel, te=te, d_model=D)
    cur = pl.BlockSpec((None, 1, width), lambda i, *_: (i, 0, 0), memory_space=pltpu.SMEM)
    nxt = pl.BlockSpec((None, 1, width), lambda i, *_: (jnp.minimum(i + 1, n - 1), 0, 0), memory_space=pltpu.SMEM)
    return pl.pallas_call(
        kern,
        grid_spec=pltpu.PrefetchScalarGridSpec(
            num_scalar_prefetch=1,
            grid=(n,),
            in_specs=[
                cur, cur, nxt, nxt,
                pl.BlockSpec((te, D), lambda i, *_: (i, 0)),
                pl.BlockSpec((te, TOP_K), lambda i, *_: (i, 0)),
                pl.BlockSpec(memory_space=pl.ANY),
            ],
            out_specs=pl.BlockSpec((te, D), lambda i, *_: (i, 0)),
            scratch_shapes=[pltpu.VMEM((2, TOP_K * te * SUBLANES, LANES), f32), pltpu.SemaphoreType.DMA((2,))],
        ),
        out_shape=jax.ShapeDtypeStruct((N, D), f32),
        compiler_params=pltpu.CompilerParams(dimension_semantics=("arbitrary",), vmem_limit_bytes=VMEM_LIMIT),
        name="combine",
    )(pstart, eid_blocks, rank_blocks, eid_blocks, rank_blocks, x1, wt_cols, ys)


def _token_blocks(a, tt):
    k, n = a.shape
    return a.reshape(k, n // tt, tt).transpose(1, 0, 2).reshape(n // tt, 1, k * tt)


def _layer(x, l, norm1_g, w_in, pool_w, pool_scale, w_pool_up, q_norm_g, k_norm_g, lambda_q1, lambda_k1,
           lambda_q2, lambda_k2, subln_g, w_attn_up, w_out, norm2_g, w_router_group, b_router_group,
           w_router_expert, b_router_expert, w_expert_gate, w_expert_up, w_expert_down):
    B, S, D = x.shape
    N = B * S
    head_dim = q_norm_g.shape[0]
    lam_init = 0.8 - 0.6 * math.exp(-0.3 * l)
    reps = (N_HEADS * 2 * head_dim) // head_dim

    qg = (jnp.tile(q_norm_g, reps) * (head_dim ** -0.5 * LOG2E))[None, :]
    kg = jnp.tile(k_norm_g, reps)[None, :]
    pg, ga, qn, kn, vt = _in_proj(x, norm1_g[None, :], w_in.astype(bf16), pool_w.astype(bf16),
                                  pool_scale[None, :], w_pool_up.astype(bf16), qg, kg)

    tk = min(ATTN_TILE, S)
    slopes2 = jnp.asarray([2.0 ** (-8.0 * (h + 1) / N_HEADS) * LOG2E for h in range(N_HEADS)], f32)
    fq, fk = _alibi_features(slopes2, 2 * tk, tk)
    o = _attention(qn, kn, vt, fq, fk, slopes2, lambda_q1[None, :], lambda_k1[None, :], lambda_q2[None, :],
                   lambda_k2[None, :], subln_g[:, None], lam_init)

    wr = jnp.zeros((ROUTER_ROWS, D), f32)
    wr = wr.at[:N_GROUPS].set(w_router_group.T).at[SUBLANES:].set(w_router_expert.T).astype(bf16)
    br = jnp.zeros((ROUTER_ROWS, 1), f32)
    br = br.at[:N_GROUPS, 0].set(b_router_group).at[SUBLANES:, 0].set(b_router_expert)
    x1, hn_g, eid, wts, rank, cnt = _merge(pg.reshape(N, D), ga.reshape(N, D), o.reshape(N, -1),
                                           x.reshape(N, D), w_attn_up.astype(bf16), w_out.astype(bf16),
                                           norm2_g[None, :], wr, br)

    R = EXPERT_ROWS
    counts = cnt[:, 0].astype(jnp.int32)
    padded = (counts + R - 1) // R * R
    pend = jnp.cumsum(padded).astype(jnp.int32)
    pstart = pend - padded
    n_blocks = -(-(N * TOP_K) // R) + N_EXPERTS
    starts = jnp.arange(n_blocks, dtype=jnp.int32) * R
    block_eid = jnp.minimum(jnp.sum((pend[None, :] <= starts[:, None]).astype(jnp.int32), axis=1), N_EXPERTS - 1)
    n_used = pend[-1:] // R

    tt = min(MOVE_TILE, N)
    eid_blocks = _token_blocks(eid, tt)
    rank_blocks = _token_blocks(rank, tt)
    xs = _dispatch(pstart, pend, eid_blocks, rank_blocks, hn_g, n_blocks * R)
    ys = _experts(block_eid, n_used, xs, w_expert_gate, w_expert_up, w_expert_down)
    out = _combine(pstart, eid_blocks, rank_blocks, x1, wts.T, ys)
    return out.reshape(B, S, D)


def kernel(x, norm1_g, w_in, pool_w, pool_scale, w_pool_up, q_norm_g, k_norm_g, lambda_q1, lambda_k1, lambda_q2,
           lambda_k2, subln_g, w_attn_up, w_out, norm2_g, w_router_group, b_router_group, w_router_expert,
           b_router_expert, w_expert_gate, w_expert_up, w_expert_down):
    params = (norm1_g, w_in, pool_w, pool_scale, w_pool_up, q_norm_g, k_norm_g, lambda_q1, lambda_k1, lambda_q2,
              lambda_k2, subln_g, w_attn_up, w_out, norm2_g, w_router_group, b_router_group, w_router_expert,
              b_router_expert, w_expert_gate, w_expert_up, w_expert_down)
    for l in range(norm1_g.shape[0]):
        x = _layer(x, l, *(p[l] for p in params))
    return x
```

```python
import functools
import math

import jax
import jax.numpy as jnp
from jax import lax
from jax.experimental import pallas as pl
from jax.experimental.pallas import tpu as pltpu

EPS = 1e-6
POOL_WINDOWS = (2, 4, 8, 16)
POOL_HALO = 16
N_HEADS = 4
N_GROUPS = 4
EXPERTS_PER_GROUP = 8
N_EXPERTS = N_GROUPS * EXPERTS_PER_GROUP
TOP_K = 2
LANES = 128
SUBLANES = 8
ROUTER_ROWS = 8 + N_EXPERTS
VT_ROWS = LANES + 16
LOG2E = 1.4426950408889634

ROW_TILE = 512
ATTN_TILE = 256
EXPERT_ROWS = 512
MOVE_TILE = 256
VMEM_LIMIT = 52 * 1024 * 1024

f32 = jnp.float32
bf16 = jnp.bfloat16


def _dot(a, b):
    return jnp.dot(a, b, preferred_element_type=f32)


def _dot_nt(a, b):
    return lax.dot_general(a, b, (((1,), (1,)), ((), ())), preferred_element_type=f32)


def _half_lane_rmsnorm(t, n_tiles, rows):
    lane = lax.broadcasted_iota(jnp.int32, (rows, LANES), 1)
    lo_mask = lane < (LANES // 2)
    outs = []
    for i in range(n_tiles):
        c = t[:, i * LANES:(i + 1) * LANES]
        sq = c * c
        lo = jnp.sum(jnp.where(lo_mask, sq, 0.0), axis=-1, keepdims=True)
        hi = jnp.sum(jnp.where(lo_mask, 0.0, sq), axis=-1, keepdims=True)
        ms = jnp.where(lo_mask, lo, hi) * (2.0 / LANES)
        outs.append(c * lax.rsqrt(ms + EPS))
    return jnp.concatenate(outs, axis=-1)


def _in_proj_kernel(x_ref, g1_ref, win_ref, poolw_ref, pscale_ref, wpu_ref, qg_ref, kg_ref,
                    pg_ref, ga_ref, qn_ref, kn_ref, vt_ref, prev_ref, wfold_ref, *, tm, tk, d_model, pool_width,
                    qk_width, attn_width):
    j = pl.program_id(1)
    x = x_ref[...]
    ms = jnp.mean(x * x, axis=-1, keepdims=True)
    h = (x * lax.rsqrt(ms + EPS) * g1_ref[...]).astype(bf16)

    def proj(lo, width):
        return _dot(h, win_ref[:, lo:lo + width])

    off_q = pool_width
    off_k = off_q + qk_width
    off_v = off_k + qk_width
    off_gp = off_v + attn_width
    off_ga = off_gp + d_model

    u = proj(0, pool_width)

    @pl.when(j == 0)
    def _():
        prev_ref[...] = jnp.zeros_like(prev_ref)

    ext = jnp.concatenate([prev_ref[...], u], axis=0)
    prev_ref[...] = u[tm - POOL_HALO:, :]
    pos = j * tm + lax.broadcasted_iota(jnp.int32, (tm, 1), 0)
    group = pool_width // len(POOL_WINDOWS)
    @pl.when(jnp.logical_and(pl.program_id(0) == 0, j == 0))
    def _():
        for g in range(len(POOL_WINDOWS)):
            rows_g = slice(g * group, (g + 1) * group)
            scaled = (poolw_ref[g].astype(f32) * pscale_ref[:, rows_g]).astype(bf16)
            wfold_ref[rows_g, :] = _dot(scaled, wpu_ref[rows_g, :]).astype(bf16)

    ds = []
    for g, w in enumerate(POOL_WINDOWS):
        acc = ext[:, g * group:(g + 1) * group]
        shift = 1
        while shift < w:
            acc = acc + pltpu.roll(acc, shift, 0)
            shift *= 2
        wsum = acc[POOL_HALO:, :]
        cnt = jnp.minimum(pos + 1, w).astype(f32)
        ds.append((wsum / cnt - u[:, g * group:(g + 1) * group]).astype(bf16))
    pool_out = _dot(jnp.concatenate(ds, axis=-1), wfold_ref[...])

    half = d_model // 2
    for c in range(2):
        gp = jax.nn.sigmoid(proj(off_gp + c * half, half))
        pg_ref[:, c * half:(c + 1) * half] = (gp * pool_out[:, c * half:(c + 1) * half]).astype(bf16)
        ga_ref[:, c * half:(c + 1) * half] = jax.nn.sigmoid(proj(off_ga + c * half, half)).astype(bf16)

    n_tiles = qk_width // LANES
    qn_ref[...] = (_half_lane_rmsnorm(proj(off_q, qk_width), n_tiles, tm) * qg_ref[...]).astype(bf16)
    kn_ref[...] = (_half_lane_rmsnorm(proj(off_k, qk_width), n_tiles, tm) * kg_ref[...]).astype(bf16)
    v = proj(off_v, attn_width)
    extra = lax.broadcasted_iota(jnp.int32, (VT_ROWS - LANES, tk), 0)
    ones_rows = jnp.where(extra == 0, 1.0, 0.0).astype(bf16)
    for hh in range(attn_width // LANES):
        for c in range(tm // tk):
            vt_ref[hh, c, 0:LANES, :] = v[c * tk:(c + 1) * tk, hh * LANES:(hh + 1) * LANES].T.astype(bf16)
            vt_ref[hh, c, LANES:VT_ROWS, :] = ones_rows


def _in_proj(x, g1, w_in, pool_w, pool_scale, w_pool_up, qg, kg):
    B, S, D = x.shape
    tm = min(ROW_TILE, S)
    pool_width = w_pool_up.shape[0]
    qk_width = qg.shape[1]
    attn_width = qk_width
    in_width = w_in.shape[1]
    const2 = lambda b, j: (0, 0)
    row = lambda b, j: (b, j, 0)
    tk = min(ATTN_TILE, S)
    n_heads = attn_width // LANES
    kern = functools.partial(_in_proj_kernel, tm=tm, tk=tk, d_model=D, pool_width=pool_width, qk_width=qk_width,
                             attn_width=attn_width)
    return pl.pallas_call(
        kern,
        grid=(B, S // tm),
        in_specs=[
            pl.BlockSpec((None, tm, D), row),
            pl.BlockSpec((1, D), const2),
            pl.BlockSpec((D, in_width), const2),
            pl.BlockSpec(pool_w.shape, lambda b, j: (0, 0, 0)),
            pl.BlockSpec((1, pool_width), const2),
            pl.BlockSpec((pool_width, D), const2),
            pl.BlockSpec((1, qk_width), const2),
            pl.BlockSpec((1, qk_width), const2),
        ],
        out_specs=[
            pl.BlockSpec((None, tm, D), row),
            pl.BlockSpec((None, tm, D), row),
            pl.BlockSpec((None, tm, qk_width), row),
            pl.BlockSpec((None, tm, qk_width), row),
            pl.BlockSpec((None, n_heads, tm // tk, VT_ROWS, tk), lambda b, j: (b, 0, j, 0, 0)),
        ],
        out_shape=[
            jax.ShapeDtypeStruct((B, S, D), bf16),
            jax.ShapeDtypeStruct((B, S, D), bf16),
            jax.ShapeDtypeStruct((B, S, qk_width), bf16),
            jax.ShapeDtypeStruct((B, S, qk_width), bf16),
            jax.ShapeDtypeStruct((B, n_heads, S // tk, VT_ROWS, tk), bf16),
        ],
        scratch_shapes=[pltpu.VMEM((POOL_HALO, pool_width), f32), pltpu.VMEM((pool_width, D), bf16)],
        compiler_params=pltpu.CompilerParams(dimension_semantics=("arbitrary", "arbitrary"),
                                             vmem_limit_bytes=VMEM_LIMIT),
        name="in_proj",
    )(x, g1, w_in, pool_w, pool_scale, w_pool_up, qg, kg)


def _attn_kernel(slopes_ref, q_ref, k_ref, vt_ref, fq_ref, fk_ref, lq1_ref, lk1_ref, lq2_ref, lk2_ref, sg_ref,
                 o_ref, sa_ref, sb_ref, pa_ref, pb_ref, m_ref, alpha_ref, acc_ref, *, tq, tk, lam_init):
    h = pl.program_id(1)
    i = pl.program_id(2)
    slope2 = slopes_ref[h]
    q = q_ref[...]
    lane = lax.broadcasted_iota(jnp.int32, q.shape, 1)
    first = lane < (LANES // 2)
    zero = jnp.zeros_like(q)
    qs = jnp.concatenate([jnp.where(first, q, zero), jnp.where(first, zero, q)], axis=0)
    qa = jnp.concatenate([qs, fq_ref[...]], axis=1)
    fk = fk_ref[...]

    def scores(n, s_out):
        kb = k_ref[pl.ds(pl.multiple_of(n * tk, tk), tk), :]
        s_out[...] = _dot_nt(jnp.concatenate([kb, fk], axis=1), qa)

    def softmax(n, s_in, p_out, key_offset=None):
        c = -slope2 * (i * tq - n * tk).astype(f32)
        s = s_in[...]
        if key_offset is not None:
            kk = lax.broadcasted_iota(jnp.int32, s.shape, 0) + key_offset
            qq = lax.broadcasted_iota(jnp.int32, s.shape, 1)
            qq = jnp.where(qq >= tq, qq - tq, qq)
            s = jnp.where(kk <= qq, s, -jnp.inf)
        m_old = m_ref[...]
        m_new = jnp.maximum(m_old, jnp.max(s, axis=0, keepdims=True) + c)
        p_out[...] = jnp.exp2(s - (m_new - c)).astype(bf16)
        m_ref[...] = m_new
        return jnp.exp2(m_old - m_new)

    def pv(n, p_in, alpha):
        acc_ref[...] = alpha * acc_ref[...] + _dot(vt_ref[jnp.maximum(n, 0)], p_in[...])

    m_ref[...] = jnp.full(m_ref.shape, -jnp.inf, f32)
    alpha_ref[...] = jnp.ones_like(alpha_ref)
    acc_ref[...] = jnp.zeros_like(acc_ref)
    pb_ref[...] = jnp.zeros_like(pb_ref)
    scores(0, sa_ref)

    def pair(t, carry):
        n = 2 * t
        scores(n + 1, sb_ref)
        a_even = softmax(n, sa_ref, pa_ref)
        pv(n - 1, pb_ref, alpha_ref[...])
        scores(n + 2, sa_ref)
        a_odd = softmax(n + 1, sb_ref, pb_ref)
        pv(n, pa_ref, a_even)
        alpha_ref[...] = a_odd
        return carry

    lax.fori_loop(0, i, pair, 0)
    n = 2 * i
    scores(n + 1, sb_ref)
    a_even = softmax(n, sa_ref, pa_ref, key_offset=0)
    pv(n - 1, pb_ref, alpha_ref[...])
    a_odd = softmax(n + 1, sb_ref, pb_ref, key_offset=tk)
    pv(n, pa_ref, a_even)
    pv(n + 1, pb_ref, a_odd)

    lam =(jnp.exp(jnp.sum(lq1_ref[...] * lk1_ref[...], keepdims=True))
           - jnp.exp(jnp.sum(lq2_ref[...] * lk2_ref[...], keepdims=True)) + lam_init)
    acc = acc_ref[...]
    o_all = acc[0:LANES] / acc[LANES:LANES + 1]
    o = o_all[:, :tq] - lam * o_all[:, tq:]
    ms = jnp.mean(o * o, axis=0, keepdims=True)
    on = o * lax.rsqrt(ms + EPS) * sg_ref[...] * (1.0 - lam_init)
    o_ref[...] = on.T.astype(bf16)


def _split_bf16(x, pieces=3):
    out = []
    for _ in range(pieces):
        p = x.astype(bf16)
        out.append(p)
        x = x - p.astype(f32)
    return out


def _alibi_features(slopes2, tq, tk):
    assert tk <= 256
    ones = jnp.ones((tk,), bf16)
    krel = jnp.arange(tk, dtype=f32).astype(bf16)
    fk = jnp.zeros((tk, LANES), bf16)
    for c in range(3):
        fk = fk.at[:, c].set(krel).at[:, 3 + c].set(ones)
    qrel = jnp.arange(tq, dtype=f32)
    fq = jnp.zeros((slopes2.shape[0], 2 * tq, LANES), bf16)
    a = _split_bf16(slopes2)
    b = _split_bf16(-slopes2[:, None] * jnp.concatenate([qrel, qrel])[None, :])
    for c in range(3):
        fq = fq.at[:, :, c].set(a[c][:, None]).at[:, :, 3 + c].set(b[c])
    return fq, fk


def _attention(qn, kn, vt, fq, fk, slopes, lq1, lk1, lq2, lk2, subln_col, lam_init):
    B, S, _ = qn.shape
    tk = fk.shape[0]
    tq = fq.shape[1] // 2
    assert tq == 2 * tk
    nkv = S // tk
    const2 = lambda b, h, i, *_: (0, 0)
    kern = functools.partial(_attn_kernel, tq=tq, tk=tk, lam_init=lam_init)
    hd = lq1.shape[1]
    return pl.pallas_call(
        kern,
        grid_spec=pltpu.PrefetchScalarGridSpec(
            num_scalar_prefetch=1,
            grid=(B, N_HEADS, S // tq),
            in_specs=[
                pl.BlockSpec((None, tq, LANES), lambda b, h, i, *_: (b, i, h)),
                pl.BlockSpec((None, S, LANES), lambda b, h, i, *_: (b, 0, h)),
                pl.BlockSpec((None, None, nkv, VT_ROWS, tk), lambda b, h, i, *_: (b, h, 0, 0, 0)),
                pl.BlockSpec((None, 2 * tq, LANES), lambda b, h, i, *_: (h, 0, 0)),
                pl.BlockSpec((tk, LANES), const2),
                pl.BlockSpec((1, hd), const2),
                pl.BlockSpec((1, hd), const2),
                pl.BlockSpec((1, hd), const2),
                pl.BlockSpec((1, hd), const2),
                pl.BlockSpec((LANES, 1), const2),
            ],
            out_specs=pl.BlockSpec((None, tq, LANES), lambda b, h, i, *_: (b, i, h)),
            scratch_shapes=[pltpu.VMEM((tk, 2 * tq), f32), pltpu.VMEM((tk, 2 * tq), f32),
                            pltpu.VMEM((tk, 2 * tq), bf16), pltpu.VMEM((tk, 2 * tq), bf16),
                            pltpu.VMEM((1, 2 * tq), f32), pltpu.VMEM((1, 2 * tq), f32),
                            pltpu.VMEM((VT_ROWS, 2 * tq), f32)],
        ),
        out_shape=jax.ShapeDtypeStruct((B, S, N_HEADS * LANES), bf16),
        compiler_params=pltpu.CompilerParams(dimension_semantics=("arbitrary", "arbitrary", "arbitrary"),
                                             vmem_limit_bytes=VMEM_LIMIT),
        name="diff_attn",
    )(slopes, qn, kn, vt, fq, fk, lq1, lk1, lq2, lk2, subln_col)


def _merge_kernel(pg_ref, ga_ref, o_ref, x_ref, wau_ref, wout_ref, g2_ref, wr_ref, br_ref,
                  x1_ref, hn_ref, eid_ref, wt_ref, rank_ref, cnt_ref, base_ref, *, tm, d_model):
    step = pl.program_id(0)

    @pl.when(step == 0)
    def _():
        base_ref[...] = jnp.zeros_like(base_ref)

    attn_out = _dot(o_ref[...], wau_ref[...])
    merged = pg_ref[...].astype(f32) + ga_ref[...].astype(f32) * attn_out
    x1 = x_ref[...] + _dot(merged.astype(bf16), wout_ref[...])
    x1_ref[...] = x1
    ms = jnp.mean(x1 * x1, axis=-1, keepdims=True)
    hn = x1 * lax.rsqrt(ms + EPS) * g2_ref[...]
    for s in range(d_model // LANES):
        hn_ref[pl.ds(s, tm, stride=SUBLANES), :] = hn[:, s * LANES:(s + 1) * LANES]

    logits = _dot_nt(wr_ref[...], hn.astype(bf16)) + br_ref[...]
    lg = logits[0:N_GROUPS]
    gmax = jnp.max(lg, axis=0, keepdims=True)
    p_top = 1.0 / jnp.sum(jnp.exp(lg - gmax), axis=0, keepdims=True)
    grow = lax.broadcasted_iota(jnp.int32, lg.shape, 0).astype(f32)
    g_idx = jnp.min(jnp.where(lg == gmax, grow, float(N_GROUPS)), axis=0, keepdims=True)

    sel = jnp.zeros((EXPERTS_PER_GROUP, tm), f32)
    for g in range(N_GROUPS):
        le_g = logits[SUBLANES + g * EXPERTS_PER_GROUP:SUBLANES + (g + 1) * EXPERTS_PER_GROUP]
        sel = jnp.where(g_idx == float(g), le_g, sel)
    erow = lax.broadcasted_iota(jnp.int32, sel.shape, 0).astype(f32)
    e1 = jnp.max(sel, axis=0, keepdims=True)
    i1 = jnp.min(jnp.where(sel == e1, erow, float(EXPERTS_PER_GROUP)), axis=0, keepdims=True)
    sel2 = jnp.where(erow == i1, -jnp.inf, sel)
    e2 = jnp.max(sel2, axis=0, keepdims=True)
    i2 = jnp.min(jnp.where(sel2 == e2, erow, float(EXPERTS_PER_GROUP)), axis=0, keepdims=True)
    r = jnp.exp(e2 - e1)
    w1 = p_top / (1.0 + r)
    w2 = p_top * r / (1.0 + r)
    eid1 = g_idx * float(EXPERTS_PER_GROUP) + i1
    eid2 = g_idx * float(EXPERTS_PER_GROUP) + i2
    eid_ref[...] = jnp.concatenate([eid1, eid2], axis=0).astype(jnp.int32)
    wt_ref[...] = jnp.concatenate([w1, w2], axis=0)

    xrow = lax.broadcasted_iota(jnp.int32, (N_EXPERTS, tm), 0).astype(f32)
    oh1 = jnp.where(xrow == eid1, 1.0, 0.0)
    oh2 = jnp.where(xrow == eid2, 1.0, 0.0)
    oh = oh1 + oh2
    a = lax.broadcasted_iota(jnp.int32, (tm, tm), 0)
    b = lax.broadcasted_iota(jnp.int32, (tm, tm), 1)
    upper = jnp.where(a <= b, 1.0, 0.0).astype(bf16)
    before = _dot(oh.astype(bf16), upper) + base_ref[...] - 1.0
    rank1 = jnp.sum(oh1 * before, axis=0, keepdims=True)
    rank2 = jnp.sum(oh2 * before, axis=0, keepdims=True)
    rank_ref[...] = jnp.concatenate([rank1, rank2], axis=0).astype(jnp.int32)
    base_ref[...] = base_ref[...] + jnp.sum(oh, axis=1, keepdims=True)
    cnt_ref[...] = jnp.broadcast_to(base_ref[...], cnt_ref.shape)


def _merge(pg, ga, o, x, w_attn_up, w_out, g2, wr, br):
    N, D = x.shape
    tm = min(ROW_TILE, N)
    aw = o.shape[1]
    const2 = lambda i: (0, 0)
    row = lambda i: (i, 0)
    colblk = lambda i: (0, i)
    kern = functools.partial(_merge_kernel, tm=tm, d_model=D)
    return pl.pallas_call(
        kern,
        grid=(N // tm,),
        in_specs=[
            pl.BlockSpec((tm, D), row),
            pl.BlockSpec((tm, D), row),
            pl.BlockSpec((tm, aw), row),
            pl.BlockSpec((tm, D), row),
            pl.BlockSpec((aw, D), const2),
            pl.BlockSpec((D, D), const2),
            pl.BlockSpec((1, D), const2),
            pl.BlockSpec((ROUTER_ROWS, D), const2),
            pl.BlockSpec((ROUTER_ROWS, 1), const2),
        ],
        out_specs=[
            pl.BlockSpec((tm, D), row),
            pl.BlockSpec((tm * SUBLANES, LANES), row),
            pl.BlockSpec((TOP_K, tm), colblk),
            pl.BlockSpec((TOP_K, tm), colblk),
            pl.BlockSpec((TOP_K, tm), colblk),
            pl.BlockSpec((N_EXPERTS, LANES), const2),
        ],
        out_shape=[
            jax.ShapeDtypeStruct((N, D), f32),
            jax.ShapeDtypeStruct((N * SUBLANES, LANES), f32),
            jax.ShapeDtypeStruct((TOP_K, N), jnp.int32),
            jax.ShapeDtypeStruct((TOP_K, N), f32),
            jax.ShapeDtypeStruct((TOP_K, N), jnp.int32),
            jax.ShapeDtypeStruct((N_EXPERTS, LANES), f32),
        ],
        scratch_shapes=[pltpu.VMEM((N_EXPERTS, 1), f32)],
        compiler_params=pltpu.CompilerParams(dimension_semantics=("arbitrary",), vmem_limit_bytes=VMEM_LIMIT),
        name="merge_router",
    )(pg, ga, o, x, w_attn_up, w_out, g2, wr, br)


def _row(ref, r):
    return ref.at[pl.ds(pl.multiple_of(r * SUBLANES, SUBLANES), SUBLANES), :]


def _wait_bytes_of(ref_like, any_hbm, sem):
    n = ref_like.shape[0]
    pltpu.make_async_copy(any_hbm.at[pl.ds(0, n), :], any_hbm.at[pl.ds(0, n), :], sem).wait()


STAGES = 3


def _dispatch_kernel(pstart_ref, pend_ref, eid_ref, rank_ref, hn_hbm, xs_hbm, zero_ref, stage, in_sem, out_sem,
                     zsem, *, tt, rows, n_blocks):
    i = pl.program_id(0)
    n = pl.num_programs(0)

    @pl.when(i == 0)
    def _():
        zero_ref[...] = jnp.zeros_like(zero_ref)
        for e in range(N_EXPERTS):
            @pl.when(pend_ref[e] > pstart_ref[e])
            def _():
                pltpu.make_async_copy(zero_ref, xs_hbm.at[pl.ds(pl.multiple_of((pend_ref[e] - rows) * SUBLANES,
                                                                               SUBLANES), rows * SUBLANES), :],
                                      zsem).start()
        def zero_block(b, carry):
            pltpu.make_async_copy(zero_ref, xs_hbm.at[pl.ds(pl.multiple_of(b * (rows * SUBLANES), SUBLANES),
                                                            rows * SUBLANES), :], zsem).start()
            return carry

        def wait_block(b, carry):
            pltpu.make_async_copy(zero_ref, xs_hbm.at[pl.ds(0, rows * SUBLANES), :], zsem).wait()
            return carry

        first_unused = pend_ref[N_EXPERTS - 1] // rows
        lax.fori_loop(first_unused, n_blocks, zero_block, 0)
        for e in range(N_EXPERTS):
            @pl.when(pend_ref[e] > pstart_ref[e])
            def _():
                wait_block(0, 0)
        lax.fori_loop(first_unused, n_blocks, wait_block, 0)

    def stage_copy(step):
        s = step % STAGES
        src = hn_hbm.at[pl.ds(pl.multiple_of(step * (tt * SUBLANES), SUBLANES), tt * SUBLANES), :]
        return pltpu.make_async_copy(src, stage.at[s], in_sem.at[s])

    def wait_rows_of(step):
        _wait_bytes_of(zero_ref.at[pl.ds(0, TOP_K * tt * SUBLANES), :], hn_hbm, out_sem.at[step % STAGES])

    @pl.when(i == 0)
    def _():
        stage_copy(i).start()

    @pl.when(i >= STAGES - 1)
    def _():
        wait_rows_of(i - (STAGES - 1))

    @pl.when(i + 1 < n)
    def _():
        stage_copy(i + 1).start()

    stage_copy(i).wait()
    slot = i % STAGES

    def body(t, carry):
        for k in range(TOP_K):
            dest = pstart_ref[eid_ref[0, k * tt + t]] + rank_ref[0, k * tt + t]
            pltpu.make_async_copy(_row(stage.at[slot], t), _row(xs_hbm, dest), out_sem.at[slot]).start(priority=k)
        return carry
    lax.fori_loop(0, tt, body, 0, unroll=8)

    @pl.when(i == n - 1)
    def _():
        for back in range(STAGES - 2, -1, -1):
            @pl.when(i - back >= 0)
            def _():
                wait_rows_of(i - back)


def _dispatch(pstart, pend, eid_blocks, rank_blocks, hn_g, n_slots):
    n, _, width = eid_blocks.shape
    tt = width // TOP_K
    rows = EXPERT_ROWS
    assert rows >= TOP_K * tt
    kern = functools.partial(_dispatch_kernel, tt=tt, rows=rows, n_blocks=n_slots // rows)
    smem_blk = pl.BlockSpec((None, 1, width), lambda i, *_: (i, 0, 0), memory_space=pltpu.SMEM)
    return pl.pallas_call(
        kern,
        grid_spec=pltpu.PrefetchScalarGridSpec(
            num_scalar_prefetch=2,
            grid=(n,),
            in_specs=[smem_blk, smem_blk, pl.BlockSpec(memory_space=pl.ANY)],
            out_specs=pl.BlockSpec(memory_space=pl.ANY),
            scratch_shapes=[pltpu.VMEM((rows * SUBLANES, LANES), f32),
                            pltpu.VMEM((STAGES, tt * SUBLANES, LANES), f32),
                            pltpu.SemaphoreType.DMA((STAGES,)), pltpu.SemaphoreType.DMA((STAGES,)),
                            pltpu.SemaphoreType.DMA(())],
        ),
        out_shape=jax.ShapeDtypeStruct((n_slots * SUBLANES, LANES), f32),
        compiler_params=pltpu.CompilerParams(dimension_semantics=("arbitrary",), vmem_limit_bytes=VMEM_LIMIT,
                                             has_side_effects=True),
        name="dispatch",
    )(pstart, pend, eid_blocks, rank_blocks, hn_g)


def _rows_from_token_major(buf, n_rows, n_tiles):
    return jnp.concatenate([buf[pl.ds(s, n_rows, stride=SUBLANES), :] for s in range(n_tiles)], axis=-1)


def _expert_kernel(beid_ref, nbu_ref, xs_ref, wg_ref, wu_ref, wd_ref, ys_ref, wg_bf, wu_bf, wd_bf, *, rows,
                   d_model):
    b = pl.program_id(0)
    changed = jnp.logical_or(b == 0, beid_ref[b] != beid_ref[jnp.maximum(b - 1, 0)])

    @pl.when(changed)
    def _():
        wg_bf[...] = wg_ref[...].astype(bf16)
        wu_bf[...] = wu_ref[...].astype(bf16)
        wd_bf[...] = wd_ref[...].astype(bf16)

    n_tiles = d_model // LANES

    @pl.when(b < nbu_ref[0])
    def _():
        xb = _rows_from_token_major(xs_ref, rows, n_tiles).astype(bf16)
        hdn = jax.nn.silu(_dot(xb, wg_bf[...])) * _dot(xb, wu_bf[...])
        y = _dot(hdn.astype(bf16), wd_bf[...])
        for s in range(n_tiles):
            ys_ref[pl.ds(s, rows, stride=SUBLANES), :] = y[:, s * LANES:(s + 1) * LANES]

    @pl.when(b >= nbu_ref[0])
    def _():
        ys_ref[...] = jnp.zeros_like(ys_ref)


def _experts(block_eid, n_used, xs, w_gate, w_up, w_down):
    n_blocks = block_eid.shape[0]
    rows = EXPERT_ROWS
    E, D, DE = w_gate.shape
    kern = functools.partial(_expert_kernel, rows=rows, d_model=D)
    xs_idx = lambda b, eid, nbu: (jnp.minimum(b, nbu[0] - 1), 0)
    return pl.pallas_call(
        kern,
        grid_spec=pltpu.PrefetchScalarGridSpec(
            num_scalar_prefetch=2,
            grid=(n_blocks,),
            in_specs=[
                pl.BlockSpec((rows * SUBLANES, LANES), xs_idx),
                pl.BlockSpec((None, D, DE), lambda b, eid, nbu: (eid[b], 0, 0)),
                pl.BlockSpec((None, D, DE), lambda b, eid, nbu: (eid[b], 0, 0)),
                pl.BlockSpec((None, DE, D), lambda b, eid, nbu: (eid[b], 0, 0)),
            ],
            out_specs=pl.BlockSpec((rows * SUBLANES, LANES), lambda b, *_: (b, 0)),
            scratch_shapes=[
                pltpu.VMEM((D, DE), bf16),
                pltpu.VMEM((D, DE), bf16),
                pltpu.VMEM((DE, D), bf16),
            ],
        ),
        out_shape=jax.ShapeDtypeStruct((n_blocks * rows * SUBLANES, LANES), f32),
        compiler_params=pltpu.CompilerParams(dimension_semantics=("arbitrary",), vmem_limit_bytes=VMEM_LIMIT),
        name="experts",
    )(block_eid, n_used, xs, w_gate, w_up, w_down)


def _combine_kernel(pstart_ref, eid_ref, rank_ref, eidn_ref, rankn_ref, x1_ref, wt_ref, ys_hbm, out_ref,
                    buf, sem, *, te, d_model):
    i = pl.program_id(0)
    n = pl.num_programs(0)
    slot = i % 2

    def start_gather(e_ref, r_ref, dst, dsem):
        def body(t, carry):
            for k in range(TOP_K):
                r = k * te + t
                dest = pstart_ref[e_ref[0, r]] + r_ref[0, r]
                pltpu.make_async_copy(_row(ys_hbm, dest), _row(dst, r), dsem).start(priority=k)
            return carry
        lax.fori_loop(0, te, body, 0, unroll=4)

    @pl.when(i == 0)
    def _():
        start_gather(eid_ref, rank_ref, buf.at[0], sem.at[0])

    @pl.when(i + 1 < n)
    def _():
        start_gather(eidn_ref, rankn_ref, buf.at[1 - slot], sem.at[1 - slot])

    _wait_bytes_of(buf.at[slot], ys_hbm, sem.at[slot])
    n_tiles = d_model // LANES
    both = _rows_from_token_major(buf.at[slot], TOP_K * te, n_tiles)
    wt = wt_ref[...]
    out_ref[...] = x1_ref[...] + (wt[:, 0:1] * both[:te] + wt[:, 1:2] * both[te:])


def _combine(pstart, eid_blocks, rank_blocks, x1, wt_cols, ys):
    N, D = x1.shape
    n, _, width = eid_blocks.shape
    te = width // TOP_K
    kern = functools.partial(_combine_kernel, te=te, d_model=D)
    cur = pl.BlockSpec((None, 1, width), lambda i, *_: (i, 0, 0), memory_space=pltpu.SMEM)
    nxt = pl.BlockSpec((None, 1, width), lambda i, *_: (jnp.minimum(i + 1, n - 1), 0, 0), memory_space=pltpu.SMEM)
    return pl.pallas_call(
        kern,
        grid_spec=pltpu.PrefetchScalarGridSpec(
            num_scalar_prefetch=1,
            grid=(n,),
            in_specs=[
                cur, cur, nxt, nxt,
                pl.BlockSpec((te, D), lambda i, *_: (i, 0)),
                pl.BlockSpec((te, TOP_K), lambda i, *_: (i, 0)),
                pl.BlockSpec(memory_space=pl.ANY),
            ],
            out_specs=pl.BlockSpec((te, D), lambda i, *_: (i, 0)),
            scratch_shapes=[pltpu.VMEM((2, TOP_K * te * SUBLANES, LANES), f32), pltpu.SemaphoreType.DMA((2,))],
        ),
        out_shape=jax.ShapeDtypeStruct((N, D), f32),
        compiler_params=pltpu.CompilerParams(dimension_semantics=("arbitrary",), vmem_limit_bytes=VMEM_LIMIT),
        name="combine",
    )(pstart, eid_blocks, rank_blocks, eid_blocks, rank_blocks, x1, wt_cols, ys)


def _token_blocks(a, tt):
    k, n = a.shape
    return a.reshape(k, n // tt, tt).transpose(1, 0, 2).reshape(n // tt, 1, k * tt)


def _layer(x, l, norm1_g, w_in, pool_w, pool_scale, w_pool_up, q_norm_g, k_norm_g, lambda_q1, lambda_k1,
           lambda_q2, lambda_k2, subln_g, w_attn_up, w_out, norm2_g, w_router_group, b_router_group,
           w_router_expert, b_router_expert, w_expert_gate, w_expert_up, w_expert_down):
    B, S, D = x.shape
    N = B * S
    head_dim = q_norm_g.shape[0]
    lam_init = 0.8 - 0.6 * math.exp(-0.3 * l)
    reps = (N_HEADS * 2 * head_dim) // head_dim

    qg = (jnp.tile(q_norm_g, reps) * (head_dim ** -0.5 * LOG2E))[None, :]
    kg = jnp.tile(k_norm_g, reps)[None, :]
    pg, ga, qn, kn, vt = _in_proj(x, norm1_g[None, :], w_in.astype(bf16), pool_w.astype(bf16),
                                  pool_scale[None, :], w_pool_up.astype(bf16), qg, kg)

    tk = min(ATTN_TILE, S)
    slopes2 = jnp.asarray([2.0 ** (-8.0 * (h + 1) / N_HEADS) * LOG2E for h in range(N_HEADS)], f32)
    fq, fk = _alibi_features(slopes2, 2 * tk, tk)
    o = _attention(qn, kn, vt, fq, fk, slopes2, lambda_q1[None, :], lambda_k1[None, :], lambda_q2[None, :],
                   lambda_k2[None, :], subln_g[:, None], lam_init)

    wr = jnp.zeros((ROUTER_ROWS, D), f32)
    wr = wr.at[:N_GROUPS].set(w_router_group.T).at[SUBLANES:].set(w_router_expert.T).astype(bf16)
    br = jnp.zeros((ROUTER_ROWS, 1), f32)
    br = br.at[:N_GROUPS, 0].set(b_router_group).at[SUBLANES:, 0].set(b_router_expert)
    x1, hn_g, eid, wts, rank, cnt = _merge(pg.reshape(N, D), ga.reshape(N, D), o.reshape(N, -1),
                                           x.reshape(N, D), w_attn_up.astype(bf16), w_out.astype(bf16),
                                           norm2_g[None, :], wr, br)

    R = EXPERT_ROWS
    counts = cnt[:, 0].astype(jnp.int32)
    padded = (counts + R - 1) // R * R
    pend = jnp.cumsum(padded).astype(jnp.int32)
    pstart = pend - padded
    n_blocks = -(-(N * TOP_K) // R) + N_EXPERTS
    starts = jnp.arange(n_blocks, dtype=jnp.int32) * R
    block_eid = jnp.minimum(jnp.sum((pend[None, :] <= starts[:, None]).astype(jnp.int32), axis=1), N_EXPERTS - 1)
    n_used = pend[-1:] // R

    tt = min(MOVE_TILE, N)
    eid_blocks = _token_blocks(eid, tt)
    rank_blocks = _token_blocks(rank, tt)
    xs = _dispatch(pstart, pend, eid_blocks, rank_blocks, hn_g, n_blocks * R)
    ys = _experts(block_eid, n_used, xs, w_expert_gate, w_expert_up, w_expert_down)
    out = _combine(pstart, eid_blocks, rank_blocks, x1, wts.T, ys)
    return out.reshape(B, S, D)


def kernel(x, norm1_g, w_in, pool_w, pool_scale, w_pool_up, q_norm_g, k_norm_g, lambda_q1, lambda_k1, lambda_q2,
           lambda_k2, subln_g, w_attn_up, w_out, norm2_g, w_router_group, b_router_group, w_router_expert,
           b_router_expert, w_expert_gate, w_expert_up, w_expert_down):
    params = (norm1_g, w_in, pool_w, pool_scale, w_pool_up, q_norm_g, k_norm_g, lambda_q1, lambda_k1, lambda_q2,
              lambda_k2, subln_g, w_attn_up, w_out, norm2_g, w_router_group, b_router_group, w_router_expert,
              b_router_expert, w_expert_gate, w_expert_up, w_expert_down)
    for l in range(norm1_g.shape[0]):
        x = _layer(x, l, *(p[l] for p in params))
    return x
```

```python
import functools
import math

import jax
import jax.numpy as jnp
from jax import lax
from jax.experimental import pallas as pl
from jax.experimental.pallas import tpu as pltpu

EPS = 1e-6
POOL_WINDOWS = (2, 4, 8, 16)
POOL_HALO = 16
N_HEADS = 4
N_GROUPS = 4
EXPERTS_PER_GROUP = 8
N_EXPERTS = N_GROUPS * EXPERTS_PER_GROUP
TOP_K = 2
LANES = 128
SUBLANES = 8
ROUTER_ROWS = 8 + N_EXPERTS
VT_ROWS = LANES + 16
LOG2E = 1.4426950408889634

ROW_TILE = 512
ATTN_TILE = 256
ATTN_HEADS_PER_STEP = 4
EXPERT_ROWS = 512
MOVE_TILE = 256
VMEM_LIMIT = 52 * 1024 * 1024

f32 = jnp.float32
bf16 = jnp.bfloat16


def _dot(a, b):
    return jnp.dot(a, b, preferred_element_type=f32)


def _dot_nt(a, b):
    return lax.dot_general(a, b, (((1,), (1,)), ((), ())), preferred_element_type=f32)


def _half_lane_rmsnorm(t, n_tiles, rows):
    lane = lax.broadcasted_iota(jnp.int32, (rows, LANES), 1)
    lo_mask = lane < (LANES // 2)
    outs = []
    for i in range(n_tiles):
        c = t[:, i * LANES:(i + 1) * LANES]
        sq = c * c
        lo = jnp.sum(jnp.where(lo_mask, sq, 0.0), axis=-1, keepdims=True)
        hi = jnp.sum(jnp.where(lo_mask, 0.0, sq), axis=-1, keepdims=True)
        ms = jnp.where(lo_mask, lo, hi) * (2.0 / LANES)
        outs.append(c * lax.rsqrt(ms + EPS))
    return jnp.concatenate(outs, axis=-1)


def _in_proj_kernel(x_ref, g1_ref, win_ref, poolw_ref, pscale_ref, wpu_ref, qg_ref, kg_ref,
                    pg_ref, ga_ref, qn_ref, kn_ref, vt_ref, prev_ref, wfold_ref, *, tm, tk, d_model, pool_width,
                    qk_width, attn_width):
    j = pl.program_id(1)
    x = x_ref[...]
    ms = jnp.mean(x * x, axis=-1, keepdims=True)
    h = (x * lax.rsqrt(ms + EPS) * g1_ref[...]).astype(bf16)

    def proj(lo, width):
        return _dot(h, win_ref[:, lo:lo + width])

    off_q = pool_width
    off_k = off_q + qk_width
    off_v = off_k + qk_width
    off_gp = off_v + attn_width
    off_ga = off_gp + d_model

    u = proj(0, pool_width)

    @pl.when(j == 0)
    def _():
        prev_ref[...] = jnp.zeros_like(prev_ref)

    ext = jnp.concatenate([prev_ref[...], u], axis=0)
    prev_ref[...] = u[tm - POOL_HALO:, :]
    pos = j * tm + lax.broadcasted_iota(jnp.int32, (tm, 1), 0)
    group = pool_width // len(POOL_WINDOWS)
    @pl.when(jnp.logical_and(pl.program_id(0) == 0, j == 0))
    def _():
        for g in range(len(POOL_WINDOWS)):
            rows_g = slice(g * group, (g + 1) * group)
            scaled = (poolw_ref[g].astype(f32) * pscale_ref[:, rows_g]).astype(bf16)
            wfold_ref[rows_g, :] = _dot(scaled, wpu_ref[rows_g, :]).astype(bf16)

    ds = []
    for g, w in enumerate(POOL_WINDOWS):
        acc = ext[:, g * group:(g + 1) * group]
        shift = 1
        while shift < w:
            acc = acc + pltpu.roll(acc, shift, 0)
            shift *= 2
        wsum = acc[POOL_HALO:, :]
        cnt = jnp.minimum(pos + 1, w).astype(f32)
        ds.append((wsum / cnt - u[:, g * group:(g + 1) * group]).astype(bf16))
    pool_out = _dot(jnp.concatenate(ds, axis=-1), wfold_ref[...])

    half = d_model // 2
    for c in range(2):
        gp = jax.nn.sigmoid(proj(off_gp + c * half, half))
        pg_ref[:, c * half:(c + 1) * half] = (gp * pool_out[:, c * half:(c + 1) * half]).astype(bf16)
        ga_ref[:, c * half:(c + 1) * half] = jax.nn.sigmoid(proj(off_ga + c * half, half)).astype(bf16)

    n_tiles = qk_width // LANES
    qn_ref[...] = (_half_lane_rmsnorm(proj(off_q, qk_width), n_tiles, tm) * qg_ref[...]).astype(bf16)
    kn_ref[...] = (_half_lane_rmsnorm(proj(off_k, qk_width), n_tiles, tm) * kg_ref[...]).astype(bf16)
    v = proj(off_v, attn_width)
    extra = lax.broadcasted_iota(jnp.int32, (VT_ROWS - LANES, tk), 0)
    ones_rows = jnp.where(extra == 0, 1.0, 0.0).astype(bf16)
    for hh in range(attn_width // LANES):
        for c in range(tm // tk):
            vt_ref[hh, c, 0:LANES, :] = v[c * tk:(c + 1) * tk, hh * LANES:(hh + 1) * LANES].T.astype(bf16)
            vt_ref[hh, c, LANES:VT_ROWS, :] = ones_rows


def _in_proj(x, g1, w_in, pool_w, pool_scale, w_pool_up, qg, kg):
    B, S, D = x.shape
    tm = min(ROW_TILE, S)
    pool_width = w_pool_up.shape[0]
    qk_width = qg.shape[1]
    attn_width = qk_width
    in_width = w_in.shape[1]
    const2 = lambda b, j: (0, 0)
    row = lambda b, j: (b, j, 0)
    tk = min(ATTN_TILE, S)
    n_heads = attn_width // LANES
    kern = functools.partial(_in_proj_kernel, tm=tm, tk=tk, d_model=D, pool_width=pool_width, qk_width=qk_width,
                             attn_width=attn_width)
    return pl.pallas_call(
        kern,
        grid=(B, S // tm),
        in_specs=[
            pl.BlockSpec((None, tm, D), row),
            pl.BlockSpec((1, D), const2),
            pl.BlockSpec((D, in_width), const2),
            pl.BlockSpec(pool_w.shape, lambda b, j: (0, 0, 0)),
            pl.BlockSpec((1, pool_width), const2),
            pl.BlockSpec((pool_width, D), const2),
            pl.BlockSpec((1, qk_width), const2),
            pl.BlockSpec((1, qk_width), const2),
        ],
        out_specs=[
            pl.BlockSpec((None, tm, D), row),
            pl.BlockSpec((None, tm, D), row),
            pl.BlockSpec((None, tm, qk_width), row),
            pl.BlockSpec((None, tm, qk_width), row),
            pl.BlockSpec((None, n_heads, tm // tk, VT_ROWS, tk), lambda b, j: (b, 0, j, 0, 0)),
        ],
        out_shape=[
            jax.ShapeDtypeStruct((B, S, D), bf16),
            jax.ShapeDtypeStruct((B, S, D), bf16),
            jax.ShapeDtypeStruct((B, S, qk_width), bf16),
            jax.ShapeDtypeStruct((B, S, qk_width), bf16),
            jax.ShapeDtypeStruct((B, n_heads, S // tk, VT_ROWS, tk), bf16),
        ],
        scratch_shapes=[pltpu.VMEM((POOL_HALO, pool_width), f32), pltpu.VMEM((pool_width, D), bf16)],
        compiler_params=pltpu.CompilerParams(dimension_semantics=("arbitrary", "arbitrary"),
                                             vmem_limit_bytes=VMEM_LIMIT),
        name="in_proj",
    )(x, g1, w_in, pool_w, pool_scale, w_pool_up, qg, kg)


def _attn_kernel(slopes_ref, q_ref, k_ref, vt_ref, fq_ref, fk_ref, lq1_ref, lk1_ref, lq2_ref, lk2_ref, sg_ref,
                 o_ref, qa_ref, sa_ref, sb_ref, pa_ref, pb_ref, m_ref, alpha_ref, acc_ref, *,
                 tq, tk, heads, lam_init):
    hg = pl.program_id(1)
    i = pl.program_id(2)
    fk = fk_ref[...]
    lane = lax.broadcasted_iota(jnp.int32, (tq, LANES), 1)
    first = lane < (LANES // 2)
    hs = range(heads)

    buf_a = (sa_ref, pa_ref)
    buf_b = (sb_ref, pb_ref)

    def scores(g, n, buf):
        kb = k_ref[pl.ds(pl.multiple_of(n * tk, tk), tk), g * LANES:(g + 1) * LANES]
        buf[0][g] = _dot_nt(jnp.concatenate([kb, fk], axis=1), qa_ref[g])

    def softmax(g, n, buf, key_offset=None):
        s_ref, p_ref = buf
        c = -slopes_ref[hg * heads + g] * (i * tq - n * tk).astype(f32)
        s = s_ref[g]
        if key_offset is not None:
            kk = lax.broadcasted_iota(jnp.int32, s.shape, 0) + key_offset
            qq = lax.broadcasted_iota(jnp.int32, s.shape, 1)
            qq = jnp.where(qq >= tq, qq - tq, qq)
            s = jnp.where(kk <= qq, s, -jnp.inf)
        m_old = m_ref[g]
        m_new = jnp.maximum(m_old, jnp.max(s, axis=0, keepdims=True) + c)
        p_ref[g] = jnp.exp2(s - (m_new - c)).astype(bf16)
        m_ref[g] = m_new
        return jnp.exp2(m_old - m_new)

    def pv(g, n, buf, alpha):
        acc_ref[g] = alpha * acc_ref[g] + _dot(vt_ref[g, jnp.maximum(n, 0)], buf[1][g])

    m_ref[...] = jnp.full(m_ref.shape, -jnp.inf, f32)
    alpha_ref[...] = jnp.ones_like(alpha_ref)
    acc_ref[...] = jnp.zeros_like(acc_ref)
    pb_ref[...] = jnp.zeros_like(pb_ref)
    zero = jnp.zeros((tq, LANES), bf16)
    for g in hs:
        q = q_ref[:, g * LANES:(g + 1) * LANES]
        qs = jnp.concatenate([jnp.where(first, q, zero), jnp.where(first, zero, q)], axis=0)
        qa_ref[g] = jnp.concatenate([qs, fq_ref[g]], axis=1)
    for g in hs:
        scores(g, 0, buf_a)

    def pair(t, carry):
        n = 2 * t
        for g in hs:
            scores(g, n + 1, buf_b)
            a_even = softmax(g, n, buf_a)
            pv(g, n - 1, buf_b, alpha_ref[g])
            scores(g, n + 2, buf_a)
            alpha_ref[g] = softmax(g, n + 1, buf_b)
            pv(g, n, buf_a, a_even)
        return carry

    lax.fori_loop(0, i, pair, 0)
    n = 2 * i
    lam = (jnp.exp(jnp.sum(lq1_ref[...] * lk1_ref[...], keepdims=True))
           - jnp.exp(jnp.sum(lq2_ref[...] * lk2_ref[...], keepdims=True)) + lam_init)
    for g in hs:
        scores(g, n + 1, buf_b)
        a_even = softmax(g, n, buf_a, key_offset=0)
        pv(g, n - 1, buf_b, alpha_ref[g])
        a_odd = softmax(g, n + 1, buf_b, key_offset=tk)
        pv(g, n, buf_a, a_even)
        pv(g, n + 1, buf_b, a_odd)
        acc = acc_ref[g]
        o_all = acc[0:LANES] / acc[LANES:LANES + 1]
        o = o_all[:, :tq] - lam * o_all[:, tq:]
        ms = jnp.mean(o * o, axis=0, keepdims=True)
        on = o * lax.rsqrt(ms + EPS) * sg_ref[...] * (1.0 - lam_init)
        o_ref[:, g * LANES:(g + 1) * LANES] = on.T.astype(bf16)


def _split_bf16(x, pieces=3):
    out = []
    for _ in range(pieces):
        p = x.astype(bf16)
        out.append(p)
        x = x - p.astype(f32)
    return out


def _alibi_features(slopes2, tq, tk):
    assert tk <= 256
    ones = jnp.ones((tk,), bf16)
    krel = jnp.arange(tk, dtype=f32).astype(bf16)
    fk = jnp.zeros((tk, LANES), bf16)
    for c in range(3):
        fk = fk.at[:, c].set(krel).at[:, 3 + c].set(ones)
    qrel = jnp.arange(tq, dtype=f32)
    fq = jnp.zeros((slopes2.shape[0], 2 * tq, LANES), bf16)
    a = _split_bf16(slopes2)
    b = _split_bf16(-slopes2[:, None] * jnp.concatenate([qrel, qrel])[None, :])
    for c in range(3):
        fq = fq.at[:, :, c].set(a[c][:, None]).at[:, :, 3 + c].set(b[c])
    return fq, fk


def _attention(qn, kn, vt, fq, fk, slopes, lq1, lk1, lq2, lk2, subln_col, lam_init):
    B, S, _ = qn.shape
    tk = fk.shape[0]
    tq = fq.shape[1] // 2
    assert tq == 2 * tk
    nkv = S // tk
    const2 = lambda b, h, i, *_: (0, 0)
    G = ATTN_HEADS_PER_STEP
    kern = functools.partial(_attn_kernel, tq=tq, tk=tk, heads=G, lam_init=lam_init)
    hd = lq1.shape[1]
    return pl.pallas_call(
        kern,
        grid_spec=pltpu.PrefetchScalarGridSpec(
            num_scalar_prefetch=1,
            grid=(B, N_HEADS // G, S // tq),
            in_specs=[
                pl.BlockSpec((None, tq, G * LANES), lambda b, h, i, *_: (b, i, h)),
                pl.BlockSpec((None, S, G * LANES), lambda b, h, i, *_: (b, 0, h)),
                pl.BlockSpec((None, G, nkv, VT_ROWS, tk), lambda b, h, i, *_: (b, h, 0, 0, 0)),
                pl.BlockSpec((G, 2 * tq, LANES), lambda b, h, i, *_: (h, 0, 0)),
                pl.BlockSpec((tk, LANES), const2),
                pl.BlockSpec((1, hd), const2),
                pl.BlockSpec((1, hd), const2),
                pl.BlockSpec((1, hd), const2),
                pl.BlockSpec((1, hd), const2),
                pl.BlockSpec((LANES, 1), const2),
            ],
            out_specs=pl.BlockSpec((None, tq, G * LANES), lambda b, h, i, *_: (b, i, h)),
            scratch_shapes=[pltpu.VMEM((G, 2 * tq, 2 * LANES), bf16),
                            pltpu.VMEM((G, tk, 2 * tq), f32), pltpu.VMEM((G, tk, 2 * tq), f32),
                            pltpu.VMEM((G, tk, 2 * tq), bf16), pltpu.VMEM((G, tk, 2 * tq), bf16),
                            pltpu.VMEM((G, 1, 2 * tq), f32), pltpu.VMEM((G, 1, 2 * tq), f32),
                            pltpu.VMEM((G, VT_ROWS, 2 * tq), f32)],
        ),
        out_shape=jax.ShapeDtypeStruct((B, S, N_HEADS * LANES), bf16),
        compiler_params=pltpu.CompilerParams(dimension_semantics=("arbitrary", "arbitrary", "arbitrary"),
                                             vmem_limit_bytes=VMEM_LIMIT),
        name="diff_attn",
    )(slopes, qn, kn, vt, fq, fk, lq1, lk1, lq2, lk2, subln_col)


def _merge_kernel(pg_ref, ga_ref, o_ref, x_ref, wau_ref, wout_ref, g2_ref, wr_ref, br_ref,
                  x1_ref, hn_ref, eid_ref, wt_ref, rank_ref, cnt_ref, base_ref, *, tm, d_model):
    step = pl.program_id(0)

    @pl.when(step == 0)
    def _():
        base_ref[...] = jnp.zeros_like(base_ref)

    attn_out = _dot(o_ref[...], wau_ref[...])
    merged = pg_ref[...].astype(f32) + ga_ref[...].astype(f32) * attn_out
    x1 = x_ref[...] + _dot(merged.astype(bf16), wout_ref[...])
    x1_ref[...] = x1
    ms = jnp.mean(x1 * x1, axis=-1, keepdims=True)
    hn = x1 * lax.rsqrt(ms + EPS) * g2_ref[...]
    for s in range(d_model // LANES):
        hn_ref[pl.ds(s, tm, stride=SUBLANES), :] = hn[:, s * LANES:(s + 1) * LANES]

    logits = _dot_nt(wr_ref[...], hn.astype(bf16)) + br_ref[...]
    lg = logits[0:N_GROUPS]
    gmax = jnp.max(lg, axis=0, keepdims=True)
    p_top = 1.0 / jnp.sum(jnp.exp(lg - gmax), axis=0, keepdims=True)
    grow = lax.broadcasted_iota(jnp.int32, lg.shape, 0).astype(f32)
    g_idx = jnp.min(jnp.where(lg == gmax, grow, float(N_GROUPS)), axis=0, keepdims=True)

    sel = jnp.zeros((EXPERTS_PER_GROUP, tm), f32)
    for g in range(N_GROUPS):
        le_g = logits[SUBLANES + g * EXPERTS_PER_GROUP:SUBLANES + (g + 1) * EXPERTS_PER_GROUP]
        sel = jnp.where(g_idx == float(g), le_g, sel)
    erow = lax.broadcasted_iota(jnp.int32, sel.shape, 0).astype(f32)
    e1 = jnp.max(sel, axis=0, keepdims=True)
    i1 = jnp.min(jnp.where(sel == e1, erow, float(EXPERTS_PER_GROUP)), axis=0, keepdims=True)
    sel2 = jnp.where(erow == i1, -jnp.inf, sel)
    e2 = jnp.max(sel2, axis=0, keepdims=True)
    i2 = jnp.min(jnp.where(sel2 == e2, erow, float(EXPERTS_PER_GROUP)), axis=0, keepdims=True)
    r = jnp.exp(e2 - e1)
    w1 = p_top / (1.0 + r)
    w2 = p_top * r / (1.0 + r)
    eid1 = g_idx * float(EXPERTS_PER_GROUP) + i1
    eid2 = g_idx * float(EXPERTS_PER_GROUP) + i2
    eid_ref[...] = jnp.concatenate([eid1, eid2], axis=0).astype(jnp.int32)
    wt_ref[...] = jnp.concatenate([w1, w2], axis=0)

    xrow = lax.broadcasted_iota(jnp.int32, (N_EXPERTS, tm), 0).astype(f32)
    oh1 = jnp.where(xrow == eid1, 1.0, 0.0)
    oh2 = jnp.where(xrow == eid2, 1.0, 0.0)
    oh = oh1 + oh2
    a = lax.broadcasted_iota(jnp.int32, (tm, tm), 0)
    b = lax.broadcasted_iota(jnp.int32, (tm, tm), 1)
    upper = jnp.where(a <= b, 1.0, 0.0).astype(bf16)
    before = _dot(oh.astype(bf16), upper) + base_ref[...] - 1.0
    rank1 = jnp.sum(oh1 * before, axis=0, keepdims=True)
    rank2 = jnp.sum(oh2 * before, axis=0, keepdims=True)
    rank_ref[...] = jnp.concatenate([rank1, rank2], axis=0).astype(jnp.int32)
    base_ref[...] = base_ref[...] + jnp.sum(oh, axis=1, keepdims=True)
    cnt_ref[...] = jnp.broadcast_to(base_ref[...], cnt_ref.shape)


def _merge(pg, ga, o, x, w_attn_up, w_out, g2, wr, br):
    N, D = x.shape
    tm = min(ROW_TILE, N)
    aw = o.shape[1]
    const2 = lambda i: (0, 0)
    row = lambda i: (i, 0)
    colblk = lambda i: (0, i)
    kern = functools.partial(_merge_kernel, tm=tm, d_model=D)
    return pl.pallas_call(
        kern,
        grid=(N // tm,),
        in_specs=[
            pl.BlockSpec((tm, D), row),
            pl.BlockSpec((tm, D), row),
            pl.BlockSpec((tm, aw), row),
            pl.BlockSpec((tm, D), row),
            pl.BlockSpec((aw, D), const2),
            pl.BlockSpec((D, D), const2),
            pl.BlockSpec((1, D), const2),
            pl.BlockSpec((ROUTER_ROWS, D), const2),
            pl.BlockSpec((ROUTER_ROWS, 1), const2),
        ],
        out_specs=[
            pl.BlockSpec((tm, D), row),
            pl.BlockSpec((tm * SUBLANES, LANES), row),
            pl.BlockSpec((TOP_K, tm), colblk),
            pl.BlockSpec((TOP_K, tm), colblk),
            pl.BlockSpec((TOP_K, tm), colblk),
            pl.BlockSpec((N_EXPERTS, LANES), const2),
        ],
        out_shape=[
            jax.ShapeDtypeStruct((N, D), f32),
            jax.ShapeDtypeStruct((N * SUBLANES, LANES), f32),
            jax.ShapeDtypeStruct((TOP_K, N), jnp.int32),
            jax.ShapeDtypeStruct((TOP_K, N), f32),
            jax.ShapeDtypeStruct((TOP_K, N), jnp.int32),
            jax.ShapeDtypeStruct((N_EXPERTS, LANES), f32),
        ],
        scratch_shapes=[pltpu.VMEM((N_EXPERTS, 1), f32)],
        compiler_params=pltpu.CompilerParams(dimension_semantics=("arbitrary",), vmem_limit_bytes=VMEM_LIMIT),
        name="merge_router",
    )(pg, ga, o, x, w_attn_up, w_out, g2, wr, br)


def _slots_kernel(pstart_ref, eid_ref, rank_ref, dest_ref):
    eid = eid_ref[...]
    start = jnp.zeros_like(eid)
    for e in range(N_EXPERTS):
        start = jnp.where(eid == e, pstart_ref[e], start)
    dest_ref[...] = start + rank_ref[...]


def _slots(pstart, eid, rank):
    k, n = eid.shape
    tn = min(n, 8192)
    blk = pl.BlockSpec((k, tn), lambda i, *_: (0, i))
    return pl.pallas_call(
        _slots_kernel,
        grid_spec=pltpu.PrefetchScalarGridSpec(num_scalar_prefetch=1, grid=(n // tn,), in_specs=[blk, blk],
                                               out_specs=blk),
        out_shape=jax.ShapeDtypeStruct((k, n), jnp.int32),
        name="slots",
    )(pstart, eid, rank)


def _row(ref, r):
    return ref.at[pl.ds(pl.multiple_of(r * SUBLANES, SUBLANES), SUBLANES), :]


def _wait_bytes_of(ref_like, any_hbm, sem):
    n = ref_like.shape[0]
    pltpu.make_async_copy(any_hbm.at[pl.ds(0, n), :], any_hbm.at[pl.ds(0, n), :], sem).wait()


STAGES = 3


def _dispatch_kernel(pstart_ref, pend_ref, dest_ref, hn_hbm, xs_hbm, zero_ref, stage, in_sem, out_sem,
                     zsem, *, tt, rows, n_blocks):
    i = pl.program_id(0)
    n = pl.num_programs(0)

    @pl.when(i == 0)
    def _():
        zero_ref[...] = jnp.zeros_like(zero_ref)
        for e in range(N_EXPERTS):
            @pl.when(pend_ref[e] > pstart_ref[e])
            def _():
                pltpu.make_async_copy(zero_ref, xs_hbm.at[pl.ds(pl.multiple_of((pend_ref[e] - rows) * SUBLANES,
                                                                               SUBLANES), rows * SUBLANES), :],
                                      zsem).start()
        def zero_block(b, carry):
            pltpu.make_async_copy(zero_ref, xs_hbm.at[pl.ds(pl.multiple_of(b * (rows * SUBLANES), SUBLANES),
                                                            rows * SUBLANES), :], zsem).start()
            return carry

        def wait_block(b, carry):
            pltpu.make_async_copy(zero_ref, xs_hbm.at[pl.ds(0, rows * SUBLANES), :], zsem).wait()
            return carry

        first_unused = pend_ref[N_EXPERTS - 1] // rows
        lax.fori_loop(first_unused, n_blocks, zero_block, 0)
        for e in range(N_EXPERTS):
            @pl.when(pend_ref[e] > pstart_ref[e])
            def _():
                wait_block(0, 0)
        lax.fori_loop(first_unused, n_blocks, wait_block, 0)

    def stage_copy(step):
        s = step % STAGES
        src = hn_hbm.at[pl.ds(pl.multiple_of(step * (tt * SUBLANES), SUBLANES), tt * SUBLANES), :]
        return pltpu.make_async_copy(src, stage.at[s], in_sem.at[s])

    def wait_rows_of(step):
        _wait_bytes_of(zero_ref.at[pl.ds(0, TOP_K * tt * SUBLANES), :], hn_hbm, out_sem.at[step % STAGES])

    @pl.when(i == 0)
    def _():
        stage_copy(i).start()

    @pl.when(i >= STAGES - 1)
    def _():
        wait_rows_of(i - (STAGES - 1))

    @pl.when(i + 1 < n)
    def _():
        stage_copy(i + 1).start()

    stage_copy(i).wait()
    slot = i % STAGES

    def body(t, carry):
        for k in range(TOP_K):
            pltpu.make_async_copy(_row(stage.at[slot], t), _row(xs_hbm, dest_ref[0, k * tt + t]),
                                  out_sem.at[slot]).start(priority=k)
        return carry
    lax.fori_loop(0, tt, body, 0, unroll=8)

    @pl.when(i == n - 1)
    def _():
        for back in range(STAGES - 2, -1, -1):
            @pl.when(i - back >= 0)
            def _():
                wait_rows_of(i - back)


def _dispatch(pstart, pend, dest_blocks, hn_g, n_slots):
    n, _, width = dest_blocks.shape
    tt = width // TOP_K
    rows = EXPERT_ROWS
    assert rows >= TOP_K * tt
    kern = functools.partial(_dispatch_kernel, tt=tt, rows=rows, n_blocks=n_slots // rows)
    smem_blk = pl.BlockSpec((None, 1, width), lambda i, *_: (i, 0, 0), memory_space=pltpu.SMEM)
    return pl.pallas_call(
        kern,
        grid_spec=pltpu.PrefetchScalarGridSpec(
            num_scalar_prefetch=2,
            grid=(n,),
            in_specs=[smem_blk, pl.BlockSpec(memory_space=pl.ANY)],
            out_specs=pl.BlockSpec(memory_space=pl.ANY),
            scratch_shapes=[pltpu.VMEM((rows * SUBLANES, LANES), f32),
                            pltpu.VMEM((STAGES, tt * SUBLANES, LANES), f32),
                            pltpu.SemaphoreType.DMA((STAGES,)), pltpu.SemaphoreType.DMA((STAGES,)),
                            pltpu.SemaphoreType.DMA(())],
        ),
        out_shape=jax.ShapeDtypeStruct((n_slots * SUBLANES, LANES), f32),
        compiler_params=pltpu.CompilerParams(dimension_semantics=("arbitrary",), vmem_limit_bytes=VMEM_LIMIT,
                                             has_side_effects=True),
        name="dispatch",
    )(pstart, pend, dest_blocks, hn_g)


def _rows_from_token_major(buf, n_rows, n_tiles):
    return jnp.concatenate([buf[pl.ds(s, n_rows, stride=SUBLANES), :] for s in range(n_tiles)], axis=-1)


def _expert_kernel(beid_ref, nbu_ref, xs_ref, wg_ref, wu_ref, wd_ref, ys_ref, wg_bf, wu_bf, wd_bf, *, rows,
                   d_model):
    b = pl.program_id(0)
    changed = jnp.logical_or(b == 0, beid_ref[b] != beid_ref[jnp.maximum(b - 1, 0)])

    @pl.when(changed)
    def _():
        wg_bf[...] = wg_ref[...].astype(bf16)
        wu_bf[...] = wu_ref[...].astype(bf16)
        wd_bf[...] = wd_ref[...].astype(bf16)

    n_tiles = d_model // LANES

    @pl.when(b < nbu_ref[0])
    def _():
        xb = _rows_from_token_major(xs_ref, rows, n_tiles).astype(bf16)
        hdn = jax.nn.silu(_dot(xb, wg_bf[...])) * _dot(xb, wu_bf[...])
        y = _dot(hdn.astype(bf16), wd_bf[...])
        for s in range(n_tiles):
            ys_ref[pl.ds(s, rows, stride=SUBLANES), :] = y[:, s * LANES:(s + 1) * LANES]

    @pl.when(b >= nbu_ref[0])
    def _():
        ys_ref[...] = jnp.zeros_like(ys_ref)


def _experts(block_eid, n_used, xs, w_gate, w_up, w_down):
    n_blocks = block_eid.shape[0]
    rows = EXPERT_ROWS
    E, D, DE = w_gate.shape
    kern = functools.partial(_expert_kernel, rows=rows, d_model=D)
    xs_idx = lambda b, eid, nbu: (jnp.minimum(b, nbu[0] - 1), 0)
    return pl.pallas_call(
        kern,
        grid_spec=pltpu.PrefetchScalarGridSpec(
            num_scalar_prefetch=2,
            grid=(n_blocks,),
            in_specs=[
                pl.BlockSpec((rows * SUBLANES, LANES), xs_idx),
                pl.BlockSpec((None, D, DE), lambda b, eid, nbu: (eid[b], 0, 0)),
                pl.BlockSpec((None, D, DE), lambda b, eid, nbu: (eid[b], 0, 0)),
                pl.BlockSpec((None, DE, D), lambda b, eid, nbu: (eid[b], 0, 0)),
            ],
            out_specs=pl.BlockSpec((rows * SUBLANES, LANES), lambda b, *_: (b, 0)),
            scratch_shapes=[
                pltpu.VMEM((D, DE), bf16),
                pltpu.VMEM((D, DE), bf16),
                pltpu.VMEM((DE, D), bf16),
            ],
        ),
        out_shape=jax.ShapeDtypeStruct((n_blocks * rows * SUBLANES, LANES), f32),
        compiler_params=pltpu.CompilerParams(dimension_semantics=("arbitrary",), vmem_limit_bytes=VMEM_LIMIT),
        name="experts",
    )(block_eid, n_used, xs, w_gate, w_up, w_down)


def _combine_kernel(dest_ref, destn_ref, x1_ref, wt_ref, ys_hbm, out_ref, buf, sem, *, te, d_model):
    i = pl.program_id(0)
    n = pl.num_programs(0)
    slot = i % 2

    def start_gather(d_ref, dst, dsem):
        def body(t, carry):
            for k in range(TOP_K):
                r = k * te + t
                pltpu.make_async_copy(_row(ys_hbm, d_ref[0, r]), _row(dst, r), dsem).start(priority=k)
            return carry
        lax.fori_loop(0, te, body, 0, unroll=8)

    @pl.when(i == 0)
    def _():
        start_gather(dest_ref, buf.at[0], sem.at[0])

    @pl.when(i + 1 < n)
    def _():
        start_gather(destn_ref, buf.at[1 - slot], sem.at[1 - slot])

    _wait_bytes_of(buf.at[slot], ys_hbm, sem.at[slot])
    n_tiles = d_model // LANES
    both = _rows_from_token_major(buf.at[slot], TOP_K * te, n_tiles)
    wt = wt_ref[...]
    out_ref[...] = x1_ref[...] + (wt[:, 0:1] * both[:te] + wt[:, 1:2] * both[te:])


def _combine(dest_blocks, x1, wt_cols, ys):
    N, D = x1.shape
    n, _, width = dest_blocks.shape
    te = width // TOP_K
    kern = functools.partial(_combine_kernel, te=te, d_model=D)
    cur = pl.BlockSpec((None, 1, width), lambda i: (i, 0, 0), memory_space=pltpu.SMEM)
    nxt = pl.BlockSpec((None, 1, width), lambda i: (jnp.minimum(i + 1, n - 1), 0, 0), memory_space=pltpu.SMEM)
    return pl.pallas_call(
        kern,
        grid=(n,),
        in_specs=[
            cur, nxt,
            pl.BlockSpec((te, D), lambda i: (i, 0)),
            pl.BlockSpec((te, TOP_K), lambda i: (i, 0)),
            pl.BlockSpec(memory_space=pl.ANY),
        ],
        out_specs=pl.BlockSpec((te, D), lambda i: (i, 0)),
        scratch_shapes=[pltpu.VMEM((2, TOP_K * te * SUBLANES, LANES), f32), pltpu.SemaphoreType.DMA((2,))],
        out_shape=jax.ShapeDtypeStruct((N, D), f32),
        compiler_params=pltpu.CompilerParams(dimension_semantics=("arbitrary",), vmem_limit_bytes=VMEM_LIMIT),
        name="combine",
    )(dest_blocks, dest_blocks, x1, wt_cols, ys)


def _token_blocks(a, tt):
    k, n = a.shape
    return a.reshape(k, n // tt, tt).transpose(1, 0, 2).reshape(n // tt, 1, k * tt)


def _layer(x, l, norm1_g, w_in, pool_w, pool_scale, w_pool_up, q_norm_g, k_norm_g, lambda_q1, lambda_k1,
           lambda_q2, lambda_k2, subln_g, w_attn_up, w_out, norm2_g, w_router_group, b_router_group,
           w_router_expert, b_router_expert, w_expert_gate, w_expert_up, w_expert_down):
    B, S, D = x.shape
    N = B * S
    head_dim = q_norm_g.shape[0]
    lam_init = 0.8 - 0.6 * math.exp(-0.3 * l)
    reps = (N_HEADS * 2 * head_dim) // head_dim

    qg = (jnp.tile(q_norm_g, reps) * (head_dim ** -0.5 * LOG2E))[None, :]
    kg = jnp.tile(k_norm_g, reps)[None, :]
    pg, ga, qn, kn, vt = _in_proj(x, norm1_g[None, :], w_in.astype(bf16), pool_w.astype(bf16),
                                  pool_scale[None, :], w_pool_up.astype(bf16), qg, kg)

    tk = min(ATTN_TILE, S)
    slopes2 = jnp.asarray([2.0 ** (-8.0 * (h + 1) / N_HEADS) * LOG2E for h in range(N_HEADS)], f32)
    fq, fk = _alibi_features(slopes2, 2 * tk, tk)
    o = _attention(qn, kn, vt, fq, fk, slopes2, lambda_q1[None, :], lambda_k1[None, :], lambda_q2[None, :],
                   lambda_k2[None, :], subln_g[:, None], lam_init)

    wr = jnp.zeros((ROUTER_ROWS, D), f32)
    wr = wr.at[:N_GROUPS].set(w_router_group.T).at[SUBLANES:].set(w_router_expert.T).astype(bf16)
    br = jnp.zeros((ROUTER_ROWS, 1), f32)
    br = br.at[:N_GROUPS, 0].set(b_router_group).at[SUBLANES:, 0].set(b_router_expert)
    x1, hn_g, eid, wts, rank, cnt = _merge(pg.reshape(N, D), ga.reshape(N, D), o.reshape(N, -1),
                                           x.reshape(N, D), w_attn_up.astype(bf16), w_out.astype(bf16),
                                           norm2_g[None, :], wr, br)

    R = EXPERT_ROWS
    counts = cnt[:, 0].astype(jnp.int32)
    padded = (counts + R - 1) // R * R
    pend = jnp.cumsum(padded).astype(jnp.int32)
    pstart = pend - padded
    n_blocks = -(-(N * TOP_K) // R) + N_EXPERTS
    starts = jnp.arange(n_blocks, dtype=jnp.int32) * R
    block_eid = jnp.minimum(jnp.sum((pend[None, :] <= starts[:, None]).astype(jnp.int32), axis=1), N_EXPERTS - 1)
    n_used = pend[-1:] // R

    tt = min(MOVE_TILE, N)
    dest_blocks = _token_blocks(_slots(pstart, eid, rank), tt)
    xs = _dispatch(pstart, pend, dest_blocks, hn_g, n_blocks * R)
    ys = _experts(block_eid, n_used, xs, w_expert_gate, w_expert_up, w_expert_down)
    out = _combine(dest_blocks, x1, wts.T, ys)
    return out.reshape(B, S, D)


def kernel(x, norm1_g, w_in, pool_w, pool_scale, w_pool_up, q_norm_g, k_norm_g, lambda_q1, lambda_k1, lambda_q2,
           lambda_k2, subln_g, w_attn_up, w_out, norm2_g, w_router_group, b_router_group, w_router_expert,
           b_router_expert, w_expert_gate, w_expert_up, w_expert_down):
    params = (norm1_g, w_in, pool_w, pool_scale, w_pool_up, q_norm_g, k_norm_g, lambda_q1, lambda_k1, lambda_q2,
              lambda_k2, subln_g, w_attn_up, w_out, norm2_g, w_router_group, b_router_group, w_router_expert,
              b_router_expert, w_expert_gate, w_expert_up, w_expert_down)
    for l in range(norm1_g.shape[0]):
        x = _layer(x, l, *(p[l] for p in params))
    return x
```

```python
import functools
import math

import jax
import jax.numpy as jnp
from jax import lax
from jax.experimental import pallas as pl
from jax.experimental.pallas import tpu as pltpu

EPS = 1e-6
POOL_WINDOWS = (2, 4, 8, 16)
POOL_HALO = 16
N_HEADS = 4
N_GROUPS = 4
EXPERTS_PER_GROUP = 8
N_EXPERTS = N_GROUPS * EXPERTS_PER_GROUP
TOP_K = 2
LANES = 128
SUBLANES = 8
ROUTER_ROWS = 8 + N_EXPERTS
VT_ROWS = LANES + 16
LOG2E = 1.4426950408889634

ROW_TILE = 512
IN_PROJ_SUBTILES = 2
MERGE_SUBTILES = 2
ATTN_TILE = 256
ATTN_HEADS_PER_STEP = 4
EXPERT_ROWS = 512
MOVE_TILE = 256
VMEM_LIMIT = 52 * 1024 * 1024

f32 = jnp.float32
bf16 = jnp.bfloat16


def _dot(a, b):
    return jnp.dot(a, b, preferred_element_type=f32)


def _dot_nt(a, b):
    return lax.dot_general(a, b, (((1,), (1,)), ((), ())), preferred_element_type=f32)


def _half_lane_rmsnorm(t, n_tiles, rows):
    lane = lax.broadcasted_iota(jnp.int32, (rows, LANES), 1)
    lo_mask = lane < (LANES // 2)
    outs = []
    for i in range(n_tiles):
        c = t[:, i * LANES:(i + 1) * LANES]
        sq = c * c
        lo = jnp.sum(jnp.where(lo_mask, sq, 0.0), axis=-1, keepdims=True)
        hi = jnp.sum(jnp.where(lo_mask, 0.0, sq), axis=-1, keepdims=True)
        ms = jnp.where(lo_mask, lo, hi) * (2.0 / LANES)
        outs.append(c * lax.rsqrt(ms + EPS))
    return jnp.concatenate(outs, axis=-1)


def _in_proj_kernel(x_ref, g1_ref, win_ref, poolw_ref, pscale_ref, wpu_ref, qg_ref, kg_ref,
                    pg_ref, ga_ref, qn_ref, kn_ref, vt_ref, prev_ref, wfold_ref, *, tm, subs, tk, d_model, pool_width,
                    qk_width, attn_width):
    j = pl.program_id(1)
    off_q = pool_width
    off_k = off_q + qk_width
    off_v = off_k + qk_width
    off_gp = off_v + attn_width
    off_ga = off_gp + d_model
    group = pool_width // len(POOL_WINDOWS)
    half = d_model // 2
    n_tiles = qk_width // LANES

    @pl.when(jnp.logical_and(pl.program_id(0) == 0, j == 0))
    def _():
        for g in range(len(POOL_WINDOWS)):
            rows_g = slice(g * group, (g + 1) * group)
            scaled = (poolw_ref[g].astype(f32) * pscale_ref[:, rows_g]).astype(bf16)
            wfold_ref[rows_g, :] = _dot(scaled, wpu_ref[rows_g, :]).astype(bf16)

    @pl.when(j == 0)
    def _():
        prev_ref[...] = jnp.zeros_like(prev_ref)

    extra = lax.broadcasted_iota(jnp.int32, (VT_ROWS - LANES, tk), 0)
    ones_rows = jnp.where(extra == 0, 1.0, 0.0).astype(bf16)
    halo = prev_ref[...]
    for sub in range(subs):
        rows = slice(sub * tm, (sub + 1) * tm)
        x = x_ref[rows, :]
        ms = jnp.mean(x * x, axis=-1, keepdims=True)
        h = (x * lax.rsqrt(ms + EPS) * g1_ref[...]).astype(bf16)

        def proj(lo, width, h=h):
            return _dot(h, win_ref[:, lo:lo + width])

        u = proj(0, pool_width)
        v = proj(off_v, attn_width)
        for hh in range(attn_width // LANES):
            for c in range(tm // tk):
                blk = sub * (tm // tk) + c
                vt_ref[hh, blk, 0:LANES, :] = v[c * tk:(c + 1) * tk, hh * LANES:(hh + 1) * LANES].T.astype(bf16)
                vt_ref[hh, blk, LANES:VT_ROWS, :] = ones_rows
        qn_ref[rows, :] = (_half_lane_rmsnorm(proj(off_q, qk_width), n_tiles, tm) * qg_ref[...]).astype(bf16)
        kn_ref[rows, :] = (_half_lane_rmsnorm(proj(off_k, qk_width), n_tiles, tm) * kg_ref[...]).astype(bf16)

        ext = jnp.concatenate([halo, u], axis=0)
        halo = u[tm - POOL_HALO:, :]
        pos = (j * subs + sub) * tm + lax.broadcasted_iota(jnp.int32, (tm, 1), 0)
        ds = []
        for g, w in enumerate(POOL_WINDOWS):
            acc = ext[:, g * group:(g + 1) * group]
            shift = 1
            while shift < w:
                acc = acc + pltpu.roll(acc, shift, 0)
                shift *= 2
            wsum = acc[POOL_HALO:, :]
            cnt = jnp.minimum(pos + 1, w).astype(f32)
            ds.append((wsum / cnt - u[:, g * group:(g + 1) * group]).astype(bf16))
        pool_out = _dot(jnp.concatenate(ds, axis=-1), wfold_ref[...])

        for c in range(2):
            cols = slice(c * half, (c + 1) * half)
            gp = jax.nn.sigmoid(proj(off_gp + c * half, half))
            pg_ref[rows, cols] = (gp * pool_out[:, cols]).astype(bf16)
            ga_ref[rows, cols] = jax.nn.sigmoid(proj(off_ga + c * half, half)).astype(bf16)
    prev_ref[...] = halo


def _in_proj(x, g1, w_in, pool_w, pool_scale, w_pool_up, qg, kg):
    B, S, D = x.shape
    subs = IN_PROJ_SUBTILES if S % (IN_PROJ_SUBTILES * ROW_TILE) == 0 else 1
    tm = min(ROW_TILE, S)
    pool_width = w_pool_up.shape[0]
    qk_width = qg.shape[1]
    attn_width = qk_width
    in_width = w_in.shape[1]
    const2 = lambda b, j: (0, 0)
    row = lambda b, j: (b, j, 0)
    tk = min(ATTN_TILE, S)
    n_heads = attn_width // LANES
    kern = functools.partial(_in_proj_kernel, tm=tm, subs=subs, tk=tk, d_model=D, pool_width=pool_width,
                             qk_width=qk_width, attn_width=attn_width)
    tb = subs * tm
    return pl.pallas_call(
        kern,
        grid=(B, S // tb),
        in_specs=[
            pl.BlockSpec((None, tb, D), row),
            pl.BlockSpec((1, D), const2),
            pl.BlockSpec((D, in_width), const2, pipeline_mode=pl.Buffered(1)),
            pl.BlockSpec(pool_w.shape, lambda b, j: (0, 0, 0)),
            pl.BlockSpec((1, pool_width), const2),
            pl.BlockSpec((pool_width, D), const2),
            pl.BlockSpec((1, qk_width), const2),
            pl.BlockSpec((1, qk_width), const2),
        ],
        out_specs=[
            pl.BlockSpec((None, tb, D), row),
            pl.BlockSpec((None, tb, D), row),
            pl.BlockSpec((None, tb, qk_width), row),
            pl.BlockSpec((None, tb, qk_width), row),
            pl.BlockSpec((None, n_heads, tb // tk, VT_ROWS, tk), lambda b, j: (b, 0, j, 0, 0)),
        ],
        out_shape=[
            jax.ShapeDtypeStruct((B, S, D), bf16),
            jax.ShapeDtypeStruct((B, S, D), bf16),
            jax.ShapeDtypeStruct((B, S, qk_width), bf16),
            jax.ShapeDtypeStruct((B, S, qk_width), bf16),
            jax.ShapeDtypeStruct((B, n_heads, S // tk, VT_ROWS, tk), bf16),
        ],
        scratch_shapes=[pltpu.VMEM((POOL_HALO, pool_width), f32), pltpu.VMEM((pool_width, D), bf16)],
        compiler_params=pltpu.CompilerParams(dimension_semantics=("arbitrary", "arbitrary"),
                                             vmem_limit_bytes=VMEM_LIMIT),
        name="in_proj",
    )(x, g1, w_in, pool_w, pool_scale, w_pool_up, qg, kg)


def _attn_kernel(slopes_ref, q_ref, k_ref, vt_ref, fq_ref, fk_ref, lq1_ref, lk1_ref, lq2_ref, lk2_ref, sg_ref,
                 o_ref, qa_ref, sa_ref, sb_ref, pa_ref, pb_ref, m_ref, alpha_ref, acc_ref, *,
                 tq, tk, heads, lam_init):
    hg = pl.program_id(1)
    i = pl.program_id(2)
    fk = fk_ref[...]
    lane = lax.broadcasted_iota(jnp.int32, (tq, LANES), 1)
    first = lane < (LANES // 2)
    hs = range(heads)

    buf_a = (sa_ref, pa_ref)
    buf_b = (sb_ref, pb_ref)

    def scores(g, n, buf):
        kb = k_ref[pl.ds(pl.multiple_of(n * tk, tk), tk), g * LANES:(g + 1) * LANES]
        buf[0][g] = _dot_nt(jnp.concatenate([kb, fk], axis=1), qa_ref[g])

    def softmax(g, n, buf, key_offset=None):
        s_ref, p_ref = buf
        c = -slopes_ref[hg * heads + g] * (i * tq - n * tk).astype(f32)
        s = s_ref[g]
        if key_offset is not None:
            kk = lax.broadcasted_iota(jnp.int32, s.shape, 0) + key_offset
            qq = lax.broadcasted_iota(jnp.int32, s.shape, 1)
            qq = jnp.where(qq >= tq, qq - tq, qq)
            s = jnp.where(kk <= qq, s, -jnp.inf)
        m_old = m_ref[g]
        m_new = jnp.maximum(m_old, jnp.max(s, axis=0, keepdims=True) + c)
        p_ref[g] = jnp.exp2(s - (m_new - c)).astype(bf16)
        m_ref[g] = m_new
        return jnp.exp2(m_old - m_new)

    def pv(g, n, buf, alpha):
        acc_ref[g] = alpha * acc_ref[g] + _dot(vt_ref[g, jnp.maximum(n, 0)], buf[1][g])

    m_ref[...] = jnp.full(m_ref.shape, -jnp.inf, f32)
    alpha_ref[...] = jnp.ones_like(alpha_ref)
    acc_ref[...] = jnp.zeros_like(acc_ref)
    pb_ref[...] = jnp.zeros_like(pb_ref)
    zero = jnp.zeros((tq, LANES), bf16)
    for g in hs:
        q = q_ref[:, g * LANES:(g + 1) * LANES]
        qs = jnp.concatenate([jnp.where(first, q, zero), jnp.where(first, zero, q)], axis=0)
        qa_ref[g] = jnp.concatenate([qs, fq_ref[g]], axis=1)
    for g in hs:
        scores(g, 0, buf_a)

    def pair(t, carry):
        n = 2 * t
        for g in hs:
            scores(g, n + 1, buf_b)
            a_even = softmax(g, n, buf_a)
            pv(g, n - 1, buf_b, alpha_ref[g])
            scores(g, n + 2, buf_a)
            alpha_ref[g] = softmax(g, n + 1, buf_b)
            pv(g, n, buf_a, a_even)
        return carry

    lax.fori_loop(0, i, pair, 0)
    n = 2 * i
    lam = (jnp.exp(jnp.sum(lq1_ref[...] * lk1_ref[...], keepdims=True))
           - jnp.exp(jnp.sum(lq2_ref[...] * lk2_ref[...], keepdims=True)) + lam_init)
    for g in hs:
        scores(g, n + 1, buf_b)
        a_even = softmax(g, n, buf_a, key_offset=0)
        pv(g, n - 1, buf_b, alpha_ref[g])
        a_odd = softmax(g, n + 1, buf_b, key_offset=tk)
        pv(g, n, buf_a, a_even)
        pv(g, n + 1, buf_b, a_odd)
        acc = acc_ref[g]
        o_all = acc[0:LANES] / acc[LANES:LANES + 1]
        o = o_all[:, :tq] - lam * o_all[:, tq:]
        ms = jnp.mean(o * o, axis=0, keepdims=True)
        on = o * lax.rsqrt(ms + EPS) * sg_ref[...] * (1.0 - lam_init)
        o_ref[:, g * LANES:(g + 1) * LANES] = on.T.astype(bf16)


def _split_bf16(x, pieces=3):
    out = []
    for _ in range(pieces):
        p = x.astype(bf16)
        out.append(p)
        x = x - p.astype(f32)
    return out


def _alibi_features(slopes2, tq, tk):
    assert tk <= 256
    ones = jnp.ones((tk,), bf16)
    krel = jnp.arange(tk, dtype=f32).astype(bf16)
    fk = jnp.zeros((tk, LANES), bf16)
    for c in range(3):
        fk = fk.at[:, c].set(krel).at[:, 3 + c].set(ones)
    qrel = jnp.arange(tq, dtype=f32)
    fq = jnp.zeros((slopes2.shape[0], 2 * tq, LANES), bf16)
    a = _split_bf16(slopes2)
    b = _split_bf16(-slopes2[:, None] * jnp.concatenate([qrel, qrel])[None, :])
    for c in range(3):
        fq = fq.at[:, :, c].set(a[c][:, None]).at[:, :, 3 + c].set(b[c])
    return fq, fk


def _attention(qn, kn, vt, fq, fk, slopes, lq1, lk1, lq2, lk2, subln_col, lam_init):
    B, S, _ = qn.shape
    tk = fk.shape[0]
    tq = fq.shape[1] // 2
    assert tq == 2 * tk
    nkv = S // tk
    const2 = lambda b, h, i, *_: (0, 0)
    G = ATTN_HEADS_PER_STEP
    kern = functools.partial(_attn_kernel, tq=tq, tk=tk, heads=G, lam_init=lam_init)
    hd = lq1.shape[1]
    return pl.pallas_call(
        kern,
        grid_spec=pltpu.PrefetchScalarGridSpec(
            num_scalar_prefetch=1,
            grid=(B, N_HEADS // G, S // tq),
            in_specs=[
                pl.BlockSpec((None, tq, G * LANES), lambda b, h, i, *_: (b, i, h)),
                pl.BlockSpec((None, S, G * LANES), lambda b, h, i, *_: (b, 0, h)),
                pl.BlockSpec((None, G, nkv, VT_ROWS, tk), lambda b, h, i, *_: (b, h, 0, 0, 0)),
                pl.BlockSpec((G, 2 * tq, LANES), lambda b, h, i, *_: (h, 0, 0)),
                pl.BlockSpec((tk, LANES), const2),
                pl.BlockSpec((1, hd), const2),
                pl.BlockSpec((1, hd), const2),
                pl.BlockSpec((1, hd), const2),
                pl.BlockSpec((1, hd), const2),
                pl.BlockSpec((LANES, 1), const2),
            ],
            out_specs=pl.BlockSpec((None, tq, G * LANES), lambda b, h, i, *_: (b, i, h)),
            scratch_shapes=[pltpu.VMEM((G, 2 * tq, 2 * LANES), bf16),
                            pltpu.VMEM((G, tk, 2 * tq), f32), pltpu.VMEM((G, tk, 2 * tq), f32),
                            pltpu.VMEM((G, tk, 2 * tq), bf16), pltpu.VMEM((G, tk, 2 * tq), bf16),
                            pltpu.VMEM((G, 1, 2 * tq), f32), pltpu.VMEM((G, 1, 2 * tq), f32),
                            pltpu.VMEM((G, VT_ROWS, 2 * tq), f32)],
        ),
        out_shape=jax.ShapeDtypeStruct((B, S, N_HEADS * LANES), bf16),
        compiler_params=pltpu.CompilerParams(dimension_semantics=("arbitrary", "arbitrary", "arbitrary"),
                                             vmem_limit_bytes=VMEM_LIMIT),
        name="diff_attn",
    )(slopes, qn, kn, vt, fq, fk, lq1, lk1, lq2, lk2, subln_col)


def _merge_kernel(pg_ref, ga_ref, o_ref, x_ref, wau_ref, wout_ref, g2_ref, wr_ref, br_ref,
                  x1_ref, hn_ref, eid_ref, wt_ref, rank_ref, cnt_ref, base_ref, *, tm, subs, d_model):
    step = pl.program_id(0)

    @pl.when(step == 0)
    def _():
        base_ref[...] = jnp.zeros_like(base_ref)

    a = lax.broadcasted_iota(jnp.int32, (tm, tm), 0)
    b = lax.broadcasted_iota(jnp.int32, (tm, tm), 1)
    upper = jnp.where(a <= b, 1.0, 0.0).astype(bf16)
    base = base_ref[...]
    all_logits = []
    for sub in range(subs):
        rows = slice(sub * tm, (sub + 1) * tm)
        attn_out = _dot(o_ref[rows, :], wau_ref[...])
        merged = pg_ref[rows, :].astype(f32) + ga_ref[rows, :].astype(f32) * attn_out
        x1 = x_ref[rows, :] + _dot(merged.astype(bf16), wout_ref[...])
        x1_ref[rows, :] = x1
        ms = jnp.mean(x1 * x1, axis=-1, keepdims=True)
        hn = x1 * lax.rsqrt(ms + EPS) * g2_ref[...]
        for s in range(d_model // LANES):
            hn_ref[pl.ds(sub * tm * SUBLANES + s, tm, stride=SUBLANES), :] = hn[:, s * LANES:(s + 1) * LANES]

        all_logits.append(_dot_nt(wr_ref[...], hn.astype(bf16)) + br_ref[...])

    for sub in range(subs):
        rows = slice(sub * tm, (sub + 1) * tm)
        logits = all_logits[sub]
        lg = logits[0:N_GROUPS]
        gmax = jnp.max(lg, axis=0, keepdims=True)
        p_top = 1.0 / jnp.sum(jnp.exp(lg - gmax), axis=0, keepdims=True)
        grow = lax.broadcasted_iota(jnp.int32, lg.shape, 0).astype(f32)
        g_idx = jnp.min(jnp.where(lg == gmax, grow, float(N_GROUPS)), axis=0, keepdims=True)

        sel = jnp.zeros((EXPERTS_PER_GROUP, tm), f32)
        for g in range(N_GROUPS):
            le_g = logits[SUBLANES + g * EXPERTS_PER_GROUP:SUBLANES + (g + 1) * EXPERTS_PER_GROUP]
            sel = jnp.where(g_idx == float(g), le_g, sel)
        erow = lax.broadcasted_iota(jnp.int32, sel.shape, 0).astype(f32)
        e1 = jnp.max(sel, axis=0, keepdims=True)
        i1 = jnp.min(jnp.where(sel == e1, erow, float(EXPERTS_PER_GROUP)), axis=0, keepdims=True)
        sel2 = jnp.where(erow == i1, -jnp.inf, sel)
        e2 = jnp.max(sel2, axis=0, keepdims=True)
        i2 = jnp.min(jnp.where(sel2 == e2, erow, float(EXPERTS_PER_GROUP)), axis=0, keepdims=True)
        r = jnp.exp(e2 - e1)
        w1 = p_top / (1.0 + r)
        w2 = p_top * r / (1.0 + r)
        eid1 = g_idx * float(EXPERTS_PER_GROUP) + i1
        eid2 = g_idx * float(EXPERTS_PER_GROUP) + i2
        eid_ref[:, rows] = jnp.concatenate([eid1, eid2], axis=0).astype(jnp.int32)
        wt_ref[:, rows] = jnp.concatenate([w1, w2], axis=0)

        xrow = lax.broadcasted_iota(jnp.int32, (N_EXPERTS, tm), 0).astype(f32)
        oh1 = jnp.where(xrow == eid1, 1.0, 0.0)
        oh2 = jnp.where(xrow == eid2, 1.0, 0.0)
        oh = oh1 + oh2
        before = _dot(oh.astype(bf16), upper) + base - 1.0
        rank1 = jnp.sum(oh1 * before, axis=0, keepdims=True)
        rank2 = jnp.sum(oh2 * before, axis=0, keepdims=True)
        rank_ref[:, rows] = jnp.concatenate([rank1, rank2], axis=0).astype(jnp.int32)
        base = base + jnp.sum(oh, axis=1, keepdims=True)
    base_ref[...] = base
    cnt_ref[...] = jnp.broadcast_to(base, cnt_ref.shape)


def _merge(pg, ga, o, x, w_attn_up, w_out, g2, wr, br):
    N, D = x.shape
    tm = min(ROW_TILE, N)
    subs = MERGE_SUBTILES if N % (MERGE_SUBTILES * tm) == 0 else 1
    tb = subs * tm
    aw = o.shape[1]
    const2 = lambda i: (0, 0)
    row = lambda i: (i, 0)
    colblk = lambda i: (0, i)
    kern = functools.partial(_merge_kernel, tm=tm, subs=subs, d_model=D)
    return pl.pallas_call(
        kern,
        grid=(N // tb,),
        in_specs=[
            pl.BlockSpec((tb, D), row),
            pl.BlockSpec((tb, D), row),
            pl.BlockSpec((tb, aw), row),
            pl.BlockSpec((tb, D), row),
            pl.BlockSpec((aw, D), const2),
            pl.BlockSpec((D, D), const2),
            pl.BlockSpec((1, D), const2),
            pl.BlockSpec((ROUTER_ROWS, D), const2),
            pl.BlockSpec((ROUTER_ROWS, 1), const2),
        ],
        out_specs=[
            pl.BlockSpec((tb, D), row),
            pl.BlockSpec((tb * SUBLANES, LANES), row),
            pl.BlockSpec((TOP_K, tb), colblk),
            pl.BlockSpec((TOP_K, tb), colblk),
            pl.BlockSpec((TOP_K, tb), colblk),
            pl.BlockSpec((N_EXPERTS, LANES), const2),
        ],
        out_shape=[
            jax.ShapeDtypeStruct((N, D), f32),
            jax.ShapeDtypeStruct((N * SUBLANES, LANES), f32),
            jax.ShapeDtypeStruct((TOP_K, N), jnp.int32),
            jax.ShapeDtypeStruct((TOP_K, N), f32),
            jax.ShapeDtypeStruct((TOP_K, N), jnp.int32),
            jax.ShapeDtypeStruct((N_EXPERTS, LANES), f32),
        ],
        scratch_shapes=[pltpu.VMEM((N_EXPERTS, 1), f32)],
        compiler_params=pltpu.CompilerParams(dimension_semantics=("arbitrary",), vmem_limit_bytes=VMEM_LIMIT),
        name="merge_router",
    )(pg, ga, o, x, w_attn_up, w_out, g2, wr, br)


def _slots_kernel(pstart_ref, eid_ref, rank_ref, dest_ref):
    eid = eid_ref[...]
    start = jnp.zeros_like(eid)
    for e in range(N_EXPERTS):
        start = jnp.where(eid == e, pstart_ref[e], start)
    dest_ref[...] = start + rank_ref[...]


def _slots(pstart, eid, rank):
    k, n = eid.shape
    tn = min(n, 8192)
    blk = pl.BlockSpec((k, tn), lambda i, *_: (0, i))
    return pl.pallas_call(
        _slots_kernel,
        grid_spec=pltpu.PrefetchScalarGridSpec(num_scalar_prefetch=1, grid=(n // tn,), in_specs=[blk, blk],
                                               out_specs=blk),
        out_shape=jax.ShapeDtypeStruct((k, n), jnp.int32),
        name="slots",
    )(pstart, eid, rank)


def _row(ref, r):
    return ref.at[pl.ds(pl.multiple_of(r * SUBLANES, SUBLANES), SUBLANES), :]


def _wait_bytes_of(ref_like, any_hbm, sem):
    n = ref_like.shape[0]
    pltpu.make_async_copy(any_hbm.at[pl.ds(0, n), :], any_hbm.at[pl.ds(0, n), :], sem).wait()


STAGES = 3


def _dispatch_kernel(pstart_ref, pend_ref, dest_ref, hn_hbm, xs_hbm, zero_ref, stage, in_sem, out_sem,
                     zsem, *, tt, rows, n_blocks):
    i = pl.program_id(0)
    n = pl.num_programs(0)

    @pl.when(i == 0)
    def _():
        zero_ref[...] = jnp.zeros_like(zero_ref)
        for e in range(N_EXPERTS):
            @pl.when(pend_ref[e] > pstart_ref[e])
            def _():
                pltpu.make_async_copy(zero_ref, xs_hbm.at[pl.ds(pl.multiple_of((pend_ref[e] - rows) * SUBLANES,
                                                                               SUBLANES), rows * SUBLANES), :],
                                      zsem).start()
        def zero_block(b, carry):
            pltpu.make_async_copy(zero_ref, xs_hbm.at[pl.ds(pl.multiple_of(b * (rows * SUBLANES), SUBLANES),
                                                            rows * SUBLANES), :], zsem).start()
            return carry

        def wait_block(b, carry):
            pltpu.make_async_copy(zero_ref, xs_hbm.at[pl.ds(0, rows * SUBLANES), :], zsem).wait()
            return carry

        first_unused = pend_ref[N_EXPERTS - 1] // rows
        lax.fori_loop(first_unused, n_blocks, zero_block, 0)
        for e in range(N_EXPERTS):
            @pl.when(pend_ref[e] > pstart_ref[e])
            def _():
                wait_block(0, 0)
        lax.fori_loop(first_unused, n_blocks, wait_block, 0)

    def stage_copy(step):
        s = step % STAGES
        src = hn_hbm.at[pl.ds(pl.multiple_of(step * (tt * SUBLANES), SUBLANES), tt * SUBLANES), :]
        return pltpu.make_async_copy(src, stage.at[s], in_sem.at[s])

    def wait_rows_of(step):
        _wait_bytes_of(zero_ref.at[pl.ds(0, TOP_K * tt * SUBLANES), :], hn_hbm, out_sem.at[step % STAGES])

    @pl.when(i == 0)
    def _():
        stage_copy(i).start()

    @pl.when(i >= STAGES - 1)
    def _():
        wait_rows_of(i - (STAGES - 1))

    @pl.when(i + 1 < n)
    def _():
        stage_copy(i + 1).start()

    stage_copy(i).wait()
    slot = i % STAGES

    def body(t, carry):
        for k in range(TOP_K):
            pltpu.make_async_copy(_row(stage.at[slot], t), _row(xs_hbm, dest_ref[0, k * tt + t]),
                                  out_sem.at[slot]).start(priority=k)
        return carry
    lax.fori_loop(0, tt, body, 0, unroll=8)

    @pl.when(i == n - 1)
    def _():
        for back in range(STAGES - 2, -1, -1):
            @pl.when(i - back >= 0)
            def _():
                wait_rows_of(i - back)


def _dispatch(pstart, pend, dest_blocks, hn_g, n_slots):
    n, _, width = dest_blocks.shape
    tt = width // TOP_K
    rows = EXPERT_ROWS
    assert rows >= TOP_K * tt
    kern = functools.partial(_dispatch_kernel, tt=tt, rows=rows, n_blocks=n_slots // rows)
    smem_blk = pl.BlockSpec((None, 1, width), lambda i, *_: (i, 0, 0), memory_space=pltpu.SMEM)
    return pl.pallas_call(
        kern,
        grid_spec=pltpu.PrefetchScalarGridSpec(
            num_scalar_prefetch=2,
            grid=(n,),
            in_specs=[smem_blk, pl.BlockSpec(memory_space=pl.ANY)],
            out_specs=pl.BlockSpec(memory_space=pl.ANY),
            scratch_shapes=[pltpu.VMEM((rows * SUBLANES, LANES), f32),
                            pltpu.VMEM((STAGES, tt * SUBLANES, LANES), f32),
                            pltpu.SemaphoreType.DMA((STAGES,)), pltpu.SemaphoreType.DMA((STAGES,)),
                            pltpu.SemaphoreType.DMA(())],
        ),
        out_shape=jax.ShapeDtypeStruct((n_slots * SUBLANES, LANES), f32),
        compiler_params=pltpu.CompilerParams(dimension_semantics=("arbitrary",), vmem_limit_bytes=VMEM_LIMIT,
                                             has_side_effects=True),
        name="dispatch",
    )(pstart, pend, dest_blocks, hn_g)


def _rows_from_token_major(buf, n_rows, n_tiles):
    return jnp.concatenate([buf[pl.ds(s, n_rows, stride=SUBLANES), :] for s in range(n_tiles)], axis=-1)


def _expert_kernel(beid_ref, nbu_ref, xs_ref, wg_ref, wu_ref, wd_ref, ys_ref, wg_bf, wu_bf, wd_bf, *, rows,
                   d_model):
    b = pl.program_id(0)
    changed = jnp.logical_or(b == 0, beid_ref[b] != beid_ref[jnp.maximum(b - 1, 0)])

    @pl.when(changed)
    def _():
        wg_bf[...] = wg_ref[...].astype(bf16)
        wu_bf[...] = wu_ref[...].astype(bf16)
        wd_bf[...] = wd_ref[...].astype(bf16)

    n_tiles = d_model // LANES

    @pl.when(b < nbu_ref[0])
    def _():
        xb = _rows_from_token_major(xs_ref, rows, n_tiles).astype(bf16)
        hdn = jax.nn.silu(_dot(xb, wg_bf[...])) * _dot(xb, wu_bf[...])
        y = _dot(hdn.astype(bf16), wd_bf[...])
        for s in range(n_tiles):
            ys_ref[pl.ds(s, rows, stride=SUBLANES), :] = y[:, s * LANES:(s + 1) * LANES]

    @pl.when(b >= nbu_ref[0])
    def _():
        ys_ref[...] = jnp.zeros_like(ys_ref)


def _experts(block_eid, n_used, xs, w_gate, w_up, w_down):
    n_blocks = block_eid.shape[0]
    rows = EXPERT_ROWS
    E, D, DE = w_gate.shape
    kern = functools.partial(_expert_kernel, rows=rows, d_model=D)
    xs_idx = lambda b, eid, nbu: (jnp.minimum(b, nbu[0] - 1), 0)
    return pl.pallas_call(
        kern,
        grid_spec=pltpu.PrefetchScalarGridSpec(
            num_scalar_prefetch=2,
            grid=(n_blocks,),
            in_specs=[
                pl.BlockSpec((rows * SUBLANES, LANES), xs_idx),
                pl.BlockSpec((None, D, DE), lambda b, eid, nbu: (eid[b], 0, 0)),
                pl.BlockSpec((None, D, DE), lambda b, eid, nbu: (eid[b], 0, 0)),
                pl.BlockSpec((None, DE, D), lambda b, eid, nbu: (eid[b], 0, 0)),
            ],
            out_specs=pl.BlockSpec((rows * SUBLANES, LANES), lambda b, *_: (b, 0)),
            scratch_shapes=[
                pltpu.VMEM((D, DE), bf16),
                pltpu.VMEM((D, DE), bf16),
                pltpu.VMEM((DE, D), bf16),
            ],
        ),
        out_shape=jax.ShapeDtypeStruct((n_blocks * rows * SUBLANES, LANES), f32),
        compiler_params=pltpu.CompilerParams(dimension_semantics=("arbitrary",), vmem_limit_bytes=VMEM_LIMIT),
        name="experts",
    )(block_eid, n_used, xs, w_gate, w_up, w_down)


def _combine_kernel(dest_ref, destn_ref, x1_ref, wt_ref, ys_hbm, out_ref, buf, sem, *, te, d_model):
    i = pl.program_id(0)
    n = pl.num_programs(0)
    slot = i % 2

    def start_gather(d_ref, dst, dsem):
        def body(t, carry):
            for k in range(TOP_K):
                r = k * te + t
                pltpu.make_async_copy(_row(ys_hbm, d_ref[0, r]), _row(dst, r), dsem).start(priority=k)
            return carry
        lax.fori_loop(0, te, body, 0, unroll=8)

    @pl.when(i == 0)
    def _():
        start_gather(dest_ref, buf.at[0], sem.at[0])

    @pl.when(i + 1 < n)
    def _():
        start_gather(destn_ref, buf.at[1 - slot], sem.at[1 - slot])

    _wait_bytes_of(buf.at[slot], ys_hbm, sem.at[slot])
    n_tiles = d_model // LANES
    both = _rows_from_token_major(buf.at[slot], TOP_K * te, n_tiles)
    wt = wt_ref[...]
    out_ref[...] = x1_ref[...] + (wt[:, 0:1] * both[:te] + wt[:, 1:2] * both[te:])


def _combine(dest_blocks, x1, wt_cols, ys):
    N, D = x1.shape
    n, _, width = dest_blocks.shape
    te = width // TOP_K
    kern = functools.partial(_combine_kernel, te=te, d_model=D)
    cur = pl.BlockSpec((None, 1, width), lambda i: (i, 0, 0), memory_space=pltpu.SMEM)
    nxt = pl.BlockSpec((None, 1, width), lambda i: (jnp.minimum(i + 1, n - 1), 0, 0), memory_space=pltpu.SMEM)
    return pl.pallas_call(
        kern,
        grid=(n,),
        in_specs=[
            cur, nxt,
            pl.BlockSpec((te, D), lambda i: (i, 0)),
            pl.BlockSpec((te, TOP_K), lambda i: (i, 0)),
            pl.BlockSpec(memory_space=pl.ANY),
        ],
        out_specs=pl.BlockSpec((te, D), lambda i: (i, 0)),
        scratch_shapes=[pltpu.VMEM((2, TOP_K * te * SUBLANES, LANES), f32), pltpu.SemaphoreType.DMA((2,))],
        out_shape=jax.ShapeDtypeStruct((N, D), f32),
        compiler_params=pltpu.CompilerParams(dimension_semantics=("arbitrary",), vmem_limit_bytes=VMEM_LIMIT),
        name="combine",
    )(dest_blocks, dest_blocks, x1, wt_cols, ys)


def _token_blocks(a, tt):
    k, n = a.shape
    return a.reshape(k, n // tt, tt).transpose(1, 0, 2).reshape(n // tt, 1, k * tt)


def _layer(x, l, norm1_g, w_in, pool_w, pool_scale, w_pool_up, q_norm_g, k_norm_g, lambda_q1, lambda_k1,
           lambda_q2, lambda_k2, subln_g, w_attn_up, w_out, norm2_g, w_router_group, b_router_group,
           w_router_expert, b_router_expert, w_expert_gate, w_expert_up, w_expert_down):
    B, S, D = x.shape
    N = B * S
    head_dim = q_norm_g.shape[0]
    lam_init = 0.8 - 0.6 * math.exp(-0.3 * l)
    reps = (N_HEADS * 2 * head_dim) // head_dim

    qg = (jnp.tile(q_norm_g, reps) * (head_dim ** -0.5 * LOG2E))[None, :]
    kg = jnp.tile(k_norm_g, reps)[None, :]
    pg, ga, qn, kn, vt = _in_proj(x, norm1_g[None, :], w_in.astype(bf16), pool_w.astype(bf16),
                                  pool_scale[None, :], w_pool_up.astype(bf16), qg, kg)

    tk = min(ATTN_TILE, S)
    slopes2 = jnp.asarray([2.0 ** (-8.0 * (h + 1) / N_HEADS) * LOG2E for h in range(N_HEADS)], f32)
    fq, fk = _alibi_features(slopes2, 2 * tk, tk)
    o = _attention(qn, kn, vt, fq, fk, slopes2, lambda_q1[None, :], lambda_k1[None, :], lambda_q2[None, :],
                   lambda_k2[None, :], subln_g[:, None], lam_init)

    wr = jnp.zeros((ROUTER_ROWS, D), f32)
    wr = wr.at[:N_GROUPS].set(w_router_group.T).at[SUBLANES:].set(w_router_expert.T).astype(bf16)
    br = jnp.zeros((ROUTER_ROWS, 1), f32)
    br = br.at[:N_GROUPS, 0].set(b_router_group).at[SUBLANES:, 0].set(b_router_expert)
    x1, hn_g, eid, wts, rank, cnt = _merge(pg.reshape(N, D), ga.reshape(N, D), o.reshape(N, -1),
                                           x.reshape(N, D), w_attn_up.astype(bf16), w_out.astype(bf16),
                                           norm2_g[None, :], wr, br)

    R = EXPERT_ROWS
    counts = cnt[:, 0].astype(jnp.int32)
    padded = (counts + R - 1) // R * R
    pend = jnp.cumsum(padded).astype(jnp.int32)
    pstart = pend - padded
    n_blocks = -(-(N * TOP_K) // R) + N_EXPERTS
    starts = jnp.arange(n_blocks, dtype=jnp.int32) * R
    block_eid = jnp.minimum(jnp.sum((pend[None, :] <= starts[:, None]).astype(jnp.int32), axis=1), N_EXPERTS - 1)
    n_used = pend[-1:] // R

    tt = min(MOVE_TILE, N)
    dest_blocks = _token_blocks(_slots(pstart, eid, rank), tt)
    xs = _dispatch(pstart, pend, dest_blocks, hn_g, n_blocks * R)
    ys = _experts(block_eid, n_used, xs, w_expert_gate, w_expert_up, w_expert_down)
    out = _combine(dest_blocks, x1, wts.T, ys)
    return out.reshape(B, S, D)


def kernel(x, norm1_g, w_in, pool_w, pool_scale, w_pool_up, q_norm_g, k_norm_g, lambda_q1, lambda_k1, lambda_q2,
           lambda_k2, subln_g, w_attn_up, w_out, norm2_g, w_router_group, b_router_group, w_router_expert,
           b_router_expert, w_expert_gate, w_expert_up, w_expert_down):
    params = (norm1_g, w_in, pool_w, pool_scale, w_pool_up, q_norm_g, k_norm_g, lambda_q1, lambda_k1, lambda_q2,
              lambda_k2, subln_g, w_attn_up, w_out, norm2_g, w_router_group, b_router_group, w_router_expert,
              b_router_expert, w_expert_gate, w_expert_up, w_expert_down)
    for l in range(norm1_g.shape[0]):
        x = _layer(x, l, *(p[l] for p in params))
    return x
```

```python
import functools
import math

import jax
import jax.numpy as jnp
from jax import lax
from jax.experimental import pallas as pl
from jax.experimental.pallas import tpu as pltpu

EPS = 1e-6
POOL_WINDOWS = (2, 4, 8, 16)
POOL_HALO = 16
N_HEADS = 4
N_GROUPS = 4
EXPERTS_PER_GROUP = 8
N_EXPERTS = N_GROUPS * EXPERTS_PER_GROUP
TOP_K = 2
LANES = 128
SUBLANES = 8
ROUTER_ROWS = 8 + N_EXPERTS
VT_ROWS = LANES + 16
LOG2E = 1.4426950408889634

ROW_TILE = 512
IN_PROJ_SUBTILES = 2
MERGE_SUBTILES = 2
ATTN_TILE = 256
ATTN_HEADS_PER_STEP = 4
EXPERT_ROWS = 512
MOVE_TILE = 512
VMEM_LIMIT = 52 * 1024 * 1024

f32 = jnp.float32
bf16 = jnp.bfloat16


def _dot(a, b):
    return jnp.dot(a, b, preferred_element_type=f32)


def _dot_nt(a, b):
    return lax.dot_general(a, b, (((1,), (1,)), ((), ())), preferred_element_type=f32)


def _half_lane_rmsnorm(t, n_tiles, rows):
    lane = lax.broadcasted_iota(jnp.int32, (rows, LANES), 1)
    lo_mask = lane < (LANES // 2)
    outs = []
    for i in range(n_tiles):
        c = t[:, i * LANES:(i + 1) * LANES]
        sq = c * c
        lo = jnp.sum(jnp.where(lo_mask, sq, 0.0), axis=-1, keepdims=True)
        hi = jnp.sum(jnp.where(lo_mask, 0.0, sq), axis=-1, keepdims=True)
        ms = jnp.where(lo_mask, lo, hi) * (2.0 / LANES)
        outs.append(c * lax.rsqrt(ms + EPS))
    return jnp.concatenate(outs, axis=-1)


def _in_proj_kernel(x_ref, g1_ref, win_ref, poolw_ref, pscale_ref, wpu_ref, qg_ref, kg_ref,
                    pg_ref, ga_ref, qn_ref, kn_ref, vt_ref, prev_ref, wfold_ref, *, tm, subs, tk, d_model, pool_width,
                    qk_width, attn_width):
    j = pl.program_id(1)
    off_q = pool_width
    off_k = off_q + qk_width
    off_v = off_k + qk_width
    off_gp = off_v + attn_width
    off_ga = off_gp + d_model
    group = pool_width // len(POOL_WINDOWS)
    half = d_model // 2
    n_tiles = qk_width // LANES

    @pl.when(jnp.logical_and(pl.program_id(0) == 0, j == 0))
    def _():
        for g in range(len(POOL_WINDOWS)):
            rows_g = slice(g * group, (g + 1) * group)
            scaled = (poolw_ref[g].astype(f32) * pscale_ref[:, rows_g]).astype(bf16)
            wfold_ref[rows_g, :] = _dot(scaled, wpu_ref[rows_g, :]).astype(bf16)

    @pl.when(j == 0)
    def _():
        prev_ref[...] = jnp.zeros_like(prev_ref)

    extra = lax.broadcasted_iota(jnp.int32, (VT_ROWS - LANES, tk), 0)
    ones_rows = jnp.where(extra == 0, 1.0, 0.0).astype(bf16)
    halo = prev_ref[...]
    for sub in range(subs):
        rows = slice(sub * tm, (sub + 1) * tm)
        x = x_ref[rows, :]
        ms = jnp.mean(x * x, axis=-1, keepdims=True)
        h = (x * lax.rsqrt(ms + EPS) * g1_ref[...]).astype(bf16)

        def proj(lo, width, h=h):
            return _dot(h, win_ref[:, lo:lo + width])

        u = proj(0, pool_width)
        v = proj(off_v, attn_width)
        for hh in range(attn_width // LANES):
            for c in range(tm // tk):
                blk = sub * (tm // tk) + c
                vt_ref[hh, blk, 0:LANES, :] = v[c * tk:(c + 1) * tk, hh * LANES:(hh + 1) * LANES].T.astype(bf16)
                vt_ref[hh, blk, LANES:VT_ROWS, :] = ones_rows
        qn_ref[rows, :] = (_half_lane_rmsnorm(proj(off_q, qk_width), n_tiles, tm) * qg_ref[...]).astype(bf16)
        kn_ref[rows, :] = (_half_lane_rmsnorm(proj(off_k, qk_width), n_tiles, tm) * kg_ref[...]).astype(bf16)

        ext = jnp.concatenate([halo, u], axis=0)
        halo = u[tm - POOL_HALO:, :]
        pos = (j * subs + sub) * tm + lax.broadcasted_iota(jnp.int32, (tm, 1), 0)
        ds = []
        for g, w in enumerate(POOL_WINDOWS):
            acc = ext[:, g * group:(g + 1) * group]
            shift = 1
            while shift < w:
                acc = acc + pltpu.roll(acc, shift, 0)
                shift *= 2
            wsum = acc[POOL_HALO:, :]
            cnt = jnp.minimum(pos + 1, w).astype(f32)
            ds.append((wsum / cnt - u[:, g * group:(g + 1) * group]).astype(bf16))
        pool_out = _dot(jnp.concatenate(ds, axis=-1), wfold_ref[...])

        for c in range(2):
            cols = slice(c * half, (c + 1) * half)
            gp = jax.nn.sigmoid(proj(off_gp + c * half, half))
            pg_ref[rows, cols] = (gp * pool_out[:, cols]).astype(bf16)
            ga_ref[rows, cols] = jax.nn.sigmoid(proj(off_ga + c * half, half)).astype(bf16)
    prev_ref[...] = halo


def _in_proj(x, g1, w_in, pool_w, pool_scale, w_pool_up, qg, kg):
    B, S, D = x.shape
    subs = IN_PROJ_SUBTILES if S % (IN_PROJ_SUBTILES * ROW_TILE) == 0 else 1
    tm = min(ROW_TILE, S)
    pool_width = w_pool_up.shape[0]
    qk_width = qg.shape[1]
    attn_width = qk_width
    in_width = w_in.shape[1]
    const2 = lambda b, j: (0, 0)
    row = lambda b, j: (b, j, 0)
    tk = min(ATTN_TILE, S)
    n_heads = attn_width // LANES
    kern = functools.partial(_in_proj_kernel, tm=tm, subs=subs, tk=tk, d_model=D, pool_width=pool_width,
                             qk_width=qk_width, attn_width=attn_width)
    tb = subs * tm
    return pl.pallas_call(
        kern,
        grid=(B, S // tb),
        in_specs=[
            pl.BlockSpec((None, tb, D), row),
            pl.BlockSpec((1, D), const2),
            pl.BlockSpec((D, in_width), const2, pipeline_mode=pl.Buffered(1)),
            pl.BlockSpec(pool_w.shape, lambda b, j: (0, 0, 0)),
            pl.BlockSpec((1, pool_width), const2),
            pl.BlockSpec((pool_width, D), const2),
            pl.BlockSpec((1, qk_width), const2),
            pl.BlockSpec((1, qk_width), const2),
        ],
        out_specs=[
            pl.BlockSpec((None, tb, D), row),
            pl.BlockSpec((None, tb, D), row),
            pl.BlockSpec((None, tb, qk_width), row),
            pl.BlockSpec((None, tb, qk_width), row),
            pl.BlockSpec((None, n_heads, tb // tk, VT_ROWS, tk), lambda b, j: (b, 0, j, 0, 0)),
        ],
        out_shape=[
            jax.ShapeDtypeStruct((B, S, D), bf16),
            jax.ShapeDtypeStruct((B, S, D), bf16),
            jax.ShapeDtypeStruct((B, S, qk_width), bf16),
            jax.ShapeDtypeStruct((B, S, qk_width), bf16),
            jax.ShapeDtypeStruct((B, n_heads, S // tk, VT_ROWS, tk), bf16),
        ],
        scratch_shapes=[pltpu.VMEM((POOL_HALO, pool_width), f32), pltpu.VMEM((pool_width, D), bf16)],
        compiler_params=pltpu.CompilerParams(dimension_semantics=("arbitrary", "arbitrary"),
                                             vmem_limit_bytes=VMEM_LIMIT),
        name="in_proj",
    )(x, g1, w_in, pool_w, pool_scale, w_pool_up, qg, kg)


def _attn_kernel(slopes_ref, q_ref, k_ref, vt_ref, fq_ref, fk_ref, lq1_ref, lk1_ref, lq2_ref, lk2_ref, sg_ref,
                 o_ref, qa_ref, sa_ref, sb_ref, pa_ref, pb_ref, m_ref, alpha_ref, acc_ref, *,
                 tq, tk, heads, lam_init):
    hg = pl.program_id(1)
    i = pl.program_id(2)
    fk = fk_ref[...]
    lane = lax.broadcasted_iota(jnp.int32, (tk, LANES), 1)
    first = lane < (LANES // 2)
    hs = range(heads)

    buf_a = (sa_ref, pa_ref)
    buf_b = (sb_ref, pb_ref)

    def scores(g, n, buf):
        kb = k_ref[pl.ds(pl.multiple_of(n * tk, tk), tk), g * LANES:(g + 1) * LANES]
        buf[0][g] = _dot_nt(jnp.concatenate([kb, fk], axis=1), qa_ref[g])

    def softmax(g, n, buf, first_half_mask=None):
        s_ref, p_ref = buf
        c = -slopes_ref[hg * heads + g] * (i * tq - n * tk).astype(f32)
        s = s_ref[g]
        if first_half_mask is not None:
            s = jnp.concatenate([jnp.where(first_half_mask, s[:, :tq], -jnp.inf), s[:, tq:]], axis=1)
        m_old = m_ref[g]
        m_new = jnp.maximum(m_old, jnp.max(s, axis=0, keepdims=True) + c)
        p_ref[g] = jnp.exp2(s - (m_new - c)).astype(bf16)
        m_ref[g] = m_new
        return jnp.exp2(m_old - m_new)

    def pv(g, n, buf, alpha):
        acc_ref[g] = alpha * acc_ref[g] + _dot(vt_ref[g, jnp.maximum(n, 0)], buf[1][g])

    m_ref[...] = jnp.full(m_ref.shape, -jnp.inf, f32)
    alpha_ref[...] = jnp.ones_like(alpha_ref)
    acc_ref[...] = jnp.zeros_like(acc_ref)
    pb_ref[...] = jnp.zeros_like(pb_ref)
    zero = jnp.zeros((tk, LANES), bf16)
    for g in hs:
        parts = []
        for half in range(2):
            q = q_ref[half * tk:(half + 1) * tk, g * LANES:(g + 1) * LANES]
            parts += [jnp.where(first, q, zero), jnp.where(first, zero, q)]
        qa_ref[g] = jnp.concatenate([jnp.concatenate(parts, axis=0), fq_ref[g]], axis=1)
    for g in hs:
        scores(g, 0, buf_a)

    def pair(t, carry):
        n = 2 * t
        for g in hs:
            scores(g, n + 1, buf_b)
            a_even = softmax(g, n, buf_a)
            pv(g, n - 1, buf_b, alpha_ref[g])
            scores(g, n + 2, buf_a)
            alpha_ref[g] = softmax(g, n + 1, buf_b)
            pv(g, n, buf_a, a_even)
        return carry

    lax.fori_loop(0, i, pair, 0)
    n = 2 * i
    lam = (jnp.exp(jnp.sum(lq1_ref[...] * lk1_ref[...], keepdims=True))
           - jnp.exp(jnp.sum(lq2_ref[...] * lk2_ref[...], keepdims=True)) + lam_init)
    late = slice(tq, 2 * tq)
    kk = lax.broadcasted_iota(jnp.int32, (tk, tq), 0)
    qq = lax.broadcasted_iota(jnp.int32, (tk, tq), 1)
    tri = kk <= jnp.where(qq >= tk, qq - tk, qq)
    for g in hs:
        a_even = softmax(g, n, buf_a, first_half_mask=tri)
        pv(g, n - 1, buf_b, alpha_ref[g])
        pv(g, n, buf_a, a_even)
        kb = k_ref[pl.ds(pl.multiple_of((n + 1) * tk, tk), tk), g * LANES:(g + 1) * LANES]
        s = _dot_nt(jnp.concatenate([kb, fk], axis=1), qa_ref[g, late, :])
        s = jnp.where(tri, s, -jnp.inf)
        c = slopes_ref[hg * heads + g] * float(tk)
        m_old = m_ref[g, :, late]
        m_new = jnp.maximum(m_old, jnp.max(s, axis=0, keepdims=True) + c)
        p = jnp.exp2(s - (m_new - c)).astype(bf16)
        acc = acc_ref[g]
        acc_late = jnp.exp2(m_old - m_new) * acc[:, late] + _dot(vt_ref[g, n + 1], p)
        acc = jnp.concatenate([acc[:, :tq], acc_late], axis=1)
        o_all = acc[0:LANES] / acc[LANES:LANES + 1]
        o = jnp.concatenate([o_all[:, 0:tk] - lam * o_all[:, tk:tq],
                             o_all[:, tq:tq + tk] - lam * o_all[:, tq + tk:]], axis=1)
        ms = jnp.mean(o * o, axis=0, keepdims=True)
        on = o * lax.rsqrt(ms + EPS) * sg_ref[...] * (1.0 - lam_init)
        o_ref[:, g * LANES:(g + 1) * LANES] = on.T.astype(bf16)


def _split_bf16(x, pieces=3):
    out = []
    for _ in range(pieces):
        p = x.astype(bf16)
        out.append(p)
        x = x - p.astype(f32)
    return out


def _alibi_features(slopes2, tq, tk):
    assert tk <= 256
    assert tq == 2 * tk
    n_heads = slopes2.shape[0]
    ones = jnp.ones((tk, 1), bf16)
    krel = jnp.arange(tk, dtype=f32).astype(bf16)[:, None]
    fk = jnp.concatenate([krel] * 3 + [ones] * 3 + [jnp.zeros((tk, LANES - 6), bf16)], axis=1)
    first, last = jnp.arange(tk, dtype=f32), jnp.arange(tk, tq, dtype=f32)
    qrel = jnp.concatenate([first, first, last, last])
    a = _split_bf16(slopes2)
    b = _split_bf16(-slopes2[:, None] * qrel[None, :])
    cols = [jnp.broadcast_to(p[:, None, None], (n_heads, 2 * tq, 1)) for p in a] + [p[:, :, None] for p in b]
    fq = jnp.concatenate(cols + [jnp.zeros((n_heads, 2 * tq, LANES - 6), bf16)], axis=2)
    return fq, fk


def _attention(qn, kn, vt, fq, fk, slopes, lq1, lk1, lq2, lk2, subln_col, lam_init):
    B, S, _ = qn.shape
    tk = fk.shape[0]
    tq = fq.shape[1] // 2
    assert tq == 2 * tk
    nkv = S // tk
    const2 = lambda b, h, i, *_: (0, 0)
    G = ATTN_HEADS_PER_STEP
    kern = functools.partial(_attn_kernel, tq=tq, tk=tk, heads=G, lam_init=lam_init)
    hd = lq1.shape[1]
    return pl.pallas_call(
        kern,
        grid_spec=pltpu.PrefetchScalarGridSpec(
            num_scalar_prefetch=1,
            grid=(B, N_HEADS // G, S // tq),
            in_specs=[
                pl.BlockSpec((None, tq, G * LANES), lambda b, h, i, *_: (b, i, h)),
                pl.BlockSpec((None, S, G * LANES), lambda b, h, i, *_: (b, 0, h)),
                pl.BlockSpec((None, G, nkv, VT_ROWS, tk), lambda b, h, i, *_: (b, h, 0, 0, 0)),
                pl.BlockSpec((G, 2 * tq, LANES), lambda b, h, i, *_: (h, 0, 0)),
                pl.BlockSpec((tk, LANES), const2),
                pl.BlockSpec((1, hd), const2),
                pl.BlockSpec((1, hd), const2),
                pl.BlockSpec((1, hd), const2),
                pl.BlockSpec((1, hd), const2),
                pl.BlockSpec((LANES, 1), const2),
            ],
            out_specs=pl.BlockSpec((None, tq, G * LANES), lambda b, h, i, *_: (b, i, h)),
            scratch_shapes=[pltpu.VMEM((G, 2 * tq, 2 * LANES), bf16),
                            pltpu.VMEM((G, tk, 2 * tq), f32), pltpu.VMEM((G, tk, 2 * tq), f32),
                            pltpu.VMEM((G, tk, 2 * tq), bf16), pltpu.VMEM((G, tk, 2 * tq), bf16),
                            pltpu.VMEM((G, 1, 2 * tq), f32), pltpu.VMEM((G, 1, 2 * tq), f32),
                            pltpu.VMEM((G, VT_ROWS, 2 * tq), f32)],
        ),
        out_shape=jax.ShapeDtypeStruct((B, S, N_HEADS * LANES), bf16),
        compiler_params=pltpu.CompilerParams(dimension_semantics=("arbitrary", "arbitrary", "arbitrary"),
                                             vmem_limit_bytes=VMEM_LIMIT),
        name="diff_attn",
    )(slopes, qn, kn, vt, fq, fk, lq1, lk1, lq2, lk2, subln_col)


def _merge_kernel(pg_ref, ga_ref, o_ref, x_ref, wau_ref, wout_ref, g2_ref, wr_ref, br_ref,
                  x1_ref, hn_ref, eid_ref, wt_ref, rank_ref, cnt_ref, base_ref, *, tm, subs, d_model):
    step = pl.program_id(0)

    @pl.when(step == 0)
    def _():
        base_ref[...] = jnp.zeros_like(base_ref)

    a = lax.broadcasted_iota(jnp.int32, (tm, tm), 0)
    b = lax.broadcasted_iota(jnp.int32, (tm, tm), 1)
    upper = jnp.where(a <= b, 1.0, 0.0).astype(bf16)
    base = base_ref[...]
    all_logits = []
    for sub in range(subs):
        rows = slice(sub * tm, (sub + 1) * tm)
        attn_out = _dot(o_ref[rows, :], wau_ref[...])
        merged = pg_ref[rows, :].astype(f32) + ga_ref[rows, :].astype(f32) * attn_out
        x1 = x_ref[rows, :] + _dot(merged.astype(bf16), wout_ref[...])
        x1_ref[rows, :] = x1
        ms = jnp.mean(x1 * x1, axis=-1, keepdims=True)
        hn = x1 * lax.rsqrt(ms + EPS) * g2_ref[...]
        for s in range(d_model // LANES):
            hn_ref[pl.ds(sub * tm * SUBLANES + s, tm, stride=SUBLANES), :] = hn[:, s * LANES:(s + 1) * LANES]

        all_logits.append(_dot_nt(wr_ref[...], hn.astype(bf16)) + br_ref[...])

    for sub in range(subs):
        rows = slice(sub * tm, (sub + 1) * tm)
        logits = all_logits[sub]
        lg = logits[0:N_GROUPS]
        gmax = jnp.max(lg, axis=0, keepdims=True)
        p_top = 1.0 / jnp.sum(jnp.exp(lg - gmax), axis=0, keepdims=True)
        grow = lax.broadcasted_iota(jnp.int32, lg.shape, 0).astype(f32)
        g_idx = jnp.min(jnp.where(lg == gmax, grow, float(N_GROUPS)), axis=0, keepdims=True)

        sel = jnp.zeros((EXPERTS_PER_GROUP, tm), f32)
        for g in range(N_GROUPS):
            le_g = logits[SUBLANES + g * EXPERTS_PER_GROUP:SUBLANES + (g + 1) * EXPERTS_PER_GROUP]
            sel = jnp.where(g_idx == float(g), le_g, sel)
        erow = lax.broadcasted_iota(jnp.int32, sel.shape, 0).astype(f32)
        e1 = jnp.max(sel, axis=0, keepdims=True)
        i1 = jnp.min(jnp.where(sel == e1, erow, float(EXPERTS_PER_GROUP)), axis=0, keepdims=True)
        sel2 = jnp.where(erow == i1, -jnp.inf, sel)
        e2 = jnp.max(sel2, axis=0, keepdims=True)
        i2 = jnp.min(jnp.where(sel2 == e2, erow, float(EXPERTS_PER_GROUP)), axis=0, keepdims=True)
        r = jnp.exp(e2 - e1)
        w1 = p_top / (1.0 + r)
        w2 = p_top * r / (1.0 + r)
        eid1 = g_idx * float(EXPERTS_PER_GROUP) + i1
        eid2 = g_idx * float(EXPERTS_PER_GROUP) + i2
        eid_ref[:, rows] = jnp.concatenate([eid1, eid2], axis=0).astype(jnp.int32)
        wt_ref[:, rows] = jnp.concatenate([w1, w2], axis=0)

        xrow = lax.broadcasted_iota(jnp.int32, (N_EXPERTS, tm), 0).astype(f32)
        oh1 = jnp.where(xrow == eid1, 1.0, 0.0)
        oh2 = jnp.where(xrow == eid2, 1.0, 0.0)
        oh = oh1 + oh2
        before = _dot(oh.astype(bf16), upper) + base - 1.0
        rank1 = jnp.sum(oh1 * before, axis=0, keepdims=True)
        rank2 = jnp.sum(oh2 * before, axis=0, keepdims=True)
        rank_ref[:, rows] = jnp.concatenate([rank1, rank2], axis=0).astype(jnp.int32)
        base = base + jnp.sum(oh, axis=1, keepdims=True)
    base_ref[...] = base
    cnt_ref[...] = jnp.broadcast_to(base, cnt_ref.shape)


def _merge(pg, ga, o, x, w_attn_up, w_out, g2, wr, br):
    N, D = x.shape
    tm = min(ROW_TILE, N)
    subs = MERGE_SUBTILES if N % (MERGE_SUBTILES * tm) == 0 else 1
    tb = subs * tm
    aw = o.shape[1]
    const2 = lambda i: (0, 0)
    row = lambda i: (i, 0)
    colblk = lambda i: (0, i)
    kern = functools.partial(_merge_kernel, tm=tm, subs=subs, d_model=D)
    return pl.pallas_call(
        kern,
        grid=(N // tb,),
        in_specs=[
            pl.BlockSpec((tb, D), row),
            pl.BlockSpec((tb, D), row),
            pl.BlockSpec((tb, aw), row),
            pl.BlockSpec((tb, D), row),
            pl.BlockSpec((aw, D), const2),
            pl.BlockSpec((D, D), const2),
            pl.BlockSpec((1, D), const2),
            pl.BlockSpec((ROUTER_ROWS, D), const2),
            pl.BlockSpec((ROUTER_ROWS, 1), const2),
        ],
        out_specs=[
            pl.BlockSpec((tb, D), row),
            pl.BlockSpec((tb * SUBLANES, LANES), row),
            pl.BlockSpec((TOP_K, tb), colblk),
            pl.BlockSpec((TOP_K, tb), colblk),
            pl.BlockSpec((TOP_K, tb), colblk),
            pl.BlockSpec((N_EXPERTS, LANES), const2),
        ],
        out_shape=[
            jax.ShapeDtypeStruct((N, D), f32),
            jax.ShapeDtypeStruct((N * SUBLANES, LANES), f32),
            jax.ShapeDtypeStruct((TOP_K, N), jnp.int32),
            jax.ShapeDtypeStruct((TOP_K, N), f32),
            jax.ShapeDtypeStruct((TOP_K, N), jnp.int32),
            jax.ShapeDtypeStruct((N_EXPERTS, LANES), f32),
        ],
        scratch_shapes=[pltpu.VMEM((N_EXPERTS, 1), f32)],
        compiler_params=pltpu.CompilerParams(dimension_semantics=("arbitrary",), vmem_limit_bytes=VMEM_LIMIT),
        name="merge_router",
    )(pg, ga, o, x, w_attn_up, w_out, g2, wr, br)


def _slots_kernel(pstart_ref, eid_ref, rank_ref, dest_ref):
    eid = eid_ref[...]
    start = jnp.zeros_like(eid)
    for e in range(N_EXPERTS):
        start = jnp.where(eid == e, pstart_ref[e], start)
    dest_ref[...] = start + rank_ref[...]


def _slots(pstart, eid, rank):
    k, n = eid.shape
    tn = min(n, 8192)
    blk = pl.BlockSpec((k, tn), lambda i, *_: (0, i))
    return pl.pallas_call(
        _slots_kernel,
        grid_spec=pltpu.PrefetchScalarGridSpec(num_scalar_prefetch=1, grid=(n // tn,), in_specs=[blk, blk],
                                               out_specs=blk),
        out_shape=jax.ShapeDtypeStruct((k, n), jnp.int32),
        name="slots",
    )(pstart, eid, rank)


def _row(ref, r):
    return ref.at[pl.ds(pl.multiple_of(r * SUBLANES, SUBLANES), SUBLANES), :]


def _wait_bytes_of(ref_like, any_hbm, sem):
    n = ref_like.shape[0]
    pltpu.make_async_copy(any_hbm.at[pl.ds(0, n), :], any_hbm.at[pl.ds(0, n), :], sem).wait()


STAGES = 3


def _dispatch_kernel(pstart_ref, pend_ref, dest_ref, hn_hbm, xs_hbm, zero_ref, stage, in_sem, out_sem,
                     zsem, *, tt, rows, n_blocks):
    i = pl.program_id(0)
    n = pl.num_programs(0)

    @pl.when(i == 0)
    def _():
        zero_ref[...] = jnp.zeros_like(zero_ref)
        for e in range(N_EXPERTS):
            @pl.when(pend_ref[e] > pstart_ref[e])
            def _():
                pltpu.make_async_copy(zero_ref, xs_hbm.at[pl.ds(pl.multiple_of((pend_ref[e] - rows) * SUBLANES,
                                                                               SUBLANES), rows * SUBLANES), :],
                                      zsem).start()
        def zero_block(b, carry):
            pltpu.make_async_copy(zero_ref, xs_hbm.at[pl.ds(pl.multiple_of(b * (rows * SUBLANES), SUBLANES),
                                                            rows * SUBLANES), :], zsem).start()
            return carry

        def wait_block(b, carry):
            pltpu.make_async_copy(zero_ref, xs_hbm.at[pl.ds(0, rows * SUBLANES), :], zsem).wait()
            return carry

        first_unused = pend_ref[N_EXPERTS - 1] // rows
        lax.fori_loop(first_unused, n_blocks, zero_block, 0)
        for e in range(N_EXPERTS):
            @pl.when(pend_ref[e] > pstart_ref[e])
            def _():
                wait_block(0, 0)
        lax.fori_loop(first_unused, n_blocks, wait_block, 0)

    def stage_copy(step):
        s = step % STAGES
        src = hn_hbm.at[pl.ds(pl.multiple_of(step * (tt * SUBLANES), SUBLANES), tt * SUBLANES), :]
        return pltpu.make_async_copy(src, stage.at[s], in_sem.at[s])

    def wait_rows_of(step):
        for _ in range(TOP_K):
            _wait_bytes_of(stage.at[0], hn_hbm, out_sem.at[step % STAGES])

    @pl.when(i == 0)
    def _():
        stage_copy(i).start()

    @pl.when(i >= STAGES - 1)
    def _():
        wait_rows_of(i - (STAGES - 1))

    @pl.when(i + 1 < n)
    def _():
        stage_copy(i + 1).start()

    stage_copy(i).wait()
    slot = i % STAGES

    def body(t, carry):
        for k in range(TOP_K):
            pltpu.make_async_copy(_row(stage.at[slot], t), _row(xs_hbm, dest_ref[0, k * tt + t]),
                                  out_sem.at[slot]).start(priority=k)
        return carry
    lax.fori_loop(0, tt, body, 0, unroll=8)

    @pl.when(i == n - 1)
    def _():
        for back in range(STAGES - 2, -1, -1):
            @pl.when(i - back >= 0)
            def _():
                wait_rows_of(i - back)


def _dispatch(pstart, pend, dest_blocks, hn_g, n_slots):
    n, _, width = dest_blocks.shape
    tt = width // TOP_K
    rows = EXPERT_ROWS
    kern = functools.partial(_dispatch_kernel, tt=tt, rows=rows, n_blocks=n_slots // rows)
    smem_blk = pl.BlockSpec((None, 1, width), lambda i, *_: (i, 0, 0), memory_space=pltpu.SMEM)
    return pl.pallas_call(
        kern,
        grid_spec=pltpu.PrefetchScalarGridSpec(
            num_scalar_prefetch=2,
            grid=(n,),
            in_specs=[smem_blk, pl.BlockSpec(memory_space=pl.ANY)],
            out_specs=pl.BlockSpec(memory_space=pl.ANY),
            scratch_shapes=[pltpu.VMEM((rows * SUBLANES, LANES), f32),
                            pltpu.VMEM((STAGES, tt * SUBLANES, LANES), f32),
                            pltpu.SemaphoreType.DMA((STAGES,)), pltpu.SemaphoreType.DMA((STAGES,)),
                            pltpu.SemaphoreType.DMA(())],
        ),
        out_shape=jax.ShapeDtypeStruct((n_slots * SUBLANES, LANES), f32),
        compiler_params=pltpu.CompilerParams(dimension_semantics=("arbitrary",), vmem_limit_bytes=VMEM_LIMIT,
                                             has_side_effects=True),
        name="dispatch",
    )(pstart, pend, dest_blocks, hn_g)


def _rows_from_token_major(buf, n_rows, n_tiles):
    return jnp.concatenate([buf[pl.ds(s, n_rows, stride=SUBLANES), :] for s in range(n_tiles)], axis=-1)


def _expert_kernel(beid_ref, nbu_ref, xs_ref, wg_ref, wu_ref, wd_ref, ys_ref, wg_bf, wu_bf, wd_bf, *, rows,
                   d_model):
    b = pl.program_id(0)
    changed = jnp.logical_or(b == 0, beid_ref[b] != beid_ref[jnp.maximum(b - 1, 0)])

    @pl.when(changed)
    def _():
        wg_bf[...] = wg_ref[...].astype(bf16)
        wu_bf[...] = wu_ref[...].astype(bf16)
        wd_bf[...] = wd_ref[...].astype(bf16)

    n_tiles = d_model // LANES

    @pl.when(b < nbu_ref[0])
    def _():
        xb = _rows_from_token_major(xs_ref, rows, n_tiles).astype(bf16)
        hdn = jax.nn.silu(_dot(xb, wg_bf[...])) * _dot(xb, wu_bf[...])
        y = _dot(hdn.astype(bf16), wd_bf[...])
        for s in range(n_tiles):
            ys_ref[pl.ds(s, rows, stride=SUBLANES), :] = y[:, s * LANES:(s + 1) * LANES]

    @pl.when(b >= nbu_ref[0])
    def _():
        ys_ref[...] = jnp.zeros_like(ys_ref)


def _experts(block_eid, n_used, xs, w_gate, w_up, w_down):
    n_blocks = block_eid.shape[0]
    rows = EXPERT_ROWS
    E, D, DE = w_gate.shape
    kern = functools.partial(_expert_kernel, rows=rows, d_model=D)
    xs_idx = lambda b, eid, nbu: (jnp.minimum(b, nbu[0] - 1), 0)
    return pl.pallas_call(
        kern,
        grid_spec=pltpu.PrefetchScalarGridSpec(
            num_scalar_prefetch=2,
            grid=(n_blocks,),
            in_specs=[
                pl.BlockSpec((rows * SUBLANES, LANES), xs_idx),
                pl.BlockSpec((None, D, DE), lambda b, eid, nbu: (eid[b], 0, 0)),
                pl.BlockSpec((None, D, DE), lambda b, eid, nbu: (eid[b], 0, 0)),
                pl.BlockSpec((None, DE, D), lambda b, eid, nbu: (eid[b], 0, 0)),
            ],
            out_specs=pl.BlockSpec((rows * SUBLANES, LANES), lambda b, *_: (b, 0)),
            scratch_shapes=[
                pltpu.VMEM((D, DE), bf16),
                pltpu.VMEM((D, DE), bf16),
                pltpu.VMEM((DE, D), bf16),
            ],
        ),
        out_shape=jax.ShapeDtypeStruct((n_blocks * rows * SUBLANES, LANES), f32),
        compiler_params=pltpu.CompilerParams(dimension_semantics=("arbitrary",), vmem_limit_bytes=VMEM_LIMIT),
        name="experts",
    )(block_eid, n_used, xs, w_gate, w_up, w_down)


def _combine_kernel(dest_ref, destn_ref, x1_ref, wt_ref, ys_hbm, out_ref, buf, sem, *, te, d_model):
    i = pl.program_id(0)
    n = pl.num_programs(0)
    slot = i % 2

    def start_gather(d_ref, dst, dsem):
        def body(t, carry):
            for k in range(TOP_K):
                r = k * te + t
                pltpu.make_async_copy(_row(ys_hbm, d_ref[0, r]), _row(dst, r), dsem).start(priority=k)
            return carry
        lax.fori_loop(0, te, body, 0, unroll=8)

    @pl.when(i == 0)
    def _():
        start_gather(dest_ref, buf.at[0], sem.at[0])

    @pl.when(i + 1 < n)
    def _():
        start_gather(destn_ref, buf.at[1 - slot], sem.at[1 - slot])

    _wait_bytes_of(buf.at[slot], ys_hbm, sem.at[slot])
    n_tiles = d_model // LANES
    both = _rows_from_token_major(buf.at[slot], TOP_K * te, n_tiles)
    wt = jnp.concatenate([wt_ref[...], jnp.zeros((SUBLANES - TOP_K, te), f32)], axis=0).T
    out_ref[...] = x1_ref[...] + (wt[:, 0:1] * both[:te] + wt[:, 1:2] * both[te:])


def _combine(dest_blocks, x1, wt_rows, ys):
    N, D = x1.shape
    n, _, width = dest_blocks.shape
    te = width // TOP_K
    kern = functools.partial(_combine_kernel, te=te, d_model=D)
    cur = pl.BlockSpec((None, 1, width), lambda i: (i, 0, 0), memory_space=pltpu.SMEM)
    nxt = pl.BlockSpec((None, 1, width), lambda i: (jnp.minimum(i + 1, n - 1), 0, 0), memory_space=pltpu.SMEM)
    return pl.pallas_call(
        kern,
        grid=(n,),
        in_specs=[
            cur, nxt,
            pl.BlockSpec((te, D), lambda i: (i, 0)),
            pl.BlockSpec((TOP_K, te), lambda i: (0, i)),
            pl.BlockSpec(memory_space=pl.ANY),
        ],
        out_specs=pl.BlockSpec((te, D), lambda i: (i, 0)),
        scratch_shapes=[pltpu.VMEM((2, TOP_K * te * SUBLANES, LANES), f32), pltpu.SemaphoreType.DMA((2,))],
        out_shape=jax.ShapeDtypeStruct((N, D), f32),
        compiler_params=pltpu.CompilerParams(dimension_semantics=("arbitrary",), vmem_limit_bytes=VMEM_LIMIT),
        name="combine",
    )(dest_blocks, dest_blocks, x1, wt_rows, ys)


def _token_blocks(a, tt):
    k, n = a.shape
    return a.reshape(k, n // tt, tt).transpose(1, 0, 2).reshape(n // tt, 1, k * tt)


def _layer(x, l, norm1_g, w_in, pool_w, pool_scale, w_pool_up, q_norm_g, k_norm_g, lambda_q1, lambda_k1,
           lambda_q2, lambda_k2, subln_g, w_attn_up, w_out, norm2_g, w_router_group, b_router_group,
           w_router_expert, b_router_expert, w_expert_gate, w_expert_up, w_expert_down):
    B, S, D = x.shape
    N = B * S
    head_dim = q_norm_g.shape[0]
    lam_init = 0.8 - 0.6 * math.exp(-0.3 * l)
    reps = (N_HEADS * 2 * head_dim) // head_dim

    qg = (jnp.tile(q_norm_g, reps) * (head_dim ** -0.5 * LOG2E))[None, :]
    kg = jnp.tile(k_norm_g, reps)[None, :]
    pg, ga, qn, kn, vt = _in_proj(x, norm1_g[None, :], w_in.astype(bf16), pool_w.astype(bf16),
                                  pool_scale[None, :], w_pool_up.astype(bf16), qg, kg)

    tk = min(ATTN_TILE, S)
    slopes2 = jnp.asarray([2.0 ** (-8.0 * (h + 1) / N_HEADS) * LOG2E for h in range(N_HEADS)], f32)
    fq, fk = _alibi_features(slopes2, 2 * tk, tk)
    o = _attention(qn, kn, vt, fq, fk, slopes2, lambda_q1[None, :], lambda_k1[None, :], lambda_q2[None, :],
                   lambda_k2[None, :], subln_g[:, None], lam_init)

    wr = jnp.zeros((ROUTER_ROWS, D), f32)
    wr = wr.at[:N_GROUPS].set(w_router_group.T).at[SUBLANES:].set(w_router_expert.T).astype(bf16)
    br = jnp.zeros((ROUTER_ROWS, 1), f32)
    br = br.at[:N_GROUPS, 0].set(b_router_group).at[SUBLANES:, 0].set(b_router_expert)
    x1, hn_g, eid, wts, rank, cnt = _merge(pg.reshape(N, D), ga.reshape(N, D), o.reshape(N, -1),
                                           x.reshape(N, D), w_attn_up.astype(bf16), w_out.astype(bf16),
                                           norm2_g[None, :], wr, br)

    R = EXPERT_ROWS
    counts = cnt[:, 0].astype(jnp.int32)
    padded = (counts + R - 1) // R * R
    pend = jnp.cumsum(padded).astype(jnp.int32)
    pstart = pend - padded
    n_blocks = -(-(N * TOP_K) // R) + N_EXPERTS
    starts = jnp.arange(n_blocks, dtype=jnp.int32) * R
    block_eid = jnp.minimum(jnp.sum((pend[None, :] <= starts[:, None]).astype(jnp.int32), axis=1), N_EXPERTS - 1)
    n_used = pend[-1:] // R

    tt = min(MOVE_TILE, N)
    dest_blocks = _token_blocks(_slots(pstart, eid, rank), tt)
    xs = _dispatch(pstart, pend, dest_blocks, hn_g, n_blocks * R)
    ys = _experts(block_eid, n_used, xs, w_expert_gate, w_expert_up, w_expert_down)
    out = _combine(dest_blocks, x1, wts, ys)
    return out.reshape(B, S, D)


def kernel(x, norm1_g, w_in, pool_w, pool_scale, w_pool_up, q_norm_g, k_norm_g, lambda_q1, lambda_k1, lambda_q2,
           lambda_k2, subln_g, w_attn_up, w_out, norm2_g, w_router_group, b_router_group, w_router_expert,
           b_router_expert, w_expert_gate, w_expert_up, w_expert_down):
    params = (norm1_g, w_in, pool_w, pool_scale, w_pool_up, q_norm_g, k_norm_g, lambda_q1, lambda_k1, lambda_q2,
              lambda_k2, subln_g, w_attn_up, w_out, norm2_g, w_router_group, b_router_group, w_router_expert,
              b_router_expert, w_expert_gate, w_expert_up, w_expert_down)
    for l in range(norm1_g.shape[0]):
        x = _layer(x, l, *(p[l] for p in params))
    return x
```

```python
import functools
import math

import jax
import jax.numpy as jnp
from jax import lax
from jax.experimental import pallas as pl
from jax.experimental.pallas import tpu as pltpu

EPS = 1e-6
POOL_WINDOWS = (2, 4, 8, 16)
POOL_HALO = 16
N_HEADS = 4
N_GROUPS = 4
EXPERTS_PER_GROUP = 8
N_EXPERTS = N_GROUPS * EXPERTS_PER_GROUP
TOP_K = 2
LANES = 128
SUBLANES = 8
ROUTER_ROWS = 8 + N_EXPERTS
VT_ROWS = LANES + 16
LOG2E = 1.4426950408889634

ROW_TILE = 512
IN_PROJ_SUBTILES = 2
MERGE_SUBTILES = 2
ATTN_TILE = 256
ATTN_HEADS_PER_STEP = 4
EXPERT_ROWS = 512
DISPATCH_TILE = 512
COMBINE_TILE = 256
VMEM_LIMIT = 52 * 1024 * 1024

f32 = jnp.float32
bf16 = jnp.bfloat16


def _dot(a, b):
    return jnp.dot(a, b, preferred_element_type=f32)


def _dot_nt(a, b):
    return lax.dot_general(a, b, (((1,), (1,)), ((), ())), preferred_element_type=f32)


def _half_lane_rmsnorm(t, n_tiles, rows):
    lane = lax.broadcasted_iota(jnp.int32, (rows, LANES), 1)
    lo_mask = lane < (LANES // 2)
    outs = []
    for i in range(n_tiles):
        c = t[:, i * LANES:(i + 1) * LANES]
        sq = c * c
        lo = jnp.sum(jnp.where(lo_mask, sq, 0.0), axis=-1, keepdims=True)
        hi = jnp.sum(jnp.where(lo_mask, 0.0, sq), axis=-1, keepdims=True)
        ms = jnp.where(lo_mask, lo, hi) * (2.0 / LANES)
        outs.append(c * lax.rsqrt(ms + EPS))
    return jnp.concatenate(outs, axis=-1)


def _in_proj_kernel(x_ref, g1_ref, win_ref, poolw_ref, pscale_ref, wpu_ref, qg_ref, kg_ref,
                    pg_ref, ga_ref, qn_ref, kn_ref, vt_ref, prev_ref, wfold_ref, *, tm, subs, tk, d_model, pool_width,
                    qk_width, attn_width):
    j = pl.program_id(1)
    off_q = pool_width
    off_k = off_q + qk_width
    off_v = off_k + qk_width
    off_gp = off_v + attn_width
    off_ga = off_gp + d_model
    group = pool_width // len(POOL_WINDOWS)
    half = d_model // 2
    n_tiles = qk_width // LANES

    @pl.when(jnp.logical_and(pl.program_id(0) == 0, j == 0))
    def _():
        for g in range(len(POOL_WINDOWS)):
            rows_g = slice(g * group, (g + 1) * group)
            scaled = (poolw_ref[g].astype(f32) * pscale_ref[:, rows_g]).astype(bf16)
            wfold_ref[rows_g, :] = _dot(scaled, wpu_ref[rows_g, :]).astype(bf16)

    @pl.when(j == 0)
    def _():
        prev_ref[...] = jnp.zeros_like(prev_ref)

    extra = lax.broadcasted_iota(jnp.int32, (VT_ROWS - LANES, tk), 0)
    ones_rows = jnp.where(extra == 0, 1.0, 0.0).astype(bf16)
    halo = prev_ref[...]
    for sub in range(subs):
        rows = slice(sub * tm, (sub + 1) * tm)
        x = x_ref[rows, :]
        ms = jnp.mean(x * x, axis=-1, keepdims=True)
        h = (x * lax.rsqrt(ms + EPS) * g1_ref[...]).astype(bf16)

        def proj(lo, width, h=h):
            return _dot(h, win_ref[:, lo:lo + width])

        u = proj(0, pool_width)
        v = proj(off_v, attn_width)
        for hh in range(attn_width // LANES):
            for c in range(tm // tk):
                blk = sub * (tm // tk) + c
                vt_ref[hh, blk, 0:LANES, :] = v[c * tk:(c + 1) * tk, hh * LANES:(hh + 1) * LANES].T.astype(bf16)
                vt_ref[hh, blk, LANES:VT_ROWS, :] = ones_rows
        qn_ref[rows, :] = (_half_lane_rmsnorm(proj(off_q, qk_width), n_tiles, tm) * qg_ref[...]).astype(bf16)
        kn_ref[rows, :] = (_half_lane_rmsnorm(proj(off_k, qk_width), n_tiles, tm) * kg_ref[...]).astype(bf16)

        ext = jnp.concatenate([halo, u], axis=0)
        halo = u[tm - POOL_HALO:, :]
        pos = (j * subs + sub) * tm + lax.broadcasted_iota(jnp.int32, (tm, 1), 0)
        ds = []
        for g, w in enumerate(POOL_WINDOWS):
            acc = ext[:, g * group:(g + 1) * group]
            shift = 1
            while shift < w:
                acc = acc + pltpu.roll(acc, shift, 0)
                shift *= 2
            wsum = acc[POOL_HALO:, :]
            cnt = jnp.minimum(pos + 1, w).astype(f32)
            ds.append((wsum / cnt - u[:, g * group:(g + 1) * group]).astype(bf16))
        pool_out = _dot(jnp.concatenate(ds, axis=-1), wfold_ref[...])

        for c in range(2):
            cols = slice(c * half, (c + 1) * half)
            gp = jax.nn.sigmoid(proj(off_gp + c * half, half))
            pg_ref[rows, cols] = (gp * pool_out[:, cols]).astype(bf16)
            ga_ref[rows, cols] = jax.nn.sigmoid(proj(off_ga + c * half, half)).astype(bf16)
    prev_ref[...] = halo


def _in_proj(x, g1, w_in, pool_w, pool_scale, w_pool_up, qg, kg):
    B, S, D = x.shape
    subs = IN_PROJ_SUBTILES if S % (IN_PROJ_SUBTILES * ROW_TILE) == 0 else 1
    tm = min(ROW_TILE, S)
    pool_width = w_pool_up.shape[0]
    qk_width = qg.shape[1]
    attn_width = qk_width
    in_width = w_in.shape[1]
    const2 = lambda b, j: (0, 0)
    row = lambda b, j: (b, j, 0)
    tk = min(ATTN_TILE, S)
    n_heads = attn_width // LANES
    kern = functools.partial(_in_proj_kernel, tm=tm, subs=subs, tk=tk, d_model=D, pool_width=pool_width,
                             qk_width=qk_width, attn_width=attn_width)
    tb = subs * tm
    return pl.pallas_call(
        kern,
        grid=(B, S // tb),
        in_specs=[
            pl.BlockSpec((None, tb, D), row),
            pl.BlockSpec((1, D), const2),
            pl.BlockSpec((D, in_width), const2, pipeline_mode=pl.Buffered(1)),
            pl.BlockSpec(pool_w.shape, lambda b, j: (0, 0, 0)),
            pl.BlockSpec((1, pool_width), const2),
            pl.BlockSpec((pool_width, D), const2),
            pl.BlockSpec((1, qk_width), const2),
            pl.BlockSpec((1, qk_width), const2),
        ],
        out_specs=[
            pl.BlockSpec((None, tb, D), row),
            pl.BlockSpec((None, tb, D), row),
            pl.BlockSpec((None, tb, qk_width), row),
            pl.BlockSpec((None, tb, qk_width), row),
            pl.BlockSpec((None, n_heads, tb // tk, VT_ROWS, tk), lambda b, j: (b, 0, j, 0, 0)),
        ],
        out_shape=[
            jax.ShapeDtypeStruct((B, S, D), bf16),
            jax.ShapeDtypeStruct((B, S, D), bf16),
            jax.ShapeDtypeStruct((B, S, qk_width), bf16),
            jax.ShapeDtypeStruct((B, S, qk_width), bf16),
            jax.ShapeDtypeStruct((B, n_heads, S // tk, VT_ROWS, tk), bf16),
        ],
        scratch_shapes=[pltpu.VMEM((POOL_HALO, pool_width), f32), pltpu.VMEM((pool_width, D), bf16)],
        compiler_params=pltpu.CompilerParams(dimension_semantics=("arbitrary", "arbitrary"),
                                             vmem_limit_bytes=VMEM_LIMIT),
        name="in_proj",
    )(x, g1, w_in, pool_w, pool_scale, w_pool_up, qg, kg)


def _attn_kernel(slopes_ref, q_ref, k_ref, vt_ref, fq_ref, fk_ref, lq1_ref, lk1_ref, lq2_ref, lk2_ref, sg_ref,
                 o_ref, qa_ref, sa_ref, sb_ref, pa_ref, pb_ref, m_ref, alpha_ref, acc_ref, *,
                 tq, tk, heads, lam_init):
    hg = pl.program_id(1)
    i = pl.program_id(2)
    fk = fk_ref[...]
    lane = lax.broadcasted_iota(jnp.int32, (tk, LANES), 1)
    first = lane < (LANES // 2)
    hs = range(heads)

    buf_a = (sa_ref, pa_ref)
    buf_b = (sb_ref, pb_ref)

    def scores(g, n, buf):
        kb = k_ref[pl.ds(pl.multiple_of(n * tk, tk), tk), g * LANES:(g + 1) * LANES]
        buf[0][g] = _dot_nt(jnp.concatenate([kb, fk], axis=1), qa_ref[g])

    def softmax(g, n, buf, first_half_mask=None):
        s_ref, p_ref = buf
        c = -slopes_ref[hg * heads + g] * (i * tq - n * tk).astype(f32)
        s = s_ref[g]
        if first_half_mask is not None:
            s = jnp.concatenate([jnp.where(first_half_mask, s[:, :tq], -jnp.inf), s[:, tq:]], axis=1)
        m_old = m_ref[g]
        m_new = jnp.maximum(m_old, jnp.max(s, axis=0, keepdims=True) + c)
        p_ref[g] = jnp.exp2(s - (m_new - c)).astype(bf16)
        m_ref[g] = m_new
        return jnp.exp2(m_old - m_new)

    def pv(g, n, buf, alpha):
        acc_ref[g] = alpha * acc_ref[g] + _dot(vt_ref[g, jnp.maximum(n, 0)], buf[1][g])

    m_ref[...] = jnp.full(m_ref.shape, -jnp.inf, f32)
    alpha_ref[...] = jnp.ones_like(alpha_ref)
    acc_ref[...] = jnp.zeros_like(acc_ref)
    pb_ref[...] = jnp.zeros_like(pb_ref)
    zero = jnp.zeros((tk, LANES), bf16)
    for g in hs:
        parts = []
        for half in range(2):
            q = q_ref[half * tk:(half + 1) * tk, g * LANES:(g + 1) * LANES]
            parts += [jnp.where(first, q, zero), jnp.where(first, zero, q)]
        qa_ref[g] = jnp.concatenate([jnp.concatenate(parts, axis=0), fq_ref[g]], axis=1)
    for g in hs:
        scores(g, 0, buf_a)

    def pair(t, carry):
        n = 2 * t
        a_prev = [alpha_ref[g] for g in hs]
        a_even = [None] * heads
        for g in hs:
            pv(g, n - 1, buf_b, a_prev[g])
            a_even[g] = softmax(g, n, buf_a)
            scores(g, n + 1, buf_b)
        for g in hs:
            pv(g, n, buf_a, a_even[g])
            alpha_ref[g] = softmax(g, n + 1, buf_b)
            scores(g, n + 2, buf_a)
        return carry

    lax.fori_loop(0, i, pair, 0)
    n = 2 * i
    lam = (jnp.exp(jnp.sum(lq1_ref[...] * lk1_ref[...], keepdims=True))
           - jnp.exp(jnp.sum(lq2_ref[...] * lk2_ref[...], keepdims=True)) + lam_init)
    late = slice(tq, 2 * tq)
    kk = lax.broadcasted_iota(jnp.int32, (tk, tq), 0)
    qq = lax.broadcasted_iota(jnp.int32, (tk, tq), 1)
    tri = kk <= jnp.where(qq >= tk, qq - tk, qq)
    for g in hs:
        a_even = softmax(g, n, buf_a, first_half_mask=tri)
        pv(g, n - 1, buf_b, alpha_ref[g])
        pv(g, n, buf_a, a_even)
        kb = k_ref[pl.ds(pl.multiple_of((n + 1) * tk, tk), tk), g * LANES:(g + 1) * LANES]
        s = _dot_nt(jnp.concatenate([kb, fk], axis=1), qa_ref[g, late, :])
        s = jnp.where(tri, s, -jnp.inf)
        c = slopes_ref[hg * heads + g] * float(tk)
        m_old = m_ref[g, :, late]
        m_new = jnp.maximum(m_old, jnp.max(s, axis=0, keepdims=True) + c)
        p = jnp.exp2(s - (m_new - c)).astype(bf16)
        acc = acc_ref[g]
        acc_late = jnp.exp2(m_old - m_new) * acc[:, late] + _dot(vt_ref[g, n + 1], p)
        acc = jnp.concatenate([acc[:, :tq], acc_late], axis=1)
        o_all = acc[0:LANES] / acc[LANES:LANES + 1]
        o = jnp.concatenate([o_all[:, 0:tk] - lam * o_all[:, tk:tq],
                             o_all[:, tq:tq + tk] - lam * o_all[:, tq + tk:]], axis=1)
        ms = jnp.mean(o * o, axis=0, keepdims=True)
        on = o * lax.rsqrt(ms + EPS) * sg_ref[...] * (1.0 - lam_init)
        o_ref[:, g * LANES:(g + 1) * LANES] = on.T.astype(bf16)


def _split_bf16(x, pieces=3):
    out = []
    for _ in range(pieces):
        p = x.astype(bf16)
        out.append(p)
        x = x - p.astype(f32)
    return out


def _alibi_features(slopes2, tq, tk):
    assert tk <= 256
    assert tq == 2 * tk
    n_heads = slopes2.shape[0]
    ones = jnp.ones((tk, 1), bf16)
    krel = jnp.arange(tk, dtype=f32).astype(bf16)[:, None]
    fk = jnp.concatenate([krel] * 3 + [ones] * 3 + [jnp.zeros((tk, LANES - 6), bf16)], axis=1)
    first, last = jnp.arange(tk, dtype=f32), jnp.arange(tk, tq, dtype=f32)
    qrel = jnp.concatenate([first, first, last, last])
    a = _split_bf16(slopes2)
    b = _split_bf16(-slopes2[:, None] * qrel[None, :])
    cols = [jnp.broadcast_to(p[:, None, None], (n_heads, 2 * tq, 1)) for p in a] + [p[:, :, None] for p in b]
    fq = jnp.concatenate(cols + [jnp.zeros((n_heads, 2 * tq, LANES - 6), bf16)], axis=2)
    return fq, fk


def _attention(qn, kn, vt, fq, fk, slopes, lq1, lk1, lq2, lk2, subln_col, lam_init):
    B, S, _ = qn.shape
    tk = fk.shape[0]
    tq = fq.shape[1] // 2
    assert tq == 2 * tk
    nkv = S // tk
    const2 = lambda b, h, i, *_: (0, 0)
    G = ATTN_HEADS_PER_STEP
    kern = functools.partial(_attn_kernel, tq=tq, tk=tk, heads=G, lam_init=lam_init)
    hd = lq1.shape[1]
    return pl.pallas_call(
        kern,
        grid_spec=pltpu.PrefetchScalarGridSpec(
            num_scalar_prefetch=1,
            grid=(B, N_HEADS // G, S // tq),
            in_specs=[
                pl.BlockSpec((None, tq, G * LANES), lambda b, h, i, *_: (b, i, h)),
                pl.BlockSpec((None, S, G * LANES), lambda b, h, i, *_: (b, 0, h)),
                pl.BlockSpec((None, G, nkv, VT_ROWS, tk), lambda b, h, i, *_: (b, h, 0, 0, 0)),
                pl.BlockSpec((G, 2 * tq, LANES), lambda b, h, i, *_: (h, 0, 0)),
                pl.BlockSpec((tk, LANES), const2),
                pl.BlockSpec((1, hd), const2),
                pl.BlockSpec((1, hd), const2),
                pl.BlockSpec((1, hd), const2),
                pl.BlockSpec((1, hd), const2),
                pl.BlockSpec((LANES, 1), const2),
            ],
            out_specs=pl.BlockSpec((None, tq, G * LANES), lambda b, h, i, *_: (b, i, h)),
            scratch_shapes=[pltpu.VMEM((G, 2 * tq, 2 * LANES), bf16),
                            pltpu.VMEM((G, tk, 2 * tq), f32), pltpu.VMEM((G, tk, 2 * tq), f32),
                            pltpu.VMEM((G, tk, 2 * tq), bf16), pltpu.VMEM((G, tk, 2 * tq), bf16),
                            pltpu.VMEM((G, 1, 2 * tq), f32), pltpu.VMEM((G, 1, 2 * tq), f32),
                            pltpu.VMEM((G, VT_ROWS, 2 * tq), f32)],
        ),
        out_shape=jax.ShapeDtypeStruct((B, S, N_HEADS * LANES), bf16),
        compiler_params=pltpu.CompilerParams(dimension_semantics=("arbitrary", "arbitrary", "arbitrary"),
                                             vmem_limit_bytes=VMEM_LIMIT),
        name="diff_attn",
    )(slopes, qn, kn, vt, fq, fk, lq1, lk1, lq2, lk2, subln_col)


def _merge_kernel(pg_ref, ga_ref, o_ref, x_ref, wau_ref, wout_ref, g2_ref, wr_ref, br_ref,
                  x1_ref, hn_ref, eid_ref, wt_ref, rank_ref, cnt_ref, base_ref, *, tm, subs, d_model):
    step = pl.program_id(0)

    @pl.when(step == 0)
    def _():
        base_ref[...] = jnp.zeros_like(base_ref)

    a = lax.broadcasted_iota(jnp.int32, (tm, tm), 0)
    b = lax.broadcasted_iota(jnp.int32, (tm, tm), 1)
    upper = jnp.where(a <= b, 1.0, 0.0).astype(bf16)
    base = base_ref[...]
    all_logits = []
    for sub in range(subs):
        rows = slice(sub * tm, (sub + 1) * tm)
        attn_out = _dot(o_ref[rows, :], wau_ref[...])
        merged = pg_ref[rows, :].astype(f32) + ga_ref[rows, :].astype(f32) * attn_out
        x1 = x_ref[rows, :] + _dot(merged.astype(bf16), wout_ref[...])
        x1_ref[rows, :] = x1
        ms = jnp.mean(x1 * x1, axis=-1, keepdims=True)
        hn = x1 * lax.rsqrt(ms + EPS) * g2_ref[...]
        for s in range(d_model // LANES):
            hn_ref[pl.ds(sub * tm * SUBLANES + s, tm, stride=SUBLANES), :] = hn[:, s * LANES:(s + 1) * LANES]

        all_logits.append(_dot_nt(wr_ref[...], hn.astype(bf16)) + br_ref[...])

    for sub in range(subs):
        rows = slice(sub * tm, (sub + 1) * tm)
        logits = all_logits[sub]
        lg = logits[0:N_GROUPS]
        gmax = jnp.max(lg, axis=0, keepdims=True)
        p_top = 1.0 / jnp.sum(jnp.exp(lg - gmax), axis=0, keepdims=True)
        grow = lax.broadcasted_iota(jnp.int32, lg.shape, 0).astype(f32)
        g_idx = jnp.min(jnp.where(lg == gmax, grow, float(N_GROUPS)), axis=0, keepdims=True)

        sel = jnp.zeros((EXPERTS_PER_GROUP, tm), f32)
        for g in range(N_GROUPS):
            le_g = logits[SUBLANES + g * EXPERTS_PER_GROUP:SUBLANES + (g + 1) * EXPERTS_PER_GROUP]
            sel = jnp.where(g_idx == float(g), le_g, sel)
        erow = lax.broadcasted_iota(jnp.int32, sel.shape, 0).astype(f32)
        e1 = jnp.max(sel, axis=0, keepdims=True)
        i1 = jnp.min(jnp.where(sel == e1, erow, float(EXPERTS_PER_GROUP)), axis=0, keepdims=True)
        sel2 = jnp.where(erow == i1, -jnp.inf, sel)
        e2 = jnp.max(sel2, axis=0, keepdims=True)
        i2 = jnp.min(jnp.where(sel2 == e2, erow, float(EXPERTS_PER_GROUP)), axis=0, keepdims=True)
        r = jnp.exp(e2 - e1)
        w1 = p_top / (1.0 + r)
        w2 = p_top * r / (1.0 + r)
        eid1 = g_idx * float(EXPERTS_PER_GROUP) + i1
        eid2 = g_idx * float(EXPERTS_PER_GROUP) + i2
        eid_ref[:, rows] = jnp.concatenate([eid1, eid2], axis=0).astype(jnp.int32)
        wt_ref[:, rows] = jnp.concatenate([w1, w2], axis=0)

        xrow = lax.broadcasted_iota(jnp.int32, (N_EXPERTS, tm), 0).astype(f32)
        oh1 = jnp.where(xrow == eid1, 1.0, 0.0)
        oh2 = jnp.where(xrow == eid2, 1.0, 0.0)
        oh = oh1 + oh2
        before = _dot(oh.astype(bf16), upper) + base - 1.0
        rank1 = jnp.sum(oh1 * before, axis=0, keepdims=True)
        rank2 = jnp.sum(oh2 * before, axis=0, keepdims=True)
        rank_ref[:, rows] = jnp.concatenate([rank1, rank2], axis=0).astype(jnp.int32)
        base = base + jnp.sum(oh, axis=1, keepdims=True)
    base_ref[...] = base
    cnt_ref[...] = jnp.broadcast_to(base, cnt_ref.shape)


def _merge(pg, ga, o, x, w_attn_up, w_out, g2, wr, br):
    N, D = x.shape
    tm = min(ROW_TILE, N)
    subs = MERGE_SUBTILES if N % (MERGE_SUBTILES * tm) == 0 else 1
    tb = subs * tm
    aw = o.shape[1]
    const2 = lambda i: (0, 0)
    row = lambda i: (i, 0)
    colblk = lambda i: (0, i)
    kern = functools.partial(_merge_kernel, tm=tm, subs=subs, d_model=D)
    return pl.pallas_call(
        kern,
        grid=(N // tb,),
        in_specs=[
            pl.BlockSpec((tb, D), row),
            pl.BlockSpec((tb, D), row),
            pl.BlockSpec((tb, aw), row),
            pl.BlockSpec((tb, D), row),
            pl.BlockSpec((aw, D), const2),
            pl.BlockSpec((D, D), const2),
            pl.BlockSpec((1, D), const2),
            pl.BlockSpec((ROUTER_ROWS, D), const2),
            pl.BlockSpec((ROUTER_ROWS, 1), const2),
        ],
        out_specs=[
            pl.BlockSpec((tb, D), row),
            pl.BlockSpec((tb * SUBLANES, LANES), row),
            pl.BlockSpec((TOP_K, tb), colblk),
            pl.BlockSpec((TOP_K, tb), colblk),
            pl.BlockSpec((TOP_K, tb), colblk),
            pl.BlockSpec((N_EXPERTS, LANES), const2),
        ],
        out_shape=[
            jax.ShapeDtypeStruct((N, D), f32),
            jax.ShapeDtypeStruct((N * SUBLANES, LANES), f32),
            jax.ShapeDtypeStruct((TOP_K, N), jnp.int32),
            jax.ShapeDtypeStruct((TOP_K, N), f32),
            jax.ShapeDtypeStruct((TOP_K, N), jnp.int32),
            jax.ShapeDtypeStruct((N_EXPERTS, LANES), f32),
        ],
        scratch_shapes=[pltpu.VMEM((N_EXPERTS, 1), f32)],
        compiler_params=pltpu.CompilerParams(dimension_semantics=("arbitrary",), vmem_limit_bytes=VMEM_LIMIT),
        name="merge_router",
    )(pg, ga, o, x, w_attn_up, w_out, g2, wr, br)


def _slots_kernel(pstart_ref, eid_ref, rank_ref, dest_ref):
    eid = eid_ref[...]
    start = jnp.zeros_like(eid)
    for e in range(N_EXPERTS):
        start = jnp.where(eid == e, pstart_ref[e], start)
    dest_ref[...] = start + rank_ref[...]


def _slots(pstart, eid, rank):
    k, n = eid.shape
    tn = min(n, 8192)
    blk = pl.BlockSpec((k, tn), lambda i, *_: (0, i))
    return pl.pallas_call(
        _slots_kernel,
        grid_spec=pltpu.PrefetchScalarGridSpec(num_scalar_prefetch=1, grid=(n // tn,), in_specs=[blk, blk],
                                               out_specs=blk),
        out_shape=jax.ShapeDtypeStruct((k, n), jnp.int32),
        name="slots",
    )(pstart, eid, rank)


def _row(ref, r):
    start = r * SUBLANES if isinstance(r, int) else pl.multiple_of(r * SUBLANES, SUBLANES)
    return ref.at[pl.ds(start, SUBLANES), :]


def _wait_bytes_of(ref_like, any_hbm, sem):
    n = ref_like.shape[0]
    pltpu.make_async_copy(any_hbm.at[pl.ds(0, n), :], any_hbm.at[pl.ds(0, n), :], sem).wait()


STAGES = 3


def _dispatch_kernel(pstart_ref, pend_ref, dest_ref, hn_hbm, xs_hbm, zero_ref, stage, in_sem, out_sem,
                     zsem, *, tt, rows, n_blocks):
    i = pl.program_id(0)
    n = pl.num_programs(0)

    @pl.when(i == 0)
    def _():
        zero_ref[...] = jnp.zeros_like(zero_ref)
        for e in range(N_EXPERTS):
            @pl.when(pend_ref[e] > pstart_ref[e])
            def _():
                pltpu.make_async_copy(zero_ref, xs_hbm.at[pl.ds(pl.multiple_of((pend_ref[e] - rows) * SUBLANES,
                                                                               SUBLANES), rows * SUBLANES), :],
                                      zsem).start()
        def zero_block(b, carry):
            pltpu.make_async_copy(zero_ref, xs_hbm.at[pl.ds(pl.multiple_of(b * (rows * SUBLANES), SUBLANES),
                                                            rows * SUBLANES), :], zsem).start()
            return carry

        def wait_block(b, carry):
            pltpu.make_async_copy(zero_ref, xs_hbm.at[pl.ds(0, rows * SUBLANES), :], zsem).wait()
            return carry

        first_unused = pend_ref[N_EXPERTS - 1] // rows
        lax.fori_loop(first_unused, n_blocks, zero_block, 0)
        for e in range(N_EXPERTS):
            @pl.when(pend_ref[e] > pstart_ref[e])
            def _():
                wait_block(0, 0)
        lax.fori_loop(first_unused, n_blocks, wait_block, 0)

    def stage_copy(step):
        s = step % STAGES
        src = hn_hbm.at[pl.ds(pl.multiple_of(step * (tt * SUBLANES), SUBLANES), tt * SUBLANES), :]
        return pltpu.make_async_copy(src, stage.at[s], in_sem.at[s])

    def wait_rows_of(step):
        for _ in range(TOP_K):
            _wait_bytes_of(stage.at[0], hn_hbm, out_sem.at[step % STAGES])

    @pl.when(i == 0)
    def _():
        stage_copy(i).start()

    @pl.when(i >= STAGES - 1)
    def _():
        wait_rows_of(i - (STAGES - 1))

    @pl.when(i + 1 < n)
    def _():
        stage_copy(i + 1).start()

    stage_copy(i).wait()
    slot = i % STAGES

    def body(t, carry):
        for k in range(TOP_K):
            pltpu.make_async_copy(_row(stage.at[slot], t), _row(xs_hbm, dest_ref[0, k * tt + t]),
                                  out_sem.at[slot]).start(priority=k)
        return carry
    lax.fori_loop(0, tt, body, 0, unroll=8)

    @pl.when(i == n - 1)
    def _():
        for back in range(STAGES - 2, -1, -1):
            @pl.when(i - back >= 0)
            def _():
                wait_rows_of(i - back)


def _dispatch(pstart, pend, dest_blocks, hn_g, n_slots):
    n, _, width = dest_blocks.shape
    tt = width // TOP_K
    rows = EXPERT_ROWS
    kern = functools.partial(_dispatch_kernel, tt=tt, rows=rows, n_blocks=n_slots // rows)
    smem_blk = pl.BlockSpec((None, 1, width), lambda i, *_: (i, 0, 0), memory_space=pltpu.SMEM)
    return pl.pallas_call(
        kern,
        grid_spec=pltpu.PrefetchScalarGridSpec(
            num_scalar_prefetch=2,
            grid=(n,),
            in_specs=[smem_blk, pl.BlockSpec(memory_space=pl.ANY)],
            out_specs=pl.BlockSpec(memory_space=pl.ANY),
            scratch_shapes=[pltpu.VMEM((rows * SUBLANES, LANES), f32),
                            pltpu.VMEM((STAGES, tt * SUBLANES, LANES), f32),
                            pltpu.SemaphoreType.DMA((STAGES,)), pltpu.SemaphoreType.DMA((STAGES,)),
                            pltpu.SemaphoreType.DMA(())],
        ),
        out_shape=jax.ShapeDtypeStruct((n_slots * SUBLANES, LANES), f32),
        compiler_params=pltpu.CompilerParams(dimension_semantics=("arbitrary",), vmem_limit_bytes=VMEM_LIMIT,
                                             has_side_effects=True),
        name="dispatch",
    )(pstart, pend, dest_blocks, hn_g)


def _rows_from_token_major(buf, n_rows, n_tiles):
    return jnp.concatenate([buf[pl.ds(s, n_rows, stride=SUBLANES), :] for s in range(n_tiles)], axis=-1)


def _expert_kernel(beid_ref, nbu_ref, xs_ref, wg_ref, wu_ref, wd_ref, ys_ref, wg_bf, wu_bf, wd_bf, *, rows,
                   d_model):
    b = pl.program_id(0)
    changed = jnp.logical_or(b == 0, beid_ref[b] != beid_ref[jnp.maximum(b - 1, 0)])

    @pl.when(changed)
    def _():
        wg_bf[...] = wg_ref[...].astype(bf16)
        wu_bf[...] = wu_ref[...].astype(bf16)
        wd_bf[...] = wd_ref[...].astype(bf16)

    n_tiles = d_model // LANES

    @pl.when(b < nbu_ref[0])
    def _():
        xb = _rows_from_token_major(xs_ref, rows, n_tiles).astype(bf16)
        hdn = jax.nn.silu(_dot(xb, wg_bf[...])) * _dot(xb, wu_bf[...])
        y = _dot(hdn.astype(bf16), wd_bf[...])
        for s in range(n_tiles):
            ys_ref[pl.ds(s, rows, stride=SUBLANES), :] = y[:, s * LANES:(s + 1) * LANES]

    @pl.when(b >= nbu_ref[0])
    def _():
        ys_ref[...] = jnp.zeros_like(ys_ref)


def _experts(block_eid, n_used, xs, w_gate, w_up, w_down):
    n_blocks = block_eid.shape[0]
    rows = EXPERT_ROWS
    E, D, DE = w_gate.shape
    kern = functools.partial(_expert_kernel, rows=rows, d_model=D)
    xs_idx = lambda b, eid, nbu: (jnp.minimum(b, nbu[0] - 1), 0)
    return pl.pallas_call(
        kern,
        grid_spec=pltpu.PrefetchScalarGridSpec(
            num_scalar_prefetch=2,
            grid=(n_blocks,),
            in_specs=[
                pl.BlockSpec((rows * SUBLANES, LANES), xs_idx),
                pl.BlockSpec((None, D, DE), lambda b, eid, nbu: (eid[b], 0, 0)),
                pl.BlockSpec((None, D, DE), lambda b, eid, nbu: (eid[b], 0, 0)),
                pl.BlockSpec((None, DE, D), lambda b, eid, nbu: (eid[b], 0, 0)),
            ],
            out_specs=pl.BlockSpec((rows * SUBLANES, LANES), lambda b, *_: (b, 0)),
            scratch_shapes=[
                pltpu.VMEM((D, DE), bf16),
                pltpu.VMEM((D, DE), bf16),
                pltpu.VMEM((DE, D), bf16),
            ],
        ),
        out_shape=jax.ShapeDtypeStruct((n_blocks * rows * SUBLANES, LANES), f32),
        compiler_params=pltpu.CompilerParams(dimension_semantics=("arbitrary",), vmem_limit_bytes=VMEM_LIMIT),
        name="experts",
    )(block_eid, n_used, xs, w_gate, w_up, w_down)


def _combine_kernel(dest_ref, destn_ref, x1_ref, wt_ref, ys_hbm, out_ref, buf, sem, *, te, d_model):
    i = pl.program_id(0)
    n = pl.num_programs(0)
    slot = i % 2

    def start_gather(d_ref, dst, dsem):
        for t in range(te):
            for k in range(TOP_K):
                r = k * te + t
                pltpu.make_async_copy(_row(ys_hbm, d_ref[0, r]), _row(dst, r), dsem).start(priority=k)

    @pl.when(i == 0)
    def _():
        start_gather(dest_ref, buf.at[0], sem.at[0])

    @pl.when(i + 1 < n)
    def _():
        start_gather(destn_ref, buf.at[1 - slot], sem.at[1 - slot])

    _wait_bytes_of(buf.at[slot], ys_hbm, sem.at[slot])
    n_tiles = d_model // LANES
    both = _rows_from_token_major(buf.at[slot], TOP_K * te, n_tiles)
    wt = jnp.concatenate([wt_ref[...], jnp.zeros((SUBLANES - TOP_K, te), f32)], axis=0).T
    out_ref[...] = x1_ref[...] + (wt[:, 0:1] * both[:te] + wt[:, 1:2] * both[te:])


def _combine(dest_blocks, x1, wt_rows, ys):
    N, D = x1.shape
    n, _, width = dest_blocks.shape
    te = width // TOP_K
    kern = functools.partial(_combine_kernel, te=te, d_model=D)
    cur = pl.BlockSpec((None, 1, width), lambda i: (i, 0, 0), memory_space=pltpu.SMEM)
    nxt = pl.BlockSpec((None, 1, width), lambda i: (jnp.minimum(i + 1, n - 1), 0, 0), memory_space=pltpu.SMEM)
    return pl.pallas_call(
        kern,
        grid=(n,),
        in_specs=[
            cur, nxt,
            pl.BlockSpec((te, D), lambda i: (i, 0)),
            pl.BlockSpec((TOP_K, te), lambda i: (0, i)),
            pl.BlockSpec(memory_space=pl.ANY),
        ],
        out_specs=pl.BlockSpec((te, D), lambda i: (i, 0)),
        scratch_shapes=[pltpu.VMEM((2, TOP_K * te * SUBLANES, LANES), f32), pltpu.SemaphoreType.DMA((2,))],
        out_shape=jax.ShapeDtypeStruct((N, D), f32),
        compiler_params=pltpu.CompilerParams(dimension_semantics=("arbitrary",), vmem_limit_bytes=VMEM_LIMIT),
        name="combine",
    )(dest_blocks, dest_blocks, x1, wt_rows, ys)


def _token_blocks(a, tt):
    k, n = a.shape
    return a.reshape(k, n // tt, tt).transpose(1, 0, 2).reshape(n // tt, 1, k * tt)


def _layer(x, l, norm1_g, w_in, pool_w, pool_scale, w_pool_up, q_norm_g, k_norm_g, lambda_q1, lambda_k1,
           lambda_q2, lambda_k2, subln_g, w_attn_up, w_out, norm2_g, w_router_group, b_router_group,
           w_router_expert, b_router_expert, w_expert_gate, w_expert_up, w_expert_down):
    B, S, D = x.shape
    N = B * S
    head_dim = q_norm_g.shape[0]
    lam_init = 0.8 - 0.6 * math.exp(-0.3 * l)
    reps = (N_HEADS * 2 * head_dim) // head_dim

    qg = (jnp.tile(q_norm_g, reps) * (head_dim ** -0.5 * LOG2E))[None, :]
    kg = jnp.tile(k_norm_g, reps)[None, :]
    pg, ga, qn, kn, vt = _in_proj(x, norm1_g[None, :], w_in.astype(bf16), pool_w.astype(bf16),
                                  pool_scale[None, :], w_pool_up.astype(bf16), qg, kg)

    tk = min(ATTN_TILE, S)
    slopes2 = jnp.asarray([2.0 ** (-8.0 * (h + 1) / N_HEADS) * LOG2E for h in range(N_HEADS)], f32)
    fq, fk = _alibi_features(slopes2, 2 * tk, tk)
    o = _attention(qn, kn, vt, fq, fk, slopes2, lambda_q1[None, :], lambda_k1[None, :], lambda_q2[None, :],
                   lambda_k2[None, :], subln_g[:, None], lam_init)

    wr = jnp.zeros((ROUTER_ROWS, D), f32)
    wr = wr.at[:N_GROUPS].set(w_router_group.T).at[SUBLANES:].set(w_router_expert.T).astype(bf16)
    br = jnp.zeros((ROUTER_ROWS, 1), f32)
    br = br.at[:N_GROUPS, 0].set(b_router_group).at[SUBLANES:, 0].set(b_router_expert)
    x1, hn_g, eid, wts, rank, cnt = _merge(pg.reshape(N, D), ga.reshape(N, D), o.reshape(N, -1),
                                           x.reshape(N, D), w_attn_up.astype(bf16), w_out.astype(bf16),
                                           norm2_g[None, :], wr, br)

    R = EXPERT_ROWS
    counts = cnt[:, 0].astype(jnp.int32)
    padded = (counts + R - 1) // R * R
    pend = jnp.cumsum(padded).astype(jnp.int32)
    pstart = pend - padded
    n_blocks = -(-(N * TOP_K) // R) + N_EXPERTS
    starts = jnp.arange(n_blocks, dtype=jnp.int32) * R
    block_eid = jnp.minimum(jnp.sum((pend[None, :] <= starts[:, None]).astype(jnp.int32), axis=1), N_EXPERTS - 1)
    n_used = pend[-1:] // R

    dest = _slots(pstart, eid, rank)
    xs = _dispatch(pstart, pend, _token_blocks(dest, min(DISPATCH_TILE, N)), hn_g, n_blocks * R)
    ys = _experts(block_eid, n_used, xs, w_expert_gate, w_expert_up, w_expert_down)
    out = _combine(_token_blocks(dest, min(COMBINE_TILE, N)), x1, wts, ys)
    return out.reshape(B, S, D)


def kernel(x, norm1_g, w_in, pool_w, pool_scale, w_pool_up, q_norm_g, k_norm_g, lambda_q1, lambda_k1, lambda_q2,
           lambda_k2, subln_g, w_attn_up, w_out, norm2_g, w_router_group, b_router_group, w_router_expert,
           b_router_expert, w_expert_gate, w_expert_up, w_expert_down):
    params = (norm1_g, w_in, pool_w, pool_scale, w_pool_up, q_norm_g, k_norm_g, lambda_q1, lambda_k1, lambda_q2,
              lambda_k2, subln_g, w_attn_up, w_out, norm2_g, w_router_group, b_router_group, w_router_expert,
              b_router_expert, w_expert_gate, w_expert_up, w_expert_down)
    for l in range(norm1_g.shape[0]):
        x = _layer(x, l, *(p[l] for p in params))
    return x
```

```python
import functools
import math

import jax
import jax.numpy as jnp
from jax import lax
from jax.experimental import pallas as pl
from jax.experimental.pallas import tpu as pltpu

EPS = 1e-6
POOL_WINDOWS = (2, 4, 8, 16)
POOL_HALO = 16
N_HEADS = 4
N_GROUPS = 4
EXPERTS_PER_GROUP = 8
N_EXPERTS = N_GROUPS * EXPERTS_PER_GROUP
TOP_K = 2
LANES = 128
SUBLANES = 8
ROUTER_ROWS = 8 + N_EXPERTS
VT_ROWS = LANES + 16
LOG2E = 1.4426950408889634

ROW_TILE = 512
IN_PROJ_SUBTILES = 2
MERGE_SUBTILES = 2
ATTN_TILE = 256
ATTN_HEADS_PER_STEP = 4
EXPERT_ROWS = 512
DISPATCH_TILE = 512
COMBINE_TILE = 256
VMEM_LIMIT = 52 * 1024 * 1024

f32 = jnp.float32
bf16 = jnp.bfloat16


def _dot(a, b):
    return jnp.dot(a, b, preferred_element_type=f32)


def _dot_nt(a, b):
    return lax.dot_general(a, b, (((1,), (1,)), ((), ())), preferred_element_type=f32)


def _half_lane_rmsnorm(t, n_tiles, rows):
    lane = lax.broadcasted_iota(jnp.int32, (rows, LANES), 1)
    lo_mask = lane < (LANES // 2)
    outs = []
    for i in range(n_tiles):
        c = t[:, i * LANES:(i + 1) * LANES]
        sq = c * c
        lo = jnp.sum(jnp.where(lo_mask, sq, 0.0), axis=-1, keepdims=True)
        hi = jnp.sum(jnp.where(lo_mask, 0.0, sq), axis=-1, keepdims=True)
        ms = jnp.where(lo_mask, lo, hi) * (2.0 / LANES)
        outs.append(c * lax.rsqrt(ms + EPS))
    return jnp.concatenate(outs, axis=-1)


def _in_proj_kernel(x_ref, g1_ref, win_ref, poolw_ref, pscale_ref, wpu_ref, qg_ref, kg_ref,
                    pg_ref, ga_ref, qn_ref, kn_ref, vt_ref, prev_ref, wfold_ref, *, tm, subs, tk, d_model, pool_width,
                    qk_width, attn_width):
    j = pl.program_id(1)
    off_q = pool_width
    off_k = off_q + qk_width
    off_v = off_k + qk_width
    off_gp = off_v + attn_width
    off_ga = off_gp + d_model
    group = pool_width // len(POOL_WINDOWS)
    half = d_model // 2
    n_tiles = qk_width // LANES

    @pl.when(jnp.logical_and(pl.program_id(0) == 0, j == 0))
    def _():
        for g in range(len(POOL_WINDOWS)):
            rows_g = slice(g * group, (g + 1) * group)
            scaled = (poolw_ref[g].astype(f32) * pscale_ref[:, rows_g]).astype(bf16)
            wfold_ref[rows_g, :] = _dot(scaled, wpu_ref[rows_g, :]).astype(bf16)

    @pl.when(j == 0)
    def _():
        prev_ref[...] = jnp.zeros_like(prev_ref)

    extra = lax.broadcasted_iota(jnp.int32, (VT_ROWS - LANES, tk), 0)
    ones_rows = jnp.where(extra == 0, 1.0, 0.0).astype(bf16)
    halo = prev_ref[...]
    for sub in range(subs):
        rows = slice(sub * tm, (sub + 1) * tm)
        x = x_ref[rows, :]
        ms = jnp.mean(x * x, axis=-1, keepdims=True)
        h = (x * lax.rsqrt(ms + EPS) * g1_ref[...]).astype(bf16)

        def proj(lo, width, h=h):
            return _dot(h, win_ref[:, lo:lo + width])

        u = proj(0, pool_width)
        v = proj(off_v, attn_width)
        for hh in range(attn_width // LANES):
            for c in range(tm // tk):
                blk = sub * (tm // tk) + c
                vt_ref[hh, blk, 0:LANES, :] = v[c * tk:(c + 1) * tk, hh * LANES:(hh + 1) * LANES].T.astype(bf16)
                vt_ref[hh, blk, LANES:VT_ROWS, :] = ones_rows
        qn_ref[rows, :] = (_half_lane_rmsnorm(proj(off_q, qk_width), n_tiles, tm) * qg_ref[...]).astype(bf16)
        kn_ref[rows, :] = (_half_lane_rmsnorm(proj(off_k, qk_width), n_tiles, tm) * kg_ref[...]).astype(bf16)

        ext = jnp.concatenate([halo, u], axis=0)
        halo = u[tm - POOL_HALO:, :]
        pos = (j * subs + sub) * tm + lax.broadcasted_iota(jnp.int32, (tm, 1), 0)
        ds = []
        for g, w in enumerate(POOL_WINDOWS):
            acc = ext[:, g * group:(g + 1) * group]
            shift = 1
            while shift < w:
                acc = acc + pltpu.roll(acc, shift, 0)
                shift *= 2
            wsum = acc[POOL_HALO:, :]
            cnt = jnp.minimum(pos + 1, w).astype(f32)
            ds.append((wsum / cnt - u[:, g * group:(g + 1) * group]).astype(bf16))
        pool_out = _dot(jnp.concatenate(ds, axis=-1), wfold_ref[...])

        for c in range(2):
            cols = slice(c * half, (c + 1) * half)
            gp = jax.nn.sigmoid(proj(off_gp + c * half, half))
            pg_ref[rows, cols] = (gp * pool_out[:, cols]).astype(bf16)
            ga_ref[rows, cols] = jax.nn.sigmoid(proj(off_ga + c * half, half)).astype(bf16)
    prev_ref[...] = halo


def _in_proj(x, g1, w_in, pool_w, pool_scale, w_pool_up, qg, kg):
    B, S, D = x.shape
    subs = IN_PROJ_SUBTILES if S % (IN_PROJ_SUBTILES * ROW_TILE) == 0 else 1
    tm = min(ROW_TILE, S)
    pool_width = w_pool_up.shape[0]
    qk_width = qg.shape[1]
    attn_width = qk_width
    in_width = w_in.shape[1]
    const2 = lambda b, j: (0, 0)
    row = lambda b, j: (b, j, 0)
    tk = min(ATTN_TILE, S)
    n_heads = attn_width // LANES
    kern = functools.partial(_in_proj_kernel, tm=tm, subs=subs, tk=tk, d_model=D, pool_width=pool_width,
                             qk_width=qk_width, attn_width=attn_width)
    tb = subs * tm
    return pl.pallas_call(
        kern,
        grid=(B, S // tb),
        in_specs=[
            pl.BlockSpec((None, tb, D), row),
            pl.BlockSpec((1, D), const2),
            pl.BlockSpec((D, in_width), const2, pipeline_mode=pl.Buffered(1)),
            pl.BlockSpec(pool_w.shape, lambda b, j: (0, 0, 0)),
            pl.BlockSpec((1, pool_width), const2),
            pl.BlockSpec((pool_width, D), const2),
            pl.BlockSpec((1, qk_width), const2),
            pl.BlockSpec((1, qk_width), const2),
        ],
        out_specs=[
            pl.BlockSpec((None, tb, D), row),
            pl.BlockSpec((None, tb, D), row),
            pl.BlockSpec((None, tb, qk_width), row),
            pl.BlockSpec((None, tb, qk_width), row),
            pl.BlockSpec((None, n_heads, tb // tk, VT_ROWS, tk), lambda b, j: (b, 0, j, 0, 0)),
        ],
        out_shape=[
            jax.ShapeDtypeStruct((B, S, D), bf16),
            jax.ShapeDtypeStruct((B, S, D), bf16),
            jax.ShapeDtypeStruct((B, S, qk_width), bf16),
            jax.ShapeDtypeStruct((B, S, qk_width), bf16),
            jax.ShapeDtypeStruct((B, n_heads, S // tk, VT_ROWS, tk), bf16),
        ],
        scratch_shapes=[pltpu.VMEM((POOL_HALO, pool_width), f32), pltpu.VMEM((pool_width, D), bf16)],
        compiler_params=pltpu.CompilerParams(dimension_semantics=("arbitrary", "arbitrary"),
                                             vmem_limit_bytes=VMEM_LIMIT),
        name="in_proj",
    )(x, g1, w_in, pool_w, pool_scale, w_pool_up, qg, kg)


def _attn_kernel(slopes_ref, q_ref, k_ref, vt_ref, fq_ref, fk_ref, lq1_ref, lk1_ref, lq2_ref, lk2_ref, sg_ref,
                 o_ref, qa_ref, sa_ref, sb_ref, pa_ref, pb_ref, m_ref, alpha_ref, acc_ref, *,
                 tq, tk, heads, lam_init):
    hg = pl.program_id(1)
    i = pl.program_id(2)
    fk = fk_ref[...]
    lane = lax.broadcasted_iota(jnp.int32, (tk, LANES), 1)
    first = lane < (LANES // 2)
    hs = range(heads)

    buf_a = (sa_ref, pa_ref)
    buf_b = (sb_ref, pb_ref)

    def scores(g, n, buf):
        kb = k_ref[pl.ds(pl.multiple_of(n * tk, tk), tk), g * LANES:(g + 1) * LANES]
        buf[0][g] = _dot_nt(jnp.concatenate([kb, fk], axis=1), qa_ref[g])

    def softmax(g, n, buf, first_half_mask=None):
        s_ref, p_ref = buf
        c = -slopes_ref[hg * heads + g] * (i * tq - n * tk).astype(f32)
        s = s_ref[g]
        if first_half_mask is not None:
            s = jnp.concatenate([jnp.where(first_half_mask, s[:, :tq], -jnp.inf), s[:, tq:]], axis=1)
        m_old = m_ref[g]
        m_new = jnp.maximum(m_old, jnp.max(s, axis=0, keepdims=True) + c)
        p_ref[g] = jnp.exp2(s - (m_new - c)).astype(bf16)
        m_ref[g] = m_new
        return jnp.exp2(m_old - m_new)

    def pv(g, n, buf, alpha):
        acc_ref[g] = alpha * acc_ref[g] + _dot(vt_ref[g, jnp.maximum(n, 0)], buf[1][g])

    m_ref[...] = jnp.full(m_ref.shape, -jnp.inf, f32)
    alpha_ref[...] = jnp.ones_like(alpha_ref)
    acc_ref[...] = jnp.zeros_like(acc_ref)
    pb_ref[...] = jnp.zeros_like(pb_ref)
    zero = jnp.zeros((tk, LANES), bf16)
    for g in hs:
        parts = []
        for half in range(2):
            q = q_ref[half * tk:(half + 1) * tk, g * LANES:(g + 1) * LANES]
            parts += [jnp.where(first, q, zero), jnp.where(first, zero, q)]
        qa_ref[g] = jnp.concatenate([jnp.concatenate(parts, axis=0), fq_ref[g]], axis=1)
    for g in hs:
        scores(g, 0, buf_a)

    def pair(t, carry):
        n = 2 * t
        a_prev = [alpha_ref[g] for g in hs]
        a_even = [None] * heads
        for g in hs:
            pv(g, n - 1, buf_b, a_prev[g])
            a_even[g] = softmax(g, n, buf_a)
            scores(g, n + 1, buf_b)
        for g in hs:
            pv(g, n, buf_a, a_even[g])
            alpha_ref[g] = softmax(g, n + 1, buf_b)
            scores(g, n + 2, buf_a)
        return carry

    lax.fori_loop(0, i, pair, 0)
    n = 2 * i
    lam = (jnp.exp(jnp.sum(lq1_ref[...] * lk1_ref[...], keepdims=True))
           - jnp.exp(jnp.sum(lq2_ref[...] * lk2_ref[...], keepdims=True)) + lam_init)
    late = slice(tq, 2 * tq)
    kk = lax.broadcasted_iota(jnp.int32, (tk, tq), 0)
    qq = lax.broadcasted_iota(jnp.int32, (tk, tq), 1)
    tri = kk <= jnp.where(qq >= tk, qq - tk, qq)
    for g in hs:
        a_even = softmax(g, n, buf_a, first_half_mask=tri)
        pv(g, n - 1, buf_b, alpha_ref[g])
        pv(g, n, buf_a, a_even)
        kb = k_ref[pl.ds(pl.multiple_of((n + 1) * tk, tk), tk), g * LANES:(g + 1) * LANES]
        s = _dot_nt(jnp.concatenate([kb, fk], axis=1), qa_ref[g, late, :])
        s = jnp.where(tri, s, -jnp.inf)
        c = slopes_ref[hg * heads + g] * float(tk)
        m_old = m_ref[g, :, late]
        m_new = jnp.maximum(m_old, jnp.max(s, axis=0, keepdims=True) + c)
        p = jnp.exp2(s - (m_new - c)).astype(bf16)
        acc = acc_ref[g]
        acc_late = jnp.exp2(m_old - m_new) * acc[:, late] + _dot(vt_ref[g, n + 1], p)
        acc = jnp.concatenate([acc[:, :tq], acc_late], axis=1)
        o_all = acc[0:LANES] / acc[LANES:LANES + 1]
        o = jnp.concatenate([o_all[:, 0:tk] - lam * o_all[:, tk:tq],
                             o_all[:, tq:tq + tk] - lam * o_all[:, tq + tk:]], axis=1)
        ms = jnp.mean(o * o, axis=0, keepdims=True)
        on = o * lax.rsqrt(ms + EPS) * sg_ref[...] * (1.0 - lam_init)
        o_ref[:, g * LANES:(g + 1) * LANES] = on.T.astype(bf16)


def _split_bf16(x, pieces=3):
    out = []
    for _ in range(pieces):
        p = x.astype(bf16)
        out.append(p)
        x = x - p.astype(f32)
    return out


def _alibi_features(slopes2, tq, tk):
    assert tk <= 256
    assert tq == 2 * tk
    n_heads = slopes2.shape[0]
    ones = jnp.ones((tk, 1), bf16)
    krel = jnp.arange(tk, dtype=f32).astype(bf16)[:, None]
    fk = jnp.concatenate([krel] * 3 + [ones] * 3 + [jnp.zeros((tk, LANES - 6), bf16)], axis=1)
    first, last = jnp.arange(tk, dtype=f32), jnp.arange(tk, tq, dtype=f32)
    qrel = jnp.concatenate([first, first, last, last])
    a = _split_bf16(slopes2)
    b = _split_bf16(-slopes2[:, None] * qrel[None, :])
    cols = [jnp.broadcast_to(p[:, None, None], (n_heads, 2 * tq, 1)) for p in a] + [p[:, :, None] for p in b]
    fq = jnp.concatenate(cols + [jnp.zeros((n_heads, 2 * tq, LANES - 6), bf16)], axis=2)
    return fq, fk


def _attention(qn, kn, vt, fq, fk, slopes, lq1, lk1, lq2, lk2, subln_col, lam_init):
    B, S, _ = qn.shape
    tk = fk.shape[0]
    tq = fq.shape[1] // 2
    assert tq == 2 * tk
    nkv = S // tk
    const2 = lambda b, h, i, *_: (0, 0)
    G = ATTN_HEADS_PER_STEP
    kern = functools.partial(_attn_kernel, tq=tq, tk=tk, heads=G, lam_init=lam_init)
    hd = lq1.shape[1]
    return pl.pallas_call(
        kern,
        grid_spec=pltpu.PrefetchScalarGridSpec(
            num_scalar_prefetch=1,
            grid=(B, N_HEADS // G, S // tq),
            in_specs=[
                pl.BlockSpec((None, tq, G * LANES), lambda b, h, i, *_: (b, i, h)),
                pl.BlockSpec((None, S, G * LANES), lambda b, h, i, *_: (b, 0, h)),
                pl.BlockSpec((None, G, nkv, VT_ROWS, tk), lambda b, h, i, *_: (b, h, 0, 0, 0)),
                pl.BlockSpec((G, 2 * tq, LANES), lambda b, h, i, *_: (h, 0, 0)),
                pl.BlockSpec((tk, LANES), const2),
                pl.BlockSpec((1, hd), const2),
                pl.BlockSpec((1, hd), const2),
                pl.BlockSpec((1, hd), const2),
                pl.BlockSpec((1, hd), const2),
                pl.BlockSpec((LANES, 1), const2),
            ],
            out_specs=pl.BlockSpec((None, tq, G * LANES), lambda b, h, i, *_: (b, i, h)),
            scratch_shapes=[pltpu.VMEM((G, 2 * tq, 2 * LANES), bf16),
                            pltpu.VMEM((G, tk, 2 * tq), f32), pltpu.VMEM((G, tk, 2 * tq), f32),
                            pltpu.VMEM((G, tk, 2 * tq), bf16), pltpu.VMEM((G, tk, 2 * tq), bf16),
                            pltpu.VMEM((G, 1, 2 * tq), f32), pltpu.VMEM((G, 1, 2 * tq), f32),
                            pltpu.VMEM((G, VT_ROWS, 2 * tq), f32)],
        ),
        out_shape=jax.ShapeDtypeStruct((B, S, N_HEADS * LANES), bf16),
        compiler_params=pltpu.CompilerParams(dimension_semantics=("arbitrary", "arbitrary", "arbitrary"),
                                             vmem_limit_bytes=VMEM_LIMIT),
        name="diff_attn",
    )(slopes, qn, kn, vt, fq, fk, lq1, lk1, lq2, lk2, subln_col)


def _merge_kernel(pg_ref, ga_ref, o_ref, x_ref, wau_ref, wout_ref, g2_ref, wr_ref, br_ref,
                  x1_ref, hn_ref, eid_ref, wt_ref, rank_ref, cnt_ref, base_ref, *, tm, subs, d_model):
    step = pl.program_id(0)

    @pl.when(step == 0)
    def _():
        base_ref[...] = jnp.zeros_like(base_ref)

    a = lax.broadcasted_iota(jnp.int32, (tm, tm), 0)
    b = lax.broadcasted_iota(jnp.int32, (tm, tm), 1)
    upper = jnp.where(a <= b, 1.0, 0.0).astype(bf16)
    base = base_ref[...]
    all_logits = []
    for sub in range(subs):
        rows = slice(sub * tm, (sub + 1) * tm)
        attn_out = _dot(o_ref[rows, :], wau_ref[...])
        merged = pg_ref[rows, :].astype(f32) + ga_ref[rows, :].astype(f32) * attn_out
        x1 = x_ref[rows, :] + _dot(merged.astype(bf16), wout_ref[...])
        x1_ref[rows, :] = x1
        ms = jnp.mean(x1 * x1, axis=-1, keepdims=True)
        hn = (x1 * lax.rsqrt(ms + EPS) * g2_ref[...]).astype(bf16)
        hn_ref[rows, :] = hn

        all_logits.append(_dot_nt(wr_ref[...], hn) + br_ref[...])

    for sub in range(subs):
        rows = slice(sub * tm, (sub + 1) * tm)
        logits = all_logits[sub]
        lg = logits[0:N_GROUPS]
        gmax = jnp.max(lg, axis=0, keepdims=True)
        p_top = 1.0 / jnp.sum(jnp.exp(lg - gmax), axis=0, keepdims=True)
        grow = lax.broadcasted_iota(jnp.int32, lg.shape, 0).astype(f32)
        g_idx = jnp.min(jnp.where(lg == gmax, grow, float(N_GROUPS)), axis=0, keepdims=True)

        sel = jnp.zeros((EXPERTS_PER_GROUP, tm), f32)
        for g in range(N_GROUPS):
            le_g = logits[SUBLANES + g * EXPERTS_PER_GROUP:SUBLANES + (g + 1) * EXPERTS_PER_GROUP]
            sel = jnp.where(g_idx == float(g), le_g, sel)
        erow = lax.broadcasted_iota(jnp.int32, sel.shape, 0).astype(f32)
        e1 = jnp.max(sel, axis=0, keepdims=True)
        i1 = jnp.min(jnp.where(sel == e1, erow, float(EXPERTS_PER_GROUP)), axis=0, keepdims=True)
        sel2 = jnp.where(erow == i1, -jnp.inf, sel)
        e2 = jnp.max(sel2, axis=0, keepdims=True)
        i2 = jnp.min(jnp.where(sel2 == e2, erow, float(EXPERTS_PER_GROUP)), axis=0, keepdims=True)
        r = jnp.exp(e2 - e1)
        w1 = p_top / (1.0 + r)
        w2 = p_top * r / (1.0 + r)
        eid1 = g_idx * float(EXPERTS_PER_GROUP) + i1
        eid2 = g_idx * float(EXPERTS_PER_GROUP) + i2
        eid_ref[:, rows] = jnp.concatenate([eid1, eid2], axis=0).astype(jnp.int32)
        wt_ref[:, rows] = jnp.concatenate([w1, w2], axis=0)

        xrow = lax.broadcasted_iota(jnp.int32, (N_EXPERTS, tm), 0).astype(f32)
        oh1 = jnp.where(xrow == eid1, 1.0, 0.0)
        oh2 = jnp.where(xrow == eid2, 1.0, 0.0)
        oh = oh1 + oh2
        before = _dot(oh.astype(bf16), upper) + base - 1.0
        rank1 = jnp.sum(oh1 * before, axis=0, keepdims=True)
        rank2 = jnp.sum(oh2 * before, axis=0, keepdims=True)
        rank_ref[:, rows] = jnp.concatenate([rank1, rank2], axis=0).astype(jnp.int32)
        base = base + jnp.sum(oh, axis=1, keepdims=True)
    base_ref[...] = base
    cnt_ref[...] = jnp.broadcast_to(base, cnt_ref.shape)


def _merge(pg, ga, o, x, w_attn_up, w_out, g2, wr, br):
    N, D = x.shape
    tm = min(ROW_TILE, N)
    subs = MERGE_SUBTILES if N % (MERGE_SUBTILES * tm) == 0 else 1
    tb = subs * tm
    aw = o.shape[1]
    const2 = lambda i: (0, 0)
    row = lambda i: (i, 0)
    colblk = lambda i: (0, i)
    kern = functools.partial(_merge_kernel, tm=tm, subs=subs, d_model=D)
    return pl.pallas_call(
        kern,
        grid=(N // tb,),
        in_specs=[
            pl.BlockSpec((tb, D), row),
            pl.BlockSpec((tb, D), row),
            pl.BlockSpec((tb, aw), row),
            pl.BlockSpec((tb, D), row),
            pl.BlockSpec((aw, D), const2),
            pl.BlockSpec((D, D), const2),
            pl.BlockSpec((1, D), const2),
            pl.BlockSpec((ROUTER_ROWS, D), const2),
            pl.BlockSpec((ROUTER_ROWS, 1), const2),
        ],
        out_specs=[
            pl.BlockSpec((tb, D), row),
            pl.BlockSpec((tb, D), row),
            pl.BlockSpec((TOP_K, tb), colblk),
            pl.BlockSpec((TOP_K, tb), colblk),
            pl.BlockSpec((TOP_K, tb), colblk),
            pl.BlockSpec((N_EXPERTS, LANES), const2),
        ],
        out_shape=[
            jax.ShapeDtypeStruct((N, D), f32),
            jax.ShapeDtypeStruct((N, D), bf16),
            jax.ShapeDtypeStruct((TOP_K, N), jnp.int32),
            jax.ShapeDtypeStruct((TOP_K, N), f32),
            jax.ShapeDtypeStruct((TOP_K, N), jnp.int32),
            jax.ShapeDtypeStruct((N_EXPERTS, LANES), f32),
        ],
        scratch_shapes=[pltpu.VMEM((N_EXPERTS, 1), f32)],
        compiler_params=pltpu.CompilerParams(dimension_semantics=("arbitrary",), vmem_limit_bytes=VMEM_LIMIT),
        name="merge_router",
    )(pg, ga, o, x, w_attn_up, w_out, g2, wr, br)


def _slots_kernel(pstart_ref, eid_ref, rank_ref, dest_ref):
    eid = eid_ref[...]
    start = jnp.zeros_like(eid)
    for e in range(N_EXPERTS):
        start = jnp.where(eid == e, pstart_ref[e], start)
    dest_ref[...] = start + rank_ref[...]


def _slots(pstart, eid, rank):
    k, n = eid.shape
    tn = min(n, 8192)
    blk = pl.BlockSpec((k, tn), lambda i, *_: (0, i))
    return pl.pallas_call(
        _slots_kernel,
        grid_spec=pltpu.PrefetchScalarGridSpec(num_scalar_prefetch=1, grid=(n // tn,), in_specs=[blk, blk],
                                               out_specs=blk),
        out_shape=jax.ShapeDtypeStruct((k, n), jnp.int32),
        name="slots",
    )(pstart, eid, rank)


def _row(ref, r):
    start = r * SUBLANES if isinstance(r, int) else pl.multiple_of(r * SUBLANES, SUBLANES)
    return ref.at[pl.ds(start, SUBLANES), :]


def _wait_bytes_of(ref_like, any_hbm, sem):
    n = ref_like.shape[0]
    pltpu.make_async_copy(any_hbm.at[pl.ds(0, n), :], any_hbm.at[pl.ds(0, n), :], sem).wait()


STAGES = 3


def _dispatch_kernel(pstart_ref, pend_ref, dest_ref, hn_ref, xs_hbm, zero_ref, stage, out_sem, zsem, *, tt, rows,
                     n_blocks, d_model):
    i = pl.program_id(0)
    n = pl.num_programs(0)

    @pl.when(i == 0)
    def _():
        zero_ref[...] = jnp.zeros_like(zero_ref)
        for e in range(N_EXPERTS):
            @pl.when(pend_ref[e] > pstart_ref[e])
            def _():
                pltpu.make_async_copy(zero_ref, xs_hbm.at[pl.ds(pl.multiple_of((pend_ref[e] - rows) * SUBLANES,
                                                                               SUBLANES), rows * SUBLANES), :],
                                      zsem).start()
        def zero_block(b, carry):
            pltpu.make_async_copy(zero_ref, xs_hbm.at[pl.ds(pl.multiple_of(b * (rows * SUBLANES), SUBLANES),
                                                            rows * SUBLANES), :], zsem).start()
            return carry

        def wait_block(b, carry):
            pltpu.make_async_copy(zero_ref, xs_hbm.at[pl.ds(0, rows * SUBLANES), :], zsem).wait()
            return carry

        first_unused = pend_ref[N_EXPERTS - 1] // rows
        lax.fori_loop(first_unused, n_blocks, zero_block, 0)
        for e in range(N_EXPERTS):
            @pl.when(pend_ref[e] > pstart_ref[e])
            def _():
                wait_block(0, 0)
        lax.fori_loop(first_unused, n_blocks, wait_block, 0)

    def wait_rows_of(step):
        for _ in range(TOP_K):
            _wait_bytes_of(stage.at[0], xs_hbm, out_sem.at[step % STAGES])

    @pl.when(i >= STAGES - 1)
    def _():
        wait_rows_of(i - (STAGES - 1))

    slot = i % STAGES
    for s in range(d_model // LANES):
        stage[slot, pl.ds(s, tt, stride=SUBLANES), :] = hn_ref[:, s * LANES:(s + 1) * LANES].astype(f32)

    def body(t, carry):
        for k in range(TOP_K):
            pltpu.make_async_copy(_row(stage.at[slot], t), _row(xs_hbm, dest_ref[0, k * tt + t]),
                                  out_sem.at[slot]).start(priority=k)
        return carry
    lax.fori_loop(0, tt, body, 0, unroll=8)

    @pl.when(i == n - 1)
    def _():
        for back in range(STAGES - 2, -1, -1):
            @pl.when(i - back >= 0)
            def _():
                wait_rows_of(i - back)


def _dispatch(pstart, pend, dest_blocks, hn, n_slots):
    n, _, width = dest_blocks.shape
    tt = width // TOP_K
    rows = EXPERT_ROWS
    D = hn.shape[1]
    kern = functools.partial(_dispatch_kernel, tt=tt, rows=rows, n_blocks=n_slots // rows, d_model=D)
    smem_blk = pl.BlockSpec((None, 1, width), lambda i, *_: (i, 0, 0), memory_space=pltpu.SMEM)
    return pl.pallas_call(
        kern,
        grid_spec=pltpu.PrefetchScalarGridSpec(
            num_scalar_prefetch=2,
            grid=(n,),
            in_specs=[smem_blk, pl.BlockSpec((tt, D), lambda i, *_: (i, 0))],
            out_specs=pl.BlockSpec(memory_space=pl.ANY),
            scratch_shapes=[pltpu.VMEM((rows * SUBLANES, LANES), f32),
                            pltpu.VMEM((STAGES, tt * SUBLANES, LANES), f32),
                            pltpu.SemaphoreType.DMA((STAGES,)),
                            pltpu.SemaphoreType.DMA(())],
        ),
        out_shape=jax.ShapeDtypeStruct((n_slots * SUBLANES, LANES), f32),
        compiler_params=pltpu.CompilerParams(dimension_semantics=("arbitrary",), vmem_limit_bytes=VMEM_LIMIT,
                                             has_side_effects=True),
        name="dispatch",
    )(pstart, pend, dest_blocks, hn)


def _rows_from_token_major(buf, n_rows, n_tiles):
    return jnp.concatenate([buf[pl.ds(s, n_rows, stride=SUBLANES), :] for s in range(n_tiles)], axis=-1)


def _expert_kernel(beid_ref, nbu_ref, xs_ref, wg_ref, wu_ref, wd_ref, ys_ref, wg_bf, wu_bf, wd_bf, *, rows,
                   d_model):
    b = pl.program_id(0)
    changed = jnp.logical_or(b == 0, beid_ref[b] != beid_ref[jnp.maximum(b - 1, 0)])

    @pl.when(changed)
    def _():
        wg_bf[...] = wg_ref[...].astype(bf16)
        wu_bf[...] = wu_ref[...].astype(bf16)
        wd_bf[...] = wd_ref[...].astype(bf16)

    n_tiles = d_model // LANES

    @pl.when(b < nbu_ref[0])
    def _():
        xb = _rows_from_token_major(xs_ref, rows, n_tiles).astype(bf16)
        hdn = jax.nn.silu(_dot(xb, wg_bf[...])) * _dot(xb, wu_bf[...])
        y = _dot(hdn.astype(bf16), wd_bf[...])
        for s in range(n_tiles):
            ys_ref[pl.ds(s, rows, stride=SUBLANES), :] = y[:, s * LANES:(s + 1) * LANES]

    @pl.when(b >= nbu_ref[0])
    def _():
        ys_ref[...] = jnp.zeros_like(ys_ref)


def _experts(block_eid, n_used, xs, w_gate, w_up, w_down):
    n_blocks = block_eid.shape[0]
    rows = EXPERT_ROWS
    E, D, DE = w_gate.shape
    kern = functools.partial(_expert_kernel, rows=rows, d_model=D)
    xs_idx = lambda b, eid, nbu: (jnp.minimum(b, nbu[0] - 1), 0)
    return pl.pallas_call(
        kern,
        grid_spec=pltpu.PrefetchScalarGridSpec(
            num_scalar_prefetch=2,
            grid=(n_blocks,),
            in_specs=[
                pl.BlockSpec((rows * SUBLANES, LANES), xs_idx),
                pl.BlockSpec((None, D, DE), lambda b, eid, nbu: (eid[b], 0, 0)),
                pl.BlockSpec((None, D, DE), lambda b, eid, nbu: (eid[b], 0, 0)),
                pl.BlockSpec((None, DE, D), lambda b, eid, nbu: (eid[b], 0, 0)),
            ],
            out_specs=pl.BlockSpec((rows * SUBLANES, LANES), lambda b, *_: (b, 0)),
            scratch_shapes=[
                pltpu.VMEM((D, DE), bf16),
                pltpu.VMEM((D, DE), bf16),
                pltpu.VMEM((DE, D), bf16),
            ],
        ),
        out_shape=jax.ShapeDtypeStruct((n_blocks * rows * SUBLANES, LANES), f32),
        compiler_params=pltpu.CompilerParams(dimension_semantics=("arbitrary",), vmem_limit_bytes=VMEM_LIMIT),
        name="experts",
    )(block_eid, n_used, xs, w_gate, w_up, w_down)


def _combine_kernel(dest_ref, destn_ref, x1_ref, wt_ref, ys_hbm, out_ref, buf, sem, *, te, d_model):
    i = pl.program_id(0)
    n = pl.num_programs(0)
    slot = i % 2

    def start_gather(d_ref, dst, dsem):
        for t in range(te):
            for k in range(TOP_K):
                r = k * te + t
                pltpu.make_async_copy(_row(ys_hbm, d_ref[0, r]), _row(dst, r), dsem).start(priority=k)

    @pl.when(i == 0)
    def _():
        start_gather(dest_ref, buf.at[0], sem.at[0])

    @pl.when(i + 1 < n)
    def _():
        start_gather(destn_ref, buf.at[1 - slot], sem.at[1 - slot])

    _wait_bytes_of(buf.at[slot], ys_hbm, sem.at[slot])
    n_tiles = d_model // LANES
    both = _rows_from_token_major(buf.at[slot], TOP_K * te, n_tiles)
    wt = jnp.concatenate([wt_ref[...], jnp.zeros((SUBLANES - TOP_K, te), f32)], axis=0).T
    out_ref[...] = x1_ref[...] + (wt[:, 0:1] * both[:te] + wt[:, 1:2] * both[te:])


def _combine(dest_blocks, x1, wt_rows, ys):
    N, D = x1.shape
    n, _, width = dest_blocks.shape
    te = width // TOP_K
    kern = functools.partial(_combine_kernel, te=te, d_model=D)
    cur = pl.BlockSpec((None, 1, width), lambda i: (i, 0, 0), memory_space=pltpu.SMEM)
    nxt = pl.BlockSpec((None, 1, width), lambda i: (jnp.minimum(i + 1, n - 1), 0, 0), memory_space=pltpu.SMEM)
    return pl.pallas_call(
        kern,
        grid=(n,),
        in_specs=[
            cur, nxt,
            pl.BlockSpec((te, D), lambda i: (i, 0)),
            pl.BlockSpec((TOP_K, te), lambda i: (0, i)),
            pl.BlockSpec(memory_space=pl.ANY),
        ],
        out_specs=pl.BlockSpec((te, D), lambda i: (i, 0)),
        scratch_shapes=[pltpu.VMEM((2, TOP_K * te * SUBLANES, LANES), f32), pltpu.SemaphoreType.DMA((2,))],
        out_shape=jax.ShapeDtypeStruct((N, D), f32),
        compiler_params=pltpu.CompilerParams(dimension_semantics=("arbitrary",), vmem_limit_bytes=VMEM_LIMIT),
        name="combine",
    )(dest_blocks, dest_blocks, x1, wt_rows, ys)


def _token_blocks(a, tt):
    k, n = a.shape
    return a.reshape(k, n // tt, tt).transpose(1, 0, 2).reshape(n // tt, 1, k * tt)


def _layer(x, l, norm1_g, w_in, pool_w, pool_scale, w_pool_up, q_norm_g, k_norm_g, lambda_q1, lambda_k1,
           lambda_q2, lambda_k2, subln_g, w_attn_up, w_out, norm2_g, w_router_group, b_router_group,
           w_router_expert, b_router_expert, w_expert_gate, w_expert_up, w_expert_down):
    B, S, D = x.shape
    N = B * S
    head_dim = q_norm_g.shape[0]
    lam_init = 0.8 - 0.6 * math.exp(-0.3 * l)
    reps = (N_HEADS * 2 * head_dim) // head_dim

    qg = (jnp.tile(q_norm_g, reps) * (head_dim ** -0.5 * LOG2E))[None, :]
    kg = jnp.tile(k_norm_g, reps)[None, :]
    pg, ga, qn, kn, vt = _in_proj(x, norm1_g[None, :], w_in.astype(bf16), pool_w.astype(bf16),
                                  pool_scale[None, :], w_pool_up.astype(bf16), qg, kg)

    tk = min(ATTN_TILE, S)
    slopes2 = jnp.asarray([2.0 ** (-8.0 * (h + 1) / N_HEADS) * LOG2E for h in range(N_HEADS)], f32)
    fq, fk = _alibi_features(slopes2, 2 * tk, tk)
    o = _attention(qn, kn, vt, fq, fk, slopes2, lambda_q1[None, :], lambda_k1[None, :], lambda_q2[None, :],
                   lambda_k2[None, :], subln_g[:, None], lam_init)

    wr = jnp.zeros((ROUTER_ROWS, D), f32)
    wr = wr.at[:N_GROUPS].set(w_router_group.T).at[SUBLANES:].set(w_router_expert.T).astype(bf16)
    br = jnp.zeros((ROUTER_ROWS, 1), f32)
    br = br.at[:N_GROUPS, 0].set(b_router_group).at[SUBLANES:, 0].set(b_router_expert)
    x1, hn, eid, wts, rank, cnt = _merge(pg.reshape(N, D), ga.reshape(N, D), o.reshape(N, -1),
                                           x.reshape(N, D), w_attn_up.astype(bf16), w_out.astype(bf16),
                                           norm2_g[None, :], wr, br)

    R = EXPERT_ROWS
    counts = cnt[:, 0].astype(jnp.int32)
    padded = (counts + R - 1) // R * R
    pend = jnp.cumsum(padded).astype(jnp.int32)
    pstart = pend - padded
    n_blocks = -(-(N * TOP_K) // R) + N_EXPERTS
    starts = jnp.arange(n_blocks, dtype=jnp.int32) * R
    block_eid = jnp.minimum(jnp.sum((pend[None, :] <= starts[:, None]).astype(jnp.int32), axis=1), N_EXPERTS - 1)
    n_used = pend[-1:] // R

    dest = _slots(pstart, eid, rank)
    xs = _dispatch(pstart, pend, _token_blocks(dest, min(DISPATCH_TILE, N)), hn, n_blocks * R)
    ys = _experts(block_eid, n_used, xs, w_expert_gate, w_expert_up, w_expert_down)
    out = _combine(_token_blocks(dest, min(COMBINE_TILE, N)), x1, wts, ys)
    return out.reshape(B, S, D)


def kernel(x, norm1_g, w_in, pool_w, pool_scale, w_pool_up, q_norm_g, k_norm_g, lambda_q1, lambda_k1, lambda_q2,
           lambda_k2, subln_g, w_attn_up, w_out, norm2_g, w_router_group, b_router_group, w_router_expert,
           b_router_expert, w_expert_gate, w_expert_up, w_expert_down):
    params = (norm1_g, w_in, pool_w, pool_scale, w_pool_up, q_norm_g, k_norm_g, lambda_q1, lambda_k1, lambda_q2,
              lambda_k2, subln_g, w_attn_up, w_out, norm2_g, w_router_group, b_router_group, w_router_expert,
              b_router_expert, w_expert_gate, w_expert_up, w_expert_down)
    for l in range(norm1_g.shape[0]):
        x = _layer(x, l, *(p[l] for p in params))
    return x
```

```python
import functools
import math

import jax
import jax.numpy as jnp
from jax import lax
from jax.experimental import pallas as pl
from jax.experimental.pallas import tpu as pltpu

EPS = 1e-6
POOL_WINDOWS = (2, 4, 8, 16)
POOL_HALO = 16
N_HEADS = 4
N_GROUPS = 4
EXPERTS_PER_GROUP = 8
N_EXPERTS = N_GROUPS * EXPERTS_PER_GROUP
TOP_K = 2
LANES = 128
SUBLANES = 8
ROUTER_ROWS = 8 + N_EXPERTS
VT_ROWS = LANES + 16
LOG2E = 1.4426950408889634

ROW_TILE = 512
IN_PROJ_SUBTILES = 2
MERGE_SUBTILES = 2
ATTN_TILE = 256
ATTN_HEADS_PER_STEP = 4
EXPERT_ROWS = 1024
DISPATCH_TILE = 512
COMBINE_TILE = 256
VMEM_LIMIT = 52 * 1024 * 1024

f32 = jnp.float32
bf16 = jnp.bfloat16


def _dot(a, b):
    return jnp.dot(a, b, preferred_element_type=f32)


def _dot_nt(a, b):
    return lax.dot_general(a, b, (((1,), (1,)), ((), ())), preferred_element_type=f32)


def _half_lane_rmsnorm(t, n_tiles, rows):
    lane = lax.broadcasted_iota(jnp.int32, (rows, LANES), 1)
    lo_mask = lane < (LANES // 2)
    outs = []
    for i in range(n_tiles):
        c = t[:, i * LANES:(i + 1) * LANES]
        sq = c * c
        lo = jnp.sum(jnp.where(lo_mask, sq, 0.0), axis=-1, keepdims=True)
        hi = jnp.sum(jnp.where(lo_mask, 0.0, sq), axis=-1, keepdims=True)
        ms = jnp.where(lo_mask, lo, hi) * (2.0 / LANES)
        outs.append(c * lax.rsqrt(ms + EPS))
    return jnp.concatenate(outs, axis=-1)


def _in_proj_kernel(x_ref, g1_ref, win_ref, poolw_ref, pscale_ref, wpu_ref, qg_ref, kg_ref,
                    pg_ref, ga_ref, qn_ref, kn_ref, vt_ref, prev_ref, wfold_ref, *, tm, subs, tk, d_model, pool_width,
                    qk_width, attn_width):
    j = pl.program_id(1)
    off_q = pool_width
    off_k = off_q + qk_width
    off_v = off_k + qk_width
    off_gp = off_v + attn_width
    off_ga = off_gp + d_model
    group = pool_width // len(POOL_WINDOWS)
    half = d_model // 2
    n_tiles = qk_width // LANES

    @pl.when(jnp.logical_and(pl.program_id(0) == 0, j == 0))
    def _():
        for g in range(len(POOL_WINDOWS)):
            rows_g = slice(g * group, (g + 1) * group)
            scaled = (poolw_ref[g].astype(f32) * pscale_ref[:, rows_g]).astype(bf16)
            wfold_ref[rows_g, :] = _dot(scaled, wpu_ref[rows_g, :]).astype(bf16)

    @pl.when(j == 0)
    def _():
        prev_ref[...] = jnp.zeros_like(prev_ref)

    extra = lax.broadcasted_iota(jnp.int32, (VT_ROWS - LANES, tk), 0)
    ones_rows = jnp.where(extra == 0, 1.0, 0.0).astype(bf16)
    halo = prev_ref[...]
    for sub in range(subs):
        rows = slice(sub * tm, (sub + 1) * tm)
        x = x_ref[rows, :]
        ms = jnp.mean(x * x, axis=-1, keepdims=True)
        h = (x * lax.rsqrt(ms + EPS) * g1_ref[...]).astype(bf16)

        def proj(lo, width, h=h):
            return _dot(h, win_ref[:, lo:lo + width])

        u = proj(0, pool_width)
        v = proj(off_v, attn_width)
        for hh in range(attn_width // LANES):
            for c in range(tm // tk):
                blk = sub * (tm // tk) + c
                vt_ref[hh, blk, 0:LANES, :] = v[c * tk:(c + 1) * tk, hh * LANES:(hh + 1) * LANES].T.astype(bf16)
                vt_ref[hh, blk, LANES:VT_ROWS, :] = ones_rows
        qn_ref[rows, :] = (_half_lane_rmsnorm(proj(off_q, qk_width), n_tiles, tm) * qg_ref[...]).astype(bf16)
        kn_ref[rows, :] = (_half_lane_rmsnorm(proj(off_k, qk_width), n_tiles, tm) * kg_ref[...]).astype(bf16)

        ext = jnp.concatenate([halo, u], axis=0)
        halo = u[tm - POOL_HALO:, :]
        pos = (j * subs + sub) * tm + lax.broadcasted_iota(jnp.int32, (tm, 1), 0)
        ds = []
        for g, w in enumerate(POOL_WINDOWS):
            acc = ext[:, g * group:(g + 1) * group]
            shift = 1
            while shift < w:
                acc = acc + pltpu.roll(acc, shift, 0)
                shift *= 2
            wsum = acc[POOL_HALO:, :]
            cnt = jnp.minimum(pos + 1, w).astype(f32)
            ds.append((wsum / cnt - u[:, g * group:(g + 1) * group]).astype(bf16))
        pool_out = _dot(jnp.concatenate(ds, axis=-1), wfold_ref[...])

        for c in range(2):
            cols = slice(c * half, (c + 1) * half)
            gp = jax.nn.sigmoid(proj(off_gp + c * half, half))
            pg_ref[rows, cols] = (gp * pool_out[:, cols]).astype(bf16)
            ga_ref[rows, cols] = jax.nn.sigmoid(proj(off_ga + c * half, half)).astype(bf16)
    prev_ref[...] = halo


def _in_proj(x, g1, w_in, pool_w, pool_scale, w_pool_up, qg, kg):
    B, S, D = x.shape
    subs = IN_PROJ_SUBTILES if S % (IN_PROJ_SUBTILES * ROW_TILE) == 0 else 1
    tm = min(ROW_TILE, S)
    pool_width = w_pool_up.shape[0]
    qk_width = qg.shape[1]
    attn_width = qk_width
    in_width = w_in.shape[1]
    const2 = lambda b, j: (0, 0)
    row = lambda b, j: (b, j, 0)
    tk = min(ATTN_TILE, S)
    n_heads = attn_width // LANES
    kern = functools.partial(_in_proj_kernel, tm=tm, subs=subs, tk=tk, d_model=D, pool_width=pool_width,
                             qk_width=qk_width, attn_width=attn_width)
    tb = subs * tm
    return pl.pallas_call(
        kern,
        grid=(B, S // tb),
        in_specs=[
            pl.BlockSpec((None, tb, D), row),
            pl.BlockSpec((1, D), const2),
            pl.BlockSpec((D, in_width), const2, pipeline_mode=pl.Buffered(1)),
            pl.BlockSpec(pool_w.shape, lambda b, j: (0, 0, 0)),
            pl.BlockSpec((1, pool_width), const2),
            pl.BlockSpec((pool_width, D), const2),
            pl.BlockSpec((1, qk_width), const2),
            pl.BlockSpec((1, qk_width), const2),
        ],
        out_specs=[
            pl.BlockSpec((None, tb, D), row),
            pl.BlockSpec((None, tb, D), row),
            pl.BlockSpec((None, tb, qk_width), row),
            pl.BlockSpec((None, tb, qk_width), row),
            pl.BlockSpec((None, n_heads, tb // tk, VT_ROWS, tk), lambda b, j: (b, 0, j, 0, 0)),
        ],
        out_shape=[
            jax.ShapeDtypeStruct((B, S, D), bf16),
            jax.ShapeDtypeStruct((B, S, D), bf16),
            jax.ShapeDtypeStruct((B, S, qk_width), bf16),
            jax.ShapeDtypeStruct((B, S, qk_width), bf16),
            jax.ShapeDtypeStruct((B, n_heads, S // tk, VT_ROWS, tk), bf16),
        ],
        scratch_shapes=[pltpu.VMEM((POOL_HALO, pool_width), f32), pltpu.VMEM((pool_width, D), bf16)],
        compiler_params=pltpu.CompilerParams(dimension_semantics=("arbitrary", "arbitrary"),
                                             vmem_limit_bytes=VMEM_LIMIT),
        name="in_proj",
    )(x, g1, w_in, pool_w, pool_scale, w_pool_up, qg, kg)


def _attn_kernel(slopes_ref, q_ref, k_ref, vt_ref, fq_ref, fk_ref, lq1_ref, lk1_ref, lq2_ref, lk2_ref, sg_ref,
                 o_ref, qa_ref, sa_ref, sb_ref, pa_ref, pb_ref, m_ref, alpha_ref, acc_ref, *,
                 tq, tk, heads, lam_init):
    hg = pl.program_id(1)
    i = pl.program_id(2)
    fk = fk_ref[...]
    lane = lax.broadcasted_iota(jnp.int32, (tk, LANES), 1)
    first = lane < (LANES // 2)
    hs = range(heads)

    buf_a = (sa_ref, pa_ref)
    buf_b = (sb_ref, pb_ref)

    def scores(g, n, buf):
        kb = k_ref[pl.ds(pl.multiple_of(n * tk, tk), tk), g * LANES:(g + 1) * LANES]
        buf[0][g] = _dot_nt(jnp.concatenate([kb, fk], axis=1), qa_ref[g])

    def softmax(g, n, buf, first_half_mask=None):
        s_ref, p_ref = buf
        c = -slopes_ref[hg * heads + g] * (i * tq - n * tk).astype(f32)
        s = s_ref[g]
        if first_half_mask is not None:
            s = jnp.concatenate([jnp.where(first_half_mask, s[:, :tq], -jnp.inf), s[:, tq:]], axis=1)
        m_old = m_ref[g]
        m_new = jnp.maximum(m_old, jnp.max(s, axis=0, keepdims=True) + c)
        p_ref[g] = jnp.exp2(s - (m_new - c)).astype(bf16)
        m_ref[g] = m_new
        return jnp.exp2(m_old - m_new)

    def pv(g, n, buf, alpha):
        acc_ref[g] = alpha * acc_ref[g] + _dot(vt_ref[g, jnp.maximum(n, 0)], buf[1][g])

    m_ref[...] = jnp.full(m_ref.shape, -jnp.inf, f32)
    alpha_ref[...] = jnp.ones_like(alpha_ref)
    acc_ref[...] = jnp.zeros_like(acc_ref)
    pb_ref[...] = jnp.zeros_like(pb_ref)
    zero = jnp.zeros((tk, LANES), bf16)
    for g in hs:
        parts = []
        for half in range(2):
            q = q_ref[half * tk:(half + 1) * tk, g * LANES:(g + 1) * LANES]
            parts += [jnp.where(first, q, zero), jnp.where(first, zero, q)]
        qa_ref[g] = jnp.concatenate([jnp.concatenate(parts, axis=0), fq_ref[g]], axis=1)
    for g in hs:
        scores(g, 0, buf_a)

    def pair(t, carry):
        n = 2 * t
        a_prev = [alpha_ref[g] for g in hs]
        a_even = [None] * heads
        for g in hs:
            pv(g, n - 1, buf_b, a_prev[g])
            a_even[g] = softmax(g, n, buf_a)
            scores(g, n + 1, buf_b)
        for g in hs:
            pv(g, n, buf_a, a_even[g])
            alpha_ref[g] = softmax(g, n + 1, buf_b)
            scores(g, n + 2, buf_a)
        return carry

    lax.fori_loop(0, i, pair, 0)
    n = 2 * i
    lam = (jnp.exp(jnp.sum(lq1_ref[...] * lk1_ref[...], keepdims=True))
           - jnp.exp(jnp.sum(lq2_ref[...] * lk2_ref[...], keepdims=True)) + lam_init)
    late = slice(tq, 2 * tq)
    kk = lax.broadcasted_iota(jnp.int32, (tk, tq), 0)
    qq = lax.broadcasted_iota(jnp.int32, (tk, tq), 1)
    tri = kk <= jnp.where(qq >= tk, qq - tk, qq)
    for g in hs:
        a_even = softmax(g, n, buf_a, first_half_mask=tri)
        pv(g, n - 1, buf_b, alpha_ref[g])
        pv(g, n, buf_a, a_even)
        kb = k_ref[pl.ds(pl.multiple_of((n + 1) * tk, tk), tk), g * LANES:(g + 1) * LANES]
        s = _dot_nt(jnp.concatenate([kb, fk], axis=1), qa_ref[g, late, :])
        s = jnp.where(tri, s, -jnp.inf)
        c = slopes_ref[hg * heads + g] * float(tk)
        m_old = m_ref[g, :, late]
        m_new = jnp.maximum(m_old, jnp.max(s, axis=0, keepdims=True) + c)
        p = jnp.exp2(s - (m_new - c)).astype(bf16)
        acc = acc_ref[g]
        acc_late = jnp.exp2(m_old - m_new) * acc[:, late] + _dot(vt_ref[g, n + 1], p)
        acc = jnp.concatenate([acc[:, :tq], acc_late], axis=1)
        o_all = acc[0:LANES] / acc[LANES:LANES + 1]
        o = jnp.concatenate([o_all[:, 0:tk] - lam * o_all[:, tk:tq],
                             o_all[:, tq:tq + tk] - lam * o_all[:, tq + tk:]], axis=1)
        ms = jnp.mean(o * o, axis=0, keepdims=True)
        on = o * lax.rsqrt(ms + EPS) * sg_ref[...] * (1.0 - lam_init)
        o_ref[:, g * LANES:(g + 1) * LANES] = on.T.astype(bf16)


def _split_bf16(x, pieces=3):
    out = []
    for _ in range(pieces):
        p = x.astype(bf16)
        out.append(p)
        x = x - p.astype(f32)
    return out


def _alibi_features(slopes2, tq, tk):
    assert tk <= 256
    assert tq == 2 * tk
    n_heads = slopes2.shape[0]
    ones = jnp.ones((tk, 1), bf16)
    krel = jnp.arange(tk, dtype=f32).astype(bf16)[:, None]
    fk = jnp.concatenate([krel] * 3 + [ones] * 3 + [jnp.zeros((tk, LANES - 6), bf16)], axis=1)
    first, last = jnp.arange(tk, dtype=f32), jnp.arange(tk, tq, dtype=f32)
    qrel = jnp.concatenate([first, first, last, last])
    a = _split_bf16(slopes2)
    b = _split_bf16(-slopes2[:, None] * qrel[None, :])
    cols = [jnp.broadcast_to(p[:, None, None], (n_heads, 2 * tq, 1)) for p in a] + [p[:, :, None] for p in b]
    fq = jnp.concatenate(cols + [jnp.zeros((n_heads, 2 * tq, LANES - 6), bf16)], axis=2)
    return fq, fk


def _attention(qn, kn, vt, fq, fk, slopes, lq1, lk1, lq2, lk2, subln_col, lam_init):
    B, S, _ = qn.shape
    tk = fk.shape[0]
    tq = fq.shape[1] // 2
    assert tq == 2 * tk
    nkv = S // tk
    const2 = lambda b, h, i, *_: (0, 0)
    G = ATTN_HEADS_PER_STEP
    kern = functools.partial(_attn_kernel, tq=tq, tk=tk, heads=G, lam_init=lam_init)
    hd = lq1.shape[1]
    return pl.pallas_call(
        kern,
        grid_spec=pltpu.PrefetchScalarGridSpec(
            num_scalar_prefetch=1,
            grid=(B, N_HEADS // G, S // tq),
            in_specs=[
                pl.BlockSpec((None, tq, G * LANES), lambda b, h, i, *_: (b, i, h)),
                pl.BlockSpec((None, S, G * LANES), lambda b, h, i, *_: (b, 0, h)),
                pl.BlockSpec((None, G, nkv, VT_ROWS, tk), lambda b, h, i, *_: (b, h, 0, 0, 0)),
                pl.BlockSpec((G, 2 * tq, LANES), lambda b, h, i, *_: (h, 0, 0)),
                pl.BlockSpec((tk, LANES), const2),
                pl.BlockSpec((1, hd), const2),
                pl.BlockSpec((1, hd), const2),
                pl.BlockSpec((1, hd), const2),
                pl.BlockSpec((1, hd), const2),
                pl.BlockSpec((LANES, 1), const2),
            ],
            out_specs=pl.BlockSpec((None, tq, G * LANES), lambda b, h, i, *_: (b, i, h)),
            scratch_shapes=[pltpu.VMEM((G, 2 * tq, 2 * LANES), bf16),
                            pltpu.VMEM((G, tk, 2 * tq), f32), pltpu.VMEM((G, tk, 2 * tq), f32),
                            pltpu.VMEM((G, tk, 2 * tq), bf16), pltpu.VMEM((G, tk, 2 * tq), bf16),
                            pltpu.VMEM((G, 1, 2 * tq), f32), pltpu.VMEM((G, 1, 2 * tq), f32),
                            pltpu.VMEM((G, VT_ROWS, 2 * tq), f32)],
        ),
        out_shape=jax.ShapeDtypeStruct((B, S, N_HEADS * LANES), bf16),
        compiler_params=pltpu.CompilerParams(dimension_semantics=("arbitrary", "arbitrary", "arbitrary"),
                                             vmem_limit_bytes=VMEM_LIMIT),
        name="diff_attn",
    )(slopes, qn, kn, vt, fq, fk, lq1, lk1, lq2, lk2, subln_col)


def _merge_kernel(pg_ref, ga_ref, o_ref, x_ref, wau_ref, wout_ref, g2_ref, wr_ref, br_ref,
                  x1_ref, hn_ref, eid_ref, wt_ref, rank_ref, cnt_ref, base_ref, *, tm, subs, d_model):
    step = pl.program_id(0)

    @pl.when(step == 0)
    def _():
        base_ref[...] = jnp.zeros_like(base_ref)

    a = lax.broadcasted_iota(jnp.int32, (tm, tm), 0)
    b = lax.broadcasted_iota(jnp.int32, (tm, tm), 1)
    upper = jnp.where(a <= b, 1.0, 0.0).astype(bf16)
    base = base_ref[...]
    all_logits = []
    for sub in range(subs):
        rows = slice(sub * tm, (sub + 1) * tm)
        attn_out = _dot(o_ref[rows, :], wau_ref[...])
        merged = pg_ref[rows, :].astype(f32) + ga_ref[rows, :].astype(f32) * attn_out
        x1 = x_ref[rows, :] + _dot(merged.astype(bf16), wout_ref[...])
        x1_ref[rows, :] = x1
        ms = jnp.mean(x1 * x1, axis=-1, keepdims=True)
        hn = (x1 * lax.rsqrt(ms + EPS) * g2_ref[...]).astype(bf16)
        hn_ref[rows, :] = hn

        all_logits.append(_dot_nt(wr_ref[...], hn) + br_ref[...])

    for sub in range(subs):
        rows = slice(sub * tm, (sub + 1) * tm)
        logits = all_logits[sub]
        lg = logits[0:N_GROUPS]
        gmax = jnp.max(lg, axis=0, keepdims=True)
        p_top = 1.0 / jnp.sum(jnp.exp(lg - gmax), axis=0, keepdims=True)
        grow = lax.broadcasted_iota(jnp.int32, lg.shape, 0).astype(f32)
        g_idx = jnp.min(jnp.where(lg == gmax, grow, float(N_GROUPS)), axis=0, keepdims=True)

        sel = jnp.zeros((EXPERTS_PER_GROUP, tm), f32)
        for g in range(N_GROUPS):
            le_g = logits[SUBLANES + g * EXPERTS_PER_GROUP:SUBLANES + (g + 1) * EXPERTS_PER_GROUP]
            sel = jnp.where(g_idx == float(g), le_g, sel)
        erow = lax.broadcasted_iota(jnp.int32, sel.shape, 0).astype(f32)
        e1 = jnp.max(sel, axis=0, keepdims=True)
        i1 = jnp.min(jnp.where(sel == e1, erow, float(EXPERTS_PER_GROUP)), axis=0, keepdims=True)
        sel2 = jnp.where(erow == i1, -jnp.inf, sel)
        e2 = jnp.max(sel2, axis=0, keepdims=True)
        i2 = jnp.min(jnp.where(sel2 == e2, erow, float(EXPERTS_PER_GROUP)), axis=0, keepdims=True)
        r = jnp.exp(e2 - e1)
        w1 = p_top / (1.0 + r)
        w2 = p_top * r / (1.0 + r)
        eid1 = g_idx * float(EXPERTS_PER_GROUP) + i1
        eid2 = g_idx * float(EXPERTS_PER_GROUP) + i2
        eid_ref[:, rows] = jnp.concatenate([eid1, eid2], axis=0).astype(jnp.int32)
        wt_ref[:, rows] = jnp.concatenate([w1, w2], axis=0)

        xrow = lax.broadcasted_iota(jnp.int32, (N_EXPERTS, tm), 0).astype(f32)
        oh1 = jnp.where(xrow == eid1, 1.0, 0.0)
        oh2 = jnp.where(xrow == eid2, 1.0, 0.0)
        oh = oh1 + oh2
        before = _dot(oh.astype(bf16), upper) + base - 1.0
        rank1 = jnp.sum(oh1 * before, axis=0, keepdims=True)
        rank2 = jnp.sum(oh2 * before, axis=0, keepdims=True)
        rank_ref[:, rows] = jnp.concatenate([rank1, rank2], axis=0).astype(jnp.int32)
        base = base + jnp.sum(oh, axis=1, keepdims=True)
    base_ref[...] = base
    cnt_ref[...] = jnp.broadcast_to(base, cnt_ref.shape)


def _merge(pg, ga, o, x, w_attn_up, w_out, g2, wr, br):
    N, D = x.shape
    tm = min(ROW_TILE, N)
    subs = MERGE_SUBTILES if N % (MERGE_SUBTILES * tm) == 0 else 1
    tb = subs * tm
    aw = o.shape[1]
    const2 = lambda i: (0, 0)
    row = lambda i: (i, 0)
    colblk = lambda i: (0, i)
    kern = functools.partial(_merge_kernel, tm=tm, subs=subs, d_model=D)
    return pl.pallas_call(
        kern,
        grid=(N // tb,),
        in_specs=[
            pl.BlockSpec((tb, D), row),
            pl.BlockSpec((tb, D), row),
            pl.BlockSpec((tb, aw), row),
            pl.BlockSpec((tb, D), row),
            pl.BlockSpec((aw, D), const2),
            pl.BlockSpec((D, D), const2),
            pl.BlockSpec((1, D), const2),
            pl.BlockSpec((ROUTER_ROWS, D), const2),
            pl.BlockSpec((ROUTER_ROWS, 1), const2),
        ],
        out_specs=[
            pl.BlockSpec((tb, D), row),
            pl.BlockSpec((tb, D), row),
            pl.BlockSpec((TOP_K, tb), colblk),
            pl.BlockSpec((TOP_K, tb), colblk),
            pl.BlockSpec((TOP_K, tb), colblk),
            pl.BlockSpec((N_EXPERTS, LANES), const2),
        ],
        out_shape=[
            jax.ShapeDtypeStruct((N, D), f32),
            jax.ShapeDtypeStruct((N, D), bf16),
            jax.ShapeDtypeStruct((TOP_K, N), jnp.int32),
            jax.ShapeDtypeStruct((TOP_K, N), f32),
            jax.ShapeDtypeStruct((TOP_K, N), jnp.int32),
            jax.ShapeDtypeStruct((N_EXPERTS, LANES), f32),
        ],
        scratch_shapes=[pltpu.VMEM((N_EXPERTS, 1), f32)],
        compiler_params=pltpu.CompilerParams(dimension_semantics=("arbitrary",), vmem_limit_bytes=VMEM_LIMIT),
        name="merge_router",
    )(pg, ga, o, x, w_attn_up, w_out, g2, wr, br)


def _slots_kernel(pstart_ref, eid_ref, rank_ref, dest_ref):
    eid = eid_ref[...]
    start = jnp.zeros_like(eid)
    for e in range(N_EXPERTS):
        start = jnp.where(eid == e, pstart_ref[e], start)
    dest_ref[...] = start + rank_ref[...]


def _slots(pstart, eid, rank):
    k, n = eid.shape
    tn = min(n, 8192)
    blk = pl.BlockSpec((k, tn), lambda i, *_: (0, i))
    return pl.pallas_call(
        _slots_kernel,
        grid_spec=pltpu.PrefetchScalarGridSpec(num_scalar_prefetch=1, grid=(n // tn,), in_specs=[blk, blk],
                                               out_specs=blk),
        out_shape=jax.ShapeDtypeStruct((k, n), jnp.int32),
        name="slots",
    )(pstart, eid, rank)


def _row(ref, r):
    start = r * SUBLANES if isinstance(r, int) else pl.multiple_of(r * SUBLANES, SUBLANES)
    return ref.at[pl.ds(start, SUBLANES), :]


def _wait_bytes_of(ref_like, any_hbm, sem):
    n = ref_like.shape[0]
    pltpu.make_async_copy(any_hbm.at[pl.ds(0, n), :], any_hbm.at[pl.ds(0, n), :], sem).wait()


STAGES = 3


def _dispatch_kernel(pstart_ref, pend_ref, dest_ref, hn_ref, xs_hbm, zero_ref, stage, out_sem, zsem, *, tt, rows,
                     n_blocks, d_model):
    i = pl.program_id(0)
    n = pl.num_programs(0)

    @pl.when(i == 0)
    def _():
        zero_ref[...] = jnp.zeros_like(zero_ref)
        for e in range(N_EXPERTS):
            @pl.when(pend_ref[e] > pstart_ref[e])
            def _():
                pltpu.make_async_copy(zero_ref, xs_hbm.at[pl.ds(pl.multiple_of((pend_ref[e] - rows) * SUBLANES,
                                                                               SUBLANES), rows * SUBLANES), :],
                                      zsem).start()
        def zero_block(b, carry):
            pltpu.make_async_copy(zero_ref, xs_hbm.at[pl.ds(pl.multiple_of(b * (rows * SUBLANES), SUBLANES),
                                                            rows * SUBLANES), :], zsem).start()
            return carry

        def wait_block(b, carry):
            pltpu.make_async_copy(zero_ref, xs_hbm.at[pl.ds(0, rows * SUBLANES), :], zsem).wait()
            return carry

        first_unused = pend_ref[N_EXPERTS - 1] // rows
        lax.fori_loop(first_unused, n_blocks, zero_block, 0)
        for e in range(N_EXPERTS):
            @pl.when(pend_ref[e] > pstart_ref[e])
            def _():
                wait_block(0, 0)
        lax.fori_loop(first_unused, n_blocks, wait_block, 0)

    def wait_rows_of(step):
        for _ in range(TOP_K):
            _wait_bytes_of(stage.at[0], xs_hbm, out_sem.at[step % STAGES])

    @pl.when(i >= STAGES - 1)
    def _():
        wait_rows_of(i - (STAGES - 1))

    slot = i % STAGES
    for s in range(d_model // LANES):
        stage[slot, pl.ds(s, tt, stride=SUBLANES), :] = hn_ref[:, s * LANES:(s + 1) * LANES].astype(f32)

    def body(t, carry):
        for k in range(TOP_K):
            pltpu.make_async_copy(_row(stage.at[slot], t), _row(xs_hbm, dest_ref[0, k * tt + t]),
                                  out_sem.at[slot]).start(priority=k)
        return carry
    lax.fori_loop(0, tt, body, 0, unroll=8)

    @pl.when(i == n - 1)
    def _():
        for back in range(STAGES - 2, -1, -1):
            @pl.when(i - back >= 0)
            def _():
                wait_rows_of(i - back)


def _dispatch(pstart, pend, dest_blocks, hn, n_slots):
    n, _, width = dest_blocks.shape
    tt = width // TOP_K
    rows = EXPERT_ROWS
    D = hn.shape[1]
    kern = functools.partial(_dispatch_kernel, tt=tt, rows=rows, n_blocks=n_slots // rows, d_model=D)
    smem_blk = pl.BlockSpec((None, 1, width), lambda i, *_: (i, 0, 0), memory_space=pltpu.SMEM)
    return pl.pallas_call(
        kern,
        grid_spec=pltpu.PrefetchScalarGridSpec(
            num_scalar_prefetch=2,
            grid=(n,),
            in_specs=[smem_blk, pl.BlockSpec((tt, D), lambda i, *_: (i, 0))],
            out_specs=pl.BlockSpec(memory_space=pl.ANY),
            scratch_shapes=[pltpu.VMEM((rows * SUBLANES, LANES), f32),
                            pltpu.VMEM((STAGES, tt * SUBLANES, LANES), f32),
                            pltpu.SemaphoreType.DMA((STAGES,)),
                            pltpu.SemaphoreType.DMA(())],
        ),
        out_shape=jax.ShapeDtypeStruct((n_slots * SUBLANES, LANES), f32),
        compiler_params=pltpu.CompilerParams(dimension_semantics=("arbitrary",), vmem_limit_bytes=VMEM_LIMIT,
                                             has_side_effects=True),
        name="dispatch",
    )(pstart, pend, dest_blocks, hn)


def _rows_from_token_major(buf, n_rows, n_tiles):
    return jnp.concatenate([buf[pl.ds(s, n_rows, stride=SUBLANES), :] for s in range(n_tiles)], axis=-1)


def _expert_kernel(beid_ref, nbu_ref, xs_ref, wg_ref, wu_ref, wd_ref, ys_ref, wg_bf, wu_bf, wd_bf, *, rows,
                   d_model):
    b = pl.program_id(0)
    changed = jnp.logical_or(b == 0, beid_ref[b] != beid_ref[jnp.maximum(b - 1, 0)])

    @pl.when(changed)
    def _():
        wg_bf[...] = wg_ref[...].astype(bf16)
        wu_bf[...] = wu_ref[...].astype(bf16)
        wd_bf[...] = wd_ref[...].astype(bf16)

    n_tiles = d_model // LANES

    @pl.when(b < nbu_ref[0])
    def _():
        xb = _rows_from_token_major(xs_ref, rows, n_tiles).astype(bf16)
        hdn = jax.nn.silu(_dot(xb, wg_bf[...])) * _dot(xb, wu_bf[...])
        y = _dot(hdn.astype(bf16), wd_bf[...])
        for s in range(n_tiles):
            ys_ref[pl.ds(s, rows, stride=SUBLANES), :] = y[:, s * LANES:(s + 1) * LANES]

    @pl.when(b >= nbu_ref[0])
    def _():
        ys_ref[...] = jnp.zeros_like(ys_ref)


def _experts(block_eid, n_used, xs, w_gate, w_up, w_down):
    n_blocks = block_eid.shape[0]
    rows = EXPERT_ROWS
    E, D, DE = w_gate.shape
    kern = functools.partial(_expert_kernel, rows=rows, d_model=D)
    xs_idx = lambda b, eid, nbu: (jnp.minimum(b, nbu[0] - 1), 0)
    return pl.pallas_call(
        kern,
        grid_spec=pltpu.PrefetchScalarGridSpec(
            num_scalar_prefetch=2,
            grid=(n_blocks,),
            in_specs=[
                pl.BlockSpec((rows * SUBLANES, LANES), xs_idx),
                pl.BlockSpec((None, D, DE), lambda b, eid, nbu: (eid[b], 0, 0)),
                pl.BlockSpec((None, D, DE), lambda b, eid, nbu: (eid[b], 0, 0)),
                pl.BlockSpec((None, DE, D), lambda b, eid, nbu: (eid[b], 0, 0)),
            ],
            out_specs=pl.BlockSpec((rows * SUBLANES, LANES), lambda b, *_: (b, 0)),
            scratch_shapes=[
                pltpu.VMEM((D, DE), bf16),
                pltpu.VMEM((D, DE), bf16),
                pltpu.VMEM((DE, D), bf16),
            ],
        ),
        out_shape=jax.ShapeDtypeStruct((n_blocks * rows * SUBLANES, LANES), f32),
        compiler_params=pltpu.CompilerParams(dimension_semantics=("arbitrary",), vmem_limit_bytes=VMEM_LIMIT),
        name="experts",
    )(block_eid, n_used, xs, w_gate, w_up, w_down)


def _combine_kernel(dest_ref, destn_ref, x1_ref, wt_ref, ys_hbm, out_ref, buf, sem, *, te, d_model):
    i = pl.program_id(0)
    n = pl.num_programs(0)
    slot = i % 2

    def start_gather(d_ref, dst, dsem):
        for t in range(te):
            for k in range(TOP_K):
                r = k * te + t
                pltpu.make_async_copy(_row(ys_hbm, d_ref[0, r]), _row(dst, r), dsem).start(priority=k)

    @pl.when(i == 0)
    def _():
        start_gather(dest_ref, buf.at[0], sem.at[0])

    @pl.when(i + 1 < n)
    def _():
        start_gather(destn_ref, buf.at[1 - slot], sem.at[1 - slot])

    _wait_bytes_of(buf.at[slot], ys_hbm, sem.at[slot])
    n_tiles = d_model // LANES
    both = _rows_from_token_major(buf.at[slot], TOP_K * te, n_tiles)
    wt = jnp.concatenate([wt_ref[...], jnp.zeros((SUBLANES - TOP_K, te), f32)], axis=0).T
    out_ref[...] = x1_ref[...] + (wt[:, 0:1] * both[:te] + wt[:, 1:2] * both[te:])


def _combine(dest_blocks, x1, wt_rows, ys):
    N, D = x1.shape
    n, _, width = dest_blocks.shape
    te = width // TOP_K
    kern = functools.partial(_combine_kernel, te=te, d_model=D)
    cur = pl.BlockSpec((None, 1, width), lambda i: (i, 0, 0), memory_space=pltpu.SMEM)
    nxt = pl.BlockSpec((None, 1, width), lambda i: (jnp.minimum(i + 1, n - 1), 0, 0), memory_space=pltpu.SMEM)
    return pl.pallas_call(
        kern,
        grid=(n,),
        in_specs=[
            cur, nxt,
            pl.BlockSpec((te, D), lambda i: (i, 0)),
            pl.BlockSpec((TOP_K, te), lambda i: (0, i)),
            pl.BlockSpec(memory_space=pl.ANY),
        ],
        out_specs=pl.BlockSpec((te, D), lambda i: (i, 0)),
        scratch_shapes=[pltpu.VMEM((2, TOP_K * te * SUBLANES, LANES), f32), pltpu.SemaphoreType.DMA((2,))],
        out_shape=jax.ShapeDtypeStruct((N, D), f32),
        compiler_params=pltpu.CompilerParams(dimension_semantics=("arbitrary",), vmem_limit_bytes=VMEM_LIMIT),
        name="combine",
    )(dest_blocks, dest_blocks, x1, wt_rows, ys)


def _token_blocks(a, tt):
    k, n = a.shape
    return a.reshape(k, n // tt, tt).transpose(1, 0, 2).reshape(n // tt, 1, k * tt)


def _layer(x, l, norm1_g, w_in, pool_w, pool_scale, w_pool_up, q_norm_g, k_norm_g, lambda_q1, lambda_k1,
           lambda_q2, lambda_k2, subln_g, w_attn_up, w_out, norm2_g, w_router_group, b_router_group,
           w_router_expert, b_router_expert, w_expert_gate, w_expert_up, w_expert_down):
    B, S, D = x.shape
    N = B * S
    head_dim = q_norm_g.shape[0]
    lam_init = 0.8 - 0.6 * math.exp(-0.3 * l)
    reps = (N_HEADS * 2 * head_dim) // head_dim

    qg = (jnp.tile(q_norm_g, reps) * (head_dim ** -0.5 * LOG2E))[None, :]
    kg = jnp.tile(k_norm_g, reps)[None, :]
    pg, ga, qn, kn, vt = _in_proj(x, norm1_g[None, :], w_in.astype(bf16), pool_w.astype(bf16),
                                  pool_scale[None, :], w_pool_up.astype(bf16), qg, kg)

    tk = min(ATTN_TILE, S)
    slopes2 = jnp.asarray([2.0 ** (-8.0 * (h + 1) / N_HEADS) * LOG2E for h in range(N_HEADS)], f32)
    fq, fk = _alibi_features(slopes2, 2 * tk, tk)
    o = _attention(qn, kn, vt, fq, fk, slopes2, lambda_q1[None, :], lambda_k1[None, :], lambda_q2[None, :],
                   lambda_k2[None, :], subln_g[:, None], lam_init)

    wr = jnp.zeros((ROUTER_ROWS, D), f32)
    wr = wr.at[:N_GROUPS].set(w_router_group.T).at[SUBLANES:].set(w_router_expert.T).astype(bf16)
    br = jnp.zeros((ROUTER_ROWS, 1), f32)
    br = br.at[:N_GROUPS, 0].set(b_router_group).at[SUBLANES:, 0].set(b_router_expert)
    x1, hn, eid, wts, rank, cnt = _merge(pg.reshape(N, D), ga.reshape(N, D), o.reshape(N, -1),
                                           x.reshape(N, D), w_attn_up.astype(bf16), w_out.astype(bf16),
                                           norm2_g[None, :], wr, br)

    R = EXPERT_ROWS
    counts = cnt[:, 0].astype(jnp.int32)
    padded = (counts + R - 1) // R * R
    pend = jnp.cumsum(padded).astype(jnp.int32)
    pstart = pend - padded
    n_blocks = -(-(N * TOP_K) // R) + N_EXPERTS
    starts = jnp.arange(n_blocks, dtype=jnp.int32) * R
    block_eid = jnp.minimum(jnp.sum((pend[None, :] <= starts[:, None]).astype(jnp.int32), axis=1), N_EXPERTS - 1)
    n_used = pend[-1:] // R

    dest = _slots(pstart, eid, rank)
    xs = _dispatch(pstart, pend, _token_blocks(dest, min(DISPATCH_TILE, N)), hn, n_blocks * R)
    ys = _experts(block_eid, n_used, xs, w_expert_gate, w_expert_up, w_expert_down)
    out = _combine(_token_blocks(dest, min(COMBINE_TILE, N)), x1, wts, ys)
    return out.reshape(B, S, D)


def kernel(x, norm1_g, w_in, pool_w, pool_scale, w_pool_up, q_norm_g, k_norm_g, lambda_q1, lambda_k1, lambda_q2,
           lambda_k2, subln_g, w_attn_up, w_out, norm2_g, w_router_group, b_router_group, w_router_expert,
           b_router_expert, w_expert_gate, w_expert_up, w_expert_down):
    params = (norm1_g, w_in, pool_w, pool_scale, w_pool_up, q_norm_g, k_norm_g, lambda_q1, lambda_k1, lambda_q2,
              lambda_k2, subln_g, w_attn_up, w_out, norm2_g, w_router_group, b_router_group, w_router_expert,
              b_router_expert, w_expert_gate, w_expert_up, w_expert_down)
    for l in range(norm1_g.shape[0]):
        x = _layer(x, l, *(p[l] for p in params))
    return x
```

```python
import functools
import math

import jax
import jax.numpy as jnp
from jax import lax
from jax.experimental import pallas as pl
from jax.experimental.pallas import tpu as pltpu

EPS = 1e-6
POOL_WINDOWS = (2, 4, 8, 16)
POOL_HALO = 16
N_HEADS = 4
N_GROUPS = 4
EXPERTS_PER_GROUP = 8
N_EXPERTS = N_GROUPS * EXPERTS_PER_GROUP
TOP_K = 2
LANES = 128
SUBLANES = 8
ROUTER_ROWS = 8 + N_EXPERTS
VT_ROWS = LANES + 16
LOG2E = 1.4426950408889634

ROW_TILE = 512
IN_PROJ_SUBTILES = 2
MERGE_SUBTILES = 2
ATTN_TILE = 256
ATTN_HEADS_PER_STEP = 4
EXPERT_ROWS = 1024
DISPATCH_TILE = 512
COMBINE_TILE = 256
VMEM_LIMIT = 52 * 1024 * 1024

f32 = jnp.float32
bf16 = jnp.bfloat16


def _dot(a, b):
    return jnp.dot(a, b, preferred_element_type=f32)


def _dot_nt(a, b):
    return lax.dot_general(a, b, (((1,), (1,)), ((), ())), preferred_element_type=f32)


def _half_lane_rmsnorm(t, n_tiles, rows):
    lane = lax.broadcasted_iota(jnp.int32, (rows, LANES), 1)
    lo_mask = lane < (LANES // 2)
    outs = []
    for i in range(n_tiles):
        c = t[:, i * LANES:(i + 1) * LANES]
        sq = c * c
        lo = jnp.sum(jnp.where(lo_mask, sq, 0.0), axis=-1, keepdims=True)
        hi = jnp.sum(jnp.where(lo_mask, 0.0, sq), axis=-1, keepdims=True)
        ms = jnp.where(lo_mask, lo, hi) * (2.0 / LANES)
        outs.append(c * lax.rsqrt(ms + EPS))
    return jnp.concatenate(outs, axis=-1)


def _in_proj_kernel(x_ref, g1_ref, win_ref, poolw_ref, pscale_ref, wpu_ref, qg_ref, kg_ref,
                    pg_ref, ga_ref, qn_ref, kn_ref, vt_ref, prev_ref, wfold_ref, *, tm, subs, tk, d_model, pool_width,
                    qk_width, attn_width):
    j = pl.program_id(1)
    off_q = pool_width
    off_k = off_q + qk_width
    off_v = off_k + qk_width
    off_gp = off_v + attn_width
    off_ga = off_gp + d_model
    group = pool_width // len(POOL_WINDOWS)
    half = d_model // 2
    n_tiles = qk_width // LANES

    @pl.when(jnp.logical_and(pl.program_id(0) == 0, j == 0))
    def _():
        for g in range(len(POOL_WINDOWS)):
            rows_g = slice(g * group, (g + 1) * group)
            scaled = (poolw_ref[g].astype(f32) * pscale_ref[:, rows_g]).astype(bf16)
            wfold_ref[rows_g, :] = _dot(scaled, wpu_ref[rows_g, :]).astype(bf16)

    @pl.when(j == 0)
    def _():
        prev_ref[...] = jnp.zeros_like(prev_ref)

    extra = lax.broadcasted_iota(jnp.int32, (VT_ROWS - LANES, tk), 0)
    ones_rows = jnp.where(extra == 0, 1.0, 0.0).astype(bf16)
    halo = prev_ref[...]
    for sub in range(subs):
        rows = slice(sub * tm, (sub + 1) * tm)
        x = x_ref[rows, :]
        ms = jnp.mean(x * x, axis=-1, keepdims=True)
        h = (x * lax.rsqrt(ms + EPS) * g1_ref[...]).astype(bf16)

        def proj(lo, width, h=h):
            return _dot(h, win_ref[:, lo:lo + width])

        u = proj(0, pool_width)
        v = proj(off_v, attn_width)
        for hh in range(attn_width // LANES):
            for c in range(tm // tk):
                blk = sub * (tm // tk) + c
                vt_ref[hh, blk, 0:LANES, :] = v[c * tk:(c + 1) * tk, hh * LANES:(hh + 1) * LANES].T.astype(bf16)
                vt_ref[hh, blk, LANES:VT_ROWS, :] = ones_rows
        qn_ref[rows, :] = (_half_lane_rmsnorm(proj(off_q, qk_width), n_tiles, tm) * qg_ref[...]).astype(bf16)
        kn_ref[rows, :] = (_half_lane_rmsnorm(proj(off_k, qk_width), n_tiles, tm) * kg_ref[...]).astype(bf16)

        ext = jnp.concatenate([halo, u], axis=0)
        halo = u[tm - POOL_HALO:, :]
        pos = (j * subs + sub) * tm + lax.broadcasted_iota(jnp.int32, (tm, 1), 0)
        ds = []
        for g, w in enumerate(POOL_WINDOWS):
            acc = ext[:, g * group:(g + 1) * group]
            shift = 1
            while shift < w:
                acc = acc + pltpu.roll(acc, shift, 0)
                shift *= 2
            wsum = acc[POOL_HALO:, :]
            cnt = jnp.minimum(pos + 1, w).astype(f32)
            ds.append((wsum / cnt - u[:, g * group:(g + 1) * group]).astype(bf16))
        pool_out = _dot(jnp.concatenate(ds, axis=-1), wfold_ref[...])

        for c in range(2):
            cols = slice(c * half, (c + 1) * half)
            gp = jax.nn.sigmoid(proj(off_gp + c * half, half))
            pg_ref[rows, cols] = (gp * pool_out[:, cols]).astype(bf16)
            ga_ref[rows, cols] = jax.nn.sigmoid(proj(off_ga + c * half, half)).astype(bf16)
    prev_ref[...] = halo


def _in_proj(x, g1, w_in, pool_w, pool_scale, w_pool_up, qg, kg):
    B, S, D = x.shape
    subs = IN_PROJ_SUBTILES if S % (IN_PROJ_SUBTILES * ROW_TILE) == 0 else 1
    tm = min(ROW_TILE, S)
    pool_width = w_pool_up.shape[0]
    qk_width = qg.shape[1]
    attn_width = qk_width
    in_width = w_in.shape[1]
    const2 = lambda b, j: (0, 0)
    row = lambda b, j: (b, j, 0)
    tk = min(ATTN_TILE, S)
    n_heads = attn_width // LANES
    kern = functools.partial(_in_proj_kernel, tm=tm, subs=subs, tk=tk, d_model=D, pool_width=pool_width,
                             qk_width=qk_width, attn_width=attn_width)
    tb = subs * tm
    return pl.pallas_call(
        kern,
        grid=(B, S // tb),
        in_specs=[
            pl.BlockSpec((None, tb, D), row),
            pl.BlockSpec((1, D), const2),
            pl.BlockSpec((D, in_width), const2, pipeline_mode=pl.Buffered(1)),
            pl.BlockSpec(pool_w.shape, lambda b, j: (0, 0, 0)),
            pl.BlockSpec((1, pool_width), const2),
            pl.BlockSpec((pool_width, D), const2),
            pl.BlockSpec((1, qk_width), const2),
            pl.BlockSpec((1, qk_width), const2),
        ],
        out_specs=[
            pl.BlockSpec((None, tb, D), row),
            pl.BlockSpec((None, tb, D), row),
            pl.BlockSpec((None, tb, qk_width), row),
            pl.BlockSpec((None, tb, qk_width), row),
            pl.BlockSpec((None, n_heads, tb // tk, VT_ROWS, tk), lambda b, j: (b, 0, j, 0, 0)),
        ],
        out_shape=[
            jax.ShapeDtypeStruct((B, S, D), bf16),
            jax.ShapeDtypeStruct((B, S, D), bf16),
            jax.ShapeDtypeStruct((B, S, qk_width), bf16),
            jax.ShapeDtypeStruct((B, S, qk_width), bf16),
            jax.ShapeDtypeStruct((B, n_heads, S // tk, VT_ROWS, tk), bf16),
        ],
        scratch_shapes=[pltpu.VMEM((POOL_HALO, pool_width), f32), pltpu.VMEM((pool_width, D), bf16)],
        compiler_params=pltpu.CompilerParams(dimension_semantics=("arbitrary", "arbitrary"),
                                             vmem_limit_bytes=VMEM_LIMIT),
        name="in_proj",
    )(x, g1, w_in, pool_w, pool_scale, w_pool_up, qg, kg)


def _attn_kernel(slopes_ref, q_ref, k_ref, vt_ref, fq_ref, fk_ref, lq1_ref, lk1_ref, lq2_ref, lk2_ref, sg_ref,
                 o_ref, qa_ref, sa_ref, sb_ref, pa_ref, pb_ref, m_ref, alpha_ref, acc_ref, *,
                 tq, tk, heads, lam_init):
    hg = pl.program_id(1)
    i = pl.program_id(2)
    fk = fk_ref[...]
    lane = lax.broadcasted_iota(jnp.int32, (tk, LANES), 1)
    first = lane < (LANES // 2)
    hs = range(heads)

    buf_a = (sa_ref, pa_ref)
    buf_b = (sb_ref, pb_ref)

    def scores(g, n, buf):
        kb = k_ref[pl.ds(pl.multiple_of(n * tk, tk), tk), g * LANES:(g + 1) * LANES]
        buf[0][g] = _dot_nt(jnp.concatenate([kb, fk], axis=1), qa_ref[g])

    def softmax(g, n, buf, first_half_mask=None):
        s_ref, p_ref = buf
        c = -slopes_ref[hg * heads + g] * (i * tq - n * tk).astype(f32)
        s = s_ref[g]
        if first_half_mask is not None:
            s = jnp.concatenate([jnp.where(first_half_mask, s[:, :tq], -jnp.inf), s[:, tq:]], axis=1)
        m_old = m_ref[g]
        m_new = jnp.maximum(m_old, jnp.max(s, axis=0, keepdims=True) + c)
        p_ref[g] = jnp.exp2(s - (m_new - c)).astype(bf16)
        m_ref[g] = m_new
        return jnp.exp2(m_old - m_new)

    def pv(g, n, buf, alpha):
        acc_ref[g] = alpha * acc_ref[g] + _dot(vt_ref[g, jnp.maximum(n, 0)], buf[1][g])

    m_ref[...] = jnp.full(m_ref.shape, -jnp.inf, f32)
    alpha_ref[...] = jnp.ones_like(alpha_ref)
    acc_ref[...] = jnp.zeros_like(acc_ref)
    pb_ref[...] = jnp.zeros_like(pb_ref)
    zero = jnp.zeros((tk, LANES), bf16)
    for g in hs:
        parts = []
        for half in range(2):
            q = q_ref[half * tk:(half + 1) * tk, g * LANES:(g + 1) * LANES]
            parts += [jnp.where(first, q, zero), jnp.where(first, zero, q)]
        qa_ref[g] = jnp.concatenate([jnp.concatenate(parts, axis=0), fq_ref[g]], axis=1)
    for g in hs:
        scores(g, 0, buf_a)

    def pair(t, carry):
        n = 2 * t
        a_prev = [alpha_ref[g] for g in hs]
        a_even = [None] * heads
        for g in hs:
            pv(g, n - 1, buf_b, a_prev[g])
            a_even[g] = softmax(g, n, buf_a)
            scores(g, n + 1, buf_b)
        for g in hs:
            pv(g, n, buf_a, a_even[g])
            alpha_ref[g] = softmax(g, n + 1, buf_b)
            scores(g, n + 2, buf_a)
        return carry

    lax.fori_loop(0, i, pair, 0)
    n = 2 * i
    lam = (jnp.exp(jnp.sum(lq1_ref[...] * lk1_ref[...], keepdims=True))
           - jnp.exp(jnp.sum(lq2_ref[...] * lk2_ref[...], keepdims=True)) + lam_init)
    late = slice(tq, 2 * tq)
    kk = lax.broadcasted_iota(jnp.int32, (tk, tq), 0)
    qq = lax.broadcasted_iota(jnp.int32, (tk, tq), 1)
    tri = kk <= jnp.where(qq >= tk, qq - tk, qq)
    for g in hs:
        a_even = softmax(g, n, buf_a, first_half_mask=tri)
        pv(g, n - 1, buf_b, alpha_ref[g])
        pv(g, n, buf_a, a_even)
        kb = k_ref[pl.ds(pl.multiple_of((n + 1) * tk, tk), tk), g * LANES:(g + 1) * LANES]
        s = _dot_nt(jnp.concatenate([kb, fk], axis=1), qa_ref[g, late, :])
        s = jnp.where(tri, s, -jnp.inf)
        c = slopes_ref[hg * heads + g] * float(tk)
        m_old = m_ref[g, :, late]
        m_new = jnp.maximum(m_old, jnp.max(s, axis=0, keepdims=True) + c)
        p = jnp.exp2(s - (m_new - c)).astype(bf16)
        acc = acc_ref[g]
        acc_late = jnp.exp2(m_old - m_new) * acc[:, late] + _dot(vt_ref[g, n + 1], p)
        acc = jnp.concatenate([acc[:, :tq], acc_late], axis=1)
        o_all = acc[0:LANES] / acc[LANES:LANES + 1]
        o = jnp.concatenate([o_all[:, 0:tk] - lam * o_all[:, tk:tq],
                             o_all[:, tq:tq + tk] - lam * o_all[:, tq + tk:]], axis=1)
        ms = jnp.mean(o * o, axis=0, keepdims=True)
        on = o * lax.rsqrt(ms + EPS) * sg_ref[...] * (1.0 - lam_init)
        o_ref[:, g * LANES:(g + 1) * LANES] = on.T.astype(bf16)


def _split_bf16(x, pieces=3):
    out = []
    for _ in range(pieces):
        p = x.astype(bf16)
        out.append(p)
        x = x - p.astype(f32)
    return out


def _alibi_features(slopes2, tq, tk):
    assert tk <= 256
    assert tq == 2 * tk
    n_heads = slopes2.shape[0]
    ones = jnp.ones((tk, 1), bf16)
    krel = jnp.arange(tk, dtype=f32).astype(bf16)[:, None]
    fk = jnp.concatenate([krel] * 3 + [ones] * 3 + [jnp.zeros((tk, LANES - 6), bf16)], axis=1)
    first, last = jnp.arange(tk, dtype=f32), jnp.arange(tk, tq, dtype=f32)
    qrel = jnp.concatenate([first, first, last, last])
    a = _split_bf16(slopes2)
    b = _split_bf16(-slopes2[:, None] * qrel[None, :])
    cols = [jnp.broadcast_to(p[:, None, None], (n_heads, 2 * tq, 1)) for p in a] + [p[:, :, None] for p in b]
    fq = jnp.concatenate(cols + [jnp.zeros((n_heads, 2 * tq, LANES - 6), bf16)], axis=2)
    return fq, fk


def _attention(qn, kn, vt, fq, fk, slopes, lq1, lk1, lq2, lk2, subln_col, lam_init):
    B, S, _ = qn.shape
    tk = fk.shape[0]
    tq = fq.shape[1] // 2
    assert tq == 2 * tk
    nkv = S // tk
    const2 = lambda b, h, i, *_: (0, 0)
    G = ATTN_HEADS_PER_STEP
    kern = functools.partial(_attn_kernel, tq=tq, tk=tk, heads=G, lam_init=lam_init)
    hd = lq1.shape[1]
    return pl.pallas_call(
        kern,
        grid_spec=pltpu.PrefetchScalarGridSpec(
            num_scalar_prefetch=1,
            grid=(B, N_HEADS // G, S // tq),
            in_specs=[
                pl.BlockSpec((None, tq, G * LANES), lambda b, h, i, *_: (b, i, h)),
                pl.BlockSpec((None, S, G * LANES), lambda b, h, i, *_: (b, 0, h)),
                pl.BlockSpec((None, G, nkv, VT_ROWS, tk), lambda b, h, i, *_: (b, h, 0, 0, 0)),
                pl.BlockSpec((G, 2 * tq, LANES), lambda b, h, i, *_: (h, 0, 0)),
                pl.BlockSpec((tk, LANES), const2),
                pl.BlockSpec((1, hd), const2),
                pl.BlockSpec((1, hd), const2),
                pl.BlockSpec((1, hd), const2),
                pl.BlockSpec((1, hd), const2),
                pl.BlockSpec((LANES, 1), const2),
            ],
            out_specs=pl.BlockSpec((None, tq, G * LANES), lambda b, h, i, *_: (b, i, h)),
            scratch_shapes=[pltpu.VMEM((G, 2 * tq, 2 * LANES), bf16),
                            pltpu.VMEM((G, tk, 2 * tq), f32), pltpu.VMEM((G, tk, 2 * tq), f32),
                            pltpu.VMEM((G, tk, 2 * tq), bf16), pltpu.VMEM((G, tk, 2 * tq), bf16),
                            pltpu.VMEM((G, 1, 2 * tq), f32), pltpu.VMEM((G, 1, 2 * tq), f32),
                            pltpu.VMEM((G, VT_ROWS, 2 * tq), f32)],
        ),
        out_shape=jax.ShapeDtypeStruct((B, S, N_HEADS * LANES), bf16),
        compiler_params=pltpu.CompilerParams(dimension_semantics=("arbitrary", "arbitrary", "arbitrary"),
                                             vmem_limit_bytes=VMEM_LIMIT),
        name="diff_attn",
    )(slopes, qn, kn, vt, fq, fk, lq1, lk1, lq2, lk2, subln_col)


def _merge_kernel(pg_ref, ga_ref, o_ref, x_ref, wau_ref, wout_ref, g2_ref, wr_ref, br_ref,
                  x1_ref, hn_ref, eid_ref, wt_ref, rank_ref, cnt_ref, base_ref, *, tm, subs, d_model):
    step = pl.program_id(0)

    @pl.when(step == 0)
    def _():
        base_ref[...] = jnp.zeros_like(base_ref)

    a = lax.broadcasted_iota(jnp.int32, (tm, tm), 0)
    b = lax.broadcasted_iota(jnp.int32, (tm, tm), 1)
    upper = jnp.where(a <= b, 1.0, 0.0).astype(bf16)
    base = base_ref[...]
    all_logits = []
    for sub in range(subs):
        rows = slice(sub * tm, (sub + 1) * tm)
        attn_out = _dot(o_ref[rows, :], wau_ref[...])
        merged = pg_ref[rows, :].astype(f32) + ga_ref[rows, :].astype(f32) * attn_out
        x1 = x_ref[rows, :] + _dot(merged.astype(bf16), wout_ref[...])
        x1_ref[rows, :] = x1
        ms = jnp.mean(x1 * x1, axis=-1, keepdims=True)
        hn = (x1 * lax.rsqrt(ms + EPS) * g2_ref[...]).astype(bf16)
        hn_ref[rows, :] = hn

        all_logits.append(_dot_nt(wr_ref[...], hn) + br_ref[...])

    for sub in range(subs):
        rows = slice(sub * tm, (sub + 1) * tm)
        logits = all_logits[sub]
        lg = logits[0:N_GROUPS]
        gmax = jnp.max(lg, axis=0, keepdims=True)
        p_top = 1.0 / jnp.sum(jnp.exp(lg - gmax), axis=0, keepdims=True)
        grow = lax.broadcasted_iota(jnp.int32, lg.shape, 0).astype(f32)
        g_idx = jnp.min(jnp.where(lg == gmax, grow, float(N_GROUPS)), axis=0, keepdims=True)

        sel = jnp.zeros((EXPERTS_PER_GROUP, tm), f32)
        for g in range(N_GROUPS):
            le_g = logits[SUBLANES + g * EXPERTS_PER_GROUP:SUBLANES + (g + 1) * EXPERTS_PER_GROUP]
            sel = jnp.where(g_idx == float(g), le_g, sel)
        erow = lax.broadcasted_iota(jnp.int32, sel.shape, 0).astype(f32)
        e1 = jnp.max(sel, axis=0, keepdims=True)
        i1 = jnp.min(jnp.where(sel == e1, erow, float(EXPERTS_PER_GROUP)), axis=0, keepdims=True)
        sel2 = jnp.where(erow == i1, -jnp.inf, sel)
        e2 = jnp.max(sel2, axis=0, keepdims=True)
        i2 = jnp.min(jnp.where(sel2 == e2, erow, float(EXPERTS_PER_GROUP)), axis=0, keepdims=True)
        r = jnp.exp(e2 - e1)
        w1 = p_top / (1.0 + r)
        w2 = p_top * r / (1.0 + r)
        eid1 = g_idx * float(EXPERTS_PER_GROUP) + i1
        eid2 = g_idx * float(EXPERTS_PER_GROUP) + i2
        eid_ref[:, rows] = jnp.concatenate([eid1, eid2], axis=0).astype(jnp.int32)
        wt_ref[:, rows] = jnp.concatenate([w1, w2], axis=0)

        xrow = lax.broadcasted_iota(jnp.int32, (N_EXPERTS, tm), 0).astype(f32)
        oh1 = jnp.where(xrow == eid1, 1.0, 0.0)
        oh2 = jnp.where(xrow == eid2, 1.0, 0.0)
        oh = oh1 + oh2
        before = _dot(oh.astype(bf16), upper) + base - 1.0
        rank1 = jnp.sum(oh1 * before, axis=0, keepdims=True)
        rank2 = jnp.sum(oh2 * before, axis=0, keepdims=True)
        rank_ref[:, rows] = jnp.concatenate([rank1, rank2], axis=0).astype(jnp.int32)
        base = base + jnp.sum(oh, axis=1, keepdims=True)
    base_ref[...] = base
    cnt_ref[...] = jnp.broadcast_to(base, cnt_ref.shape)


def _merge(pg, ga, o, x, w_attn_up, w_out, g2, wr, br):
    N, D = x.shape
    tm = min(ROW_TILE, N)
    subs = MERGE_SUBTILES if N % (MERGE_SUBTILES * tm) == 0 else 1
    tb = subs * tm
    aw = o.shape[1]
    const2 = lambda i: (0, 0)
    row = lambda i: (i, 0)
    colblk = lambda i: (0, i)
    kern = functools.partial(_merge_kernel, tm=tm, subs=subs, d_model=D)
    return pl.pallas_call(
        kern,
        grid=(N // tb,),
        in_specs=[
            pl.BlockSpec((tb, D), row),
            pl.BlockSpec((tb, D), row),
            pl.BlockSpec((tb, aw), row),
            pl.BlockSpec((tb, D), row),
            pl.BlockSpec((aw, D), const2),
            pl.BlockSpec((D, D), const2),
            pl.BlockSpec((1, D), const2),
            pl.BlockSpec((ROUTER_ROWS, D), const2),
            pl.BlockSpec((ROUTER_ROWS, 1), const2),
        ],
        out_specs=[
            pl.BlockSpec((tb, D), row),
            pl.BlockSpec((tb, D), row),
            pl.BlockSpec((TOP_K, tb), colblk),
            pl.BlockSpec((TOP_K, tb), colblk),
            pl.BlockSpec((TOP_K, tb), colblk),
            pl.BlockSpec((N_EXPERTS, LANES), const2),
        ],
        out_shape=[
            jax.ShapeDtypeStruct((N, D), f32),
            jax.ShapeDtypeStruct((N, D), bf16),
            jax.ShapeDtypeStruct((TOP_K, N), jnp.int32),
            jax.ShapeDtypeStruct((TOP_K, N), f32),
            jax.ShapeDtypeStruct((TOP_K, N), jnp.int32),
            jax.ShapeDtypeStruct((N_EXPERTS, LANES), f32),
        ],
        scratch_shapes=[pltpu.VMEM((N_EXPERTS, 1), f32)],
        compiler_params=pltpu.CompilerParams(dimension_semantics=("arbitrary",), vmem_limit_bytes=VMEM_LIMIT),
        name="merge_router",
    )(pg, ga, o, x, w_attn_up, w_out, g2, wr, br)


def _slots_kernel(pstart_ref, eid_ref, rank_ref, dest_ref):
    eid = eid_ref[...]
    start = jnp.zeros_like(eid)
    for e in range(N_EXPERTS):
        start = jnp.where(eid == e, pstart_ref[e], start)
    dest_ref[...] = start + rank_ref[...]


def _slots(pstart, eid, rank):
    k, n = eid.shape
    tn = min(n, 8192)
    blk = pl.BlockSpec((k, tn), lambda i, *_: (0, i))
    return pl.pallas_call(
        _slots_kernel,
        grid_spec=pltpu.PrefetchScalarGridSpec(num_scalar_prefetch=1, grid=(n // tn,), in_specs=[blk, blk],
                                               out_specs=blk),
        out_shape=jax.ShapeDtypeStruct((k, n), jnp.int32),
        name="slots",
    )(pstart, eid, rank)


def _row(ref, r):
    start = r * SUBLANES if isinstance(r, int) else pl.multiple_of(r * SUBLANES, SUBLANES)
    return ref.at[pl.ds(start, SUBLANES), :]


def _wait_bytes_of(ref_like, any_hbm, sem):
    n = ref_like.shape[0]
    pltpu.make_async_copy(any_hbm.at[pl.ds(0, n), :], any_hbm.at[pl.ds(0, n), :], sem).wait()


STAGES = 3


def _dispatch_kernel(pstart_ref, pend_ref, count_ref, dest_ref, hn_ref, xs_hbm, zero_ref, stage, out_sem, zsem, *,
                     tt, rows, n_blocks, d_model):
    i = pl.program_id(0)
    n = pl.num_programs(0)

    def fill_unwritten(start):
        def piece(n_rows, first_row):
            cp = pltpu.make_async_copy(zero_ref.at[pl.ds(0, n_rows * SUBLANES), :],
                                       xs_hbm.at[pl.ds(pl.multiple_of(first_row * SUBLANES, SUBLANES),
                                                       n_rows * SUBLANES), :], zsem)
            if start:
                cp.start()
            else:
                cp.wait()

        for e in range(N_EXPERTS):
            row = pstart_ref[e] + count_ref[e]
            n_pad = pend_ref[e] - row
            bit = rows // 2
            while bit >= 1:
                pl.when((n_pad & bit) != 0)(functools.partial(piece, bit, row))
                row = row + (n_pad & bit)
                bit //= 2

        def block(b, carry):
            piece(rows, b * rows)
            return carry
        lax.fori_loop(pend_ref[N_EXPERTS - 1] // rows, n_blocks, block, 0)

    @pl.when(i == 0)
    def _():
        zero_ref[...] = jnp.zeros_like(zero_ref)
        fill_unwritten(start=True)

    def wait_rows_of(step):
        for _ in range(TOP_K):
            _wait_bytes_of(stage.at[0], xs_hbm, out_sem.at[step % STAGES])

    @pl.when(i >= STAGES - 1)
    def _():
        wait_rows_of(i - (STAGES - 1))

    slot = i % STAGES
    for s in range(d_model // LANES):
        stage[slot, pl.ds(s, tt, stride=SUBLANES), :] = hn_ref[:, s * LANES:(s + 1) * LANES].astype(f32)

    def body(t, carry):
        for k in range(TOP_K):
            pltpu.make_async_copy(_row(stage.at[slot], t), _row(xs_hbm, dest_ref[0, k * tt + t]),
                                  out_sem.at[slot]).start(priority=k)
        return carry
    lax.fori_loop(0, tt, body, 0, unroll=8)

    @pl.when(i == n - 1)
    def _():
        for back in range(STAGES - 2, -1, -1):
            @pl.when(i - back >= 0)
            def _():
                wait_rows_of(i - back)
        fill_unwritten(start=False)


def _dispatch(pstart, pend, counts, dest_blocks, hn, n_slots):
    n, _, width = dest_blocks.shape
    tt = width // TOP_K
    rows = EXPERT_ROWS
    D = hn.shape[1]
    kern = functools.partial(_dispatch_kernel, tt=tt, rows=rows, n_blocks=n_slots // rows, d_model=D)
    smem_blk = pl.BlockSpec((None, 1, width), lambda i, *_: (i, 0, 0), memory_space=pltpu.SMEM)
    return pl.pallas_call(
        kern,
        grid_spec=pltpu.PrefetchScalarGridSpec(
            num_scalar_prefetch=3,
            grid=(n,),
            in_specs=[smem_blk, pl.BlockSpec((tt, D), lambda i, *_: (i, 0))],
            out_specs=pl.BlockSpec(memory_space=pl.ANY),
            scratch_shapes=[pltpu.VMEM((rows * SUBLANES, LANES), f32),
                            pltpu.VMEM((STAGES, tt * SUBLANES, LANES), f32),
                            pltpu.SemaphoreType.DMA((STAGES,)),
                            pltpu.SemaphoreType.DMA(())],
        ),
        out_shape=jax.ShapeDtypeStruct((n_slots * SUBLANES, LANES), f32),
        compiler_params=pltpu.CompilerParams(dimension_semantics=("arbitrary",), vmem_limit_bytes=VMEM_LIMIT,
                                             has_side_effects=True),
        name="dispatch",
    )(pstart, pend, counts, dest_blocks, hn)


def _rows_from_token_major(buf, n_rows, n_tiles):
    return jnp.concatenate([buf[pl.ds(s, n_rows, stride=SUBLANES), :] for s in range(n_tiles)], axis=-1)


def _expert_kernel(beid_ref, nbu_ref, xs_ref, wg_ref, wu_ref, wd_ref, ys_ref, wg_bf, wu_bf, wd_bf, *, rows,
                   d_model):
    b = pl.program_id(0)
    changed = jnp.logical_or(b == 0, beid_ref[b] != beid_ref[jnp.maximum(b - 1, 0)])

    @pl.when(changed)
    def _():
        wg_bf[...] = wg_ref[...].astype(bf16)
        wu_bf[...] = wu_ref[...].astype(bf16)
        wd_bf[...] = wd_ref[...].astype(bf16)

    n_tiles = d_model // LANES

    @pl.when(b < nbu_ref[0])
    def _():
        xb = _rows_from_token_major(xs_ref, rows, n_tiles).astype(bf16)
        hdn = jax.nn.silu(_dot(xb, wg_bf[...])) * _dot(xb, wu_bf[...])
        y = _dot(hdn.astype(bf16), wd_bf[...])
        for s in range(n_tiles):
            ys_ref[pl.ds(s, rows, stride=SUBLANES), :] = y[:, s * LANES:(s + 1) * LANES]

    @pl.when(b >= nbu_ref[0])
    def _():
        ys_ref[...] = jnp.zeros_like(ys_ref)


def _experts(block_eid, n_used, xs, w_gate, w_up, w_down):
    n_blocks = block_eid.shape[0]
    rows = EXPERT_ROWS
    E, D, DE = w_gate.shape
    kern = functools.partial(_expert_kernel, rows=rows, d_model=D)
    xs_idx = lambda b, eid, nbu: (jnp.minimum(b, nbu[0] - 1), 0)
    return pl.pallas_call(
        kern,
        grid_spec=pltpu.PrefetchScalarGridSpec(
            num_scalar_prefetch=2,
            grid=(n_blocks,),
            in_specs=[
                pl.BlockSpec((rows * SUBLANES, LANES), xs_idx),
                pl.BlockSpec((None, D, DE), lambda b, eid, nbu: (eid[b], 0, 0)),
                pl.BlockSpec((None, D, DE), lambda b, eid, nbu: (eid[b], 0, 0)),
                pl.BlockSpec((None, DE, D), lambda b, eid, nbu: (eid[b], 0, 0)),
            ],
            out_specs=pl.BlockSpec((rows * SUBLANES, LANES), lambda b, *_: (b, 0)),
            scratch_shapes=[
                pltpu.VMEM((D, DE), bf16),
                pltpu.VMEM((D, DE), bf16),
                pltpu.VMEM((DE, D), bf16),
            ],
        ),
        out_shape=jax.ShapeDtypeStruct((n_blocks * rows * SUBLANES, LANES), f32),
        compiler_params=pltpu.CompilerParams(dimension_semantics=("arbitrary",), vmem_limit_bytes=VMEM_LIMIT),
        name="experts",
    )(block_eid, n_used, xs, w_gate, w_up, w_down)


def _combine_kernel(dest_ref, destn_ref, x1_ref, wt_ref, ys_hbm, out_ref, buf, sem, *, te, d_model):
    i = pl.program_id(0)
    n = pl.num_programs(0)
    slot = i % 2

    def start_gather(d_ref, dst, dsem):
        for t in range(te):
            for k in range(TOP_K):
                r = k * te + t
                pltpu.make_async_copy(_row(ys_hbm, d_ref[0, r]), _row(dst, r), dsem).start(priority=k)

    @pl.when(i == 0)
    def _():
        start_gather(dest_ref, buf.at[0], sem.at[0])

    @pl.when(i + 1 < n)
    def _():
        start_gather(destn_ref, buf.at[1 - slot], sem.at[1 - slot])

    _wait_bytes_of(buf.at[slot], ys_hbm, sem.at[slot])
    n_tiles = d_model // LANES
    both = _rows_from_token_major(buf.at[slot], TOP_K * te, n_tiles)
    wt = jnp.concatenate([wt_ref[...], jnp.zeros((SUBLANES - TOP_K, te), f32)], axis=0).T
    out_ref[...] = x1_ref[...] + (wt[:, 0:1] * both[:te] + wt[:, 1:2] * both[te:])


def _combine(dest_blocks, x1, wt_rows, ys):
    N, D = x1.shape
    n, _, width = dest_blocks.shape
    te = width // TOP_K
    kern = functools.partial(_combine_kernel, te=te, d_model=D)
    cur = pl.BlockSpec((None, 1, width), lambda i: (i, 0, 0), memory_space=pltpu.SMEM)
    nxt = pl.BlockSpec((None, 1, width), lambda i: (jnp.minimum(i + 1, n - 1), 0, 0), memory_space=pltpu.SMEM)
    return pl.pallas_call(
        kern,
        grid=(n,),
        in_specs=[
            cur, nxt,
            pl.BlockSpec((te, D), lambda i: (i, 0)),
            pl.BlockSpec((TOP_K, te), lambda i: (0, i)),
            pl.BlockSpec(memory_space=pl.ANY),
        ],
        out_specs=pl.BlockSpec((te, D), lambda i: (i, 0)),
        scratch_shapes=[pltpu.VMEM((2, TOP_K * te * SUBLANES, LANES), f32), pltpu.SemaphoreType.DMA((2,))],
        out_shape=jax.ShapeDtypeStruct((N, D), f32),
        compiler_params=pltpu.CompilerParams(dimension_semantics=("arbitrary",), vmem_limit_bytes=VMEM_LIMIT),
        name="combine",
    )(dest_blocks, dest_blocks, x1, wt_rows, ys)


def _token_blocks(a, tt):
    k, n = a.shape
    return a.reshape(k, n // tt, tt).transpose(1, 0, 2).reshape(n // tt, 1, k * tt)


def _layer(x, l, norm1_g, w_in, pool_w, pool_scale, w_pool_up, q_norm_g, k_norm_g, lambda_q1, lambda_k1,
           lambda_q2, lambda_k2, subln_g, w_attn_up, w_out, norm2_g, w_router_group, b_router_group,
           w_router_expert, b_router_expert, w_expert_gate, w_expert_up, w_expert_down):
    B, S, D = x.shape
    N = B * S
    head_dim = q_norm_g.shape[0]
    lam_init = 0.8 - 0.6 * math.exp(-0.3 * l)
    reps = (N_HEADS * 2 * head_dim) // head_dim

    qg = (jnp.tile(q_norm_g, reps) * (head_dim ** -0.5 * LOG2E))[None, :]
    kg = jnp.tile(k_norm_g, reps)[None, :]
    pg, ga, qn, kn, vt = _in_proj(x, norm1_g[None, :], w_in.astype(bf16), pool_w.astype(bf16),
                                  pool_scale[None, :], w_pool_up.astype(bf16), qg, kg)

    tk = min(ATTN_TILE, S)
    slopes2 = jnp.asarray([2.0 ** (-8.0 * (h + 1) / N_HEADS) * LOG2E for h in range(N_HEADS)], f32)
    fq, fk = _alibi_features(slopes2, 2 * tk, tk)
    o = _attention(qn, kn, vt, fq, fk, slopes2, lambda_q1[None, :], lambda_k1[None, :], lambda_q2[None, :],
                   lambda_k2[None, :], subln_g[:, None], lam_init)

    wr = jnp.zeros((ROUTER_ROWS, D), f32)
    wr = wr.at[:N_GROUPS].set(w_router_group.T).at[SUBLANES:].set(w_router_expert.T).astype(bf16)
    br = jnp.zeros((ROUTER_ROWS, 1), f32)
    br = br.at[:N_GROUPS, 0].set(b_router_group).at[SUBLANES:, 0].set(b_router_expert)
    x1, hn, eid, wts, rank, cnt = _merge(pg.reshape(N, D), ga.reshape(N, D), o.reshape(N, -1),
                                           x.reshape(N, D), w_attn_up.astype(bf16), w_out.astype(bf16),
                                           norm2_g[None, :], wr, br)

    R = EXPERT_ROWS
    counts = cnt[:, 0].astype(jnp.int32)
    padded = (counts + R - 1) // R * R
    pend = jnp.cumsum(padded).astype(jnp.int32)
    pstart = pend - padded
    n_blocks = -(-(N * TOP_K) // R) + N_EXPERTS
    starts = jnp.arange(n_blocks, dtype=jnp.int32) * R
    block_eid = jnp.minimum(jnp.sum((pend[None, :] <= starts[:, None]).astype(jnp.int32), axis=1), N_EXPERTS - 1)
    n_used = pend[-1:] // R

    dest = _slots(pstart, eid, rank)
    xs = _dispatch(pstart, pend, counts, _token_blocks(dest, min(DISPATCH_TILE, N)), hn, n_blocks * R)
    ys = _experts(block_eid, n_used, xs, w_expert_gate, w_expert_up, w_expert_down)
    out = _combine(_token_blocks(dest, min(COMBINE_TILE, N)), x1, wts, ys)
    return out.reshape(B, S, D)


def kernel(x, norm1_g, w_in, pool_w, pool_scale, w_pool_up, q_norm_g, k_norm_g, lambda_q1, lambda_k1, lambda_q2,
           lambda_k2, subln_g, w_attn_up, w_out, norm2_g, w_router_group, b_router_group, w_router_expert,
           b_router_expert, w_expert_gate, w_expert_up, w_expert_down):
    params = (norm1_g, w_in, pool_w, pool_scale, w_pool_up, q_norm_g, k_norm_g, lambda_q1, lambda_k1, lambda_q2,
              lambda_k2, subln_g, w_attn_up, w_out, norm2_g, w_router_group, b_router_group, w_router_expert,
              b_router_expert, w_expert_gate, w_expert_up, w_expert_down)
    for l in range(norm1_g.shape[0]):
        x = _layer(x, l, *(p[l] for p in params))
    return x
```

```python
import functools
import math

import jax
import jax.numpy as jnp
from jax import lax
from jax.experimental import pallas as pl
from jax.experimental.pallas import tpu as pltpu

EPS = 1e-6
POOL_WINDOWS = (2, 4, 8, 16)
POOL_HALO = 16
N_HEADS = 4
N_GROUPS = 4
EXPERTS_PER_GROUP = 8
N_EXPERTS = N_GROUPS * EXPERTS_PER_GROUP
TOP_K = 2
LANES = 128
SUBLANES = 8
MXU_COLS = 256
ROUTER_ROWS = 8 + N_EXPERTS
VT_ROWS = LANES + 16
LOG2E = 1.4426950408889634

ROW_TILE = 512
IN_PROJ_SUBTILES = 2
MERGE_SUBTILES = 2
ATTN_TILE = 256
ATTN_HEADS_PER_STEP = 4
EXPERT_ROWS = 1024
DISPATCH_TILE = 512
COMBINE_TILE = 256
VMEM_LIMIT = 52 * 1024 * 1024

f32 = jnp.float32
bf16 = jnp.bfloat16


def _dot(a, b):
    return jnp.dot(a, b, preferred_element_type=f32)


def _dot_nt(a, b):
    return lax.dot_general(a, b, (((1,), (1,)), ((), ())), preferred_element_type=f32)


def _half_lane_rmsnorm(t, n_tiles, rows):
    lane = lax.broadcasted_iota(jnp.int32, (rows, LANES), 1)
    lo_mask = lane < (LANES // 2)
    outs = []
    for i in range(n_tiles):
        c = t[:, i * LANES:(i + 1) * LANES]
        sq = c * c
        lo = jnp.sum(jnp.where(lo_mask, sq, 0.0), axis=-1, keepdims=True)
        hi = jnp.sum(jnp.where(lo_mask, 0.0, sq), axis=-1, keepdims=True)
        ms = jnp.where(lo_mask, lo, hi) * (2.0 / LANES)
        outs.append(c * lax.rsqrt(ms + EPS))
    return jnp.concatenate(outs, axis=-1)


def _in_proj_kernel(x_ref, g1_ref, win_ref, poolw_ref, pscale_ref, wpu_ref, qg_ref, kg_ref,
                    pg_ref, ga_ref, qn_ref, kn_ref, vt_ref, prev_ref, wfold_ref, *, tm, subs, tk, d_model, pool_width,
                    qk_width, attn_width):
    j = pl.program_id(1)
    off_q = pool_width
    off_k = off_q + qk_width
    off_v = off_k + qk_width
    off_gp = off_v + attn_width
    off_ga = off_gp + d_model
    group = pool_width // len(POOL_WINDOWS)
    half = d_model // 2
    n_tiles = qk_width // LANES

    @pl.when(jnp.logical_and(pl.program_id(0) == 0, j == 0))
    def _():
        for g in range(len(POOL_WINDOWS)):
            rows_g = slice(g * group, (g + 1) * group)
            scaled = (poolw_ref[g].astype(f32) * pscale_ref[:, rows_g]).astype(bf16)
            wfold_ref[rows_g, :] = _dot(scaled, wpu_ref[rows_g, :]).astype(bf16)

    @pl.when(j == 0)
    def _():
        prev_ref[...] = jnp.zeros_like(prev_ref)

    extra = lax.broadcasted_iota(jnp.int32, (VT_ROWS - LANES, tk), 0)
    ones_rows = jnp.where(extra == 0, 1.0, 0.0).astype(bf16)
    halo = prev_ref[...]
    for sub in range(subs):
        rows = slice(sub * tm, (sub + 1) * tm)
        x = x_ref[rows, :]
        ms = jnp.mean(x * x, axis=-1, keepdims=True)
        h = (x * lax.rsqrt(ms + EPS) * g1_ref[...]).astype(bf16)

        def proj(lo, width, h=h):
            return _dot(h, win_ref[:, lo:lo + width])

        u = proj(0, pool_width)
        v = proj(off_v, attn_width)
        for hh in range(attn_width // LANES):
            for c in range(tm // tk):
                blk = sub * (tm // tk) + c
                vt_ref[hh, blk, 0:LANES, :] = v[c * tk:(c + 1) * tk, hh * LANES:(hh + 1) * LANES].T.astype(bf16)
                vt_ref[hh, blk, LANES:VT_ROWS, :] = ones_rows
        qn_ref[rows, :] = (_half_lane_rmsnorm(proj(off_q, qk_width), n_tiles, tm) * qg_ref[...]).astype(bf16)
        kn_ref[rows, :] = (_half_lane_rmsnorm(proj(off_k, qk_width), n_tiles, tm) * kg_ref[...]).astype(bf16)

        ext = jnp.concatenate([halo, u], axis=0)
        halo = u[tm - POOL_HALO:, :]
        pos = (j * subs + sub) * tm + lax.broadcasted_iota(jnp.int32, (tm, 1), 0)
        ds = []
        for g, w in enumerate(POOL_WINDOWS):
            acc = ext[:, g * group:(g + 1) * group]
            shift = 1
            while shift < w:
                acc = acc + pltpu.roll(acc, shift, 0)
                shift *= 2
            wsum = acc[POOL_HALO:, :]
            cnt = jnp.minimum(pos + 1, w).astype(f32)
            ds.append((wsum / cnt - u[:, g * group:(g + 1) * group]).astype(bf16))
        pool_out = _dot(jnp.concatenate(ds, axis=-1), wfold_ref[...])

        for c in range(2):
            cols = slice(c * half, (c + 1) * half)
            gp = jax.nn.sigmoid(proj(off_gp + c * half, half))
            pg_ref[rows, cols] = (gp * pool_out[:, cols]).astype(bf16)
            ga_ref[rows, cols] = jax.nn.sigmoid(proj(off_ga + c * half, half)).astype(bf16)
    prev_ref[...] = halo


def _in_proj(x, g1, w_in, pool_w, pool_scale, w_pool_up, qg, kg):
    B, S, D = x.shape
    subs = IN_PROJ_SUBTILES if S % (IN_PROJ_SUBTILES * ROW_TILE) == 0 else 1
    tm = min(ROW_TILE, S)
    pool_width = w_pool_up.shape[0]
    qk_width = qg.shape[1]
    attn_width = qk_width
    in_width = w_in.shape[1]
    const2 = lambda b, j: (0, 0)
    row = lambda b, j: (b, j, 0)
    tk = min(ATTN_TILE, S)
    n_heads = attn_width // LANES
    kern = functools.partial(_in_proj_kernel, tm=tm, subs=subs, tk=tk, d_model=D, pool_width=pool_width,
                             qk_width=qk_width, attn_width=attn_width)
    tb = subs * tm
    return pl.pallas_call(
        kern,
        grid=(B, S // tb),
        in_specs=[
            pl.BlockSpec((None, tb, D), row),
            pl.BlockSpec((1, D), const2),
            pl.BlockSpec((D, in_width), const2, pipeline_mode=pl.Buffered(1)),
            pl.BlockSpec(pool_w.shape, lambda b, j: (0, 0, 0)),
            pl.BlockSpec((1, pool_width), const2),
            pl.BlockSpec((pool_width, D), const2),
            pl.BlockSpec((1, qk_width), const2),
            pl.BlockSpec((1, qk_width), const2),
        ],
        out_specs=[
            pl.BlockSpec((None, tb, D), row),
            pl.BlockSpec((None, tb, D), row),
            pl.BlockSpec((None, tb, qk_width), row),
            pl.BlockSpec((None, tb, qk_width), row),
            pl.BlockSpec((None, n_heads, tb // tk, VT_ROWS, tk), lambda b, j: (b, 0, j, 0, 0)),
        ],
        out_shape=[
            jax.ShapeDtypeStruct((B, S, D), bf16),
            jax.ShapeDtypeStruct((B, S, D), bf16),
            jax.ShapeDtypeStruct((B, S, qk_width), bf16),
            jax.ShapeDtypeStruct((B, S, qk_width), bf16),
            jax.ShapeDtypeStruct((B, n_heads, S // tk, VT_ROWS, tk), bf16),
        ],
        scratch_shapes=[pltpu.VMEM((POOL_HALO, pool_width), f32), pltpu.VMEM((pool_width, D), bf16)],
        compiler_params=pltpu.CompilerParams(dimension_semantics=("arbitrary", "arbitrary"),
                                             vmem_limit_bytes=VMEM_LIMIT),
        name="in_proj",
    )(x, g1, w_in, pool_w, pool_scale, w_pool_up, qg, kg)


def _attn_kernel(slopes_ref, q_ref, k_ref, vt_ref, fq_ref, fk_ref, lq1_ref, lk1_ref, lq2_ref, lk2_ref, sg_ref,
                 o_ref, qa_ref, sa_ref, sb_ref, pa_ref, pb_ref, m_ref, alpha_ref, acc_ref, *,
                 tq, tk, heads, lam_init):
    hg = pl.program_id(1)
    i = pl.program_id(2)
    fk = fk_ref[...]
    lane = lax.broadcasted_iota(jnp.int32, (tk, LANES), 1)
    first = lane < (LANES // 2)
    hs = range(heads)

    buf_a = (sa_ref, pa_ref)
    buf_b = (sb_ref, pb_ref)

    def scores(g, n, buf):
        kb = k_ref[pl.ds(pl.multiple_of(n * tk, tk), tk), g * LANES:(g + 1) * LANES]
        ka = jnp.concatenate([kb, fk], axis=1)
        for c0 in range(0, 2 * tq, MXU_COLS):
            buf[0][g, :, c0:c0 + MXU_COLS] = _dot_nt(ka, qa_ref[g, c0:c0 + MXU_COLS, :])

    def softmax(g, n, buf, first_half_mask=None):
        s_ref, p_ref = buf
        c = -slopes_ref[hg * heads + g] * (i * tq - n * tk).astype(f32)
        alphas = []
        for c0 in range(0, 2 * tq, LANES):
            cols = slice(c0, c0 + LANES)
            s = s_ref[g, :, cols]
            if first_half_mask is not None and c0 < tq:
                s = jnp.where(first_half_mask[:, cols], s, -jnp.inf)
            m_old = m_ref[g, :, cols]
            m_new = jnp.maximum(m_old, jnp.max(s, axis=0, keepdims=True) + c)
            p_ref[g, :, cols] = jnp.exp2(s - (m_new - c)).astype(bf16)
            m_ref[g, :, cols] = m_new
            alphas.append(jnp.exp2(m_old - m_new))
        return jnp.concatenate(alphas, axis=1)

    def pv(g, n, buf, alpha):
        vt = vt_ref[g, jnp.maximum(n, 0)]
        for c0 in range(0, 2 * tq, MXU_COLS):
            cols = slice(c0, c0 + MXU_COLS)
            acc_ref[g, :, cols] = alpha[:, cols] * acc_ref[g, :, cols] + _dot(vt, buf[1][g, :, cols])

    m_ref[...] = jnp.full(m_ref.shape, -jnp.inf, f32)
    alpha_ref[...] = jnp.ones_like(alpha_ref)
    acc_ref[...] = jnp.zeros_like(acc_ref)
    pb_ref[...] = jnp.zeros_like(pb_ref)
    zero = jnp.zeros((tk, LANES), bf16)
    for g in hs:
        parts = []
        for half in range(2):
            q = q_ref[half * tk:(half + 1) * tk, g * LANES:(g + 1) * LANES]
            parts += [jnp.where(first, q, zero), jnp.where(first, zero, q)]
        qa_ref[g] = jnp.concatenate([jnp.concatenate(parts, axis=0), fq_ref[g]], axis=1)
    for g in hs:
        scores(g, 0, buf_a)

    def pair(t, carry):
        n = 2 * t
        a_prev = [alpha_ref[g] for g in hs]
        a_even = [None] * heads
        for g in hs:
            pv(g, n - 1, buf_b, a_prev[g])
            a_even[g] = softmax(g, n, buf_a)
            scores(g, n + 1, buf_b)
        for g in hs:
            pv(g, n, buf_a, a_even[g])
            alpha_ref[g] = softmax(g, n + 1, buf_b)
            scores(g, n + 2, buf_a)
        return carry

    lax.fori_loop(0, i, pair, 0)
    n = 2 * i
    lam = (jnp.exp(jnp.sum(lq1_ref[...] * lk1_ref[...], keepdims=True))
           - jnp.exp(jnp.sum(lq2_ref[...] * lk2_ref[...], keepdims=True)) + lam_init)
    late = slice(tq, 2 * tq)
    kk = lax.broadcasted_iota(jnp.int32, (tk, tq), 0)
    qq = lax.broadcasted_iota(jnp.int32, (tk, tq), 1)
    tri = kk <= jnp.where(qq >= tk, qq - tk, qq)
    for g in hs:
        a_even = softmax(g, n, buf_a, first_half_mask=tri)
        pv(g, n - 1, buf_b, alpha_ref[g])
        pv(g, n, buf_a, a_even)
        kb = k_ref[pl.ds(pl.multiple_of((n + 1) * tk, tk), tk), g * LANES:(g + 1) * LANES]
        s = _dot_nt(jnp.concatenate([kb, fk], axis=1), qa_ref[g, late, :])
        s = jnp.where(tri, s, -jnp.inf)
        c = slopes_ref[hg * heads + g] * float(tk)
        m_old = m_ref[g, :, late]
        m_new = jnp.maximum(m_old, jnp.max(s, axis=0, keepdims=True) + c)
        p = jnp.exp2(s - (m_new - c)).astype(bf16)
        acc = acc_ref[g]
        acc_late = jnp.exp2(m_old - m_new) * acc[:, late] + _dot(vt_ref[g, n + 1], p)
        acc = jnp.concatenate([acc[:, :tq], acc_late], axis=1)
        o_all = acc[0:LANES] * (1.0 / acc[LANES:LANES + 1])
        o = jnp.concatenate([o_all[:, 0:tk] - lam * o_all[:, tk:tq],
                             o_all[:, tq:tq + tk] - lam * o_all[:, tq + tk:]], axis=1)
        ms = jnp.mean(o * o, axis=0, keepdims=True)
        on = o * lax.rsqrt(ms + EPS) * sg_ref[...] * (1.0 - lam_init)
        o_ref[:, g * LANES:(g + 1) * LANES] = on.T.astype(bf16)


def _split_bf16(x, pieces=3):
    out = []
    for _ in range(pieces):
        p = x.astype(bf16)
        out.append(p)
        x = x - p.astype(f32)
    return out


def _alibi_features(slopes2, tq, tk):
    assert tk <= 256
    assert tq == 2 * tk
    n_heads = slopes2.shape[0]
    ones = jnp.ones((tk, 1), bf16)
    krel = jnp.arange(tk, dtype=f32).astype(bf16)[:, None]
    fk = jnp.concatenate([krel] * 3 + [ones] * 3 + [jnp.zeros((tk, LANES - 6), bf16)], axis=1)
    first, last = jnp.arange(tk, dtype=f32), jnp.arange(tk, tq, dtype=f32)
    qrel = jnp.concatenate([first, first, last, last])
    a = _split_bf16(slopes2)
    b = _split_bf16(-slopes2[:, None] * qrel[None, :])
    cols = [jnp.broadcast_to(p[:, None, None], (n_heads, 2 * tq, 1)) for p in a] + [p[:, :, None] for p in b]
    fq = jnp.concatenate(cols + [jnp.zeros((n_heads, 2 * tq, LANES - 6), bf16)], axis=2)
    return fq, fk


def _attention(qn, kn, vt, fq, fk, slopes, lq1, lk1, lq2, lk2, subln_col, lam_init):
    B, S, _ = qn.shape
    tk = fk.shape[0]
    tq = fq.shape[1] // 2
    assert tq == 2 * tk
    nkv = S // tk
    const2 = lambda b, h, i, *_: (0, 0)
    G = ATTN_HEADS_PER_STEP
    kern = functools.partial(_attn_kernel, tq=tq, tk=tk, heads=G, lam_init=lam_init)
    hd = lq1.shape[1]
    return pl.pallas_call(
        kern,
        grid_spec=pltpu.PrefetchScalarGridSpec(
            num_scalar_prefetch=1,
            grid=(B, N_HEADS // G, S // tq),
            in_specs=[
                pl.BlockSpec((None, tq, G * LANES), lambda b, h, i, *_: (b, i, h)),
                pl.BlockSpec((None, S, G * LANES), lambda b, h, i, *_: (b, 0, h)),
                pl.BlockSpec((None, G, nkv, VT_ROWS, tk), lambda b, h, i, *_: (b, h, 0, 0, 0)),
                pl.BlockSpec((G, 2 * tq, LANES), lambda b, h, i, *_: (h, 0, 0)),
                pl.BlockSpec((tk, LANES), const2),
                pl.BlockSpec((1, hd), const2),
                pl.BlockSpec((1, hd), const2),
                pl.BlockSpec((1, hd), const2),
                pl.BlockSpec((1, hd), const2),
                pl.BlockSpec((LANES, 1), const2),
            ],
            out_specs=pl.BlockSpec((None, tq, G * LANES), lambda b, h, i, *_: (b, i, h)),
            scratch_shapes=[pltpu.VMEM((G, 2 * tq, 2 * LANES), bf16),
                            pltpu.VMEM((G, tk, 2 * tq), f32), pltpu.VMEM((G, tk, 2 * tq), f32),
                            pltpu.VMEM((G, tk, 2 * tq), bf16), pltpu.VMEM((G, tk, 2 * tq), bf16),
                            pltpu.VMEM((G, 1, 2 * tq), f32), pltpu.VMEM((G, 1, 2 * tq), f32),
                            pltpu.VMEM((G, VT_ROWS, 2 * tq), f32)],
        ),
        out_shape=jax.ShapeDtypeStruct((B, S, N_HEADS * LANES), bf16),
        compiler_params=pltpu.CompilerParams(dimension_semantics=("arbitrary", "arbitrary", "arbitrary"),
                                             vmem_limit_bytes=VMEM_LIMIT),
        name="diff_attn",
    )(slopes, qn, kn, vt, fq, fk, lq1, lk1, lq2, lk2, subln_col)


def _merge_kernel(pg_ref, ga_ref, o_ref, x_ref, wau_ref, wout_ref, g2_ref, wr_ref, br_ref,
                  x1_ref, hn_ref, eid_ref, wt_ref, rank_ref, cnt_ref, base_ref, *, tm, subs, d_model):
    step = pl.program_id(0)

    @pl.when(step == 0)
    def _():
        base_ref[...] = jnp.zeros_like(base_ref)

    a = lax.broadcasted_iota(jnp.int32, (tm, tm), 0)
    b = lax.broadcasted_iota(jnp.int32, (tm, tm), 1)
    upper = jnp.where(a <= b, 1.0, 0.0).astype(bf16)
    base = base_ref[...]
    all_logits = []
    for sub in range(subs):
        rows = slice(sub * tm, (sub + 1) * tm)
        attn_out = _dot(o_ref[rows, :], wau_ref[...])
        merged = pg_ref[rows, :].astype(f32) + ga_ref[rows, :].astype(f32) * attn_out
        x1 = x_ref[rows, :] + _dot(merged.astype(bf16), wout_ref[...])
        x1_ref[rows, :] = x1
        ms = jnp.mean(x1 * x1, axis=-1, keepdims=True)
        hn = (x1 * lax.rsqrt(ms + EPS) * g2_ref[...]).astype(bf16)
        hn_ref[rows, :] = hn

        all_logits.append(_dot_nt(wr_ref[...], hn) + br_ref[...])

    for sub in range(subs):
        rows = slice(sub * tm, (sub + 1) * tm)
        logits = all_logits[sub]
        lg = logits[0:N_GROUPS]
        gmax = jnp.max(lg, axis=0, keepdims=True)
        p_top = 1.0 / jnp.sum(jnp.exp(lg - gmax), axis=0, keepdims=True)
        grow = lax.broadcasted_iota(jnp.int32, lg.shape, 0).astype(f32)
        g_idx = jnp.min(jnp.where(lg == gmax, grow, float(N_GROUPS)), axis=0, keepdims=True)

        sel = jnp.zeros((EXPERTS_PER_GROUP, tm), f32)
        for g in range(N_GROUPS):
            le_g = logits[SUBLANES + g * EXPERTS_PER_GROUP:SUBLANES + (g + 1) * EXPERTS_PER_GROUP]
            sel = jnp.where(g_idx == float(g), le_g, sel)
        erow = lax.broadcasted_iota(jnp.int32, sel.shape, 0).astype(f32)
        e1 = jnp.max(sel, axis=0, keepdims=True)
        i1 = jnp.min(jnp.where(sel == e1, erow, float(EXPERTS_PER_GROUP)), axis=0, keepdims=True)
        sel2 = jnp.where(erow == i1, -jnp.inf, sel)
        e2 = jnp.max(sel2, axis=0, keepdims=True)
        i2 = jnp.min(jnp.where(sel2 == e2, erow, float(EXPERTS_PER_GROUP)), axis=0, keepdims=True)
        r = jnp.exp(e2 - e1)
        w1 = p_top / (1.0 + r)
        w2 = p_top * r / (1.0 + r)
        eid1 = g_idx * float(EXPERTS_PER_GROUP) + i1
        eid2 = g_idx * float(EXPERTS_PER_GROUP) + i2
        eid_ref[:, rows] = jnp.concatenate([eid1, eid2], axis=0).astype(jnp.int32)
        wt_ref[:, rows] = jnp.concatenate([w1, w2], axis=0)

        xrow = lax.broadcasted_iota(jnp.int32, (N_EXPERTS, tm), 0).astype(f32)
        oh1 = jnp.where(xrow == eid1, 1.0, 0.0)
        oh2 = jnp.where(xrow == eid2, 1.0, 0.0)
        oh = oh1 + oh2
        before = _dot(oh.astype(bf16), upper) + base - 1.0
        rank1 = jnp.sum(oh1 * before, axis=0, keepdims=True)
        rank2 = jnp.sum(oh2 * before, axis=0, keepdims=True)
        rank_ref[:, rows] = jnp.concatenate([rank1, rank2], axis=0).astype(jnp.int32)
        base = base + jnp.sum(oh, axis=1, keepdims=True)
    base_ref[...] = base
    cnt_ref[...] = jnp.broadcast_to(base, cnt_ref.shape)


def _merge(pg, ga, o, x, w_attn_up, w_out, g2, wr, br):
    N, D = x.shape
    tm = min(ROW_TILE, N)
    subs = MERGE_SUBTILES if N % (MERGE_SUBTILES * tm) == 0 else 1
    tb = subs * tm
    aw = o.shape[1]
    const2 = lambda i: (0, 0)
    row = lambda i: (i, 0)
    colblk = lambda i: (0, i)
    kern = functools.partial(_merge_kernel, tm=tm, subs=subs, d_model=D)
    return pl.pallas_call(
        kern,
        grid=(N // tb,),
        in_specs=[
            pl.BlockSpec((tb, D), row),
            pl.BlockSpec((tb, D), row),
            pl.BlockSpec((tb, aw), row),
            pl.BlockSpec((tb, D), row),
            pl.BlockSpec((aw, D), const2),
            pl.BlockSpec((D, D), const2),
            pl.BlockSpec((1, D), const2),
            pl.BlockSpec((ROUTER_ROWS, D), const2),
            pl.BlockSpec((ROUTER_ROWS, 1), const2),
        ],
        out_specs=[
            pl.BlockSpec((tb, D), row),
            pl.BlockSpec((tb, D), row),
            pl.BlockSpec((TOP_K, tb), colblk),
            pl.BlockSpec((TOP_K, tb), colblk),
            pl.BlockSpec((TOP_K, tb), colblk),
            pl.BlockSpec((N_EXPERTS, LANES), const2),
        ],
        out_shape=[
            jax.ShapeDtypeStruct((N, D), f32),
            jax.ShapeDtypeStruct((N, D), bf16),
            jax.ShapeDtypeStruct((TOP_K, N), jnp.int32),
            jax.ShapeDtypeStruct((TOP_K, N), f32),
            jax.ShapeDtypeStruct((TOP_K, N), jnp.int32),
            jax.ShapeDtypeStruct((N_EXPERTS, LANES), f32),
        ],
        scratch_shapes=[pltpu.VMEM((N_EXPERTS, 1), f32)],
        compiler_params=pltpu.CompilerParams(dimension_semantics=("arbitrary",), vmem_limit_bytes=VMEM_LIMIT),
        name="merge_router",
    )(pg, ga, o, x, w_attn_up, w_out, g2, wr, br)


def _slots_kernel(pstart_ref, eid_ref, rank_ref, dest_ref):
    eid = eid_ref[...]
    start = jnp.zeros_like(eid)
    for e in range(N_EXPERTS):
        start = jnp.where(eid == e, pstart_ref[e], start)
    dest_ref[...] = start + rank_ref[...]


def _slots(pstart, eid, rank):
    k, n = eid.shape
    tn = min(n, 8192)
    blk = pl.BlockSpec((k, tn), lambda i, *_: (0, i))
    return pl.pallas_call(
        _slots_kernel,
        grid_spec=pltpu.PrefetchScalarGridSpec(num_scalar_prefetch=1, grid=(n // tn,), in_specs=[blk, blk],
                                               out_specs=blk),
        out_shape=jax.ShapeDtypeStruct((k, n), jnp.int32),
        name="slots",
    )(pstart, eid, rank)


def _row(ref, r):
    start = r * SUBLANES if isinstance(r, int) else pl.multiple_of(r * SUBLANES, SUBLANES)
    return ref.at[pl.ds(start, SUBLANES), :]


def _wait_bytes_of(ref_like, any_hbm, sem):
    n = ref_like.shape[0]
    pltpu.make_async_copy(any_hbm.at[pl.ds(0, n), :], any_hbm.at[pl.ds(0, n), :], sem).wait()


STAGES = 3


def _dispatch_kernel(pstart_ref, pend_ref, count_ref, dest_ref, hn_ref, xs_hbm, zero_ref, stage, out_sem, zsem, *,
                     tt, rows, n_blocks, d_model):
    i = pl.program_id(0)
    n = pl.num_programs(0)

    def fill_unwritten(start):
        def piece(n_rows, first_row):
            cp = pltpu.make_async_copy(zero_ref.at[pl.ds(0, n_rows * SUBLANES), :],
                                       xs_hbm.at[pl.ds(pl.multiple_of(first_row * SUBLANES, SUBLANES),
                                                       n_rows * SUBLANES), :], zsem)
            if start:
                cp.start()
            else:
                cp.wait()

        for e in range(N_EXPERTS):
            row = pstart_ref[e] + count_ref[e]
            n_pad = pend_ref[e] - row
            bit = rows // 2
            while bit >= 1:
                pl.when((n_pad & bit) != 0)(functools.partial(piece, bit, row))
                row = row + (n_pad & bit)
                bit //= 2

        def block(b, carry):
            piece(rows, b * rows)
            return carry
        lax.fori_loop(pend_ref[N_EXPERTS - 1] // rows, n_blocks, block, 0)

    @pl.when(i == 0)
    def _():
        zero_ref[...] = jnp.zeros_like(zero_ref)
        fill_unwritten(start=True)

    def wait_rows_of(step):
        for _ in range(TOP_K):
            _wait_bytes_of(stage.at[0], xs_hbm, out_sem.at[step % STAGES])

    @pl.when(i >= STAGES - 1)
    def _():
        wait_rows_of(i - (STAGES - 1))

    slot = i % STAGES
    for s in range(d_model // LANES):
        stage[slot, pl.ds(s, tt, stride=SUBLANES), :] = hn_ref[:, s * LANES:(s + 1) * LANES].astype(f32)

    def body(t, carry):
        for k in range(TOP_K):
            pltpu.make_async_copy(_row(stage.at[slot], t), _row(xs_hbm, dest_ref[0, k * tt + t]),
                                  out_sem.at[slot]).start(priority=k)
        return carry
    lax.fori_loop(0, tt, body, 0, unroll=8)

    @pl.when(i == n - 1)
    def _():
        for back in range(STAGES - 2, -1, -1):
            @pl.when(i - back >= 0)
            def _():
                wait_rows_of(i - back)
        fill_unwritten(start=False)


def _dispatch(pstart, pend, counts, dest_blocks, hn, n_slots):
    n, _, width = dest_blocks.shape
    tt = width // TOP_K
    rows = EXPERT_ROWS
    D = hn.shape[1]
    kern = functools.partial(_dispatch_kernel, tt=tt, rows=rows, n_blocks=n_slots // rows, d_model=D)
    smem_blk = pl.BlockSpec((None, 1, width), lambda i, *_: (i, 0, 0), memory_space=pltpu.SMEM)
    return pl.pallas_call(
        kern,
        grid_spec=pltpu.PrefetchScalarGridSpec(
            num_scalar_prefetch=3,
            grid=(n,),
            in_specs=[smem_blk, pl.BlockSpec((tt, D), lambda i, *_: (i, 0))],
            out_specs=pl.BlockSpec(memory_space=pl.ANY),
            scratch_shapes=[pltpu.VMEM((rows * SUBLANES, LANES), f32),
                            pltpu.VMEM((STAGES, tt * SUBLANES, LANES), f32),
                            pltpu.SemaphoreType.DMA((STAGES,)),
                            pltpu.SemaphoreType.DMA(())],
        ),
        out_shape=jax.ShapeDtypeStruct((n_slots * SUBLANES, LANES), f32),
        compiler_params=pltpu.CompilerParams(dimension_semantics=("arbitrary",), vmem_limit_bytes=VMEM_LIMIT,
                                             has_side_effects=True),
        name="dispatch",
    )(pstart, pend, counts, dest_blocks, hn)


def _rows_from_token_major(buf, n_rows, n_tiles):
    return jnp.concatenate([buf[pl.ds(s, n_rows, stride=SUBLANES), :] for s in range(n_tiles)], axis=-1)


def _expert_kernel(beid_ref, nbu_ref, xs_ref, wg_ref, wu_ref, wd_ref, ys_ref, wg_bf, wu_bf, wd_bf, *, rows,
                   d_model):
    b = pl.program_id(0)
    changed = jnp.logical_or(b == 0, beid_ref[b] != beid_ref[jnp.maximum(b - 1, 0)])

    @pl.when(changed)
    def _():
        wg_bf[...] = wg_ref[...].astype(bf16)
        wu_bf[...] = wu_ref[...].astype(bf16)
        wd_bf[...] = wd_ref[...].astype(bf16)

    n_tiles = d_model // LANES

    @pl.when(b < nbu_ref[0])
    def _():
        xb = _rows_from_token_major(xs_ref, rows, n_tiles).astype(bf16)
        hdn = jax.nn.silu(_dot(xb, wg_bf[...])) * _dot(xb, wu_bf[...])
        y = _dot(hdn.astype(bf16), wd_bf[...])
        for s in range(n_tiles):
            ys_ref[pl.ds(s, rows, stride=SUBLANES), :] = y[:, s * LANES:(s + 1) * LANES]

    @pl.when(b >= nbu_ref[0])
    def _():
        ys_ref[...] = jnp.zeros_like(ys_ref)


def _experts(block_eid, n_used, xs, w_gate, w_up, w_down):
    n_blocks = block_eid.shape[0]
    rows = EXPERT_ROWS
    E, D, DE = w_gate.shape
    kern = functools.partial(_expert_kernel, rows=rows, d_model=D)
    xs_idx = lambda b, eid, nbu: (jnp.minimum(b, nbu[0] - 1), 0)
    return pl.pallas_call(
        kern,
        grid_spec=pltpu.PrefetchScalarGridSpec(
            num_scalar_prefetch=2,
            grid=(n_blocks,),
            in_specs=[
                pl.BlockSpec((rows * SUBLANES, LANES), xs_idx),
                pl.BlockSpec((None, D, DE), lambda b, eid, nbu: (eid[b], 0, 0)),
                pl.BlockSpec((None, D, DE), lambda b, eid, nbu: (eid[b], 0, 0)),
                pl.BlockSpec((None, DE, D), lambda b, eid, nbu: (eid[b], 0, 0)),
            ],
            out_specs=pl.BlockSpec((rows * SUBLANES, LANES), lambda b, *_: (b, 0)),
            scratch_shapes=[
                pltpu.VMEM((D, DE), bf16),
                pltpu.VMEM((D, DE), bf16),
                pltpu.VMEM((DE, D), bf16),
            ],
        ),
        out_shape=jax.ShapeDtypeStruct((n_blocks * rows * SUBLANES, LANES), f32),
        compiler_params=pltpu.CompilerParams(dimension_semantics=("arbitrary",), vmem_limit_bytes=VMEM_LIMIT),
        name="experts",
    )(block_eid, n_used, xs, w_gate, w_up, w_down)


def _combine_kernel(dest_ref, destn_ref, x1_ref, wt_ref, ys_hbm, out_ref, buf, sem, *, te, d_model):
    i = pl.program_id(0)
    n = pl.num_programs(0)
    slot = i % 2

    def start_gather(d_ref, dst, dsem):
        for t in range(te):
            for k in range(TOP_K):
                r = k * te + t
                pltpu.make_async_copy(_row(ys_hbm, d_ref[0, r]), _row(dst, r), dsem).start(priority=k)

    @pl.when(i == 0)
    def _():
        start_gather(dest_ref, buf.at[0], sem.at[0])

    @pl.when(i + 1 < n)
    def _():
        start_gather(destn_ref, buf.at[1 - slot], sem.at[1 - slot])

    _wait_bytes_of(buf.at[slot], ys_hbm, sem.at[slot])
    n_tiles = d_model // LANES
    both = _rows_from_token_major(buf.at[slot], TOP_K * te, n_tiles)
    wt = jnp.concatenate([wt_ref[...], jnp.zeros((SUBLANES - TOP_K, te), f32)], axis=0).T
    out_ref[...] = x1_ref[...] + (wt[:, 0:1] * both[:te] + wt[:, 1:2] * both[te:])


def _combine(dest_blocks, x1, wt_rows, ys):
    N, D = x1.shape
    n, _, width = dest_blocks.shape
    te = width // TOP_K
    kern = functools.partial(_combine_kernel, te=te, d_model=D)
    cur = pl.BlockSpec((None, 1, width), lambda i: (i, 0, 0), memory_space=pltpu.SMEM)
    nxt = pl.BlockSpec((None, 1, width), lambda i: (jnp.minimum(i + 1, n - 1), 0, 0), memory_space=pltpu.SMEM)
    return pl.pallas_call(
        kern,
        grid=(n,),
        in_specs=[
            cur, nxt,
            pl.BlockSpec((te, D), lambda i: (i, 0)),
            pl.BlockSpec((TOP_K, te), lambda i: (0, i)),
            pl.BlockSpec(memory_space=pl.ANY),
        ],
        out_specs=pl.BlockSpec((te, D), lambda i: (i, 0)),
        scratch_shapes=[pltpu.VMEM((2, TOP_K * te * SUBLANES, LANES), f32), pltpu.SemaphoreType.DMA((2,))],
        out_shape=jax.ShapeDtypeStruct((N, D), f32),
        compiler_params=pltpu.CompilerParams(dimension_semantics=("arbitrary",), vmem_limit_bytes=VMEM_LIMIT),
        name="combine",
    )(dest_blocks, dest_blocks, x1, wt_rows, ys)


def _token_blocks(a, tt):
    k, n = a.shape
    return a.reshape(k, n // tt, tt).transpose(1, 0, 2).reshape(n // tt, 1, k * tt)


def _layer(x, l, norm1_g, w_in, pool_w, pool_scale, w_pool_up, q_norm_g, k_norm_g, lambda_q1, lambda_k1,
           lambda_q2, lambda_k2, subln_g, w_attn_up, w_out, norm2_g, w_router_group, b_router_group,
           w_router_expert, b_router_expert, w_expert_gate, w_expert_up, w_expert_down):
    B, S, D = x.shape
    N = B * S
    head_dim = q_norm_g.shape[0]
    lam_init = 0.8 - 0.6 * math.exp(-0.3 * l)
    reps = (N_HEADS * 2 * head_dim) // head_dim

    qg = (jnp.tile(q_norm_g, reps) * (head_dim ** -0.5 * LOG2E))[None, :]
    kg = jnp.tile(k_norm_g, reps)[None, :]
    pg, ga, qn, kn, vt = _in_proj(x, norm1_g[None, :], w_in.astype(bf16), pool_w.astype(bf16),
                                  pool_scale[None, :], w_pool_up.astype(bf16), qg, kg)

    tk = min(ATTN_TILE, S)
    slopes2 = jnp.asarray([2.0 ** (-8.0 * (h + 1) / N_HEADS) * LOG2E for h in range(N_HEADS)], f32)
    fq, fk = _alibi_features(slopes2, 2 * tk, tk)
    o = _attention(qn, kn, vt, fq, fk, slopes2, lambda_q1[None, :], lambda_k1[None, :], lambda_q2[None, :],
                   lambda_k2[None, :], subln_g[:, None], lam_init)

    wr = jnp.zeros((ROUTER_ROWS, D), f32)
    wr = wr.at[:N_GROUPS].set(w_router_group.T).at[SUBLANES:].set(w_router_expert.T).astype(bf16)
    br = jnp.zeros((ROUTER_ROWS, 1), f32)
    br = br.at[:N_GROUPS, 0].set(b_router_group).at[SUBLANES:, 0].set(b_router_expert)
    x1, hn, eid, wts, rank, cnt = _merge(pg.reshape(N, D), ga.reshape(N, D), o.reshape(N, -1),
                                           x.reshape(N, D), w_attn_up.astype(bf16), w_out.astype(bf16),
                                           norm2_g[None, :], wr, br)

    R = EXPERT_ROWS
    counts = cnt[:, 0].astype(jnp.int32)
    padded = (counts + R - 1) // R * R
    pend = jnp.cumsum(padded).astype(jnp.int32)
    pstart = pend - padded
    n_blocks = -(-(N * TOP_K) // R) + N_EXPERTS
    starts = jnp.arange(n_blocks, dtype=jnp.int32) * R
    block_eid = jnp.minimum(jnp.sum((pend[None, :] <= starts[:, None]).astype(jnp.int32), axis=1), N_EXPERTS - 1)
    n_used = pend[-1:] // R

    dest = _slots(pstart, eid, rank)
    xs = _dispatch(pstart, pend, counts, _token_blocks(dest, min(DISPATCH_TILE, N)), hn, n_blocks * R)
    ys = _experts(block_eid, n_used, xs, w_expert_gate, w_expert_up, w_expert_down)
    out = _combine(_token_blocks(dest, min(COMBINE_TILE, N)), x1, wts, ys)
    return out.reshape(B, S, D)


def kernel(x, norm1_g, w_in, pool_w, pool_scale, w_pool_up, q_norm_g, k_norm_g, lambda_q1, lambda_k1, lambda_q2,
           lambda_k2, subln_g, w_attn_up, w_out, norm2_g, w_router_group, b_router_group, w_router_expert,
           b_router_expert, w_expert_gate, w_expert_up, w_expert_down):
    params = (norm1_g, w_in, pool_w, pool_scale, w_pool_up, q_norm_g, k_norm_g, lambda_q1, lambda_k1, lambda_q2,
              lambda_k2, subln_g, w_attn_up, w_out, norm2_g, w_router_group, b_router_group, w_router_expert,
              b_router_expert, w_expert_gate, w_expert_up, w_expert_down)
    for l in range(norm1_g.shape[0]):
        x = _layer(x, l, *(p[l] for p in params))
    return x
```

```python
import functools
import math

import jax
import jax.numpy as jnp
from jax import lax
from jax.experimental import pallas as pl
from jax.experimental.pallas import tpu as pltpu

EPS = 1e-6
POOL_WINDOWS = (2, 4, 8, 16)
POOL_HALO = 16
N_HEADS = 4
N_GROUPS = 4
EXPERTS_PER_GROUP = 8
N_EXPERTS = N_GROUPS * EXPERTS_PER_GROUP
TOP_K = 2
LANES = 128
SUBLANES = 8
ROUTER_ROWS = 8 + N_EXPERTS
VT_ROWS = LANES + 16
LOG2E = 1.4426950408889634

ROW_TILE = 512
IN_PROJ_SUBTILES = 2
MERGE_SUBTILES = 2
ATTN_TILE = 256
ATTN_HEADS_PER_STEP = 4
EXPERT_ROWS = 1024
DISPATCH_TILE = 512
COMBINE_TILE = 256
V7X_VMEM_BYTES = 64 * 1024 * 1024
VMEM_LIMIT = V7X_VMEM_BYTES * 13 // 16

f32 = jnp.float32
bf16 = jnp.bfloat16


def _dot(a, b):
    return jnp.dot(a, b, preferred_element_type=f32)


def _dot_nt(a, b):
    return lax.dot_general(a, b, (((1,), (1,)), ((), ())), preferred_element_type=f32)


def _half_lane_rmsnorm(t, n_tiles, rows):
    lane = lax.broadcasted_iota(jnp.int32, (rows, LANES), 1)
    lo_mask = lane < (LANES // 2)
    outs = []
    for i in range(n_tiles):
        c = t[:, i * LANES:(i + 1) * LANES]
        sq = c * c
        lo = jnp.sum(jnp.where(lo_mask, sq, 0.0), axis=-1, keepdims=True)
        hi = jnp.sum(jnp.where(lo_mask, 0.0, sq), axis=-1, keepdims=True)
        ms = jnp.where(lo_mask, lo, hi) * (2.0 / LANES)
        outs.append(c * lax.rsqrt(ms + EPS))
    return jnp.concatenate(outs, axis=-1)


def _in_proj_kernel(x_ref, g1_ref, win_ref, poolw_ref, pscale_ref, wpu_ref, qg_ref, kg_ref,
                    pg_ref, ga_ref, qn_ref, kn_ref, vt_ref, prev_ref, wfold_ref, *, tm, subs, tk, d_model, pool_width,
                    qk_width, attn_width):
    j = pl.program_id(1)
    off_q = pool_width
    off_k = off_q + qk_width
    off_v = off_k + qk_width
    off_gp = off_v + attn_width
    off_ga = off_gp + d_model
    group = pool_width // len(POOL_WINDOWS)
    half = d_model // 2
    n_tiles = qk_width // LANES

    @pl.when(jnp.logical_and(pl.program_id(0) == 0, j == 0))
    def _():
        for g in range(len(POOL_WINDOWS)):
            rows_g = slice(g * group, (g + 1) * group)
            scaled = (poolw_ref[g].astype(f32) * pscale_ref[:, rows_g]).astype(bf16)
            wfold_ref[rows_g, :] = _dot(scaled, wpu_ref[rows_g, :]).astype(bf16)

    @pl.when(j == 0)
    def _():
        prev_ref[...] = jnp.zeros_like(prev_ref)

    extra = lax.broadcasted_iota(jnp.int32, (VT_ROWS - LANES, tk), 0)
    ones_rows = jnp.where(extra == 0, 1.0, 0.0).astype(bf16)
    halo = prev_ref[...]
    for sub in range(subs):
        rows = slice(sub * tm, (sub + 1) * tm)
        x = x_ref[rows, :]
        ms = jnp.mean(x * x, axis=-1, keepdims=True)
        h = (x * lax.rsqrt(ms + EPS) * g1_ref[...]).astype(bf16)

        def proj(lo, width, h=h):
            return _dot(h, win_ref[:, lo:lo + width])

        u = proj(0, pool_width)
        v = proj(off_v, attn_width)
        for hh in range(attn_width // LANES):
            for c in range(tm // tk):
                blk = sub * (tm // tk) + c
                vt_ref[hh, blk, 0:LANES, :] = v[c * tk:(c + 1) * tk, hh * LANES:(hh + 1) * LANES].T.astype(bf16)
                vt_ref[hh, blk, LANES:VT_ROWS, :] = ones_rows
        qn_ref[rows, :] = (_half_lane_rmsnorm(proj(off_q, qk_width), n_tiles, tm) * qg_ref[...]).astype(bf16)
        kn_ref[rows, :] = (_half_lane_rmsnorm(proj(off_k, qk_width), n_tiles, tm) * kg_ref[...]).astype(bf16)

        ext = jnp.concatenate([halo, u], axis=0)
        halo = u[tm - POOL_HALO:, :]
        pos = (j * subs + sub) * tm + lax.broadcasted_iota(jnp.int32, (tm, 1), 0)
        ds = []
        for g, w in enumerate(POOL_WINDOWS):
            acc = ext[:, g * group:(g + 1) * group]
            shift = 1
            while shift < w:
                acc = acc + pltpu.roll(acc, shift, 0)
                shift *= 2
            wsum = acc[POOL_HALO:, :]
            cnt = jnp.minimum(pos + 1, w).astype(f32)
            ds.append((wsum / cnt - u[:, g * group:(g + 1) * group]).astype(bf16))
        pool_out = _dot(jnp.concatenate(ds, axis=-1), wfold_ref[...])

        for c in range(2):
            cols = slice(c * half, (c + 1) * half)
            gp = jax.nn.sigmoid(proj(off_gp + c * half, half))
            pg_ref[rows, cols] = (gp * pool_out[:, cols]).astype(bf16)
            ga_ref[rows, cols] = jax.nn.sigmoid(proj(off_ga + c * half, half)).astype(bf16)
    prev_ref[...] = halo


def _in_proj(x, g1, w_in, pool_w, pool_scale, w_pool_up, qg, kg):
    B, S, D = x.shape
    subs = IN_PROJ_SUBTILES if S % (IN_PROJ_SUBTILES * ROW_TILE) == 0 else 1
    tm = min(ROW_TILE, S)
    pool_width = w_pool_up.shape[0]
    qk_width = qg.shape[1]
    attn_width = qk_width
    in_width = w_in.shape[1]
    const2 = lambda b, j: (0, 0)
    row = lambda b, j: (b, j, 0)
    tk = min(ATTN_TILE, S)
    n_heads = attn_width // LANES
    kern = functools.partial(_in_proj_kernel, tm=tm, subs=subs, tk=tk, d_model=D, pool_width=pool_width,
                             qk_width=qk_width, attn_width=attn_width)
    tb = subs * tm
    return pl.pallas_call(
        kern,
        grid=(B, S // tb),
        in_specs=[
            pl.BlockSpec((None, tb, D), row),
            pl.BlockSpec((1, D), const2),
            pl.BlockSpec((D, in_width), const2, pipeline_mode=pl.Buffered(1)),
            pl.BlockSpec(pool_w.shape, lambda b, j: (0, 0, 0)),
            pl.BlockSpec((1, pool_width), const2),
            pl.BlockSpec((pool_width, D), const2),
            pl.BlockSpec((1, qk_width), const2),
            pl.BlockSpec((1, qk_width), const2),
        ],
        out_specs=[
            pl.BlockSpec((None, tb, D), row),
            pl.BlockSpec((None, tb, D), row),
            pl.BlockSpec((None, tb, qk_width), row),
            pl.BlockSpec((None, tb, qk_width), row),
            pl.BlockSpec((None, n_heads, tb // tk, VT_ROWS, tk), lambda b, j: (b, 0, j, 0, 0)),
        ],
        out_shape=[
            jax.ShapeDtypeStruct((B, S, D), bf16),
            jax.ShapeDtypeStruct((B, S, D), bf16),
            jax.ShapeDtypeStruct((B, S, qk_width), bf16),
            jax.ShapeDtypeStruct((B, S, qk_width), bf16),
            jax.ShapeDtypeStruct((B, n_heads, S // tk, VT_ROWS, tk), bf16),
        ],
        scratch_shapes=[pltpu.VMEM((POOL_HALO, pool_width), f32), pltpu.VMEM((pool_width, D), bf16)],
        compiler_params=pltpu.CompilerParams(dimension_semantics=("arbitrary", "arbitrary"),
                                             vmem_limit_bytes=VMEM_LIMIT),
        name="in_proj",
    )(x, g1, w_in, pool_w, pool_scale, w_pool_up, qg, kg)


def _attn_kernel(slopes_ref, q_ref, k_ref, vt_ref, fq_ref, fk_ref, lq1_ref, lk1_ref, lq2_ref, lk2_ref, sg_ref,
                 o_ref, qa_ref, sa_ref, sb_ref, pa_ref, pb_ref, m_ref, alpha_ref, acc_ref, *,
                 tq, tk, heads, lam_init):
    hg = pl.program_id(1)
    i = pl.program_id(2)
    fk = fk_ref[...]
    lane = lax.broadcasted_iota(jnp.int32, (tk, LANES), 1)
    first = lane < (LANES // 2)
    hs = range(heads)

    buf_a = (sa_ref, pa_ref)
    buf_b = (sb_ref, pb_ref)

    def scores(g, n, buf):
        kb = k_ref[pl.ds(pl.multiple_of(n * tk, tk), tk), g * LANES:(g + 1) * LANES]
        buf[0][g] = _dot_nt(jnp.concatenate([kb, fk], axis=1), qa_ref[g])

    def softmax(g, n, buf, first_half_mask=None):
        s_ref, p_ref = buf
        c = -slopes_ref[hg * heads + g] * (i * tq - n * tk).astype(f32)
        s = s_ref[g]
        if first_half_mask is not None:
            s = jnp.concatenate([jnp.where(first_half_mask, s[:, :tq], -jnp.inf), s[:, tq:]], axis=1)
        m_old = m_ref[g]
        m_new = jnp.maximum(m_old, jnp.max(s, axis=0, keepdims=True) + c)
        p_ref[g] = jnp.exp2(s - (m_new - c)).astype(bf16)
        m_ref[g] = m_new
        return jnp.exp2(m_old - m_new)

    def pv(g, n, buf, alpha):
        acc_ref[g] = alpha * acc_ref[g] + _dot(vt_ref[g, jnp.maximum(n, 0)], buf[1][g])

    m_ref[...] = jnp.full(m_ref.shape, -jnp.inf, f32)
    alpha_ref[...] = jnp.ones_like(alpha_ref)
    acc_ref[...] = jnp.zeros_like(acc_ref)
    pb_ref[...] = jnp.zeros_like(pb_ref)
    zero = jnp.zeros((tk, LANES), bf16)
    for g in hs:
        parts = []
        for half in range(2):
            q = q_ref[half * tk:(half + 1) * tk, g * LANES:(g + 1) * LANES]
            parts += [jnp.where(first, q, zero), jnp.where(first, zero, q)]
        qa_ref[g] = jnp.concatenate([jnp.concatenate(parts, axis=0), fq_ref[g]], axis=1)
    for g in hs:
        scores(g, 0, buf_a)

    def pair(t, carry):
        n = 2 * t
        a_prev = [alpha_ref[g] for g in hs]
        a_even = [None] * heads
        for g in hs:
            pv(g, n - 1, buf_b, a_prev[g])
            a_even[g] = softmax(g, n, buf_a)
            scores(g, n + 1, buf_b)
        for g in hs:
            pv(g, n, buf_a, a_even[g])
            alpha_ref[g] = softmax(g, n + 1, buf_b)
            scores(g, n + 2, buf_a)
        return carry

    lax.fori_loop(0, i, pair, 0)
    n = 2 * i
    lam = (jnp.exp(jnp.sum(lq1_ref[...] * lk1_ref[...], keepdims=True))
           - jnp.exp(jnp.sum(lq2_ref[...] * lk2_ref[...], keepdims=True)) + lam_init)
    late = slice(tq, 2 * tq)
    kk = lax.broadcasted_iota(jnp.int32, (tk, tq), 0)
    qq = lax.broadcasted_iota(jnp.int32, (tk, tq), 1)
    tri = kk <= jnp.where(qq >= tk, qq - tk, qq)
    for g in hs:
        a_even = softmax(g, n, buf_a, first_half_mask=tri)
        pv(g, n - 1, buf_b, alpha_ref[g])
        pv(g, n, buf_a, a_even)
        kb = k_ref[pl.ds(pl.multiple_of((n + 1) * tk, tk), tk), g * LANES:(g + 1) * LANES]
        s = _dot_nt(jnp.concatenate([kb, fk], axis=1), qa_ref[g, late, :])
        s = jnp.where(tri, s, -jnp.inf)
        c = slopes_ref[hg * heads + g] * float(tk)
        m_old = m_ref[g, :, late]
        m_new = jnp.maximum(m_old, jnp.max(s, axis=0, keepdims=True) + c)
        p = jnp.exp2(s - (m_new - c)).astype(bf16)
        acc = acc_ref[g]
        acc_late = jnp.exp2(m_old - m_new) * acc[:, late] + _dot(vt_ref[g, n + 1], p)
        acc = jnp.concatenate([acc[:, :tq], acc_late], axis=1)
        o_all = acc[0:LANES] / acc[LANES:LANES + 1]
        o = jnp.concatenate([o_all[:, 0:tk] - lam * o_all[:, tk:tq],
                             o_all[:, tq:tq + tk] - lam * o_all[:, tq + tk:]], axis=1)
        ms = jnp.mean(o * o, axis=0, keepdims=True)
        on = o * lax.rsqrt(ms + EPS) * sg_ref[...] * (1.0 - lam_init)
        o_ref[:, g * LANES:(g + 1) * LANES] = on.T.astype(bf16)


def _split_bf16(x, pieces=3):
    out = []
    for _ in range(pieces):
        p = x.astype(bf16)
        out.append(p)
        x = x - p.astype(f32)
    return out


def _alibi_features(slopes2, tq, tk):
    assert tk <= 256
    assert tq == 2 * tk
    n_heads = slopes2.shape[0]
    ones = jnp.ones((tk, 1), bf16)
    krel = jnp.arange(tk, dtype=f32).astype(bf16)[:, None]
    fk = jnp.concatenate([krel] * 3 + [ones] * 3 + [jnp.zeros((tk, LANES - 6), bf16)], axis=1)
    first, last = jnp.arange(tk, dtype=f32), jnp.arange(tk, tq, dtype=f32)
    qrel = jnp.concatenate([first, first, last, last])
    a = _split_bf16(slopes2)
    b = _split_bf16(-slopes2[:, None] * qrel[None, :])
    cols = [jnp.broadcast_to(p[:, None, None], (n_heads, 2 * tq, 1)) for p in a] + [p[:, :, None] for p in b]
    fq = jnp.concatenate(cols + [jnp.zeros((n_heads, 2 * tq, LANES - 6), bf16)], axis=2)
    return fq, fk


def _attention(qn, kn, vt, fq, fk, slopes, lq1, lk1, lq2, lk2, subln_col, lam_init):
    B, S, _ = qn.shape
    tk = fk.shape[0]
    tq = fq.shape[1] // 2
    assert tq == 2 * tk
    nkv = S // tk
    const2 = lambda b, h, i, *_: (0, 0)
    G = ATTN_HEADS_PER_STEP
    kern = functools.partial(_attn_kernel, tq=tq, tk=tk, heads=G, lam_init=lam_init)
    hd = lq1.shape[1]
    return pl.pallas_call(
        kern,
        grid_spec=pltpu.PrefetchScalarGridSpec(
            num_scalar_prefetch=1,
            grid=(B, N_HEADS // G, S // tq),
            in_specs=[
                pl.BlockSpec((None, tq, G * LANES), lambda b, h, i, *_: (b, i, h)),
                pl.BlockSpec((None, S, G * LANES), lambda b, h, i, *_: (b, 0, h)),
                pl.BlockSpec((None, G, nkv, VT_ROWS, tk), lambda b, h, i, *_: (b, h, 0, 0, 0)),
                pl.BlockSpec((G, 2 * tq, LANES), lambda b, h, i, *_: (h, 0, 0)),
                pl.BlockSpec((tk, LANES), const2),
                pl.BlockSpec((1, hd), const2),
                pl.BlockSpec((1, hd), const2),
                pl.BlockSpec((1, hd), const2),
                pl.BlockSpec((1, hd), const2),
                pl.BlockSpec((LANES, 1), const2),
            ],
            out_specs=pl.BlockSpec((None, tq, G * LANES), lambda b, h, i, *_: (b, i, h)),
            scratch_shapes=[pltpu.VMEM((G, 2 * tq, 2 * LANES), bf16),
                            pltpu.VMEM((G, tk, 2 * tq), f32), pltpu.VMEM((G, tk, 2 * tq), f32),
                            pltpu.VMEM((G, tk, 2 * tq), bf16), pltpu.VMEM((G, tk, 2 * tq), bf16),
                            pltpu.VMEM((G, 1, 2 * tq), f32), pltpu.VMEM((G, 1, 2 * tq), f32),
                            pltpu.VMEM((G, VT_ROWS, 2 * tq), f32)],
        ),
        out_shape=jax.ShapeDtypeStruct((B, S, N_HEADS * LANES), bf16),
        compiler_params=pltpu.CompilerParams(dimension_semantics=("arbitrary", "arbitrary", "arbitrary"),
                                             vmem_limit_bytes=VMEM_LIMIT),
        name="diff_attn",
    )(slopes, qn, kn, vt, fq, fk, lq1, lk1, lq2, lk2, subln_col)


def _merge_kernel(pg_ref, ga_ref, o_ref, x_ref, wau_ref, wout_ref, g2_ref, wr_ref, br_ref,
                  x1_ref, hn_ref, eid_ref, wt_ref, rank_ref, cnt_ref, base_ref, *, tm, subs, d_model):
    step = pl.program_id(0)

    @pl.when(step == 0)
    def _():
        base_ref[...] = jnp.zeros_like(base_ref)

    a = lax.broadcasted_iota(jnp.int32, (tm, tm), 0)
    b = lax.broadcasted_iota(jnp.int32, (tm, tm), 1)
    upper = jnp.where(a <= b, 1.0, 0.0).astype(bf16)
    base = base_ref[...]
    all_logits = []
    for sub in range(subs):
        rows = slice(sub * tm, (sub + 1) * tm)
        attn_out = _dot(o_ref[rows, :], wau_ref[...])
        merged = pg_ref[rows, :].astype(f32) + ga_ref[rows, :].astype(f32) * attn_out
        x1 = x_ref[rows, :] + _dot(merged.astype(bf16), wout_ref[...])
        x1_ref[rows, :] = x1
        ms = jnp.mean(x1 * x1, axis=-1, keepdims=True)
        hn = (x1 * lax.rsqrt(ms + EPS) * g2_ref[...]).astype(bf16)
        hn_ref[rows, :] = hn

        all_logits.append(_dot_nt(wr_ref[...], hn) + br_ref[...])

    for sub in range(subs):
        rows = slice(sub * tm, (sub + 1) * tm)
        logits = all_logits[sub]
        lg = logits[0:N_GROUPS]
        gmax = jnp.max(lg, axis=0, keepdims=True)
        p_top = 1.0 / jnp.sum(jnp.exp(lg - gmax), axis=0, keepdims=True)
        grow = lax.broadcasted_iota(jnp.int32, lg.shape, 0).astype(f32)
        g_idx = jnp.min(jnp.where(lg == gmax, grow, float(N_GROUPS)), axis=0, keepdims=True)

        sel = jnp.zeros((EXPERTS_PER_GROUP, tm), f32)
        for g in range(N_GROUPS):
            le_g = logits[SUBLANES + g * EXPERTS_PER_GROUP:SUBLANES + (g + 1) * EXPERTS_PER_GROUP]
            sel = jnp.where(g_idx == float(g), le_g, sel)
        erow = lax.broadcasted_iota(jnp.int32, sel.shape, 0).astype(f32)
        e1 = jnp.max(sel, axis=0, keepdims=True)
        i1 = jnp.min(jnp.where(sel == e1, erow, float(EXPERTS_PER_GROUP)), axis=0, keepdims=True)
        sel2 = jnp.where(erow == i1, -jnp.inf, sel)
        e2 = jnp.max(sel2, axis=0, keepdims=True)
        i2 = jnp.min(jnp.where(sel2 == e2, erow, float(EXPERTS_PER_GROUP)), axis=0, keepdims=True)
        r = jnp.exp(e2 - e1)
        w1 = p_top / (1.0 + r)
        w2 = p_top * r / (1.0 + r)
        eid1 = g_idx * float(EXPERTS_PER_GROUP) + i1
        eid2 = g_idx * float(EXPERTS_PER_GROUP) + i2
        eid_ref[:, rows] = jnp.concatenate([eid1, eid2], axis=0).astype(jnp.int32)
        wt_ref[:, rows] = jnp.concatenate([w1, w2], axis=0)

        xrow = lax.broadcasted_iota(jnp.int32, (N_EXPERTS, tm), 0).astype(f32)
        oh1 = jnp.where(xrow == eid1, 1.0, 0.0)
        oh2 = jnp.where(xrow == eid2, 1.0, 0.0)
        oh = oh1 + oh2
        before = _dot(oh.astype(bf16), upper) + base - 1.0
        rank1 = jnp.sum(oh1 * before, axis=0, keepdims=True)
        rank2 = jnp.sum(oh2 * before, axis=0, keepdims=True)
        rank_ref[:, rows] = jnp.concatenate([rank1, rank2], axis=0).astype(jnp.int32)
        base = base + jnp.sum(oh, axis=1, keepdims=True)
    base_ref[...] = base
    cnt_ref[...] = jnp.broadcast_to(base, cnt_ref.shape)


def _merge(pg, ga, o, x, w_attn_up, w_out, g2, wr, br):
    N, D = x.shape
    tm = min(ROW_TILE, N)
    subs = MERGE_SUBTILES if N % (MERGE_SUBTILES * tm) == 0 else 1
    tb = subs * tm
    aw = o.shape[1]
    const2 = lambda i: (0, 0)
    row = lambda i: (i, 0)
    colblk = lambda i: (0, i)
    kern = functools.partial(_merge_kernel, tm=tm, subs=subs, d_model=D)
    return pl.pallas_call(
        kern,
        grid=(N // tb,),
        in_specs=[
            pl.BlockSpec((tb, D), row),
            pl.BlockSpec((tb, D), row),
            pl.BlockSpec((tb, aw), row),
            pl.BlockSpec((tb, D), row),
            pl.BlockSpec((aw, D), const2),
            pl.BlockSpec((D, D), const2),
            pl.BlockSpec((1, D), const2),
            pl.BlockSpec((ROUTER_ROWS, D), const2),
            pl.BlockSpec((ROUTER_ROWS, 1), const2),
        ],
        out_specs=[
            pl.BlockSpec((tb, D), row),
            pl.BlockSpec((tb, D), row),
            pl.BlockSpec((TOP_K, tb), colblk),
            pl.BlockSpec((TOP_K, tb), colblk),
            pl.BlockSpec((TOP_K, tb), colblk),
            pl.BlockSpec((N_EXPERTS, LANES), const2),
        ],
        out_shape=[
            jax.ShapeDtypeStruct((N, D), f32),
            jax.ShapeDtypeStruct((N, D), bf16),
            jax.ShapeDtypeStruct((TOP_K, N), jnp.int32),
            jax.ShapeDtypeStruct((TOP_K, N), f32),
            jax.ShapeDtypeStruct((TOP_K, N), jnp.int32),
            jax.ShapeDtypeStruct((N_EXPERTS, LANES), f32),
        ],
        scratch_shapes=[pltpu.VMEM((N_EXPERTS, 1), f32)],
        compiler_params=pltpu.CompilerParams(dimension_semantics=("arbitrary",), vmem_limit_bytes=VMEM_LIMIT),
        name="merge_router",
    )(pg, ga, o, x, w_attn_up, w_out, g2, wr, br)


def _slots_kernel(pstart_ref, eid_ref, rank_ref, dest_ref):
    eid = eid_ref[...]
    start = jnp.zeros_like(eid)
    for e in range(N_EXPERTS):
        start = jnp.where(eid == e, pstart_ref[e], start)
    dest_ref[...] = start + rank_ref[...]


def _slots(pstart, eid, rank):
    k, n = eid.shape
    tn = min(n, 8192)
    blk = pl.BlockSpec((k, tn), lambda i, *_: (0, i))
    return pl.pallas_call(
        _slots_kernel,
        grid_spec=pltpu.PrefetchScalarGridSpec(num_scalar_prefetch=1, grid=(n // tn,), in_specs=[blk, blk],
                                               out_specs=blk),
        out_shape=jax.ShapeDtypeStruct((k, n), jnp.int32),
        name="slots",
    )(pstart, eid, rank)


def _row(ref, r):
    start = r * SUBLANES if isinstance(r, int) else pl.multiple_of(r * SUBLANES, SUBLANES)
    return ref.at[pl.ds(start, SUBLANES), :]


def _wait_bytes_of(ref_like, any_hbm, sem):
    n = ref_like.shape[0]
    pltpu.make_async_copy(any_hbm.at[pl.ds(0, n), :], any_hbm.at[pl.ds(0, n), :], sem).wait()


STAGES = 3


def _dispatch_kernel(pstart_ref, pend_ref, count_ref, dest_ref, hn_ref, xs_hbm, zero_ref, stage, out_sem, zsem, *,
                     tt, rows, n_blocks, d_model):
    i = pl.program_id(0)
    n = pl.num_programs(0)

    def fill_unwritten(start):
        def piece(n_rows, first_row):
            cp = pltpu.make_async_copy(zero_ref.at[pl.ds(0, n_rows * SUBLANES), :],
                                       xs_hbm.at[pl.ds(pl.multiple_of(first_row * SUBLANES, SUBLANES),
                                                       n_rows * SUBLANES), :], zsem)
            if start:
                cp.start()
            else:
                cp.wait()

        for e in range(N_EXPERTS):
            row = pstart_ref[e] + count_ref[e]
            n_pad = pend_ref[e] - row
            bit = rows // 2
            while bit >= 1:
                pl.when((n_pad & bit) != 0)(functools.partial(piece, bit, row))
                row = row + (n_pad & bit)
                bit //= 2

        def block(b, carry):
            piece(rows, b * rows)
            return carry
        lax.fori_loop(pend_ref[N_EXPERTS - 1] // rows, n_blocks, block, 0)

    @pl.when(i == 0)
    def _():
        zero_ref[...] = jnp.zeros_like(zero_ref)
        fill_unwritten(start=True)

    def wait_rows_of(step):
        for _ in range(TOP_K):
            _wait_bytes_of(stage.at[0], xs_hbm, out_sem.at[step % STAGES])

    @pl.when(i >= STAGES - 1)
    def _():
        wait_rows_of(i - (STAGES - 1))

    slot = i % STAGES
    for s in range(d_model // LANES):
        stage[slot, pl.ds(s, tt, stride=SUBLANES), :] = hn_ref[:, s * LANES:(s + 1) * LANES].astype(f32)

    def body(t, carry):
        for k in range(TOP_K):
            pltpu.make_async_copy(_row(stage.at[slot], t), _row(xs_hbm, dest_ref[0, k * tt + t]),
                                  out_sem.at[slot]).start(priority=k)
        return carry
    lax.fori_loop(0, tt, body, 0, unroll=8)

    @pl.when(i == n - 1)
    def _():
        for back in range(STAGES - 2, -1, -1):
            @pl.when(i - back >= 0)
            def _():
                wait_rows_of(i - back)
        fill_unwritten(start=False)


def _dispatch(pstart, pend, counts, dest_blocks, hn, n_slots):
    n, _, width = dest_blocks.shape
    tt = width // TOP_K
    rows = EXPERT_ROWS
    D = hn.shape[1]
    kern = functools.partial(_dispatch_kernel, tt=tt, rows=rows, n_blocks=n_slots // rows, d_model=D)
    smem_blk = pl.BlockSpec((None, 1, width), lambda i, *_: (i, 0, 0), memory_space=pltpu.SMEM)
    return pl.pallas_call(
        kern,
        grid_spec=pltpu.PrefetchScalarGridSpec(
            num_scalar_prefetch=3,
            grid=(n,),
            in_specs=[smem_blk, pl.BlockSpec((tt, D), lambda i, *_: (i, 0))],
            out_specs=pl.BlockSpec(memory_space=pl.ANY),
            scratch_shapes=[pltpu.VMEM((rows * SUBLANES, LANES), f32),
                            pltpu.VMEM((STAGES, tt * SUBLANES, LANES), f32),
                            pltpu.SemaphoreType.DMA((STAGES,)),
                            pltpu.SemaphoreType.DMA(())],
        ),
        out_shape=jax.ShapeDtypeStruct((n_slots * SUBLANES, LANES), f32),
        compiler_params=pltpu.CompilerParams(dimension_semantics=("arbitrary",), vmem_limit_bytes=VMEM_LIMIT),
        name="dispatch",
    )(pstart, pend, counts, dest_blocks, hn)


def _rows_from_token_major(buf, n_rows, n_tiles):
    return jnp.concatenate([buf[pl.ds(s, n_rows, stride=SUBLANES), :] for s in range(n_tiles)], axis=-1)


def _expert_kernel(beid_ref, nbu_ref, xs_ref, wg_ref, wu_ref, wd_ref, ys_ref, wg_bf, wu_bf, wd_bf, *, rows,
                   d_model):
    b = pl.program_id(0)
    changed = jnp.logical_or(b == 0, beid_ref[b] != beid_ref[jnp.maximum(b - 1, 0)])

    @pl.when(changed)
    def _():
        wg_bf[...] = wg_ref[...].astype(bf16)
        wu_bf[...] = wu_ref[...].astype(bf16)
        wd_bf[...] = wd_ref[...].astype(bf16)

    n_tiles = d_model // LANES

    @pl.when(b < nbu_ref[0])
    def _():
        xb = _rows_from_token_major(xs_ref, rows, n_tiles).astype(bf16)
        hdn = jax.nn.silu(_dot(xb, wg_bf[...])) * _dot(xb, wu_bf[...])
        y = _dot(hdn.astype(bf16), wd_bf[...])
        for s in range(n_tiles):
            ys_ref[pl.ds(s, rows, stride=SUBLANES), :] = y[:, s * LANES:(s + 1) * LANES]

    @pl.when(b >= nbu_ref[0])
    def _():
        ys_ref[...] = jnp.zeros_like(ys_ref)


def _experts(block_eid, n_used, xs, w_gate, w_up, w_down):
    n_blocks = block_eid.shape[0]
    rows = EXPERT_ROWS
    E, D, DE = w_gate.shape
    kern = functools.partial(_expert_kernel, rows=rows, d_model=D)
    xs_idx = lambda b, eid, nbu: (jnp.minimum(b, nbu[0] - 1), 0)
    return pl.pallas_call(
        kern,
        grid_spec=pltpu.PrefetchScalarGridSpec(
            num_scalar_prefetch=2,
            grid=(n_blocks,),
            in_specs=[
                pl.BlockSpec((rows * SUBLANES, LANES), xs_idx),
                pl.BlockSpec((None, D, DE), lambda b, eid, nbu: (eid[b], 0, 0)),
                pl.BlockSpec((None, D, DE), lambda b, eid, nbu: (eid[b], 0, 0)),
                pl.BlockSpec((None, DE, D), lambda b, eid, nbu: (eid[b], 0, 0)),
            ],
            out_specs=pl.BlockSpec((rows * SUBLANES, LANES), lambda b, *_: (b, 0)),
            scratch_shapes=[
                pltpu.VMEM((D, DE), bf16),
                pltpu.VMEM((D, DE), bf16),
                pltpu.VMEM((DE, D), bf16),
            ],
        ),
        out_shape=jax.ShapeDtypeStruct((n_blocks * rows * SUBLANES, LANES), f32),
        compiler_params=pltpu.CompilerParams(dimension_semantics=("arbitrary",), vmem_limit_bytes=VMEM_LIMIT),
        name="experts",
    )(block_eid, n_used, xs, w_gate, w_up, w_down)


def _combine_kernel(dest_ref, destn_ref, x1_ref, wt_ref, ys_hbm, out_ref, buf, sem, *, te, d_model):
    i = pl.program_id(0)
    n = pl.num_programs(0)
    slot = i % 2

    def start_gather(d_ref, dst, dsem):
        for t in range(te):
            for k in range(TOP_K):
                r = k * te + t
                pltpu.make_async_copy(_row(ys_hbm, d_ref[0, r]), _row(dst, r), dsem).start(priority=k)

    @pl.when(i == 0)
    def _():
        start_gather(dest_ref, buf.at[0], sem.at[0])

    @pl.when(i + 1 < n)
    def _():
        start_gather(destn_ref, buf.at[1 - slot], sem.at[1 - slot])

    _wait_bytes_of(buf.at[slot], ys_hbm, sem.at[slot])
    n_tiles = d_model // LANES
    both = _rows_from_token_major(buf.at[slot], TOP_K * te, n_tiles)
    wt = jnp.concatenate([wt_ref[...], jnp.zeros((SUBLANES - TOP_K, te), f32)], axis=0).T
    out_ref[...] = x1_ref[...] + (wt[:, 0:1] * both[:te] + wt[:, 1:2] * both[te:])


def _combine(dest_blocks, x1, wt_rows, ys):
    N, D = x1.shape
    n, _, width = dest_blocks.shape
    te = width // TOP_K
    kern = functools.partial(_combine_kernel, te=te, d_model=D)
    cur = pl.BlockSpec((None, 1, width), lambda i: (i, 0, 0), memory_space=pltpu.SMEM)
    nxt = pl.BlockSpec((None, 1, width), lambda i: (jnp.minimum(i + 1, n - 1), 0, 0), memory_space=pltpu.SMEM)
    return pl.pallas_call(
        kern,
        grid=(n,),
        in_specs=[
            cur, nxt,
            pl.BlockSpec((te, D), lambda i: (i, 0)),
            pl.BlockSpec((TOP_K, te), lambda i: (0, i)),
            pl.BlockSpec(memory_space=pl.ANY),
        ],
        out_specs=pl.BlockSpec((te, D), lambda i: (i, 0)),
        scratch_shapes=[pltpu.VMEM((2, TOP_K * te * SUBLANES, LANES), f32), pltpu.SemaphoreType.DMA((2,))],
        out_shape=jax.ShapeDtypeStruct((N, D), f32),
        compiler_params=pltpu.CompilerParams(dimension_semantics=("arbitrary",), vmem_limit_bytes=VMEM_LIMIT),
        name="combine",
    )(dest_blocks, dest_blocks, x1, wt_rows, ys)


def _token_blocks(a, tt):
    k, n = a.shape
    return a.reshape(k, n // tt, tt).transpose(1, 0, 2).reshape(n // tt, 1, k * tt)


def _layer(x, l, norm1_g, w_in, pool_w, pool_scale, w_pool_up, q_norm_g, k_norm_g, lambda_q1, lambda_k1,
           lambda_q2, lambda_k2, subln_g, w_attn_up, w_out, norm2_g, w_router_group, b_router_group,
           w_router_expert, b_router_expert, w_expert_gate, w_expert_up, w_expert_down):
    B, S, D = x.shape
    N = B * S
    head_dim = q_norm_g.shape[0]
    lam_init = 0.8 - 0.6 * math.exp(-0.3 * l)
    reps = (N_HEADS * 2 * head_dim) // head_dim

    qg = (jnp.tile(q_norm_g, reps) * (head_dim ** -0.5 * LOG2E))[None, :]
    kg = jnp.tile(k_norm_g, reps)[None, :]
    pg, ga, qn, kn, vt = _in_proj(x, norm1_g[None, :], w_in.astype(bf16), pool_w.astype(bf16),
                                  pool_scale[None, :], w_pool_up.astype(bf16), qg, kg)

    tk = min(ATTN_TILE, S)
    slopes2 = jnp.asarray([2.0 ** (-8.0 * (h + 1) / N_HEADS) * LOG2E for h in range(N_HEADS)], f32)
    fq, fk = _alibi_features(slopes2, 2 * tk, tk)
    o = _attention(qn, kn, vt, fq, fk, slopes2, lambda_q1[None, :], lambda_k1[None, :], lambda_q2[None, :],
                   lambda_k2[None, :], subln_g[:, None], lam_init)

    wr = jnp.zeros((ROUTER_ROWS, D), f32)
    wr = wr.at[:N_GROUPS].set(w_router_group.T).at[SUBLANES:].set(w_router_expert.T).astype(bf16)
    br = jnp.zeros((ROUTER_ROWS, 1), f32)
    br = br.at[:N_GROUPS, 0].set(b_router_group).at[SUBLANES:, 0].set(b_router_expert)
    x1, hn, eid, wts, rank, cnt = _merge(pg.reshape(N, D), ga.reshape(N, D), o.reshape(N, -1),
                                           x.reshape(N, D), w_attn_up.astype(bf16), w_out.astype(bf16),
                                           norm2_g[None, :], wr, br)

    R = EXPERT_ROWS
    counts = cnt[:, 0].astype(jnp.int32)
    padded = (counts + R - 1) // R * R
    pend = jnp.cumsum(padded).astype(jnp.int32)
    pstart = pend - padded
    n_blocks = -(-(N * TOP_K) // R) + N_EXPERTS
    starts = jnp.arange(n_blocks, dtype=jnp.int32) * R
    block_eid = jnp.minimum(jnp.sum((pend[None, :] <= starts[:, None]).astype(jnp.int32), axis=1), N_EXPERTS - 1)
    n_used = pend[-1:] // R

    dest = _slots(pstart, eid, rank)
    xs = _dispatch(pstart, pend, counts, _token_blocks(dest, min(DISPATCH_TILE, N)), hn, n_blocks * R)
    ys = _experts(block_eid, n_used, xs, w_expert_gate, w_expert_up, w_expert_down)
    out = _combine(_token_blocks(dest, min(COMBINE_TILE, N)), x1, wts, ys)
    return out.reshape(B, S, D)


def kernel(x, norm1_g, w_in, pool_w, pool_scale, w_pool_up, q_norm_g, k_norm_g, lambda_q1, lambda_k1, lambda_q2,
           lambda_k2, subln_g, w_attn_up, w_out, norm2_g, w_router_group, b_router_group, w_router_expert,
           b_router_expert, w_expert_gate, w_expert_up, w_expert_down):
    params = (norm1_g, w_in, pool_w, pool_scale, w_pool_up, q_norm_g, k_norm_g, lambda_q1, lambda_k1, lambda_q2,
              lambda_k2, subln_g, w_attn_up, w_out, norm2_g, w_router_group, b_router_group, w_router_expert,
              b_router_expert, w_expert_gate, w_expert_up, w_expert_down)
    for l in range(norm1_g.shape[0]):
        x = _layer(x, l, *(p[l] for p in params))
    return x
```

```python
import functools
import math

import jax
import jax.numpy as jnp
from jax import lax
from jax.experimental import pallas as pl
from jax.experimental.pallas import tpu as pltpu

EPS = 1e-6
POOL_WINDOWS = (2, 4, 8, 16)
POOL_HALO = 16
N_HEADS = 4
N_GROUPS = 4
EXPERTS_PER_GROUP = 8
N_EXPERTS = N_GROUPS * EXPERTS_PER_GROUP
TOP_K = 2
LANES = 128
SUBLANES = 8
ROUTER_ROWS = 8 + N_EXPERTS
VT_ROWS = LANES + 16
LOG2E = 1.4426950408889634

ROW_TILE = 512
IN_PROJ_SUBTILES = 2
MERGE_SUBTILES = 2
ATTN_TILE = 256
ATTN_HEADS_PER_STEP = 4
EXPERT_ROWS = 1024
DISPATCH_TILE = 512
COMBINE_TILE = 256
V7X_VMEM_BYTES = 64 * 1024 * 1024
VMEM_LIMIT = V7X_VMEM_BYTES * 13 // 16

f32 = jnp.float32
bf16 = jnp.bfloat16


def _dot(a, b):
    return jnp.dot(a, b, preferred_element_type=f32)


def _dot_nt(a, b):
    return lax.dot_general(a, b, (((1,), (1,)), ((), ())), preferred_element_type=f32)


def _half_lane_rmsnorm(t, n_tiles, rows):
    lane = lax.broadcasted_iota(jnp.int32, (rows, LANES), 1)
    lo_mask = lane < (LANES // 2)
    outs = []
    for i in range(n_tiles):
        c = t[:, i * LANES:(i + 1) * LANES]
        sq = c * c
        lo = jnp.sum(jnp.where(lo_mask, sq, 0.0), axis=-1, keepdims=True)
        hi = jnp.sum(jnp.where(lo_mask, 0.0, sq), axis=-1, keepdims=True)
        ms = jnp.where(lo_mask, lo, hi) * (2.0 / LANES)
        outs.append(c * lax.rsqrt(ms + EPS))
    return jnp.concatenate(outs, axis=-1)


def _in_proj_kernel(x_ref, g1_ref, win_ref, poolw_ref, pscale_ref, wpu_ref, qg_ref, kg_ref,
                    pg_ref, ga_ref, qn_ref, kn_ref, vt_ref, prev_ref, wfold_ref, *, tm, subs, tk, d_model, pool_width,
                    qk_width, attn_width):
    j = pl.program_id(1)
    off_q = pool_width
    off_k = off_q + qk_width
    off_v = off_k + qk_width
    off_gp = off_v + attn_width
    off_ga = off_gp + d_model
    group = pool_width // len(POOL_WINDOWS)
    half = d_model // 2
    n_tiles = qk_width // LANES

    @pl.when(jnp.logical_and(pl.program_id(0) == 0, j == 0))
    def _():
        for g in range(len(POOL_WINDOWS)):
            rows_g = slice(g * group, (g + 1) * group)
            scaled = (poolw_ref[g].astype(f32) * pscale_ref[:, rows_g]).astype(bf16)
            wfold_ref[rows_g, :] = _dot(scaled, wpu_ref[rows_g, :]).astype(bf16)

    @pl.when(j == 0)
    def _():
        prev_ref[...] = jnp.zeros_like(prev_ref)

    extra = lax.broadcasted_iota(jnp.int32, (VT_ROWS - LANES, tk), 0)
    ones_rows = jnp.where(extra == 0, 1.0, 0.0).astype(bf16)
    halo = prev_ref[...]
    for sub in range(subs):
        rows = slice(sub * tm, (sub + 1) * tm)
        x = x_ref[rows, :]
        ms = jnp.mean(x * x, axis=-1, keepdims=True)
        h = (x * lax.rsqrt(ms + EPS) * g1_ref[...]).astype(bf16)

        def proj(lo, width, h=h):
            return _dot(h, win_ref[:, lo:lo + width])

        u = proj(0, pool_width)
        v = proj(off_v, attn_width)
        for hh in range(attn_width // LANES):
            for c in range(tm // tk):
                blk = sub * (tm // tk) + c
                vt_ref[hh, blk, 0:LANES, :] = v[c * tk:(c + 1) * tk, hh * LANES:(hh + 1) * LANES].T.astype(bf16)
                vt_ref[hh, blk, LANES:VT_ROWS, :] = ones_rows
        qn_ref[rows, :] = (_half_lane_rmsnorm(proj(off_q, qk_width), n_tiles, tm) * qg_ref[...]).astype(bf16)
        kn_ref[rows, :] = (_half_lane_rmsnorm(proj(off_k, qk_width), n_tiles, tm) * kg_ref[...]).astype(bf16)

        ext = jnp.concatenate([halo, u], axis=0)
        halo = u[tm - POOL_HALO:, :]
        pos = (j * subs + sub) * tm + lax.broadcasted_iota(jnp.int32, (tm, 1), 0)
        ds = []
        for g, w in enumerate(POOL_WINDOWS):
            acc = ext[:, g * group:(g + 1) * group]
            shift = 1
            while shift < w:
                acc = acc + pltpu.roll(acc, shift, 0)
                shift *= 2
            wsum = acc[POOL_HALO:, :]
            cnt = jnp.minimum(pos + 1, w).astype(f32)
            ds.append((wsum / cnt - u[:, g * group:(g + 1) * group]).astype(bf16))
        pool_out = _dot(jnp.concatenate(ds, axis=-1), wfold_ref[...])

        for c in range(2):
            cols = slice(c * half, (c + 1) * half)
            gp = jax.nn.sigmoid(proj(off_gp + c * half, half))
            pg_ref[rows, cols] = (gp * pool_out[:, cols]).astype(bf16)
            ga_ref[rows, cols] = jax.nn.sigmoid(proj(off_ga + c * half, half)).astype(bf16)
    prev_ref[...] = halo


def _in_proj(x, g1, w_in, pool_w, pool_scale, w_pool_up, qg, kg):
    B, S, D = x.shape
    subs = IN_PROJ_SUBTILES if S % (IN_PROJ_SUBTILES * ROW_TILE) == 0 else 1
    tm = min(ROW_TILE, S)
    pool_width = w_pool_up.shape[0]
    qk_width = qg.shape[1]
    attn_width = qk_width
    in_width = w_in.shape[1]
    const2 = lambda b, j: (0, 0)
    row = lambda b, j: (b, j, 0)
    tk = min(ATTN_TILE, S)
    n_heads = attn_width // LANES
    kern = functools.partial(_in_proj_kernel, tm=tm, subs=subs, tk=tk, d_model=D, pool_width=pool_width,
                             qk_width=qk_width, attn_width=attn_width)
    tb = subs * tm
    return pl.pallas_call(
        kern,
        grid=(B, S // tb),
        in_specs=[
            pl.BlockSpec((None, tb, D), row),
            pl.BlockSpec((1, D), const2),
            pl.BlockSpec((D, in_width), const2, pipeline_mode=pl.Buffered(1)),
            pl.BlockSpec(pool_w.shape, lambda b, j: (0, 0, 0)),
            pl.BlockSpec((1, pool_width), const2),
            pl.BlockSpec((pool_width, D), const2),
            pl.BlockSpec((1, qk_width), const2),
            pl.BlockSpec((1, qk_width), const2),
        ],
        out_specs=[
            pl.BlockSpec((None, tb, D), row),
            pl.BlockSpec((None, tb, D), row),
            pl.BlockSpec((None, tb, qk_width), row),
            pl.BlockSpec((None, tb, qk_width), row),
            pl.BlockSpec((None, n_heads, tb // tk, VT_ROWS, tk), lambda b, j: (b, 0, j, 0, 0)),
        ],
        out_shape=[
            jax.ShapeDtypeStruct((B, S, D), bf16),
            jax.ShapeDtypeStruct((B, S, D), bf16),
            jax.ShapeDtypeStruct((B, S, qk_width), bf16),
            jax.ShapeDtypeStruct((B, S, qk_width), bf16),
            jax.ShapeDtypeStruct((B, n_heads, S // tk, VT_ROWS, tk), bf16),
        ],
        scratch_shapes=[pltpu.VMEM((POOL_HALO, pool_width), f32), pltpu.VMEM((pool_width, D), bf16)],
        compiler_params=pltpu.CompilerParams(dimension_semantics=("arbitrary", "arbitrary"),
                                             vmem_limit_bytes=VMEM_LIMIT),
        name="in_proj",
    )(x, g1, w_in, pool_w, pool_scale, w_pool_up, qg, kg)


def _attn_kernel(slopes_ref, q_ref, k_ref, vt_ref, fq_ref, fk_ref, lq1_ref, lk1_ref, lq2_ref, lk2_ref, sg_ref,
                 o_ref, qa_ref, sa_ref, sb_ref, pa_ref, pb_ref, m_ref, alpha_ref, acc_ref, *,
                 tq, tk, heads, lam_init):
    hg = pl.program_id(1)
    i = pl.program_id(2)
    fk = fk_ref[...]
    lane = lax.broadcasted_iota(jnp.int32, (tk, LANES), 1)
    first = lane < (LANES // 2)
    hs = range(heads)

    buf_a = (sa_ref, pa_ref)
    buf_b = (sb_ref, pb_ref)

    def scores(g, n, buf):
        kb = k_ref[pl.ds(pl.multiple_of(n * tk, tk), tk), g * LANES:(g + 1) * LANES]
        buf[0][g] = _dot_nt(jnp.concatenate([kb, fk], axis=1), qa_ref[g])

    def softmax(g, n, buf, first_half_mask=None):
        s_ref, p_ref = buf
        c = -slopes_ref[hg * heads + g] * (i * tq - n * tk).astype(f32)
        s = s_ref[g]
        if first_half_mask is not None:
            s = jnp.concatenate([jnp.where(first_half_mask, s[:, :tq], -jnp.inf), s[:, tq:]], axis=1)
        m_old = m_ref[g]
        m_new = jnp.maximum(m_old, jnp.max(s, axis=0, keepdims=True) + c)
        p_ref[g] = jnp.exp2(s - (m_new - c)).astype(bf16)
        m_ref[g] = m_new
        return jnp.exp2(m_old - m_new)

    def pv(g, n, buf, alpha):
        acc_ref[g] = alpha * acc_ref[g] + _dot(vt_ref[g, jnp.maximum(n, 0)], buf[1][g])

    m_ref[...] = jnp.full(m_ref.shape, -jnp.inf, f32)
    alpha_ref[...] = jnp.ones_like(alpha_ref)
    acc_ref[...] = jnp.zeros_like(acc_ref)
    pb_ref[...] = jnp.zeros_like(pb_ref)
    zero = jnp.zeros((tk, LANES), bf16)
    for g in hs:
        parts = []
        for half in range(2):
            q = q_ref[half * tk:(half + 1) * tk, g * LANES:(g + 1) * LANES]
            parts += [jnp.where(first, q, zero), jnp.where(first, zero, q)]
        qa_ref[g] = jnp.concatenate([jnp.concatenate(parts, axis=0), fq_ref[g]], axis=1)
    for g in hs:
        scores(g, 0, buf_a)

    def pair(t, carry):
        n = 2 * t
        a_prev = [alpha_ref[g] for g in hs]
        a_even = [None] * heads
        for g in hs:
            pv(g, n - 1, buf_b, a_prev[g])
            a_even[g] = softmax(g, n, buf_a)
            scores(g, n + 1, buf_b)
        for g in hs:
            pv(g, n, buf_a, a_even[g])
            alpha_ref[g] = softmax(g, n + 1, buf_b)
            scores(g, n + 2, buf_a)
        return carry

    lax.fori_loop(0, i, pair, 0)
    n = 2 * i
    lam = (jnp.exp(jnp.sum(lq1_ref[...] * lk1_ref[...], keepdims=True))
           - jnp.exp(jnp.sum(lq2_ref[...] * lk2_ref[...], keepdims=True)) + lam_init)
    late = slice(tq, 2 * tq)
    kk = lax.broadcasted_iota(jnp.int32, (tk, tq), 0)
    qq = lax.broadcasted_iota(jnp.int32, (tk, tq), 1)
    tri = kk <= jnp.where(qq >= tk, qq - tk, qq)
    for g in hs:
        a_even = softmax(g, n, buf_a, first_half_mask=tri)
        pv(g, n - 1, buf_b, alpha_ref[g])
        pv(g, n, buf_a, a_even)
        kb = k_ref[pl.ds(pl.multiple_of((n + 1) * tk, tk), tk), g * LANES:(g + 1) * LANES]
        s = _dot_nt(jnp.concatenate([kb, fk], axis=1), qa_ref[g, late, :])
        s = jnp.where(tri, s, -jnp.inf)
        c = slopes_ref[hg * heads + g] * float(tk)
        m_old = m_ref[g, :, late]
        m_new = jnp.maximum(m_old, jnp.max(s, axis=0, keepdims=True) + c)
        p = jnp.exp2(s - (m_new - c)).astype(bf16)
        acc = acc_ref[g]
        acc_late = jnp.exp2(m_old - m_new) * acc[:, late] + _dot(vt_ref[g, n + 1], p)
        acc = jnp.concatenate([acc[:, :tq], acc_late], axis=1)
        o_all = acc[0:LANES] / acc[LANES:LANES + 1]
        o = jnp.concatenate([o_all[:, 0:tk] - lam * o_all[:, tk:tq],
                             o_all[:, tq:tq + tk] - lam * o_all[:, tq + tk:]], axis=1)
        ms = jnp.mean(o * o, axis=0, keepdims=True)
        on = o * lax.rsqrt(ms + EPS) * sg_ref[...] * (1.0 - lam_init)
        o_ref[:, g * LANES:(g + 1) * LANES] = on.T.astype(bf16)


def _split_bf16(x, pieces=3):
    out = []
    for _ in range(pieces):
        p = x.astype(bf16)
        out.append(p)
        x = x - p.astype(f32)
    return out


def _alibi_features(slopes2, tq, tk):
    assert tk <= 256
    assert tq == 2 * tk
    n_heads = slopes2.shape[0]
    ones = jnp.ones((tk, 1), bf16)
    krel = jnp.arange(tk, dtype=f32).astype(bf16)[:, None]
    fk = jnp.concatenate([krel] * 3 + [ones] * 3 + [jnp.zeros((tk, LANES - 6), bf16)], axis=1)
    first, last = jnp.arange(tk, dtype=f32), jnp.arange(tk, tq, dtype=f32)
    qrel = jnp.concatenate([first, first, last, last])
    a = _split_bf16(slopes2)
    b = _split_bf16(-slopes2[:, None] * qrel[None, :])
    cols = [jnp.broadcast_to(p[:, None, None], (n_heads, 2 * tq, 1)) for p in a] + [p[:, :, None] for p in b]
    fq = jnp.concatenate(cols + [jnp.zeros((n_heads, 2 * tq, LANES - 6), bf16)], axis=2)
    return fq, fk


def _attention(qn, kn, vt, fq, fk, slopes, lq1, lk1, lq2, lk2, subln_col, lam_init):
    B, S, _ = qn.shape
    tk = fk.shape[0]
    tq = fq.shape[1] // 2
    assert tq == 2 * tk
    nkv = S // tk
    const2 = lambda b, h, i, *_: (0, 0)
    G = ATTN_HEADS_PER_STEP
    kern = functools.partial(_attn_kernel, tq=tq, tk=tk, heads=G, lam_init=lam_init)
    hd = lq1.shape[1]
    return pl.pallas_call(
        kern,
        grid_spec=pltpu.PrefetchScalarGridSpec(
            num_scalar_prefetch=1,
            grid=(B, N_HEADS // G, S // tq),
            in_specs=[
                pl.BlockSpec((None, tq, G * LANES), lambda b, h, i, *_: (b, i, h)),
                pl.BlockSpec((None, S, G * LANES), lambda b, h, i, *_: (b, 0, h)),
                pl.BlockSpec((None, G, nkv, VT_ROWS, tk), lambda b, h, i, *_: (b, h, 0, 0, 0)),
                pl.BlockSpec((G, 2 * tq, LANES), lambda b, h, i, *_: (h, 0, 0)),
                pl.BlockSpec((tk, LANES), const2),
                pl.BlockSpec((1, hd), const2),
                pl.BlockSpec((1, hd), const2),
                pl.BlockSpec((1, hd), const2),
                pl.BlockSpec((1, hd), const2),
                pl.BlockSpec((LANES, 1), const2),
            ],
            out_specs=pl.BlockSpec((None, tq, G * LANES), lambda b, h, i, *_: (b, i, h)),
            scratch_shapes=[pltpu.VMEM((G, 2 * tq, 2 * LANES), bf16),
                            pltpu.VMEM((G, tk, 2 * tq), f32), pltpu.VMEM((G, tk, 2 * tq), f32),
                            pltpu.VMEM((G, tk, 2 * tq), bf16), pltpu.VMEM((G, tk, 2 * tq), bf16),
                            pltpu.VMEM((G, 1, 2 * tq), f32), pltpu.VMEM((G, 1, 2 * tq), f32),
                            pltpu.VMEM((G, VT_ROWS, 2 * tq), f32)],
        ),
        out_shape=jax.ShapeDtypeStruct((B, S, N_HEADS * LANES), bf16),
        compiler_params=pltpu.CompilerParams(dimension_semantics=("arbitrary", "arbitrary", "arbitrary"),
                                             vmem_limit_bytes=VMEM_LIMIT),
        name="diff_attn",
    )(slopes, qn, kn, vt, fq, fk, lq1, lk1, lq2, lk2, subln_col)


def _merge_kernel(pg_ref, ga_ref, o_ref, x_ref, wau_ref, wout_ref, g2_ref, wr_ref, br_ref,
                  x1_ref, hn_ref, eid_ref, wt_ref, rank_ref, cnt_ref, base_ref, *, tm, subs, d_model):
    step = pl.program_id(0)

    @pl.when(step == 0)
    def _():
        base_ref[...] = jnp.zeros_like(base_ref)

    a = lax.broadcasted_iota(jnp.int32, (tm, tm), 0)
    b = lax.broadcasted_iota(jnp.int32, (tm, tm), 1)
    upper = jnp.where(a <= b, 1.0, 0.0).astype(bf16)
    base = base_ref[...]
    all_logits = []
    for sub in range(subs):
        rows = slice(sub * tm, (sub + 1) * tm)
        attn_out = _dot(o_ref[rows, :], wau_ref[...])
        merged = pg_ref[rows, :].astype(f32) + ga_ref[rows, :].astype(f32) * attn_out
        x1 = x_ref[rows, :] + _dot(merged.astype(bf16), wout_ref[...])
        x1_ref[rows, :] = x1
        ms = jnp.mean(x1 * x1, axis=-1, keepdims=True)
        hn = (x1 * lax.rsqrt(ms + EPS) * g2_ref[...]).astype(bf16)
        hn_ref[rows, :] = hn

        all_logits.append(_dot_nt(wr_ref[...], hn) + br_ref[...])

    for sub in range(subs):
        rows = slice(sub * tm, (sub + 1) * tm)
        logits = all_logits[sub]
        lg = logits[0:N_GROUPS]
        gmax = jnp.max(lg, axis=0, keepdims=True)
        p_top = 1.0 / jnp.sum(jnp.exp(lg - gmax), axis=0, keepdims=True)
        grow = lax.broadcasted_iota(jnp.int32, lg.shape, 0).astype(f32)
        g_idx = jnp.min(jnp.where(lg == gmax, grow, float(N_GROUPS)), axis=0, keepdims=True)

        sel = jnp.zeros((EXPERTS_PER_GROUP, tm), f32)
        for g in range(N_GROUPS):
            le_g = logits[SUBLANES + g * EXPERTS_PER_GROUP:SUBLANES + (g + 1) * EXPERTS_PER_GROUP]
            sel = jnp.where(g_idx == float(g), le_g, sel)
        erow = lax.broadcasted_iota(jnp.int32, sel.shape, 0).astype(f32)
        e1 = jnp.max(sel, axis=0, keepdims=True)
        i1 = jnp.min(jnp.where(sel == e1, erow, float(EXPERTS_PER_GROUP)), axis=0, keepdims=True)
        sel2 = jnp.where(erow == i1, -jnp.inf, sel)
        e2 = jnp.max(sel2, axis=0, keepdims=True)
        i2 = jnp.min(jnp.where(sel2 == e2, erow, float(EXPERTS_PER_GROUP)), axis=0, keepdims=True)
        r = jnp.exp(e2 - e1)
        w1 = p_top / (1.0 + r)
        w2 = p_top * r / (1.0 + r)
        eid1 = g_idx * float(EXPERTS_PER_GROUP) + i1
        eid2 = g_idx * float(EXPERTS_PER_GROUP) + i2
        eid_ref[:, rows] = jnp.concatenate([eid1, eid2], axis=0).astype(jnp.int32)
        wt_ref[:, rows] = jnp.concatenate([w1, w2], axis=0)

        xrow = lax.broadcasted_iota(jnp.int32, (N_EXPERTS, tm), 0).astype(f32)
        oh1 = jnp.where(xrow == eid1, 1.0, 0.0)
        oh2 = jnp.where(xrow == eid2, 1.0, 0.0)
        oh = oh1 + oh2
        before = _dot(oh.astype(bf16), upper) + base - 1.0
        rank1 = jnp.sum(oh1 * before, axis=0, keepdims=True)
        rank2 = jnp.sum(oh2 * before, axis=0, keepdims=True)
        rank_ref[:, rows] = jnp.concatenate([rank1, rank2], axis=0).astype(jnp.int32)
        base = base + jnp.sum(oh, axis=1, keepdims=True)
    base_ref[...] = base
    cnt_ref[...] = jnp.broadcast_to(base, cnt_ref.shape)


def _merge(pg, ga, o, x, w_attn_up, w_out, g2, wr, br):
    N, D = x.shape
    tm = min(ROW_TILE, N)
    subs = MERGE_SUBTILES if N % (MERGE_SUBTILES * tm) == 0 else 1
    tb = subs * tm
    aw = o.shape[1]
    const2 = lambda i: (0, 0)
    row = lambda i: (i, 0)
    colblk = lambda i: (0, i)
    kern = functools.partial(_merge_kernel, tm=tm, subs=subs, d_model=D)
    return pl.pallas_call(
        kern,
        grid=(N // tb,),
        in_specs=[
            pl.BlockSpec((tb, D), row),
            pl.BlockSpec((tb, D), row),
            pl.BlockSpec((tb, aw), row),
            pl.BlockSpec((tb, D), row),
            pl.BlockSpec((aw, D), const2),
            pl.BlockSpec((D, D), const2),
            pl.BlockSpec((1, D), const2),
            pl.BlockSpec((ROUTER_ROWS, D), const2),
            pl.BlockSpec((ROUTER_ROWS, 1), const2),
        ],
        out_specs=[
            pl.BlockSpec((tb, D), row),
            pl.BlockSpec((tb, D), row),
            pl.BlockSpec((TOP_K, tb), colblk),
            pl.BlockSpec((TOP_K, tb), colblk),
            pl.BlockSpec((TOP_K, tb), colblk),
            pl.BlockSpec((N_EXPERTS, LANES), const2),
        ],
        out_shape=[
            jax.ShapeDtypeStruct((N, D), f32),
            jax.ShapeDtypeStruct((N, D), bf16),
            jax.ShapeDtypeStruct((TOP_K, N), jnp.int32),
            jax.ShapeDtypeStruct((TOP_K, N), f32),
            jax.ShapeDtypeStruct((TOP_K, N), jnp.int32),
            jax.ShapeDtypeStruct((N_EXPERTS, LANES), f32),
        ],
        scratch_shapes=[pltpu.VMEM((N_EXPERTS, 1), f32)],
        compiler_params=pltpu.CompilerParams(dimension_semantics=("arbitrary",), vmem_limit_bytes=VMEM_LIMIT),
        name="merge_router",
    )(pg, ga, o, x, w_attn_up, w_out, g2, wr, br)


def _slots_kernel(pstart_ref, eid_ref, rank_ref, dest_ref):
    eid = eid_ref[...]
    start = jnp.zeros_like(eid)
    for e in range(N_EXPERTS):
        start = jnp.where(eid == e, pstart_ref[e], start)
    dest_ref[...] = start + rank_ref[...]


def _slots(pstart, eid, rank):
    k, n = eid.shape
    tn = min(n, 8192)
    blk = pl.BlockSpec((k, tn), lambda i, *_: (0, i))
    return pl.pallas_call(
        _slots_kernel,
        grid_spec=pltpu.PrefetchScalarGridSpec(num_scalar_prefetch=1, grid=(n // tn,), in_specs=[blk, blk],
                                               out_specs=blk),
        out_shape=jax.ShapeDtypeStruct((k, n), jnp.int32),
        name="slots",
    )(pstart, eid, rank)


def _row(ref, r):
    start = r * SUBLANES if isinstance(r, int) else pl.multiple_of(r * SUBLANES, SUBLANES)
    return ref.at[pl.ds(start, SUBLANES), :]


def _wait_bytes_of(ref_like, any_hbm, sem):
    n = ref_like.shape[0]
    pltpu.make_async_copy(any_hbm.at[pl.ds(0, n), :], any_hbm.at[pl.ds(0, n), :], sem).wait()


STAGES = 3


def _dispatch_kernel(pstart_ref, pend_ref, count_ref, dest_ref, hn_ref, xs_hbm, zero_ref, stage, out_sem, zsem, *,
                     tt, rows, n_blocks, d_model):
    i = pl.program_id(0)
    n = pl.num_programs(0)

    def fill_unwritten(start):
        def piece(n_rows, first_row):
            cp = pltpu.make_async_copy(zero_ref.at[pl.ds(0, n_rows * SUBLANES), :],
                                       xs_hbm.at[pl.ds(pl.multiple_of(first_row * SUBLANES, SUBLANES),
                                                       n_rows * SUBLANES), :], zsem)
            if start:
                cp.start()
            else:
                cp.wait()

        for e in range(N_EXPERTS):
            row = pstart_ref[e] + count_ref[e]
            n_pad = pend_ref[e] - row
            bit = rows // 2
            while bit >= 1:
                pl.when((n_pad & bit) != 0)(functools.partial(piece, bit, row))
                row = row + (n_pad & bit)
                bit //= 2

        def block(b, carry):
            piece(rows, b * rows)
            return carry
        lax.fori_loop(pend_ref[N_EXPERTS - 1] // rows, n_blocks, block, 0)

    @pl.when(i == 0)
    def _():
        zero_ref[...] = jnp.zeros_like(zero_ref)
        fill_unwritten(start=True)

    def wait_rows_of(step):
        for _ in range(TOP_K):
            _wait_bytes_of(stage.at[0], xs_hbm, out_sem.at[step % STAGES])

    @pl.when(i >= STAGES - 1)
    def _():
        wait_rows_of(i - (STAGES - 1))

    slot = i % STAGES
    for s in range(d_model // LANES):
        stage[slot, pl.ds(s, tt, stride=SUBLANES), :] = hn_ref[:, s * LANES:(s + 1) * LANES].astype(f32)

    def body(t, carry):
        for k in range(TOP_K):
            pltpu.make_async_copy(_row(stage.at[slot], t), _row(xs_hbm, dest_ref[0, k * tt + t]),
                                  out_sem.at[slot]).start(priority=k)
        return carry
    lax.fori_loop(0, tt, body, 0, unroll=8)

    @pl.when(i == n - 1)
    def _():
        for back in range(STAGES - 2, -1, -1):
            @pl.when(i - back >= 0)
            def _():
                wait_rows_of(i - back)
        fill_unwritten(start=False)


def _dispatch(pstart, pend, counts, dest_blocks, hn, n_slots):
    n, _, width = dest_blocks.shape
    tt = width // TOP_K
    rows = EXPERT_ROWS
    D = hn.shape[1]
    kern = functools.partial(_dispatch_kernel, tt=tt, rows=rows, n_blocks=n_slots // rows, d_model=D)
    smem_blk = pl.BlockSpec((None, 1, width), lambda i, *_: (i, 0, 0), memory_space=pltpu.SMEM)
    return pl.pallas_call(
        kern,
        grid_spec=pltpu.PrefetchScalarGridSpec(
            num_scalar_prefetch=3,
            grid=(n,),
            in_specs=[smem_blk, pl.BlockSpec((tt, D), lambda i, *_: (i, 0))],
            out_specs=pl.BlockSpec(memory_space=pl.ANY),
            scratch_shapes=[pltpu.VMEM((rows * SUBLANES, LANES), f32),
                            pltpu.VMEM((STAGES, tt * SUBLANES, LANES), f32),
                            pltpu.SemaphoreType.DMA((STAGES,)),
                            pltpu.SemaphoreType.DMA(())],
        ),
        out_shape=jax.ShapeDtypeStruct((n_slots * SUBLANES, LANES), f32),
        compiler_params=pltpu.CompilerParams(dimension_semantics=("arbitrary",), vmem_limit_bytes=VMEM_LIMIT),
        name="dispatch",
    )(pstart, pend, counts, dest_blocks, hn)


def _rows_from_token_major(buf, n_rows, n_tiles):
    return jnp.concatenate([buf[pl.ds(s, n_rows, stride=SUBLANES), :] for s in range(n_tiles)], axis=-1)


def _expert_kernel(beid_ref, nbu_ref, next_ref, slot_ref, xs_ref, wg_hbm, wu_hbm, wd_hbm, ys_ref,
                   wg_f, wu_f, wd_f, wsem, wg_bf, wu_bf, wd_bf, *, rows, d_model):
    b = pl.program_id(0)

    def weight_copies(e, s):
        return [pltpu.make_async_copy(src.at[e], dst.at[s], wsem.at[s])
                for src, dst in ((wg_hbm, wg_f), (wu_hbm, wu_f), (wd_hbm, wd_f))]

    s = slot_ref[b]

    @pl.when(b == 0)
    def _():
        for cp in weight_copies(beid_ref[0], s):
            cp.start()

    first_of_expert = jnp.logical_or(b == 0, beid_ref[b] != beid_ref[jnp.maximum(b - 1, 0)])

    @pl.when(jnp.logical_and(b < nbu_ref[0], first_of_expert))
    def _():
        for cp in weight_copies(beid_ref[b], s):
            cp.wait()

        @pl.when(next_ref[b] >= 0)
        def _():
            for cp in weight_copies(next_ref[b], 1 - s):
                cp.start()

        wg_bf[...] = wg_f[s].astype(bf16)
        wu_bf[...] = wu_f[s].astype(bf16)
        wd_bf[...] = wd_f[s].astype(bf16)

    n_tiles = d_model // LANES

    @pl.when(b < nbu_ref[0])
    def _():
        xb = _rows_from_token_major(xs_ref, rows, n_tiles).astype(bf16)
        hdn = jax.nn.silu(_dot(xb, wg_bf[...])) * _dot(xb, wu_bf[...])
        y = _dot(hdn.astype(bf16), wd_bf[...])
        for s in range(n_tiles):
            ys_ref[pl.ds(s, rows, stride=SUBLANES), :] = y[:, s * LANES:(s + 1) * LANES]

    @pl.when(b >= nbu_ref[0])
    def _():
        ys_ref[...] = jnp.zeros_like(ys_ref)


def _experts(block_eid, n_used, next_eid, w_slot, xs, w_gate, w_up, w_down):
    n_blocks = block_eid.shape[0]
    rows = EXPERT_ROWS
    E, D, DE = w_gate.shape
    kern = functools.partial(_expert_kernel, rows=rows, d_model=D)
    xs_idx = lambda b, eid, nbu, *_: (jnp.minimum(b, nbu[0] - 1), 0)
    hbm = pl.BlockSpec(memory_space=pl.ANY)
    return pl.pallas_call(
        kern,
        grid_spec=pltpu.PrefetchScalarGridSpec(
            num_scalar_prefetch=4,
            grid=(n_blocks,),
            in_specs=[pl.BlockSpec((rows * SUBLANES, LANES), xs_idx), hbm, hbm, hbm],
            out_specs=pl.BlockSpec((rows * SUBLANES, LANES), lambda b, *_: (b, 0)),
            scratch_shapes=[
                pltpu.VMEM((2, D, DE), f32),
                pltpu.VMEM((2, D, DE), f32),
                pltpu.VMEM((2, DE, D), f32),
                pltpu.SemaphoreType.DMA((2,)),
                pltpu.VMEM((D, DE), bf16),
                pltpu.VMEM((D, DE), bf16),
                pltpu.VMEM((DE, D), bf16),
            ],
        ),
        out_shape=jax.ShapeDtypeStruct((n_blocks * rows * SUBLANES, LANES), f32),
        compiler_params=pltpu.CompilerParams(dimension_semantics=("arbitrary",), vmem_limit_bytes=VMEM_LIMIT),
        name="experts",
    )(block_eid, n_used, next_eid, w_slot, xs, w_gate, w_up, w_down)


def _combine_kernel(dest_ref, destn_ref, x1_ref, wt_ref, ys_hbm, out_ref, buf, sem, *, te, d_model):
    i = pl.program_id(0)
    n = pl.num_programs(0)
    slot = i % 2

    def start_gather(d_ref, dst, dsem):
        for t in range(te):
            for k in range(TOP_K):
                r = k * te + t
                pltpu.make_async_copy(_row(ys_hbm, d_ref[0, r]), _row(dst, r), dsem).start(priority=k)

    @pl.when(i == 0)
    def _():
        start_gather(dest_ref, buf.at[0], sem.at[0])

    @pl.when(i + 1 < n)
    def _():
        start_gather(destn_ref, buf.at[1 - slot], sem.at[1 - slot])

    _wait_bytes_of(buf.at[slot], ys_hbm, sem.at[slot])
    n_tiles = d_model // LANES
    both = _rows_from_token_major(buf.at[slot], TOP_K * te, n_tiles)
    wt = jnp.concatenate([wt_ref[...], jnp.zeros((SUBLANES - TOP_K, te), f32)], axis=0).T
    out_ref[...] = x1_ref[...] + (wt[:, 0:1] * both[:te] + wt[:, 1:2] * both[te:])


def _combine(dest_blocks, x1, wt_rows, ys):
    N, D = x1.shape
    n, _, width = dest_blocks.shape
    te = width // TOP_K
    kern = functools.partial(_combine_kernel, te=te, d_model=D)
    cur = pl.BlockSpec((None, 1, width), lambda i: (i, 0, 0), memory_space=pltpu.SMEM)
    nxt = pl.BlockSpec((None, 1, width), lambda i: (jnp.minimum(i + 1, n - 1), 0, 0), memory_space=pltpu.SMEM)
    return pl.pallas_call(
        kern,
        grid=(n,),
        in_specs=[
            cur, nxt,
            pl.BlockSpec((te, D), lambda i: (i, 0)),
            pl.BlockSpec((TOP_K, te), lambda i: (0, i)),
            pl.BlockSpec(memory_space=pl.ANY),
        ],
        out_specs=pl.BlockSpec((te, D), lambda i: (i, 0)),
        scratch_shapes=[pltpu.VMEM((2, TOP_K * te * SUBLANES, LANES), f32), pltpu.SemaphoreType.DMA((2,))],
        out_shape=jax.ShapeDtypeStruct((N, D), f32),
        compiler_params=pltpu.CompilerParams(dimension_semantics=("arbitrary",), vmem_limit_bytes=VMEM_LIMIT),
        name="combine",
    )(dest_blocks, dest_blocks, x1, wt_rows, ys)


def _token_blocks(a, tt):
    k, n = a.shape
    return a.reshape(k, n // tt, tt).transpose(1, 0, 2).reshape(n // tt, 1, k * tt)


def _layer(x, l, norm1_g, w_in, pool_w, pool_scale, w_pool_up, q_norm_g, k_norm_g, lambda_q1, lambda_k1,
           lambda_q2, lambda_k2, subln_g, w_attn_up, w_out, norm2_g, w_router_group, b_router_group,
           w_router_expert, b_router_expert, w_expert_gate, w_expert_up, w_expert_down):
    B, S, D = x.shape
    N = B * S
    head_dim = q_norm_g.shape[0]
    lam_init = 0.8 - 0.6 * math.exp(-0.3 * l)
    reps = (N_HEADS * 2 * head_dim) // head_dim

    qg = (jnp.tile(q_norm_g, reps) * (head_dim ** -0.5 * LOG2E))[None, :]
    kg = jnp.tile(k_norm_g, reps)[None, :]
    pg, ga, qn, kn, vt = _in_proj(x, norm1_g[None, :], w_in.astype(bf16), pool_w.astype(bf16),
                                  pool_scale[None, :], w_pool_up.astype(bf16), qg, kg)

    tk = min(ATTN_TILE, S)
    slopes2 = jnp.asarray([2.0 ** (-8.0 * (h + 1) / N_HEADS) * LOG2E for h in range(N_HEADS)], f32)
    fq, fk = _alibi_features(slopes2, 2 * tk, tk)
    o = _attention(qn, kn, vt, fq, fk, slopes2, lambda_q1[None, :], lambda_k1[None, :], lambda_q2[None, :],
                   lambda_k2[None, :], subln_g[:, None], lam_init)

    wr = jnp.zeros((ROUTER_ROWS, D), f32)
    wr = wr.at[:N_GROUPS].set(w_router_group.T).at[SUBLANES:].set(w_router_expert.T).astype(bf16)
    br = jnp.zeros((ROUTER_ROWS, 1), f32)
    br = br.at[:N_GROUPS, 0].set(b_router_group).at[SUBLANES:, 0].set(b_router_expert)
    x1, hn, eid, wts, rank, cnt = _merge(pg.reshape(N, D), ga.reshape(N, D), o.reshape(N, -1),
                                           x.reshape(N, D), w_attn_up.astype(bf16), w_out.astype(bf16),
                                           norm2_g[None, :], wr, br)

    R = EXPERT_ROWS
    counts = cnt[:, 0].astype(jnp.int32)
    padded = (counts + R - 1) // R * R
    pend = jnp.cumsum(padded).astype(jnp.int32)
    pstart = pend - padded
    n_blocks = -(-(N * TOP_K) // R) + N_EXPERTS
    starts = jnp.arange(n_blocks, dtype=jnp.int32) * R
    block_eid = jnp.minimum(jnp.sum((pend[None, :] <= starts[:, None]).astype(jnp.int32), axis=1), N_EXPERTS - 1)
    n_used = pend[-1:] // R
    ids = jnp.arange(N_EXPERTS, dtype=jnp.int32)
    present = padded > 0
    later = jnp.where(jnp.logical_and(present[None, :], ids[None, :] > ids[:, None]), ids[None, :], N_EXPERTS)
    next_present = jnp.min(later, axis=1)
    next_present = jnp.where(next_present == N_EXPERTS, -1, next_present)
    order = jnp.cumsum(present.astype(jnp.int32)) - 1
    onehot = (block_eid[:, None] == ids[None, :]).astype(jnp.int32)
    next_eid = jnp.sum(onehot * next_present[None, :], axis=1)
    w_slot = jnp.sum(onehot * order[None, :], axis=1) % 2

    dest = _slots(pstart, eid, rank)
    xs = _dispatch(pstart, pend, counts, _token_blocks(dest, min(DISPATCH_TILE, N)), hn, n_blocks * R)
    ys = _experts(block_eid, n_used, next_eid, w_slot, xs, w_expert_gate, w_expert_up, w_expert_down)
    out = _combine(_token_blocks(dest, min(COMBINE_TILE, N)), x1, wts, ys)
    return out.reshape(B, S, D)


def kernel(x, norm1_g, w_in, pool_w, pool_scale, w_pool_up, q_norm_g, k_norm_g, lambda_q1, lambda_k1, lambda_q2,
           lambda_k2, subln_g, w_attn_up, w_out, norm2_g, w_router_group, b_router_group, w_router_expert,
           b_router_expert, w_expert_gate, w_expert_up, w_expert_down):
    params = (norm1_g, w_in, pool_w, pool_scale, w_pool_up, q_norm_g, k_norm_g, lambda_q1, lambda_k1, lambda_q2,
              lambda_k2, subln_g, w_attn_up, w_out, norm2_g, w_router_group, b_router_group, w_router_expert,
              b_router_expert, w_expert_gate, w_expert_up, w_expert_down)
    for l in range(norm1_g.shape[0]):
        x = _layer(x, l, *(p[l] for p in params))
    return x
```

```python
import functools
import math

import jax
import jax.numpy as jnp
from jax import lax
from jax.experimental import pallas as pl
from jax.experimental.pallas import tpu as pltpu

EPS = 1e-6
POOL_WINDOWS = (2, 4, 8, 16)
POOL_HALO = 16
N_HEADS = 4
N_GROUPS = 4
EXPERTS_PER_GROUP = 8
N_EXPERTS = N_GROUPS * EXPERTS_PER_GROUP
TOP_K = 2
LANES = 128
SUBLANES = 8
ROUTER_ROWS = 8 + N_EXPERTS
VT_ROWS = LANES + 16
LOG2E = 1.4426950408889634

ROW_TILE = 512
IN_PROJ_SUBTILES = 2
MERGE_SUBTILES = 2
ATTN_TILE = 256
ATTN_HEADS_PER_STEP = 4
EXPERT_ROWS = 1024
DISPATCH_TILE = 1024
COMBINE_TILE = 256
V7X_VMEM_BYTES = 64 * 1024 * 1024
VMEM_LIMIT = V7X_VMEM_BYTES * 13 // 16

f32 = jnp.float32
bf16 = jnp.bfloat16


def _dot(a, b):
    return jnp.dot(a, b, preferred_element_type=f32)


def _dot_nt(a, b):
    return lax.dot_general(a, b, (((1,), (1,)), ((), ())), preferred_element_type=f32)


def _half_lane_rmsnorm(t, n_tiles, rows):
    lane = lax.broadcasted_iota(jnp.int32, (rows, LANES), 1)
    lo_mask = lane < (LANES // 2)
    outs = []
    for i in range(n_tiles):
        c = t[:, i * LANES:(i + 1) * LANES]
        sq = c * c
        lo = jnp.sum(jnp.where(lo_mask, sq, 0.0), axis=-1, keepdims=True)
        hi = jnp.sum(jnp.where(lo_mask, 0.0, sq), axis=-1, keepdims=True)
        ms = jnp.where(lo_mask, lo, hi) * (2.0 / LANES)
        outs.append(c * lax.rsqrt(ms + EPS))
    return jnp.concatenate(outs, axis=-1)


def _in_proj_kernel(x_ref, g1_ref, win_ref, poolw_ref, pscale_ref, wpu_ref, qg_ref, kg_ref,
                    pg_ref, ga_ref, qn_ref, kn_ref, vt_ref, prev_ref, wfold_ref, *, tm, subs, tk, d_model, pool_width,
                    qk_width, attn_width):
    j = pl.program_id(1)
    off_q = pool_width
    off_k = off_q + qk_width
    off_v = off_k + qk_width
    off_gp = off_v + attn_width
    off_ga = off_gp + d_model
    group = pool_width // len(POOL_WINDOWS)
    half = d_model // 2
    n_tiles = qk_width // LANES

    @pl.when(jnp.logical_and(pl.program_id(0) == 0, j == 0))
    def _():
        for g in range(len(POOL_WINDOWS)):
            rows_g = slice(g * group, (g + 1) * group)
            scaled = (poolw_ref[g].astype(f32) * pscale_ref[:, rows_g]).astype(bf16)
            wfold_ref[rows_g, :] = _dot(scaled, wpu_ref[rows_g, :]).astype(bf16)

    @pl.when(j == 0)
    def _():
        prev_ref[...] = jnp.zeros_like(prev_ref)

    extra = lax.broadcasted_iota(jnp.int32, (VT_ROWS - LANES, tk), 0)
    ones_rows = jnp.where(extra == 0, 1.0, 0.0).astype(bf16)
    halo = prev_ref[...]
    for sub in range(subs):
        rows = slice(sub * tm, (sub + 1) * tm)
        x = x_ref[rows, :]
        ms = jnp.mean(x * x, axis=-1, keepdims=True)
        h = (x * lax.rsqrt(ms + EPS) * g1_ref[...]).astype(bf16)

        def proj(lo, width, h=h):
            return _dot(h, win_ref[:, lo:lo + width])

        u = proj(0, pool_width)
        v = proj(off_v, attn_width)
        for hh in range(attn_width // LANES):
            for c in range(tm // tk):
                blk = sub * (tm // tk) + c
                vt_ref[hh, blk, 0:LANES, :] = v[c * tk:(c + 1) * tk, hh * LANES:(hh + 1) * LANES].T.astype(bf16)
                vt_ref[hh, blk, LANES:VT_ROWS, :] = ones_rows
        qn_ref[rows, :] = (_half_lane_rmsnorm(proj(off_q, qk_width), n_tiles, tm) * qg_ref[...]).astype(bf16)
        kn_ref[rows, :] = (_half_lane_rmsnorm(proj(off_k, qk_width), n_tiles, tm) * kg_ref[...]).astype(bf16)

        ext = jnp.concatenate([halo, u], axis=0)
        halo = u[tm - POOL_HALO:, :]
        pos = (j * subs + sub) * tm + lax.broadcasted_iota(jnp.int32, (tm, 1), 0)
        ds = []
        for g, w in enumerate(POOL_WINDOWS):
            acc = ext[:, g * group:(g + 1) * group]
            shift = 1
            while shift < w:
                acc = acc + pltpu.roll(acc, shift, 0)
                shift *= 2
            wsum = acc[POOL_HALO:, :]
            cnt = jnp.minimum(pos + 1, w).astype(f32)
            ds.append((wsum / cnt - u[:, g * group:(g + 1) * group]).astype(bf16))
        pool_out = _dot(jnp.concatenate(ds, axis=-1), wfold_ref[...])

        for c in range(2):
            cols = slice(c * half, (c + 1) * half)
            gp = jax.nn.sigmoid(proj(off_gp + c * half, half))
            pg_ref[rows, cols] = (gp * pool_out[:, cols]).astype(bf16)
            ga_ref[rows, cols] = jax.nn.sigmoid(proj(off_ga + c * half, half)).astype(bf16)
    prev_ref[...] = halo


def _in_proj(x, g1, w_in, pool_w, pool_scale, w_pool_up, qg, kg):
    B, S, D = x.shape
    subs = IN_PROJ_SUBTILES if S % (IN_PROJ_SUBTILES * ROW_TILE) == 0 else 1
    tm = min(ROW_TILE, S)
    pool_width = w_pool_up.shape[0]
    qk_width = qg.shape[1]
    attn_width = qk_width
    in_width = w_in.shape[1]
    const2 = lambda b, j: (0, 0)
    row = lambda b, j: (b, j, 0)
    tk = min(ATTN_TILE, S)
    n_heads = attn_width // LANES
    kern = functools.partial(_in_proj_kernel, tm=tm, subs=subs, tk=tk, d_model=D, pool_width=pool_width,
                             qk_width=qk_width, attn_width=attn_width)
    tb = subs * tm
    return pl.pallas_call(
        kern,
        grid=(B, S // tb),
        in_specs=[
            pl.BlockSpec((None, tb, D), row),
            pl.BlockSpec((1, D), const2),
            pl.BlockSpec((D, in_width), const2, pipeline_mode=pl.Buffered(1)),
            pl.BlockSpec(pool_w.shape, lambda b, j: (0, 0, 0)),
            pl.BlockSpec((1, pool_width), const2),
            pl.BlockSpec((pool_width, D), const2),
            pl.BlockSpec((1, qk_width), const2),
            pl.BlockSpec((1, qk_width), const2),
        ],
        out_specs=[
            pl.BlockSpec((None, tb, D), row),
            pl.BlockSpec((None, tb, D), row),
            pl.BlockSpec((None, tb, qk_width), row),
            pl.BlockSpec((None, tb, qk_width), row),
            pl.BlockSpec((None, n_heads, tb // tk, VT_ROWS, tk), lambda b, j: (b, 0, j, 0, 0)),
        ],
        out_shape=[
            jax.ShapeDtypeStruct((B, S, D), bf16),
            jax.ShapeDtypeStruct((B, S, D), bf16),
            jax.ShapeDtypeStruct((B, S, qk_width), bf16),
            jax.ShapeDtypeStruct((B, S, qk_width), bf16),
            jax.ShapeDtypeStruct((B, n_heads, S // tk, VT_ROWS, tk), bf16),
        ],
        scratch_shapes=[pltpu.VMEM((POOL_HALO, pool_width), f32), pltpu.VMEM((pool_width, D), bf16)],
        compiler_params=pltpu.CompilerParams(dimension_semantics=("arbitrary", "arbitrary"),
                                             vmem_limit_bytes=VMEM_LIMIT),
        name="in_proj",
    )(x, g1, w_in, pool_w, pool_scale, w_pool_up, qg, kg)


def _attn_kernel(slopes_ref, q_ref, k_ref, vt_ref, fq_ref, fk_ref, lq1_ref, lk1_ref, lq2_ref, lk2_ref, sg_ref,
                 o_ref, qa_ref, sa_ref, sb_ref, pa_ref, pb_ref, m_ref, alpha_ref, acc_ref, *,
                 tq, tk, heads, lam_init):
    hg = pl.program_id(1)
    i = pl.program_id(2)
    fk = fk_ref[...]
    lane = lax.broadcasted_iota(jnp.int32, (tk, LANES), 1)
    first = lane < (LANES // 2)
    hs = range(heads)

    buf_a = (sa_ref, pa_ref)
    buf_b = (sb_ref, pb_ref)

    def scores(g, n, buf):
        kb = k_ref[pl.ds(pl.multiple_of(n * tk, tk), tk), g * LANES:(g + 1) * LANES]
        buf[0][g] = _dot_nt(jnp.concatenate([kb, fk], axis=1), qa_ref[g])

    def softmax(g, n, buf, first_half_mask=None):
        s_ref, p_ref = buf
        c = -slopes_ref[hg * heads + g] * (i * tq - n * tk).astype(f32)
        s = s_ref[g]
        if first_half_mask is not None:
            s = jnp.concatenate([jnp.where(first_half_mask, s[:, :tq], -jnp.inf), s[:, tq:]], axis=1)
        m_old = m_ref[g]
        m_new = jnp.maximum(m_old, jnp.max(s, axis=0, keepdims=True) + c)
        p_ref[g] = jnp.exp2(s - (m_new - c)).astype(bf16)
        m_ref[g] = m_new
        return jnp.exp2(m_old - m_new)

    def pv(g, n, buf, alpha):
        acc_ref[g] = alpha * acc_ref[g] + _dot(vt_ref[g, jnp.maximum(n, 0)], buf[1][g])

    m_ref[...] = jnp.full(m_ref.shape, -jnp.inf, f32)
    alpha_ref[...] = jnp.ones_like(alpha_ref)
    acc_ref[...] = jnp.zeros_like(acc_ref)
    pb_ref[...] = jnp.zeros_like(pb_ref)
    zero = jnp.zeros((tk, LANES), bf16)
    for g in hs:
        parts = []
        for half in range(2):
            q = q_ref[half * tk:(half + 1) * tk, g * LANES:(g + 1) * LANES]
            parts += [jnp.where(first, q, zero), jnp.where(first, zero, q)]
        qa_ref[g] = jnp.concatenate([jnp.concatenate(parts, axis=0), fq_ref[g]], axis=1)
    for g in hs:
        scores(g, 0, buf_a)

    def pair(t, carry):
        n = 2 * t
        a_prev = [alpha_ref[g] for g in hs]
        a_even = [None] * heads
        for g in hs:
            pv(g, n - 1, buf_b, a_prev[g])
            a_even[g] = softmax(g, n, buf_a)
            scores(g, n + 1, buf_b)
        for g in hs:
            pv(g, n, buf_a, a_even[g])
            alpha_ref[g] = softmax(g, n + 1, buf_b)
            scores(g, n + 2, buf_a)
        return carry

    lax.fori_loop(0, i, pair, 0)
    n = 2 * i
    lam = (jnp.exp(jnp.sum(lq1_ref[...] * lk1_ref[...], keepdims=True))
           - jnp.exp(jnp.sum(lq2_ref[...] * lk2_ref[...], keepdims=True)) + lam_init)
    late = slice(tq, 2 * tq)
    kk = lax.broadcasted_iota(jnp.int32, (tk, tq), 0)
    qq = lax.broadcasted_iota(jnp.int32, (tk, tq), 1)
    tri = kk <= jnp.where(qq >= tk, qq - tk, qq)
    for g in hs:
        a_even = softmax(g, n, buf_a, first_half_mask=tri)
        pv(g, n - 1, buf_b, alpha_ref[g])
        pv(g, n, buf_a, a_even)
        kb = k_ref[pl.ds(pl.multiple_of((n + 1) * tk, tk), tk), g * LANES:(g + 1) * LANES]
        s = _dot_nt(jnp.concatenate([kb, fk], axis=1), qa_ref[g, late, :])
        s = jnp.where(tri, s, -jnp.inf)
        c = slopes_ref[hg * heads + g] * float(tk)
        m_old = m_ref[g, :, late]
        m_new = jnp.maximum(m_old, jnp.max(s, axis=0, keepdims=True) + c)
        p = jnp.exp2(s - (m_new - c)).astype(bf16)
        acc = acc_ref[g]
        acc_late = jnp.exp2(m_old - m_new) * acc[:, late] + _dot(vt_ref[g, n + 1], p)
        acc = jnp.concatenate([acc[:, :tq], acc_late], axis=1)
        o_all = acc[0:LANES] / acc[LANES:LANES + 1]
        o = jnp.concatenate([o_all[:, 0:tk] - lam * o_all[:, tk:tq],
                             o_all[:, tq:tq + tk] - lam * o_all[:, tq + tk:]], axis=1)
        ms = jnp.mean(o * o, axis=0, keepdims=True)
        on = o * lax.rsqrt(ms + EPS) * sg_ref[...] * (1.0 - lam_init)
        o_ref[:, g * LANES:(g + 1) * LANES] = on.T.astype(bf16)


def _split_bf16(x, pieces=3):
    out = []
    for _ in range(pieces):
        p = x.astype(bf16)
        out.append(p)
        x = x - p.astype(f32)
    return out


def _alibi_features(slopes2, tq, tk):
    assert tk <= 256
    assert tq == 2 * tk
    n_heads = slopes2.shape[0]
    ones = jnp.ones((tk, 1), bf16)
    krel = jnp.arange(tk, dtype=f32).astype(bf16)[:, None]
    fk = jnp.concatenate([krel] * 3 + [ones] * 3 + [jnp.zeros((tk, LANES - 6), bf16)], axis=1)
    first, last = jnp.arange(tk, dtype=f32), jnp.arange(tk, tq, dtype=f32)
    qrel = jnp.concatenate([first, first, last, last])
    a = _split_bf16(slopes2)
    b = _split_bf16(-slopes2[:, None] * qrel[None, :])
    cols = [jnp.broadcast_to(p[:, None, None], (n_heads, 2 * tq, 1)) for p in a] + [p[:, :, None] for p in b]
    fq = jnp.concatenate(cols + [jnp.zeros((n_heads, 2 * tq, LANES - 6), bf16)], axis=2)
    return fq, fk


def _attention(qn, kn, vt, fq, fk, slopes, lq1, lk1, lq2, lk2, subln_col, lam_init):
    B, S, _ = qn.shape
    tk = fk.shape[0]
    tq = fq.shape[1] // 2
    assert tq == 2 * tk
    nkv = S // tk
    const2 = lambda b, h, i, *_: (0, 0)
    G = ATTN_HEADS_PER_STEP
    kern = functools.partial(_attn_kernel, tq=tq, tk=tk, heads=G, lam_init=lam_init)
    hd = lq1.shape[1]
    return pl.pallas_call(
        kern,
        grid_spec=pltpu.PrefetchScalarGridSpec(
            num_scalar_prefetch=1,
            grid=(B, N_HEADS // G, S // tq),
            in_specs=[
                pl.BlockSpec((None, tq, G * LANES), lambda b, h, i, *_: (b, i, h)),
                pl.BlockSpec((None, S, G * LANES), lambda b, h, i, *_: (b, 0, h)),
                pl.BlockSpec((None, G, nkv, VT_ROWS, tk), lambda b, h, i, *_: (b, h, 0, 0, 0)),
                pl.BlockSpec((G, 2 * tq, LANES), lambda b, h, i, *_: (h, 0, 0)),
                pl.BlockSpec((tk, LANES), const2),
                pl.BlockSpec((1, hd), const2),
                pl.BlockSpec((1, hd), const2),
                pl.BlockSpec((1, hd), const2),
                pl.BlockSpec((1, hd), const2),
                pl.BlockSpec((LANES, 1), const2),
            ],
            out_specs=pl.BlockSpec((None, tq, G * LANES), lambda b, h, i, *_: (b, i, h)),
            scratch_shapes=[pltpu.VMEM((G, 2 * tq, 2 * LANES), bf16),
                            pltpu.VMEM((G, tk, 2 * tq), f32), pltpu.VMEM((G, tk, 2 * tq), f32),
                            pltpu.VMEM((G, tk, 2 * tq), bf16), pltpu.VMEM((G, tk, 2 * tq), bf16),
                            pltpu.VMEM((G, 1, 2 * tq), f32), pltpu.VMEM((G, 1, 2 * tq), f32),
                            pltpu.VMEM((G, VT_ROWS, 2 * tq), f32)],
        ),
        out_shape=jax.ShapeDtypeStruct((B, S, N_HEADS * LANES), bf16),
        compiler_params=pltpu.CompilerParams(dimension_semantics=("arbitrary", "arbitrary", "arbitrary"),
                                             vmem_limit_bytes=VMEM_LIMIT),
        name="diff_attn",
    )(slopes, qn, kn, vt, fq, fk, lq1, lk1, lq2, lk2, subln_col)


def _merge_kernel(pg_ref, ga_ref, o_ref, x_ref, wau_ref, wout_ref, g2_ref, wr_ref, br_ref,
                  x1_ref, hn_ref, eid_ref, wt_ref, rank_ref, cnt_ref, base_ref, *, tm, subs, d_model):
    step = pl.program_id(0)

    @pl.when(step == 0)
    def _():
        base_ref[...] = jnp.zeros_like(base_ref)

    a = lax.broadcasted_iota(jnp.int32, (tm, tm), 0)
    b = lax.broadcasted_iota(jnp.int32, (tm, tm), 1)
    upper = jnp.where(a <= b, 1.0, 0.0).astype(bf16)
    base = base_ref[...]
    all_logits = []
    for sub in range(subs):
        rows = slice(sub * tm, (sub + 1) * tm)
        attn_out = _dot(o_ref[rows, :], wau_ref[...])
        merged = pg_ref[rows, :].astype(f32) + ga_ref[rows, :].astype(f32) * attn_out
        x1 = x_ref[rows, :] + _dot(merged.astype(bf16), wout_ref[...])
        x1_ref[rows, :] = x1
        ms = jnp.mean(x1 * x1, axis=-1, keepdims=True)
        hn = (x1 * lax.rsqrt(ms + EPS) * g2_ref[...]).astype(bf16)
        hn_ref[rows, :] = hn

        all_logits.append(_dot_nt(wr_ref[...], hn) + br_ref[...])

    for sub in range(subs):
        rows = slice(sub * tm, (sub + 1) * tm)
        logits = all_logits[sub]
        lg = logits[0:N_GROUPS]
        gmax = jnp.max(lg, axis=0, keepdims=True)
        p_top = 1.0 / jnp.sum(jnp.exp(lg - gmax), axis=0, keepdims=True)
        grow = lax.broadcasted_iota(jnp.int32, lg.shape, 0).astype(f32)
        g_idx = jnp.min(jnp.where(lg == gmax, grow, float(N_GROUPS)), axis=0, keepdims=True)

        sel = jnp.zeros((EXPERTS_PER_GROUP, tm), f32)
        for g in range(N_GROUPS):
            le_g = logits[SUBLANES + g * EXPERTS_PER_GROUP:SUBLANES + (g + 1) * EXPERTS_PER_GROUP]
            sel = jnp.where(g_idx == float(g), le_g, sel)
        erow = lax.broadcasted_iota(jnp.int32, sel.shape, 0).astype(f32)
        e1 = jnp.max(sel, axis=0, keepdims=True)
        i1 = jnp.min(jnp.where(sel == e1, erow, float(EXPERTS_PER_GROUP)), axis=0, keepdims=True)
        sel2 = jnp.where(erow == i1, -jnp.inf, sel)
        e2 = jnp.max(sel2, axis=0, keepdims=True)
        i2 = jnp.min(jnp.where(sel2 == e2, erow, float(EXPERTS_PER_GROUP)), axis=0, keepdims=True)
        r = jnp.exp(e2 - e1)
        w1 = p_top / (1.0 + r)
        w2 = p_top * r / (1.0 + r)
        eid1 = g_idx * float(EXPERTS_PER_GROUP) + i1
        eid2 = g_idx * float(EXPERTS_PER_GROUP) + i2
        eid_ref[:, rows] = jnp.concatenate([eid1, eid2], axis=0).astype(jnp.int32)
        wt_ref[:, rows] = jnp.concatenate([w1, w2], axis=0)

        xrow = lax.broadcasted_iota(jnp.int32, (N_EXPERTS, tm), 0).astype(f32)
        oh1 = jnp.where(xrow == eid1, 1.0, 0.0)
        oh2 = jnp.where(xrow == eid2, 1.0, 0.0)
        oh = oh1 + oh2
        before = _dot(oh.astype(bf16), upper) + base - 1.0
        rank1 = jnp.sum(oh1 * before, axis=0, keepdims=True)
        rank2 = jnp.sum(oh2 * before, axis=0, keepdims=True)
        rank_ref[:, rows] = jnp.concatenate([rank1, rank2], axis=0).astype(jnp.int32)
        base = base + jnp.sum(oh, axis=1, keepdims=True)
    base_ref[...] = base
    cnt_ref[...] = jnp.broadcast_to(base, cnt_ref.shape)


def _merge(pg, ga, o, x, w_attn_up, w_out, g2, wr, br):
    N, D = x.shape
    tm = min(ROW_TILE, N)
    subs = MERGE_SUBTILES if N % (MERGE_SUBTILES * tm) == 0 else 1
    tb = subs * tm
    aw = o.shape[1]
    const2 = lambda i: (0, 0)
    row = lambda i: (i, 0)
    colblk = lambda i: (0, i)
    kern = functools.partial(_merge_kernel, tm=tm, subs=subs, d_model=D)
    return pl.pallas_call(
        kern,
        grid=(N // tb,),
        in_specs=[
            pl.BlockSpec((tb, D), row),
            pl.BlockSpec((tb, D), row),
            pl.BlockSpec((tb, aw), row),
            pl.BlockSpec((tb, D), row),
            pl.BlockSpec((aw, D), const2),
            pl.BlockSpec((D, D), const2),
            pl.BlockSpec((1, D), const2),
            pl.BlockSpec((ROUTER_ROWS, D), const2),
            pl.BlockSpec((ROUTER_ROWS, 1), const2),
        ],
        out_specs=[
            pl.BlockSpec((tb, D), row),
            pl.BlockSpec((tb, D), row),
            pl.BlockSpec((TOP_K, tb), colblk),
            pl.BlockSpec((TOP_K, tb), colblk),
            pl.BlockSpec((TOP_K, tb), colblk),
            pl.BlockSpec((N_EXPERTS, LANES), const2),
        ],
        out_shape=[
            jax.ShapeDtypeStruct((N, D), f32),
            jax.ShapeDtypeStruct((N, D), bf16),
            jax.ShapeDtypeStruct((TOP_K, N), jnp.int32),
            jax.ShapeDtypeStruct((TOP_K, N), f32),
            jax.ShapeDtypeStruct((TOP_K, N), jnp.int32),
            jax.ShapeDtypeStruct((N_EXPERTS, LANES), f32),
        ],
        scratch_shapes=[pltpu.VMEM((N_EXPERTS, 1), f32)],
        compiler_params=pltpu.CompilerParams(dimension_semantics=("arbitrary",), vmem_limit_bytes=VMEM_LIMIT),
        name="merge_router",
    )(pg, ga, o, x, w_attn_up, w_out, g2, wr, br)


def _slots_kernel(pstart_ref, eid_ref, rank_ref, dest_ref):
    eid = eid_ref[...]
    start = jnp.zeros_like(eid)
    for e in range(N_EXPERTS):
        start = jnp.where(eid == e, pstart_ref[e], start)
    dest_ref[...] = start + rank_ref[...]


def _slots(pstart, eid, rank):
    k, n = eid.shape
    tn = min(n, 8192)
    blk = pl.BlockSpec((k, tn), lambda i, *_: (0, i))
    return pl.pallas_call(
        _slots_kernel,
        grid_spec=pltpu.PrefetchScalarGridSpec(num_scalar_prefetch=1, grid=(n // tn,), in_specs=[blk, blk],
                                               out_specs=blk),
        out_shape=jax.ShapeDtypeStruct((k, n), jnp.int32),
        name="slots",
    )(pstart, eid, rank)


def _row(ref, r):
    start = r * SUBLANES if isinstance(r, int) else pl.multiple_of(r * SUBLANES, SUBLANES)
    return ref.at[pl.ds(start, SUBLANES), :]


def _wait_bytes_of(ref_like, any_hbm, sem):
    n = ref_like.shape[0]
    pltpu.make_async_copy(any_hbm.at[pl.ds(0, n), :], any_hbm.at[pl.ds(0, n), :], sem).wait()


STAGES = 3


def _dispatch_kernel(pstart_ref, pend_ref, count_ref, dest_ref, hn_ref, xs_hbm, zero_ref, stage, out_sem, zsem, *,
                     tt, rows, n_blocks, d_model):
    i = pl.program_id(0)
    n = pl.num_programs(0)

    def fill_unwritten(start):
        def piece(n_rows, first_row):
            cp = pltpu.make_async_copy(zero_ref.at[pl.ds(0, n_rows * SUBLANES), :],
                                       xs_hbm.at[pl.ds(pl.multiple_of(first_row * SUBLANES, SUBLANES),
                                                       n_rows * SUBLANES), :], zsem)
            if start:
                cp.start()
            else:
                cp.wait()

        for e in range(N_EXPERTS):
            row = pstart_ref[e] + count_ref[e]
            n_pad = pend_ref[e] - row
            bit = rows // 2
            while bit >= 1:
                pl.when((n_pad & bit) != 0)(functools.partial(piece, bit, row))
                row = row + (n_pad & bit)
                bit //= 2

        def block(b, carry):
            piece(rows, b * rows)
            return carry
        lax.fori_loop(pend_ref[N_EXPERTS - 1] // rows, n_blocks, block, 0)

    @pl.when(i == 0)
    def _():
        zero_ref[...] = jnp.zeros_like(zero_ref)
        fill_unwritten(start=True)

    def wait_rows_of(step):
        for _ in range(TOP_K):
            _wait_bytes_of(stage.at[0], xs_hbm, out_sem.at[step % STAGES])

    @pl.when(i >= STAGES - 1)
    def _():
        wait_rows_of(i - (STAGES - 1))

    slot = i % STAGES
    for s in range(d_model // LANES):
        stage[slot, pl.ds(s, tt, stride=SUBLANES), :] = hn_ref[:, s * LANES:(s + 1) * LANES].astype(f32)

    def body(t, carry):
        for k in range(TOP_K):
            pltpu.make_async_copy(_row(stage.at[slot], t), _row(xs_hbm, dest_ref[0, k * tt + t]),
                                  out_sem.at[slot]).start(priority=k)
        return carry
    lax.fori_loop(0, tt, body, 0, unroll=8)

    @pl.when(i == n - 1)
    def _():
        for back in range(STAGES - 2, -1, -1):
            @pl.when(i - back >= 0)
            def _():
                wait_rows_of(i - back)
        fill_unwritten(start=False)


def _dispatch(pstart, pend, counts, dest_blocks, hn, n_slots):
    n, _, width = dest_blocks.shape
    tt = width // TOP_K
    rows = EXPERT_ROWS
    D = hn.shape[1]
    kern = functools.partial(_dispatch_kernel, tt=tt, rows=rows, n_blocks=n_slots // rows, d_model=D)
    smem_blk = pl.BlockSpec((None, 1, width), lambda i, *_: (i, 0, 0), memory_space=pltpu.SMEM)
    return pl.pallas_call(
        kern,
        grid_spec=pltpu.PrefetchScalarGridSpec(
            num_scalar_prefetch=3,
            grid=(n,),
            in_specs=[smem_blk, pl.BlockSpec((tt, D), lambda i, *_: (i, 0))],
            out_specs=pl.BlockSpec(memory_space=pl.ANY),
            scratch_shapes=[pltpu.VMEM((rows * SUBLANES, LANES), f32),
                            pltpu.VMEM((STAGES, tt * SUBLANES, LANES), f32),
                            pltpu.SemaphoreType.DMA((STAGES,)),
                            pltpu.SemaphoreType.DMA(())],
        ),
        out_shape=jax.ShapeDtypeStruct((n_slots * SUBLANES, LANES), f32),
        compiler_params=pltpu.CompilerParams(dimension_semantics=("arbitrary",), vmem_limit_bytes=VMEM_LIMIT),
        name="dispatch",
    )(pstart, pend, counts, dest_blocks, hn)


def _rows_from_token_major(buf, n_rows, n_tiles):
    return jnp.concatenate([buf[pl.ds(s, n_rows, stride=SUBLANES), :] for s in range(n_tiles)], axis=-1)


def _expert_kernel(beid_ref, nbu_ref, next_ref, slot_ref, xs_ref, wg_hbm, wu_hbm, wd_hbm, ys_ref,
                   wg_f, wu_f, wd_f, wsem, wg_bf, wu_bf, wd_bf, *, rows, d_model):
    b = pl.program_id(0)

    def weight_copies(e, s):
        return [pltpu.make_async_copy(src.at[e], dst.at[s], wsem.at[s])
                for src, dst in ((wg_hbm, wg_f), (wu_hbm, wu_f), (wd_hbm, wd_f))]

    s = slot_ref[b]

    @pl.when(b == 0)
    def _():
        for cp in weight_copies(beid_ref[0], s):
            cp.start()

    first_of_expert = jnp.logical_or(b == 0, beid_ref[b] != beid_ref[jnp.maximum(b - 1, 0)])

    @pl.when(jnp.logical_and(b < nbu_ref[0], first_of_expert))
    def _():
        for cp in weight_copies(beid_ref[b], s):
            cp.wait()

        @pl.when(next_ref[b] >= 0)
        def _():
            for cp in weight_copies(next_ref[b], 1 - s):
                cp.start()

        wg_bf[...] = wg_f[s].astype(bf16)
        wu_bf[...] = wu_f[s].astype(bf16)
        wd_bf[...] = wd_f[s].astype(bf16)

    n_tiles = d_model // LANES

    @pl.when(b < nbu_ref[0])
    def _():
        xb = _rows_from_token_major(xs_ref, rows, n_tiles).astype(bf16)
        hdn = jax.nn.silu(_dot(xb, wg_bf[...])) * _dot(xb, wu_bf[...])
        y = _dot(hdn.astype(bf16), wd_bf[...])
        for s in range(n_tiles):
            ys_ref[pl.ds(s, rows, stride=SUBLANES), :] = y[:, s * LANES:(s + 1) * LANES]

    @pl.when(b >= nbu_ref[0])
    def _():
        ys_ref[...] = jnp.zeros_like(ys_ref)


def _experts(block_eid, n_used, next_eid, w_slot, xs, w_gate, w_up, w_down):
    n_blocks = block_eid.shape[0]
    rows = EXPERT_ROWS
    E, D, DE = w_gate.shape
    kern = functools.partial(_expert_kernel, rows=rows, d_model=D)
    xs_idx = lambda b, eid, nbu, *_: (jnp.minimum(b, nbu[0] - 1), 0)
    hbm = pl.BlockSpec(memory_space=pl.ANY)
    return pl.pallas_call(
        kern,
        grid_spec=pltpu.PrefetchScalarGridSpec(
            num_scalar_prefetch=4,
            grid=(n_blocks,),
            in_specs=[pl.BlockSpec((rows * SUBLANES, LANES), xs_idx), hbm, hbm, hbm],
            out_specs=pl.BlockSpec((rows * SUBLANES, LANES), lambda b, *_: (b, 0)),
            scratch_shapes=[
                pltpu.VMEM((2, D, DE), f32),
                pltpu.VMEM((2, D, DE), f32),
                pltpu.VMEM((2, DE, D), f32),
                pltpu.SemaphoreType.DMA((2,)),
                pltpu.VMEM((D, DE), bf16),
                pltpu.VMEM((D, DE), bf16),
                pltpu.VMEM((DE, D), bf16),
            ],
        ),
        out_shape=jax.ShapeDtypeStruct((n_blocks * rows * SUBLANES, LANES), f32),
        compiler_params=pltpu.CompilerParams(dimension_semantics=("arbitrary",), vmem_limit_bytes=VMEM_LIMIT),
        name="experts",
    )(block_eid, n_used, next_eid, w_slot, xs, w_gate, w_up, w_down)


def _combine_kernel(dest_ref, destn_ref, x1_ref, wt_ref, ys_hbm, out_ref, buf, sem, *, te, d_model):
    i = pl.program_id(0)
    n = pl.num_programs(0)
    slot = i % 2

    def start_gather(d_ref, dst, dsem):
        for t in range(te):
            for k in range(TOP_K):
                r = k * te + t
                pltpu.make_async_copy(_row(ys_hbm, d_ref[0, r]), _row(dst, r), dsem).start(priority=k)

    @pl.when(i == 0)
    def _():
        start_gather(dest_ref, buf.at[0], sem.at[0])

    @pl.when(i + 1 < n)
    def _():
        start_gather(destn_ref, buf.at[1 - slot], sem.at[1 - slot])

    _wait_bytes_of(buf.at[slot], ys_hbm, sem.at[slot])
    n_tiles = d_model // LANES
    both = _rows_from_token_major(buf.at[slot], TOP_K * te, n_tiles)
    wt = jnp.concatenate([wt_ref[...], jnp.zeros((SUBLANES - TOP_K, te), f32)], axis=0).T
    out_ref[...] = x1_ref[...] + (wt[:, 0:1] * both[:te] + wt[:, 1:2] * both[te:])


def _combine(dest_blocks, x1, wt_rows, ys):
    N, D = x1.shape
    n, _, width = dest_blocks.shape
    te = width // TOP_K
    kern = functools.partial(_combine_kernel, te=te, d_model=D)
    cur = pl.BlockSpec((None, 1, width), lambda i: (i, 0, 0), memory_space=pltpu.SMEM)
    nxt = pl.BlockSpec((None, 1, width), lambda i: (jnp.minimum(i + 1, n - 1), 0, 0), memory_space=pltpu.SMEM)
    return pl.pallas_call(
        kern,
        grid=(n,),
        in_specs=[
            cur, nxt,
            pl.BlockSpec((te, D), lambda i: (i, 0)),
            pl.BlockSpec((TOP_K, te), lambda i: (0, i)),
            pl.BlockSpec(memory_space=pl.ANY),
        ],
        out_specs=pl.BlockSpec((te, D), lambda i: (i, 0)),
        scratch_shapes=[pltpu.VMEM((2, TOP_K * te * SUBLANES, LANES), f32), pltpu.SemaphoreType.DMA((2,))],
        out_shape=jax.ShapeDtypeStruct((N, D), f32),
        compiler_params=pltpu.CompilerParams(dimension_semantics=("arbitrary",), vmem_limit_bytes=VMEM_LIMIT),
        name="combine",
    )(dest_blocks, dest_blocks, x1, wt_rows, ys)


def _token_blocks(a, tt):
    k, n = a.shape
    return a.reshape(k, n // tt, tt).transpose(1, 0, 2).reshape(n // tt, 1, k * tt)


def _layer(x, l, norm1_g, w_in, pool_w, pool_scale, w_pool_up, q_norm_g, k_norm_g, lambda_q1, lambda_k1,
           lambda_q2, lambda_k2, subln_g, w_attn_up, w_out, norm2_g, w_router_group, b_router_group,
           w_router_expert, b_router_expert, w_expert_gate, w_expert_up, w_expert_down):
    B, S, D = x.shape
    N = B * S
    head_dim = q_norm_g.shape[0]
    lam_init = 0.8 - 0.6 * math.exp(-0.3 * l)
    reps = (N_HEADS * 2 * head_dim) // head_dim

    qg = (jnp.tile(q_norm_g, reps) * (head_dim ** -0.5 * LOG2E))[None, :]
    kg = jnp.tile(k_norm_g, reps)[None, :]
    pg, ga, qn, kn, vt = _in_proj(x, norm1_g[None, :], w_in.astype(bf16), pool_w.astype(bf16),
                                  pool_scale[None, :], w_pool_up.astype(bf16), qg, kg)

    tk = min(ATTN_TILE, S)
    slopes2 = jnp.asarray([2.0 ** (-8.0 * (h + 1) / N_HEADS) * LOG2E for h in range(N_HEADS)], f32)
    fq, fk = _alibi_features(slopes2, 2 * tk, tk)
    o = _attention(qn, kn, vt, fq, fk, slopes2, lambda_q1[None, :], lambda_k1[None, :], lambda_q2[None, :],
                   lambda_k2[None, :], subln_g[:, None], lam_init)

    wr = jnp.zeros((ROUTER_ROWS, D), f32)
    wr = wr.at[:N_GROUPS].set(w_router_group.T).at[SUBLANES:].set(w_router_expert.T).astype(bf16)
    br = jnp.zeros((ROUTER_ROWS, 1), f32)
    br = br.at[:N_GROUPS, 0].set(b_router_group).at[SUBLANES:, 0].set(b_router_expert)
    x1, hn, eid, wts, rank, cnt = _merge(pg.reshape(N, D), ga.reshape(N, D), o.reshape(N, -1),
                                           x.reshape(N, D), w_attn_up.astype(bf16), w_out.astype(bf16),
                                           norm2_g[None, :], wr, br)

    R = EXPERT_ROWS
    counts = cnt[:, 0].astype(jnp.int32)
    padded = (counts + R - 1) // R * R
    pend = jnp.cumsum(padded).astype(jnp.int32)
    pstart = pend - padded
    n_blocks = -(-(N * TOP_K) // R) + N_EXPERTS
    starts = jnp.arange(n_blocks, dtype=jnp.int32) * R
    block_eid = jnp.minimum(jnp.sum((pend[None, :] <= starts[:, None]).astype(jnp.int32), axis=1), N_EXPERTS - 1)
    n_used = pend[-1:] // R
    ids = jnp.arange(N_EXPERTS, dtype=jnp.int32)
    present = padded > 0
    later = jnp.where(jnp.logical_and(present[None, :], ids[None, :] > ids[:, None]), ids[None, :], N_EXPERTS)
    next_present = jnp.min(later, axis=1)
    next_present = jnp.where(next_present == N_EXPERTS, -1, next_present)
    order = jnp.cumsum(present.astype(jnp.int32)) - 1
    onehot = (block_eid[:, None] == ids[None, :]).astype(jnp.int32)
    next_eid = jnp.sum(onehot * next_present[None, :], axis=1)
    w_slot = jnp.sum(onehot * order[None, :], axis=1) % 2

    dest = _slots(pstart, eid, rank)
    xs = _dispatch(pstart, pend, counts, _token_blocks(dest, min(DISPATCH_TILE, N)), hn, n_blocks * R)
    ys = _experts(block_eid, n_used, next_eid, w_slot, xs, w_expert_gate, w_expert_up, w_expert_down)
    out = _combine(_token_blocks(dest, min(COMBINE_TILE, N)), x1, wts, ys)
    return out.reshape(B, S, D)


def kernel(x, norm1_g, w_in, pool_w, pool_scale, w_pool_up, q_norm_g, k_norm_g, lambda_q1, lambda_k1, lambda_q2,
           lambda_k2, subln_g, w_attn_up, w_out, norm2_g, w_router_group, b_router_group, w_router_expert,
           b_router_expert, w_expert_gate, w_expert_up, w_expert_down):
    params = (norm1_g, w_in, pool_w, pool_scale, w_pool_up, q_norm_g, k_norm_g, lambda_q1, lambda_k1, lambda_q2,
              lambda_k2, subln_g, w_attn_up, w_out, norm2_g, w_router_group, b_router_group, w_router_expert,
              b_router_expert, w_expert_gate, w_expert_up, w_expert_down)
    for l in range(norm1_g.shape[0]):
        x = _layer(x, l, *(p[l] for p in params))
    return x
```

```python
import functools
import math

import jax
import jax.numpy as jnp
from jax import lax
from jax.experimental import pallas as pl
from jax.experimental.pallas import tpu as pltpu

EPS = 1e-6
POOL_WINDOWS = (2, 4, 8, 16)
POOL_HALO = 16
N_HEADS = 4
N_GROUPS = 4
EXPERTS_PER_GROUP = 8
N_EXPERTS = N_GROUPS * EXPERTS_PER_GROUP
TOP_K = 2
LANES = 128
SUBLANES = 8
ROUTER_ROWS = 8 + N_EXPERTS
VT_ROWS = LANES + 16
LOG2E = 1.4426950408889634

ROW_TILE = 512
IN_PROJ_SUBTILES = 2
MERGE_SUBTILES = 2
ATTN_TILE = 256
ATTN_HEADS_PER_STEP = 2
EXPERT_ROWS = 1024
DISPATCH_TILE = 1024
COMBINE_TILE = 256
V7X_VMEM_BYTES = 64 * 1024 * 1024
VMEM_LIMIT = V7X_VMEM_BYTES * 13 // 16

f32 = jnp.float32
bf16 = jnp.bfloat16


def _dot(a, b):
    return jnp.dot(a, b, preferred_element_type=f32)


def _dot_nt(a, b):
    return lax.dot_general(a, b, (((1,), (1,)), ((), ())), preferred_element_type=f32)


def _half_lane_rmsnorm(t, n_tiles, rows):
    lane = lax.broadcasted_iota(jnp.int32, (rows, LANES), 1)
    lo_mask = lane < (LANES // 2)
    outs = []
    for i in range(n_tiles):
        c = t[:, i * LANES:(i + 1) * LANES]
        sq = c * c
        lo = jnp.sum(jnp.where(lo_mask, sq, 0.0), axis=-1, keepdims=True)
        hi = jnp.sum(jnp.where(lo_mask, 0.0, sq), axis=-1, keepdims=True)
        ms = jnp.where(lo_mask, lo, hi) * (2.0 / LANES)
        outs.append(c * lax.rsqrt(ms + EPS))
    return jnp.concatenate(outs, axis=-1)


def _in_proj_kernel(x_ref, g1_ref, win_ref, poolw_ref, pscale_ref, wpu_ref, qg_ref, kg_ref,
                    pg_ref, ga_ref, qn_ref, kn_ref, vt_ref, prev_ref, wfold_ref, *, tm, subs, tk, d_model, pool_width,
                    qk_width, attn_width):
    j = pl.program_id(1)
    off_q = pool_width
    off_k = off_q + qk_width
    off_v = off_k + qk_width
    off_gp = off_v + attn_width
    off_ga = off_gp + d_model
    group = pool_width // len(POOL_WINDOWS)
    half = d_model // 2
    n_tiles = qk_width // LANES

    @pl.when(jnp.logical_and(pl.program_id(0) == 0, j == 0))
    def _():
        for g in range(len(POOL_WINDOWS)):
            rows_g = slice(g * group, (g + 1) * group)
            scaled = (poolw_ref[g].astype(f32) * pscale_ref[:, rows_g]).astype(bf16)
            wfold_ref[rows_g, :] = _dot(scaled, wpu_ref[rows_g, :]).astype(bf16)

    @pl.when(j == 0)
    def _():
        prev_ref[...] = jnp.zeros_like(prev_ref)

    extra = lax.broadcasted_iota(jnp.int32, (VT_ROWS - LANES, tk), 0)
    ones_rows = jnp.where(extra == 0, 1.0, 0.0).astype(bf16)
    halo = prev_ref[...]
    for sub in range(subs):
        rows = slice(sub * tm, (sub + 1) * tm)
        x = x_ref[rows, :]
        ms = jnp.mean(x * x, axis=-1, keepdims=True)
        h = (x * lax.rsqrt(ms + EPS) * g1_ref[...]).astype(bf16)

        def proj(lo, width, h=h):
            return _dot(h, win_ref[:, lo:lo + width])

        u = proj(0, pool_width)
        v = proj(off_v, attn_width)
        for hh in range(attn_width // LANES):
            for c in range(tm // tk):
                blk = sub * (tm // tk) + c
                vt_ref[hh, blk, 0:LANES, :] = v[c * tk:(c + 1) * tk, hh * LANES:(hh + 1) * LANES].T.astype(bf16)
                vt_ref[hh, blk, LANES:VT_ROWS, :] = ones_rows
        qn_ref[rows, :] = (_half_lane_rmsnorm(proj(off_q, qk_width), n_tiles, tm) * qg_ref[...]).astype(bf16)
        kn_ref[rows, :] = (_half_lane_rmsnorm(proj(off_k, qk_width), n_tiles, tm) * kg_ref[...]).astype(bf16)

        ext = jnp.concatenate([halo, u], axis=0)
        halo = u[tm - POOL_HALO:, :]
        pos = (j * subs + sub) * tm + lax.broadcasted_iota(jnp.int32, (tm, 1), 0)
        ds = []
        for g, w in enumerate(POOL_WINDOWS):
            acc = ext[:, g * group:(g + 1) * group]
            shift = 1
            while shift < w:
                acc = acc + pltpu.roll(acc, shift, 0)
                shift *= 2
            wsum = acc[POOL_HALO:, :]
            cnt = jnp.minimum(pos + 1, w).astype(f32)
            ds.append((wsum / cnt - u[:, g * group:(g + 1) * group]).astype(bf16))
        pool_out = _dot(jnp.concatenate(ds, axis=-1), wfold_ref[...])

        for c in range(2):
            cols = slice(c * half, (c + 1) * half)
            gp = jax.nn.sigmoid(proj(off_gp + c * half, half))
            pg_ref[rows, cols] = (gp * pool_out[:, cols]).astype(bf16)
            ga_ref[rows, cols] = jax.nn.sigmoid(proj(off_ga + c * half, half)).astype(bf16)
    prev_ref[...] = halo


def _in_proj(x, g1, w_in, pool_w, pool_scale, w_pool_up, qg, kg):
    B, S, D = x.shape
    subs = IN_PROJ_SUBTILES if S % (IN_PROJ_SUBTILES * ROW_TILE) == 0 else 1
    tm = min(ROW_TILE, S)
    pool_width = w_pool_up.shape[0]
    qk_width = qg.shape[1]
    attn_width = qk_width
    in_width = w_in.shape[1]
    const2 = lambda b, j: (0, 0)
    row = lambda b, j: (b, j, 0)
    tk = min(ATTN_TILE, S)
    n_heads = attn_width // LANES
    kern = functools.partial(_in_proj_kernel, tm=tm, subs=subs, tk=tk, d_model=D, pool_width=pool_width,
                             qk_width=qk_width, attn_width=attn_width)
    tb = subs * tm
    return pl.pallas_call(
        kern,
        grid=(B, S // tb),
        in_specs=[
            pl.BlockSpec((None, tb, D), row),
            pl.BlockSpec((1, D), const2),
            pl.BlockSpec((D, in_width), const2, pipeline_mode=pl.Buffered(1)),
            pl.BlockSpec(pool_w.shape, lambda b, j: (0, 0, 0)),
            pl.BlockSpec((1, pool_width), const2),
            pl.BlockSpec((pool_width, D), const2),
            pl.BlockSpec((1, qk_width), const2),
            pl.BlockSpec((1, qk_width), const2),
        ],
        out_specs=[
            pl.BlockSpec((None, tb, D), row),
            pl.BlockSpec((None, tb, D), row),
            pl.BlockSpec((None, tb, qk_width), row),
            pl.BlockSpec((None, tb, qk_width), row),
            pl.BlockSpec((None, n_heads, tb // tk, VT_ROWS, tk), lambda b, j: (b, 0, j, 0, 0)),
        ],
        out_shape=[
            jax.ShapeDtypeStruct((B, S, D), bf16),
            jax.ShapeDtypeStruct((B, S, D), bf16),
            jax.ShapeDtypeStruct((B, S, qk_width), bf16),
            jax.ShapeDtypeStruct((B, S, qk_width), bf16),
            jax.ShapeDtypeStruct((B, n_heads, S // tk, VT_ROWS, tk), bf16),
        ],
        scratch_shapes=[pltpu.VMEM((POOL_HALO, pool_width), f32), pltpu.VMEM((pool_width, D), bf16)],
        compiler_params=pltpu.CompilerParams(dimension_semantics=("arbitrary", "arbitrary"),
                                             vmem_limit_bytes=VMEM_LIMIT),
        name="in_proj",
    )(x, g1, w_in, pool_w, pool_scale, w_pool_up, qg, kg)


def _attn_kernel(slopes_ref, q_ref, k_ref, vt_ref, fq_ref, fk_ref, lq1_ref, lk1_ref, lq2_ref, lk2_ref, sg_ref,
                 o_ref, qa_ref, sa_ref, sb_ref, pa_ref, pb_ref, m_ref, alpha_ref, acc_ref, *,
                 tq, tk, heads, lam_init):
    hg = pl.program_id(1)
    i = pl.program_id(2)
    fk = fk_ref[...]
    lane = lax.broadcasted_iota(jnp.int32, (tk, LANES), 1)
    first = lane < (LANES // 2)
    hs = range(heads)

    buf_a = (sa_ref, pa_ref)
    buf_b = (sb_ref, pb_ref)

    def scores(g, n, buf):
        kb = k_ref[pl.ds(pl.multiple_of(n * tk, tk), tk), g * LANES:(g + 1) * LANES]
        buf[0][g] = _dot_nt(jnp.concatenate([kb, fk], axis=1), qa_ref[g])

    def softmax(g, n, buf, first_half_mask=None):
        s_ref, p_ref = buf
        c = -slopes_ref[hg * heads + g] * (i * tq - n * tk).astype(f32)
        s = s_ref[g]
        if first_half_mask is not None:
            s = jnp.concatenate([jnp.where(first_half_mask, s[:, :tq], -jnp.inf), s[:, tq:]], axis=1)
        m_old = m_ref[g]
        m_new = jnp.maximum(m_old, jnp.max(s, axis=0, keepdims=True) + c)
        p_ref[g] = jnp.exp2(s - (m_new - c)).astype(bf16)
        m_ref[g] = m_new
        return jnp.exp2(m_old - m_new)

    def pv(g, n, buf, alpha):
        acc_ref[g] = alpha * acc_ref[g] + _dot(vt_ref[g, jnp.maximum(n, 0)], buf[1][g])

    m_ref[...] = jnp.full(m_ref.shape, -jnp.inf, f32)
    alpha_ref[...] = jnp.ones_like(alpha_ref)
    acc_ref[...] = jnp.zeros_like(acc_ref)
    pb_ref[...] = jnp.zeros_like(pb_ref)
    zero = jnp.zeros((tk, LANES), bf16)
    for g in hs:
        parts = []
        for half in range(2):
            q = q_ref[half * tk:(half + 1) * tk, g * LANES:(g + 1) * LANES]
            parts += [jnp.where(first, q, zero), jnp.where(first, zero, q)]
        qa_ref[g] = jnp.concatenate([jnp.concatenate(parts, axis=0), fq_ref[g]], axis=1)
    for g in hs:
        scores(g, 0, buf_a)

    def pair(t, carry):
        n = 2 * t
        a_prev = [alpha_ref[g] for g in hs]
        a_even = [None] * heads
        for g in hs:
            pv(g, n - 1, buf_b, a_prev[g])
            a_even[g] = softmax(g, n, buf_a)
            scores(g, n + 1, buf_b)
        for g in hs:
            pv(g, n, buf_a, a_even[g])
            alpha_ref[g] = softmax(g, n + 1, buf_b)
            scores(g, n + 2, buf_a)
        return carry

    lax.fori_loop(0, i, pair, 0)
    n = 2 * i
    lam = (jnp.exp(jnp.sum(lq1_ref[...] * lk1_ref[...], keepdims=True))
           - jnp.exp(jnp.sum(lq2_ref[...] * lk2_ref[...], keepdims=True)) + lam_init)
    late = slice(tq, 2 * tq)
    kk = lax.broadcasted_iota(jnp.int32, (tk, tq), 0)
    qq = lax.broadcasted_iota(jnp.int32, (tk, tq), 1)
    tri = kk <= jnp.where(qq >= tk, qq - tk, qq)
    for g in hs:
        a_even = softmax(g, n, buf_a, first_half_mask=tri)
        pv(g, n - 1, buf_b, alpha_ref[g])
        pv(g, n, buf_a, a_even)
        kb = k_ref[pl.ds(pl.multiple_of((n + 1) * tk, tk), tk), g * LANES:(g + 1) * LANES]
        s = _dot_nt(jnp.concatenate([kb, fk], axis=1), qa_ref[g, late, :])
        s = jnp.where(tri, s, -jnp.inf)
        c = slopes_ref[hg * heads + g] * float(tk)
        m_old = m_ref[g, :, late]
        m_new = jnp.maximum(m_old, jnp.max(s, axis=0, keepdims=True) + c)
        p = jnp.exp2(s - (m_new - c)).astype(bf16)
        acc = acc_ref[g]
        acc_late = jnp.exp2(m_old - m_new) * acc[:, late] + _dot(vt_ref[g, n + 1], p)
        acc = jnp.concatenate([acc[:, :tq], acc_late], axis=1)
        o_all = acc[0:LANES] / acc[LANES:LANES + 1]
        o = jnp.concatenate([o_all[:, 0:tk] - lam * o_all[:, tk:tq],
                             o_all[:, tq:tq + tk] - lam * o_all[:, tq + tk:]], axis=1)
        ms = jnp.mean(o * o, axis=0, keepdims=True)
        on = o * lax.rsqrt(ms + EPS) * sg_ref[...] * (1.0 - lam_init)
        o_ref[:, g * LANES:(g + 1) * LANES] = on.T.astype(bf16)


def _split_bf16(x, pieces=3):
    out = []
    for _ in range(pieces):
        p = x.astype(bf16)
        out.append(p)
        x = x - p.astype(f32)
    return out


def _alibi_features(slopes2, tq, tk):
    assert tk <= 256
    assert tq == 2 * tk
    n_heads = slopes2.shape[0]
    ones = jnp.ones((tk, 1), bf16)
    krel = jnp.arange(tk, dtype=f32).astype(bf16)[:, None]
    fk = jnp.concatenate([krel] * 3 + [ones] * 3 + [jnp.zeros((tk, LANES - 6), bf16)], axis=1)
    first, last = jnp.arange(tk, dtype=f32), jnp.arange(tk, tq, dtype=f32)
    qrel = jnp.concatenate([first, first, last, last])
    a = _split_bf16(slopes2)
    b = _split_bf16(-slopes2[:, None] * qrel[None, :])
    cols = [jnp.broadcast_to(p[:, None, None], (n_heads, 2 * tq, 1)) for p in a] + [p[:, :, None] for p in b]
    fq = jnp.concatenate(cols + [jnp.zeros((n_heads, 2 * tq, LANES - 6), bf16)], axis=2)
    return fq, fk


def _attention(qn, kn, vt, fq, fk, slopes, lq1, lk1, lq2, lk2, subln_col, lam_init):
    B, S, _ = qn.shape
    tk = fk.shape[0]
    tq = fq.shape[1] // 2
    assert tq == 2 * tk
    nkv = S // tk
    const2 = lambda b, h, i, *_: (0, 0)
    G = ATTN_HEADS_PER_STEP
    kern = functools.partial(_attn_kernel, tq=tq, tk=tk, heads=G, lam_init=lam_init)
    hd = lq1.shape[1]
    return pl.pallas_call(
        kern,
        grid_spec=pltpu.PrefetchScalarGridSpec(
            num_scalar_prefetch=1,
            grid=(B, N_HEADS // G, S // tq),
            in_specs=[
                pl.BlockSpec((None, tq, G * LANES), lambda b, h, i, *_: (b, i, h)),
                pl.BlockSpec((None, S, G * LANES), lambda b, h, i, *_: (b, 0, h)),
                pl.BlockSpec((None, G, nkv, VT_ROWS, tk), lambda b, h, i, *_: (b, h, 0, 0, 0)),
                pl.BlockSpec((G, 2 * tq, LANES), lambda b, h, i, *_: (h, 0, 0)),
                pl.BlockSpec((tk, LANES), const2),
                pl.BlockSpec((1, hd), const2),
                pl.BlockSpec((1, hd), const2),
                pl.BlockSpec((1, hd), const2),
                pl.BlockSpec((1, hd), const2),
                pl.BlockSpec((LANES, 1), const2),
            ],
            out_specs=pl.BlockSpec((None, tq, G * LANES), lambda b, h, i, *_: (b, i, h)),
            scratch_shapes=[pltpu.VMEM((G, 2 * tq, 2 * LANES), bf16),
                            pltpu.VMEM((G, tk, 2 * tq), f32), pltpu.VMEM((G, tk, 2 * tq), f32),
                            pltpu.VMEM((G, tk, 2 * tq), bf16), pltpu.VMEM((G, tk, 2 * tq), bf16),
                            pltpu.VMEM((G, 1, 2 * tq), f32), pltpu.VMEM((G, 1, 2 * tq), f32),
                            pltpu.VMEM((G, VT_ROWS, 2 * tq), f32)],
        ),
        out_shape=jax.ShapeDtypeStruct((B, S, N_HEADS * LANES), bf16),
        compiler_params=pltpu.CompilerParams(dimension_semantics=("arbitrary", "arbitrary", "arbitrary"),
                                             vmem_limit_bytes=VMEM_LIMIT),
        name="diff_attn",
    )(slopes, qn, kn, vt, fq, fk, lq1, lk1, lq2, lk2, subln_col)


def _merge_kernel(pg_ref, ga_ref, o_ref, x_ref, wau_ref, wout_ref, g2_ref, wr_ref, br_ref,
                  x1_ref, hn_ref, eid_ref, wt_ref, rank_ref, cnt_ref, base_ref, *, tm, subs, d_model):
    step = pl.program_id(0)

    @pl.when(step == 0)
    def _():
        base_ref[...] = jnp.zeros_like(base_ref)

    a = lax.broadcasted_iota(jnp.int32, (tm, tm), 0)
    b = lax.broadcasted_iota(jnp.int32, (tm, tm), 1)
    upper = jnp.where(a <= b, 1.0, 0.0).astype(bf16)
    base = base_ref[...]
    all_logits = []
    for sub in range(subs):
        rows = slice(sub * tm, (sub + 1) * tm)
        attn_out = _dot(o_ref[rows, :], wau_ref[...])
        merged = pg_ref[rows, :].astype(f32) + ga_ref[rows, :].astype(f32) * attn_out
        x1 = x_ref[rows, :] + _dot(merged.astype(bf16), wout_ref[...])
        x1_ref[rows, :] = x1
        ms = jnp.mean(x1 * x1, axis=-1, keepdims=True)
        hn = (x1 * lax.rsqrt(ms + EPS) * g2_ref[...]).astype(bf16)
        hn_ref[rows, :] = hn

        all_logits.append(_dot_nt(wr_ref[...], hn) + br_ref[...])

    for sub in range(subs):
        rows = slice(sub * tm, (sub + 1) * tm)
        logits = all_logits[sub]
        lg = logits[0:N_GROUPS]
        gmax = jnp.max(lg, axis=0, keepdims=True)
        p_top = 1.0 / jnp.sum(jnp.exp(lg - gmax), axis=0, keepdims=True)
        grow = lax.broadcasted_iota(jnp.int32, lg.shape, 0).astype(f32)
        g_idx = jnp.min(jnp.where(lg == gmax, grow, float(N_GROUPS)), axis=0, keepdims=True)

        sel = jnp.zeros((EXPERTS_PER_GROUP, tm), f32)
        for g in range(N_GROUPS):
            le_g = logits[SUBLANES + g * EXPERTS_PER_GROUP:SUBLANES + (g + 1) * EXPERTS_PER_GROUP]
            sel = jnp.where(g_idx == float(g), le_g, sel)
        erow = lax.broadcasted_iota(jnp.int32, sel.shape, 0).astype(f32)
        e1 = jnp.max(sel, axis=0, keepdims=True)
        i1 = jnp.min(jnp.where(sel == e1, erow, float(EXPERTS_PER_GROUP)), axis=0, keepdims=True)
        sel2 = jnp.where(erow == i1, -jnp.inf, sel)
        e2 = jnp.max(sel2, axis=0, keepdims=True)
        i2 = jnp.min(jnp.where(sel2 == e2, erow, float(EXPERTS_PER_GROUP)), axis=0, keepdims=True)
        r = jnp.exp(e2 - e1)
        w1 = p_top / (1.0 + r)
        w2 = p_top * r / (1.0 + r)
        eid1 = g_idx * float(EXPERTS_PER_GROUP) + i1
        eid2 = g_idx * float(EXPERTS_PER_GROUP) + i2
        eid_ref[:, rows] = jnp.concatenate([eid1, eid2], axis=0).astype(jnp.int32)
        wt_ref[:, rows] = jnp.concatenate([w1, w2], axis=0)

        xrow = lax.broadcasted_iota(jnp.int32, (N_EXPERTS, tm), 0).astype(f32)
        oh1 = jnp.where(xrow == eid1, 1.0, 0.0)
        oh2 = jnp.where(xrow == eid2, 1.0, 0.0)
        oh = oh1 + oh2
        before = _dot(oh.astype(bf16), upper) + base - 1.0
        rank1 = jnp.sum(oh1 * before, axis=0, keepdims=True)
        rank2 = jnp.sum(oh2 * before, axis=0, keepdims=True)
        rank_ref[:, rows] = jnp.concatenate([rank1, rank2], axis=0).astype(jnp.int32)
        base = base + jnp.sum(oh, axis=1, keepdims=True)
    base_ref[...] = base
    cnt_ref[...] = jnp.broadcast_to(base, cnt_ref.shape)


def _merge(pg, ga, o, x, w_attn_up, w_out, g2, wr, br):
    N, D = x.shape
    tm = min(ROW_TILE, N)
    subs = MERGE_SUBTILES if N % (MERGE_SUBTILES * tm) == 0 else 1
    tb = subs * tm
    aw = o.shape[1]
    const2 = lambda i: (0, 0)
    row = lambda i: (i, 0)
    colblk = lambda i: (0, i)
    kern = functools.partial(_merge_kernel, tm=tm, subs=subs, d_model=D)
    return pl.pallas_call(
        kern,
        grid=(N // tb,),
        in_specs=[
            pl.BlockSpec((tb, D), row),
            pl.BlockSpec((tb, D), row),
            pl.BlockSpec((tb, aw), row),
            pl.BlockSpec((tb, D), row),
            pl.BlockSpec((aw, D), const2),
            pl.BlockSpec((D, D), const2),
            pl.BlockSpec((1, D), const2),
            pl.BlockSpec((ROUTER_ROWS, D), const2),
            pl.BlockSpec((ROUTER_ROWS, 1), const2),
        ],
        out_specs=[
            pl.BlockSpec((tb, D), row),
            pl.BlockSpec((tb, D), row),
            pl.BlockSpec((TOP_K, tb), colblk),
            pl.BlockSpec((TOP_K, tb), colblk),
            pl.BlockSpec((TOP_K, tb), colblk),
            pl.BlockSpec((N_EXPERTS, LANES), const2),
        ],
        out_shape=[
            jax.ShapeDtypeStruct((N, D), f32),
            jax.ShapeDtypeStruct((N, D), bf16),
            jax.ShapeDtypeStruct((TOP_K, N), jnp.int32),
            jax.ShapeDtypeStruct((TOP_K, N), f32),
            jax.ShapeDtypeStruct((TOP_K, N), jnp.int32),
            jax.ShapeDtypeStruct((N_EXPERTS, LANES), f32),
        ],
        scratch_shapes=[pltpu.VMEM((N_EXPERTS, 1), f32)],
        compiler_params=pltpu.CompilerParams(dimension_semantics=("arbitrary",), vmem_limit_bytes=VMEM_LIMIT),
        name="merge_router",
    )(pg, ga, o, x, w_attn_up, w_out, g2, wr, br)


def _slots_kernel(pstart_ref, eid_ref, rank_ref, dest_ref):
    eid = eid_ref[...]
    start = jnp.zeros_like(eid)
    for e in range(N_EXPERTS):
        start = jnp.where(eid == e, pstart_ref[e], start)
    dest_ref[...] = start + rank_ref[...]


def _slots(pstart, eid, rank):
    k, n = eid.shape
    tn = min(n, 8192)
    blk = pl.BlockSpec((k, tn), lambda i, *_: (0, i))
    return pl.pallas_call(
        _slots_kernel,
        grid_spec=pltpu.PrefetchScalarGridSpec(num_scalar_prefetch=1, grid=(n // tn,), in_specs=[blk, blk],
                                               out_specs=blk),
        out_shape=jax.ShapeDtypeStruct((k, n), jnp.int32),
        name="slots",
    )(pstart, eid, rank)


def _row(ref, r):
    start = r * SUBLANES if isinstance(r, int) else pl.multiple_of(r * SUBLANES, SUBLANES)
    return ref.at[pl.ds(start, SUBLANES), :]


def _wait_bytes_of(ref_like, any_hbm, sem):
    n = ref_like.shape[0]
    pltpu.make_async_copy(any_hbm.at[pl.ds(0, n), :], any_hbm.at[pl.ds(0, n), :], sem).wait()


STAGES = 3


def _dispatch_kernel(pstart_ref, pend_ref, count_ref, dest_ref, hn_ref, xs_hbm, zero_ref, stage, out_sem, zsem, *,
                     tt, rows, n_blocks, d_model):
    i = pl.program_id(0)
    n = pl.num_programs(0)

    def fill_unwritten(start):
        def piece(n_rows, first_row):
            cp = pltpu.make_async_copy(zero_ref.at[pl.ds(0, n_rows * SUBLANES), :],
                                       xs_hbm.at[pl.ds(pl.multiple_of(first_row * SUBLANES, SUBLANES),
                                                       n_rows * SUBLANES), :], zsem)
            if start:
                cp.start()
            else:
                cp.wait()

        for e in range(N_EXPERTS):
            row = pstart_ref[e] + count_ref[e]
            n_pad = pend_ref[e] - row
            bit = rows // 2
            while bit >= 1:
                pl.when((n_pad & bit) != 0)(functools.partial(piece, bit, row))
                row = row + (n_pad & bit)
                bit //= 2

        def block(b, carry):
            piece(rows, b * rows)
            return carry
        lax.fori_loop(pend_ref[N_EXPERTS - 1] // rows, n_blocks, block, 0)

    @pl.when(i == 0)
    def _():
        zero_ref[...] = jnp.zeros_like(zero_ref)
        fill_unwritten(start=True)

    def wait_rows_of(step):
        for _ in range(TOP_K):
            _wait_bytes_of(stage.at[0], xs_hbm, out_sem.at[step % STAGES])

    @pl.when(i >= STAGES - 1)
    def _():
        wait_rows_of(i - (STAGES - 1))

    slot = i % STAGES
    for s in range(d_model // LANES):
        stage[slot, pl.ds(s, tt, stride=SUBLANES), :] = hn_ref[:, s * LANES:(s + 1) * LANES].astype(f32)

    def body(t, carry):
        for k in range(TOP_K):
            pltpu.make_async_copy(_row(stage.at[slot], t), _row(xs_hbm, dest_ref[0, k * tt + t]),
                                  out_sem.at[slot]).start(priority=k)
        return carry
    lax.fori_loop(0, tt, body, 0, unroll=8)

    @pl.when(i == n - 1)
    def _():
        for back in range(STAGES - 2, -1, -1):
            @pl.when(i - back >= 0)
            def _():
                wait_rows_of(i - back)
        fill_unwritten(start=False)


def _dispatch(pstart, pend, counts, dest_blocks, hn, n_slots):
    n, _, width = dest_blocks.shape
    tt = width // TOP_K
    rows = EXPERT_ROWS
    D = hn.shape[1]
    kern = functools.partial(_dispatch_kernel, tt=tt, rows=rows, n_blocks=n_slots // rows, d_model=D)
    smem_blk = pl.BlockSpec((None, 1, width), lambda i, *_: (i, 0, 0), memory_space=pltpu.SMEM)
    return pl.pallas_call(
        kern,
        grid_spec=pltpu.PrefetchScalarGridSpec(
            num_scalar_prefetch=3,
            grid=(n,),
            in_specs=[smem_blk, pl.BlockSpec((tt, D), lambda i, *_: (i, 0))],
            out_specs=pl.BlockSpec(memory_space=pl.ANY),
            scratch_shapes=[pltpu.VMEM((rows * SUBLANES, LANES), f32),
                            pltpu.VMEM((STAGES, tt * SUBLANES, LANES), f32),
                            pltpu.SemaphoreType.DMA((STAGES,)),
                            pltpu.SemaphoreType.DMA(())],
        ),
        out_shape=jax.ShapeDtypeStruct((n_slots * SUBLANES, LANES), f32),
        compiler_params=pltpu.CompilerParams(dimension_semantics=("arbitrary",), vmem_limit_bytes=VMEM_LIMIT),
        name="dispatch",
    )(pstart, pend, counts, dest_blocks, hn)


def _rows_from_token_major(buf, n_rows, n_tiles):
    return jnp.concatenate([buf[pl.ds(s, n_rows, stride=SUBLANES), :] for s in range(n_tiles)], axis=-1)


def _expert_kernel(beid_ref, nbu_ref, next_ref, slot_ref, xs_ref, wg_hbm, wu_hbm, wd_hbm, ys_ref,
                   wg_f, wu_f, wd_f, wsem, wg_bf, wu_bf, wd_bf, *, rows, d_model):
    b = pl.program_id(0)

    def weight_copies(e, s):
        return [pltpu.make_async_copy(src.at[e], dst.at[s], wsem.at[s])
                for src, dst in ((wg_hbm, wg_f), (wu_hbm, wu_f), (wd_hbm, wd_f))]

    s = slot_ref[b]

    @pl.when(b == 0)
    def _():
        for cp in weight_copies(beid_ref[0], s):
            cp.start()

    first_of_expert = jnp.logical_or(b == 0, beid_ref[b] != beid_ref[jnp.maximum(b - 1, 0)])

    @pl.when(jnp.logical_and(b < nbu_ref[0], first_of_expert))
    def _():
        for cp in weight_copies(beid_ref[b], s):
            cp.wait()

        @pl.when(next_ref[b] >= 0)
        def _():
            for cp in weight_copies(next_ref[b], 1 - s):
                cp.start()

        wg_bf[...] = wg_f[s].astype(bf16)
        wu_bf[...] = wu_f[s].astype(bf16)
        wd_bf[...] = wd_f[s].astype(bf16)

    n_tiles = d_model // LANES

    @pl.when(b < nbu_ref[0])
    def _():
        xb = _rows_from_token_major(xs_ref, rows, n_tiles).astype(bf16)
        hdn = jax.nn.silu(_dot(xb, wg_bf[...])) * _dot(xb, wu_bf[...])
        y = _dot(hdn.astype(bf16), wd_bf[...])
        for s in range(n_tiles):
            ys_ref[pl.ds(s, rows, stride=SUBLANES), :] = y[:, s * LANES:(s + 1) * LANES]

    @pl.when(b >= nbu_ref[0])
    def _():
        ys_ref[...] = jnp.zeros_like(ys_ref)


def _experts(block_eid, n_used, next_eid, w_slot, xs, w_gate, w_up, w_down):
    n_blocks = block_eid.shape[0]
    rows = EXPERT_ROWS
    E, D, DE = w_gate.shape
    kern = functools.partial(_expert_kernel, rows=rows, d_model=D)
    xs_idx = lambda b, eid, nbu, *_: (jnp.minimum(b, nbu[0] - 1), 0)
    hbm = pl.BlockSpec(memory_space=pl.ANY)
    return pl.pallas_call(
        kern,
        grid_spec=pltpu.PrefetchScalarGridSpec(
            num_scalar_prefetch=4,
            grid=(n_blocks,),
            in_specs=[pl.BlockSpec((rows * SUBLANES, LANES), xs_idx), hbm, hbm, hbm],
            out_specs=pl.BlockSpec((rows * SUBLANES, LANES), lambda b, *_: (b, 0)),
            scratch_shapes=[
                pltpu.VMEM((2, D, DE), f32),
                pltpu.VMEM((2, D, DE), f32),
                pltpu.VMEM((2, DE, D), f32),
                pltpu.SemaphoreType.DMA((2,)),
                pltpu.VMEM((D, DE), bf16),
                pltpu.VMEM((D, DE), bf16),
                pltpu.VMEM((DE, D), bf16),
            ],
        ),
        out_shape=jax.ShapeDtypeStruct((n_blocks * rows * SUBLANES, LANES), f32),
        compiler_params=pltpu.CompilerParams(dimension_semantics=("arbitrary",), vmem_limit_bytes=VMEM_LIMIT),
        name="experts",
    )(block_eid, n_used, next_eid, w_slot, xs, w_gate, w_up, w_down)


def _combine_kernel(dest_ref, destn_ref, x1_ref, wt_ref, ys_hbm, out_ref, buf, sem, *, te, d_model):
    i = pl.program_id(0)
    n = pl.num_programs(0)
    slot = i % 2

    def start_gather(d_ref, dst, dsem):
        for t in range(te):
            for k in range(TOP_K):
                r = k * te + t
                pltpu.make_async_copy(_row(ys_hbm, d_ref[0, r]), _row(dst, r), dsem).start(priority=k)

    @pl.when(i == 0)
    def _():
        start_gather(dest_ref, buf.at[0], sem.at[0])

    @pl.when(i + 1 < n)
    def _():
        start_gather(destn_ref, buf.at[1 - slot], sem.at[1 - slot])

    _wait_bytes_of(buf.at[slot], ys_hbm, sem.at[slot])
    n_tiles = d_model // LANES
    both = _rows_from_token_major(buf.at[slot], TOP_K * te, n_tiles)
    wt = jnp.concatenate([wt_ref[...], jnp.zeros((SUBLANES - TOP_K, te), f32)], axis=0).T
    out_ref[...] = x1_ref[...] + (wt[:, 0:1] * both[:te] + wt[:, 1:2] * both[te:])


def _combine(dest_blocks, x1, wt_rows, ys):
    N, D = x1.shape
    n, _, width = dest_blocks.shape
    te = width // TOP_K
    kern = functools.partial(_combine_kernel, te=te, d_model=D)
    cur = pl.BlockSpec((None, 1, width), lambda i: (i, 0, 0), memory_space=pltpu.SMEM)
    nxt = pl.BlockSpec((None, 1, width), lambda i: (jnp.minimum(i + 1, n - 1), 0, 0), memory_space=pltpu.SMEM)
    return pl.pallas_call(
        kern,
        grid=(n,),
        in_specs=[
            cur, nxt,
            pl.BlockSpec((te, D), lambda i: (i, 0)),
            pl.BlockSpec((TOP_K, te), lambda i: (0, i)),
            pl.BlockSpec(memory_space=pl.ANY),
        ],
        out_specs=pl.BlockSpec((te, D), lambda i: (i, 0)),
        scratch_shapes=[pltpu.VMEM((2, TOP_K * te * SUBLANES, LANES), f32), pltpu.SemaphoreType.DMA((2,))],
        out_shape=jax.ShapeDtypeStruct((N, D), f32),
        compiler_params=pltpu.CompilerParams(dimension_semantics=("arbitrary",), vmem_limit_bytes=VMEM_LIMIT),
        name="combine",
    )(dest_blocks, dest_blocks, x1, wt_rows, ys)


def _token_blocks(a, tt):
    k, n = a.shape
    return a.reshape(k, n // tt, tt).transpose(1, 0, 2).reshape(n // tt, 1, k * tt)


def _layer(x, l, norm1_g, w_in, pool_w, pool_scale, w_pool_up, q_norm_g, k_norm_g, lambda_q1, lambda_k1,
           lambda_q2, lambda_k2, subln_g, w_attn_up, w_out, norm2_g, w_router_group, b_router_group,
           w_router_expert, b_router_expert, w_expert_gate, w_expert_up, w_expert_down):
    B, S, D = x.shape
    N = B * S
    head_dim = q_norm_g.shape[0]
    lam_init = 0.8 - 0.6 * math.exp(-0.3 * l)
    reps = (N_HEADS * 2 * head_dim) // head_dim

    qg = (jnp.tile(q_norm_g, reps) * (head_dim ** -0.5 * LOG2E))[None, :]
    kg = jnp.tile(k_norm_g, reps)[None, :]
    pg, ga, qn, kn, vt = _in_proj(x, norm1_g[None, :], w_in.astype(bf16), pool_w.astype(bf16),
                                  pool_scale[None, :], w_pool_up.astype(bf16), qg, kg)

    tk = min(ATTN_TILE, S)
    slopes2 = jnp.asarray([2.0 ** (-8.0 * (h + 1) / N_HEADS) * LOG2E for h in range(N_HEADS)], f32)
    fq, fk = _alibi_features(slopes2, 2 * tk, tk)
    o = _attention(qn, kn, vt, fq, fk, slopes2, lambda_q1[None, :], lambda_k1[None, :], lambda_q2[None, :],
                   lambda_k2[None, :], subln_g[:, None], lam_init)

    wr = jnp.zeros((ROUTER_ROWS, D), f32)
    wr = wr.at[:N_GROUPS].set(w_router_group.T).at[SUBLANES:].set(w_router_expert.T).astype(bf16)
    br = jnp.zeros((ROUTER_ROWS, 1), f32)
    br = br.at[:N_GROUPS, 0].set(b_router_group).at[SUBLANES:, 0].set(b_router_expert)
    x1, hn, eid, wts, rank, cnt = _merge(pg.reshape(N, D), ga.reshape(N, D), o.reshape(N, -1),
                                           x.reshape(N, D), w_attn_up.astype(bf16), w_out.astype(bf16),
                                           norm2_g[None, :], wr, br)

    R = EXPERT_ROWS
    counts = cnt[:, 0].astype(jnp.int32)
    padded = (counts + R - 1) // R * R
    pend = jnp.cumsum(padded).astype(jnp.int32)
    pstart = pend - padded
    n_blocks = -(-(N * TOP_K) // R) + N_EXPERTS
    starts = jnp.arange(n_blocks, dtype=jnp.int32) * R
    block_eid = jnp.minimum(jnp.sum((pend[None, :] <= starts[:, None]).astype(jnp.int32), axis=1), N_EXPERTS - 1)
    n_used = pend[-1:] // R
    ids = jnp.arange(N_EXPERTS, dtype=jnp.int32)
    present = padded > 0
    later = jnp.where(jnp.logical_and(present[None, :], ids[None, :] > ids[:, None]), ids[None, :], N_EXPERTS)
    next_present = jnp.min(later, axis=1)
    next_present = jnp.where(next_present == N_EXPERTS, -1, next_present)
    order = jnp.cumsum(present.astype(jnp.int32)) - 1
    onehot = (block_eid[:, None] == ids[None, :]).astype(jnp.int32)
    next_eid = jnp.sum(onehot * next_present[None, :], axis=1)
    w_slot = jnp.sum(onehot * order[None, :], axis=1) % 2

    dest = _slots(pstart, eid, rank)
    xs = _dispatch(pstart, pend, counts, _token_blocks(dest, min(DISPATCH_TILE, N)), hn, n_blocks * R)
    ys = _experts(block_eid, n_used, next_eid, w_slot, xs, w_expert_gate, w_expert_up, w_expert_down)
    out = _combine(_token_blocks(dest, min(COMBINE_TILE, N)), x1, wts, ys)
    return out.reshape(B, S, D)


def kernel(x, norm1_g, w_in, pool_w, pool_scale, w_pool_up, q_norm_g, k_norm_g, lambda_q1, lambda_k1, lambda_q2,
           lambda_k2, subln_g, w_attn_up, w_out, norm2_g, w_router_group, b_router_group, w_router_expert,
           b_router_expert, w_expert_gate, w_expert_up, w_expert_down):
    params = (norm1_g, w_in, pool_w, pool_scale, w_pool_up, q_norm_g, k_norm_g, lambda_q1, lambda_k1, lambda_q2,
              lambda_k2, subln_g, w_attn_up, w_out, norm2_g, w_router_group, b_router_group, w_router_expert,
              b_router_expert, w_expert_gate, w_expert_up, w_expert_down)
    for l in range(norm1_g.shape[0]):
        x = _layer(x, l, *(p[l] for p in params))
    return x
```

```python
import functools
import math

import jax
import jax.numpy as jnp
from jax import lax
from jax.experimental import pallas as pl
from jax.experimental.pallas import tpu as pltpu

EPS = 1e-6
POOL_WINDOWS = (2, 4, 8, 16)
POOL_HALO = 16
N_HEADS = 4
N_GROUPS = 4
EXPERTS_PER_GROUP = 8
N_EXPERTS = N_GROUPS * EXPERTS_PER_GROUP
TOP_K = 2
LANES = 128
SUBLANES = 8
ROUTER_ROWS = 8 + N_EXPERTS
VT_ROWS = LANES + 16
LOG2E = 1.4426950408889634

ROW_TILE = 512
IN_PROJ_SUBTILES = 2
MERGE_SUBTILES = 2
ATTN_TILE = 256
ATTN_HEADS_PER_STEP = 4
EXPERT_ROWS = 1024
DISPATCH_TILE = 1024
COMBINE_TILE = 256
V7X_VMEM_BYTES = 64 * 1024 * 1024
VMEM_LIMIT = V7X_VMEM_BYTES * 13 // 16

f32 = jnp.float32
bf16 = jnp.bfloat16


def _dot(a, b):
    return jnp.dot(a, b, preferred_element_type=f32)


def _dot_nt(a, b):
    return lax.dot_general(a, b, (((1,), (1,)), ((), ())), preferred_element_type=f32)


def _half_lane_rmsnorm(t, n_tiles, rows):
    lane = lax.broadcasted_iota(jnp.int32, (rows, LANES), 1)
    lo_mask = lane < (LANES // 2)
    outs = []
    for i in range(n_tiles):
        c = t[:, i * LANES:(i + 1) * LANES]
        sq = c * c
        lo = jnp.sum(jnp.where(lo_mask, sq, 0.0), axis=-1, keepdims=True)
        hi = jnp.sum(jnp.where(lo_mask, 0.0, sq), axis=-1, keepdims=True)
        ms = jnp.where(lo_mask, lo, hi) * (2.0 / LANES)
        outs.append(c * lax.rsqrt(ms + EPS))
    return jnp.concatenate(outs, axis=-1)


def _in_proj_kernel(x_ref, g1_ref, win_ref, poolw_ref, pscale_ref, wpu_ref, qg_ref, kg_ref,
                    pg_ref, ga_ref, qn_ref, kn_ref, vt_ref, prev_ref, wfold_ref, *, tm, subs, tk, d_model, pool_width,
                    qk_width, attn_width):
    j = pl.program_id(1)
    off_q = pool_width
    off_k = off_q + qk_width
    off_v = off_k + qk_width
    off_gp = off_v + attn_width
    off_ga = off_gp + d_model
    group = pool_width // len(POOL_WINDOWS)
    half = d_model // 2
    n_tiles = qk_width // LANES

    @pl.when(jnp.logical_and(pl.program_id(0) == 0, j == 0))
    def _():
        for g in range(len(POOL_WINDOWS)):
            rows_g = slice(g * group, (g + 1) * group)
            scaled = (poolw_ref[g].astype(f32) * pscale_ref[:, rows_g]).astype(bf16)
            wfold_ref[rows_g, :] = _dot(scaled, wpu_ref[rows_g, :]).astype(bf16)

    @pl.when(j == 0)
    def _():
        prev_ref[...] = jnp.zeros_like(prev_ref)

    extra = lax.broadcasted_iota(jnp.int32, (VT_ROWS - LANES, tk), 0)
    ones_rows = jnp.where(extra == 0, 1.0, 0.0).astype(bf16)
    halo = prev_ref[...]
    for sub in range(subs):
        rows = slice(sub * tm, (sub + 1) * tm)
        x = x_ref[rows, :]
        ms = jnp.mean(x * x, axis=-1, keepdims=True)
        h = (x * lax.rsqrt(ms + EPS) * g1_ref[...]).astype(bf16)

        def proj(lo, width, h=h):
            return _dot(h, win_ref[:, lo:lo + width])

        u = proj(0, pool_width)
        v = proj(off_v, attn_width)
        for hh in range(attn_width // LANES):
            for c in range(tm // tk):
                blk = sub * (tm // tk) + c
                vt_ref[hh, blk, 0:LANES, :] = v[c * tk:(c + 1) * tk, hh * LANES:(hh + 1) * LANES].T.astype(bf16)
                vt_ref[hh, blk, LANES:VT_ROWS, :] = ones_rows
        qn_ref[rows, :] = (_half_lane_rmsnorm(proj(off_q, qk_width), n_tiles, tm) * qg_ref[...]).astype(bf16)
        kn_ref[rows, :] = (_half_lane_rmsnorm(proj(off_k, qk_width), n_tiles, tm) * kg_ref[...]).astype(bf16)

        ext = jnp.concatenate([halo, u], axis=0)
        halo = u[tm - POOL_HALO:, :]
        pos = (j * subs + sub) * tm + lax.broadcasted_iota(jnp.int32, (tm, 1), 0)
        ds = []
        for g, w in enumerate(POOL_WINDOWS):
            acc = ext[:, g * group:(g + 1) * group]
            shift = 1
            while shift < w:
                acc = acc + pltpu.roll(acc, shift, 0)
                shift *= 2
            wsum = acc[POOL_HALO:, :]
            cnt = jnp.minimum(pos + 1, w).astype(f32)
            ds.append((wsum / cnt - u[:, g * group:(g + 1) * group]).astype(bf16))
        pool_out = _dot(jnp.concatenate(ds, axis=-1), wfold_ref[...])

        for c in range(2):
            cols = slice(c * half, (c + 1) * half)
            gp = jax.nn.sigmoid(proj(off_gp + c * half, half))
            pg_ref[rows, cols] = (gp * pool_out[:, cols]).astype(bf16)
            ga_ref[rows, cols] = jax.nn.sigmoid(proj(off_ga + c * half, half)).astype(bf16)
    prev_ref[...] = halo


def _in_proj(x, g1, w_in, pool_w, pool_scale, w_pool_up, qg, kg):
    B, S, D = x.shape
    subs = IN_PROJ_SUBTILES if S % (IN_PROJ_SUBTILES * ROW_TILE) == 0 else 1
    tm = min(ROW_TILE, S)
    pool_width = w_pool_up.shape[0]
    qk_width = qg.shape[1]
    attn_width = qk_width
    in_width = w_in.shape[1]
    const2 = lambda b, j: (0, 0)
    row = lambda b, j: (b, j, 0)
    tk = min(ATTN_TILE, S)
    n_heads = attn_width // LANES
    kern = functools.partial(_in_proj_kernel, tm=tm, subs=subs, tk=tk, d_model=D, pool_width=pool_width,
                             qk_width=qk_width, attn_width=attn_width)
    tb = subs * tm
    return pl.pallas_call(
        kern,
        grid=(B, S // tb),
        in_specs=[
            pl.BlockSpec((None, tb, D), row),
            pl.BlockSpec((1, D), const2),
            pl.BlockSpec((D, in_width), const2, pipeline_mode=pl.Buffered(1)),
            pl.BlockSpec(pool_w.shape, lambda b, j: (0, 0, 0)),
            pl.BlockSpec((1, pool_width), const2),
            pl.BlockSpec((pool_width, D), const2),
            pl.BlockSpec((1, qk_width), const2),
            pl.BlockSpec((1, qk_width), const2),
        ],
        out_specs=[
            pl.BlockSpec((None, tb, D), row),
            pl.BlockSpec((None, tb, D), row),
            pl.BlockSpec((None, tb, qk_width), row),
            pl.BlockSpec((None, tb, qk_width), row),
            pl.BlockSpec((None, n_heads, tb // tk, VT_ROWS, tk), lambda b, j: (b, 0, j, 0, 0)),
        ],
        out_shape=[
            jax.ShapeDtypeStruct((B, S, D), bf16),
            jax.ShapeDtypeStruct((B, S, D), bf16),
            jax.ShapeDtypeStruct((B, S, qk_width), bf16),
            jax.ShapeDtypeStruct((B, S, qk_width), bf16),
            jax.ShapeDtypeStruct((B, n_heads, S // tk, VT_ROWS, tk), bf16),
        ],
        scratch_shapes=[pltpu.VMEM((POOL_HALO, pool_width), f32), pltpu.VMEM((pool_width, D), bf16)],
        compiler_params=pltpu.CompilerParams(dimension_semantics=("arbitrary", "arbitrary"),
                                             vmem_limit_bytes=VMEM_LIMIT),
        name="in_proj",
    )(x, g1, w_in, pool_w, pool_scale, w_pool_up, qg, kg)


def _attn_kernel(slopes_ref, q_ref, k_ref, vt_ref, fq_ref, fk_ref, lq1_ref, lk1_ref, lq2_ref, lk2_ref, sg_ref,
                 o_ref, qa_ref, sa_ref, sb_ref, pa_ref, pb_ref, m_ref, alpha_ref, acc_ref, *,
                 tq, tk, heads, lam_init):
    hg = pl.program_id(1)
    i = pl.program_id(2)
    fk = fk_ref[...]
    lane = lax.broadcasted_iota(jnp.int32, (tk, LANES), 1)
    first = lane < (LANES // 2)
    hs = range(heads)

    buf_a = (sa_ref, pa_ref)
    buf_b = (sb_ref, pb_ref)

    def scores(g, n, buf):
        kb = k_ref[pl.ds(pl.multiple_of(n * tk, tk), tk), g * LANES:(g + 1) * LANES]
        buf[0][g] = _dot_nt(jnp.concatenate([kb, fk], axis=1), qa_ref[g])

    def softmax(g, n, buf, first_half_mask=None):
        s_ref, p_ref = buf
        c = -slopes_ref[hg * heads + g] * (i * tq - n * tk).astype(f32)
        s = s_ref[g]
        if first_half_mask is not None:
            s = jnp.concatenate([jnp.where(first_half_mask, s[:, :tq], -jnp.inf), s[:, tq:]], axis=1)
        m_old = m_ref[g]
        m_new = jnp.maximum(m_old, jnp.max(s, axis=0, keepdims=True) + c)
        p_ref[g] = jnp.exp2(s - (m_new - c)).astype(bf16)
        m_ref[g] = m_new
        return jnp.exp2(m_old - m_new)

    def pv(g, n, buf, alpha):
        acc_ref[g] = alpha * acc_ref[g] + _dot(vt_ref[g, jnp.maximum(n, 0)], buf[1][g])

    m_ref[...] = jnp.full(m_ref.shape, -jnp.inf, f32)
    acc_ref[...] = jnp.zeros_like(acc_ref)
    zero = jnp.zeros((tk, LANES), bf16)
    for g in hs:
        parts = []
        for half in range(2):
            q = q_ref[half * tk:(half + 1) * tk, g * LANES:(g + 1) * LANES]
            parts += [jnp.where(first, q, zero), jnp.where(first, zero, q)]
        qa_ref[g] = jnp.concatenate([jnp.concatenate(parts, axis=0), fq_ref[g]], axis=1)
    for g in hs:
        scores(g, 0, buf_a)

    def pair(t, carry):
        n = 2 * t
        for g in hs:
            scores(g, n + 1, buf_b)
            pv(g, n, buf_a, softmax(g, n, buf_a))
        for g in hs:
            scores(g, n + 2, buf_a)
            pv(g, n + 1, buf_b, softmax(g, n + 1, buf_b))
        return carry

    lax.fori_loop(0, i, pair, 0)
    n = 2 * i
    lam = (jnp.exp(jnp.sum(lq1_ref[...] * lk1_ref[...], keepdims=True))
           - jnp.exp(jnp.sum(lq2_ref[...] * lk2_ref[...], keepdims=True)) + lam_init)
    late = slice(tq, 2 * tq)
    kk = lax.broadcasted_iota(jnp.int32, (tk, tq), 0)
    qq = lax.broadcasted_iota(jnp.int32, (tk, tq), 1)
    tri = kk <= jnp.where(qq >= tk, qq - tk, qq)
    for g in hs:
        pv(g, n, buf_a, softmax(g, n, buf_a, first_half_mask=tri))
        kb = k_ref[pl.ds(pl.multiple_of((n + 1) * tk, tk), tk), g * LANES:(g + 1) * LANES]
        s = _dot_nt(jnp.concatenate([kb, fk], axis=1), qa_ref[g, late, :])
        s = jnp.where(tri, s, -jnp.inf)
        c = slopes_ref[hg * heads + g] * float(tk)
        m_old = m_ref[g, :, late]
        m_new = jnp.maximum(m_old, jnp.max(s, axis=0, keepdims=True) + c)
        p = jnp.exp2(s - (m_new - c)).astype(bf16)
        acc = acc_ref[g]
        acc_late = jnp.exp2(m_old - m_new) * acc[:, late] + _dot(vt_ref[g, n + 1], p)
        acc = jnp.concatenate([acc[:, :tq], acc_late], axis=1)
        o_all = acc[0:LANES] / acc[LANES:LANES + 1]
        o = jnp.concatenate([o_all[:, 0:tk] - lam * o_all[:, tk:tq],
                             o_all[:, tq:tq + tk] - lam * o_all[:, tq + tk:]], axis=1)
        ms = jnp.mean(o * o, axis=0, keepdims=True)
        on = o * lax.rsqrt(ms + EPS) * sg_ref[...] * (1.0 - lam_init)
        o_ref[:, g * LANES:(g + 1) * LANES] = on.T.astype(bf16)


def _split_bf16(x, pieces=3):
    out = []
    for _ in range(pieces):
        p = x.astype(bf16)
        out.append(p)
        x = x - p.astype(f32)
    return out


def _alibi_features(slopes2, tq, tk):
    assert tk <= 256
    assert tq == 2 * tk
    n_heads = slopes2.shape[0]
    ones = jnp.ones((tk, 1), bf16)
    krel = jnp.arange(tk, dtype=f32).astype(bf16)[:, None]
    fk = jnp.concatenate([krel] * 3 + [ones] * 3 + [jnp.zeros((tk, LANES - 6), bf16)], axis=1)
    first, last = jnp.arange(tk, dtype=f32), jnp.arange(tk, tq, dtype=f32)
    qrel = jnp.concatenate([first, first, last, last])
    a = _split_bf16(slopes2)
    b = _split_bf16(-slopes2[:, None] * qrel[None, :])
    cols = [jnp.broadcast_to(p[:, None, None], (n_heads, 2 * tq, 1)) for p in a] + [p[:, :, None] for p in b]
    fq = jnp.concatenate(cols + [jnp.zeros((n_heads, 2 * tq, LANES - 6), bf16)], axis=2)
    return fq, fk


def _attention(qn, kn, vt, fq, fk, slopes, lq1, lk1, lq2, lk2, subln_col, lam_init):
    B, S, _ = qn.shape
    tk = fk.shape[0]
    tq = fq.shape[1] // 2
    assert tq == 2 * tk
    nkv = S // tk
    const2 = lambda b, h, i, *_: (0, 0)
    G = ATTN_HEADS_PER_STEP
    kern = functools.partial(_attn_kernel, tq=tq, tk=tk, heads=G, lam_init=lam_init)
    hd = lq1.shape[1]
    return pl.pallas_call(
        kern,
        grid_spec=pltpu.PrefetchScalarGridSpec(
            num_scalar_prefetch=1,
            grid=(B, N_HEADS // G, S // tq),
            in_specs=[
                pl.BlockSpec((None, tq, G * LANES), lambda b, h, i, *_: (b, i, h)),
                pl.BlockSpec((None, S, G * LANES), lambda b, h, i, *_: (b, 0, h)),
                pl.BlockSpec((None, G, nkv, VT_ROWS, tk), lambda b, h, i, *_: (b, h, 0, 0, 0)),
                pl.BlockSpec((G, 2 * tq, LANES), lambda b, h, i, *_: (h, 0, 0)),
                pl.BlockSpec((tk, LANES), const2),
                pl.BlockSpec((1, hd), const2),
                pl.BlockSpec((1, hd), const2),
                pl.BlockSpec((1, hd), const2),
                pl.BlockSpec((1, hd), const2),
                pl.BlockSpec((LANES, 1), const2),
            ],
            out_specs=pl.BlockSpec((None, tq, G * LANES), lambda b, h, i, *_: (b, i, h)),
            scratch_shapes=[pltpu.VMEM((G, 2 * tq, 2 * LANES), bf16),
                            pltpu.VMEM((G, tk, 2 * tq), f32), pltpu.VMEM((G, tk, 2 * tq), f32),
                            pltpu.VMEM((G, tk, 2 * tq), bf16), pltpu.VMEM((G, tk, 2 * tq), bf16),
                            pltpu.VMEM((G, 1, 2 * tq), f32), pltpu.VMEM((G, 1, 2 * tq), f32),
                            pltpu.VMEM((G, VT_ROWS, 2 * tq), f32)],
        ),
        out_shape=jax.ShapeDtypeStruct((B, S, N_HEADS * LANES), bf16),
        compiler_params=pltpu.CompilerParams(dimension_semantics=("arbitrary", "arbitrary", "arbitrary"),
                                             vmem_limit_bytes=VMEM_LIMIT),
        name="diff_attn",
    )(slopes, qn, kn, vt, fq, fk, lq1, lk1, lq2, lk2, subln_col)


def _merge_kernel(pg_ref, ga_ref, o_ref, x_ref, wau_ref, wout_ref, g2_ref, wr_ref, br_ref,
                  x1_ref, hn_ref, eid_ref, wt_ref, rank_ref, cnt_ref, base_ref, *, tm, subs, d_model):
    step = pl.program_id(0)

    @pl.when(step == 0)
    def _():
        base_ref[...] = jnp.zeros_like(base_ref)

    a = lax.broadcasted_iota(jnp.int32, (tm, tm), 0)
    b = lax.broadcasted_iota(jnp.int32, (tm, tm), 1)
    upper = jnp.where(a <= b, 1.0, 0.0).astype(bf16)
    base = base_ref[...]
    all_logits = []
    for sub in range(subs):
        rows = slice(sub * tm, (sub + 1) * tm)
        attn_out = _dot(o_ref[rows, :], wau_ref[...])
        merged = pg_ref[rows, :].astype(f32) + ga_ref[rows, :].astype(f32) * attn_out
        x1 = x_ref[rows, :] + _dot(merged.astype(bf16), wout_ref[...])
        x1_ref[rows, :] = x1
        ms = jnp.mean(x1 * x1, axis=-1, keepdims=True)
        hn = (x1 * lax.rsqrt(ms + EPS) * g2_ref[...]).astype(bf16)
        hn_ref[rows, :] = hn

        all_logits.append(_dot_nt(wr_ref[...], hn) + br_ref[...])

    for sub in range(subs):
        rows = slice(sub * tm, (sub + 1) * tm)
        logits = all_logits[sub]
        lg = logits[0:N_GROUPS]
        gmax = jnp.max(lg, axis=0, keepdims=True)
        p_top = 1.0 / jnp.sum(jnp.exp(lg - gmax), axis=0, keepdims=True)
        grow = lax.broadcasted_iota(jnp.int32, lg.shape, 0).astype(f32)
        g_idx = jnp.min(jnp.where(lg == gmax, grow, float(N_GROUPS)), axis=0, keepdims=True)

        sel = jnp.zeros((EXPERTS_PER_GROUP, tm), f32)
        for g in range(N_GROUPS):
            le_g = logits[SUBLANES + g * EXPERTS_PER_GROUP:SUBLANES + (g + 1) * EXPERTS_PER_GROUP]
            sel = jnp.where(g_idx == float(g), le_g, sel)
        erow = lax.broadcasted_iota(jnp.int32, sel.shape, 0).astype(f32)
        e1 = jnp.max(sel, axis=0, keepdims=True)
        i1 = jnp.min(jnp.where(sel == e1, erow, float(EXPERTS_PER_GROUP)), axis=0, keepdims=True)
        sel2 = jnp.where(erow == i1, -jnp.inf, sel)
        e2 = jnp.max(sel2, axis=0, keepdims=True)
        i2 = jnp.min(jnp.where(sel2 == e2, erow, float(EXPERTS_PER_GROUP)), axis=0, keepdims=True)
        r = jnp.exp(e2 - e1)
        w1 = p_top / (1.0 + r)
        w2 = p_top * r / (1.0 + r)
        eid1 = g_idx * float(EXPERTS_PER_GROUP) + i1
        eid2 = g_idx * float(EXPERTS_PER_GROUP) + i2
        eid_ref[:, rows] = jnp.concatenate([eid1, eid2], axis=0).astype(jnp.int32)
        wt_ref[:, rows] = jnp.concatenate([w1, w2], axis=0)

        xrow = lax.broadcasted_iota(jnp.int32, (N_EXPERTS, tm), 0).astype(f32)
        oh1 = jnp.where(xrow == eid1, 1.0, 0.0)
        oh2 = jnp.where(xrow == eid2, 1.0, 0.0)
        oh = oh1 + oh2
        before = _dot(oh.astype(bf16), upper) + base - 1.0
        rank1 = jnp.sum(oh1 * before, axis=0, keepdims=True)
        rank2 = jnp.sum(oh2 * before, axis=0, keepdims=True)
        rank_ref[:, rows] = jnp.concatenate([rank1, rank2], axis=0).astype(jnp.int32)
        base = base + jnp.sum(oh, axis=1, keepdims=True)
    base_ref[...] = base
    cnt_ref[...] = jnp.broadcast_to(base, cnt_ref.shape)


def _merge(pg, ga, o, x, w_attn_up, w_out, g2, wr, br):
    N, D = x.shape
    tm = min(ROW_TILE, N)
    subs = MERGE_SUBTILES if N % (MERGE_SUBTILES * tm) == 0 else 1
    tb = subs * tm
    aw = o.shape[1]
    const2 = lambda i: (0, 0)
    row = lambda i: (i, 0)
    colblk = lambda i: (0, i)
    kern = functools.partial(_merge_kernel, tm=tm, subs=subs, d_model=D)
    return pl.pallas_call(
        kern,
        grid=(N // tb,),
        in_specs=[
            pl.BlockSpec((tb, D), row),
            pl.BlockSpec((tb, D), row),
            pl.BlockSpec((tb, aw), row),
            pl.BlockSpec((tb, D), row),
            pl.BlockSpec((aw, D), const2),
            pl.BlockSpec((D, D), const2),
            pl.BlockSpec((1, D), const2),
            pl.BlockSpec((ROUTER_ROWS, D), const2),
            pl.BlockSpec((ROUTER_ROWS, 1), const2),
        ],
        out_specs=[
            pl.BlockSpec((tb, D), row),
            pl.BlockSpec((tb, D), row),
            pl.BlockSpec((TOP_K, tb), colblk),
            pl.BlockSpec((TOP_K, tb), colblk),
            pl.BlockSpec((TOP_K, tb), colblk),
            pl.BlockSpec((N_EXPERTS, LANES), const2),
        ],
        out_shape=[
            jax.ShapeDtypeStruct((N, D), f32),
            jax.ShapeDtypeStruct((N, D), bf16),
            jax.ShapeDtypeStruct((TOP_K, N), jnp.int32),
            jax.ShapeDtypeStruct((TOP_K, N), f32),
            jax.ShapeDtypeStruct((TOP_K, N), jnp.int32),
            jax.ShapeDtypeStruct((N_EXPERTS, LANES), f32),
        ],
        scratch_shapes=[pltpu.VMEM((N_EXPERTS, 1), f32)],
        compiler_params=pltpu.CompilerParams(dimension_semantics=("arbitrary",), vmem_limit_bytes=VMEM_LIMIT),
        name="merge_router",
    )(pg, ga, o, x, w_attn_up, w_out, g2, wr, br)


def _slots_kernel(pstart_ref, eid_ref, rank_ref, dest_ref):
    eid = eid_ref[...]
    start = jnp.zeros_like(eid)
    for e in range(N_EXPERTS):
        start = jnp.where(eid == e, pstart_ref[e], start)
    dest_ref[...] = start + rank_ref[...]


def _slots(pstart, eid, rank):
    k, n = eid.shape
    tn = min(n, 8192)
    blk = pl.BlockSpec((k, tn), lambda i, *_: (0, i))
    return pl.pallas_call(
        _slots_kernel,
        grid_spec=pltpu.PrefetchScalarGridSpec(num_scalar_prefetch=1, grid=(n // tn,), in_specs=[blk, blk],
                                               out_specs=blk),
        out_shape=jax.ShapeDtypeStruct((k, n), jnp.int32),
        name="slots",
    )(pstart, eid, rank)


def _row(ref, r):
    start = r * SUBLANES if isinstance(r, int) else pl.multiple_of(r * SUBLANES, SUBLANES)
    return ref.at[pl.ds(start, SUBLANES), :]


def _wait_bytes_of(ref_like, any_hbm, sem):
    n = ref_like.shape[0]
    pltpu.make_async_copy(any_hbm.at[pl.ds(0, n), :], any_hbm.at[pl.ds(0, n), :], sem).wait()


STAGES = 3


def _dispatch_kernel(pstart_ref, pend_ref, count_ref, dest_ref, hn_ref, xs_hbm, zero_ref, stage, out_sem, zsem, *,
                     tt, rows, n_blocks, d_model):
    i = pl.program_id(0)
    n = pl.num_programs(0)

    def fill_unwritten(start):
        def piece(n_rows, first_row):
            cp = pltpu.make_async_copy(zero_ref.at[pl.ds(0, n_rows * SUBLANES), :],
                                       xs_hbm.at[pl.ds(pl.multiple_of(first_row * SUBLANES, SUBLANES),
                                                       n_rows * SUBLANES), :], zsem)
            if start:
                cp.start()
            else:
                cp.wait()

        for e in range(N_EXPERTS):
            row = pstart_ref[e] + count_ref[e]
            n_pad = pend_ref[e] - row
            bit = rows // 2
            while bit >= 1:
                pl.when((n_pad & bit) != 0)(functools.partial(piece, bit, row))
                row = row + (n_pad & bit)
                bit //= 2

        def block(b, carry):
            piece(rows, b * rows)
            return carry
        lax.fori_loop(pend_ref[N_EXPERTS - 1] // rows, n_blocks, block, 0)

    @pl.when(i == 0)
    def _():
        zero_ref[...] = jnp.zeros_like(zero_ref)
        fill_unwritten(start=True)

    def wait_rows_of(step):
        for _ in range(TOP_K):
            _wait_bytes_of(stage.at[0], xs_hbm, out_sem.at[step % STAGES])

    @pl.when(i >= STAGES - 1)
    def _():
        wait_rows_of(i - (STAGES - 1))

    slot = i % STAGES
    for s in range(d_model // LANES):
        stage[slot, pl.ds(s, tt, stride=SUBLANES), :] = hn_ref[:, s * LANES:(s + 1) * LANES].astype(f32)

    def body(t, carry):
        for k in range(TOP_K):
            pltpu.make_async_copy(_row(stage.at[slot], t), _row(xs_hbm, dest_ref[0, k * tt + t]),
                                  out_sem.at[slot]).start(priority=k)
        return carry
    lax.fori_loop(0, tt, body, 0, unroll=8)

    @pl.when(i == n - 1)
    def _():
        for back in range(STAGES - 2, -1, -1):
            @pl.when(i - back >= 0)
            def _():
                wait_rows_of(i - back)
        fill_unwritten(start=False)


def _dispatch(pstart, pend, counts, dest_blocks, hn, n_slots):
    n, _, width = dest_blocks.shape
    tt = width // TOP_K
    rows = EXPERT_ROWS
    D = hn.shape[1]
    kern = functools.partial(_dispatch_kernel, tt=tt, rows=rows, n_blocks=n_slots // rows, d_model=D)
    smem_blk = pl.BlockSpec((None, 1, width), lambda i, *_: (i, 0, 0), memory_space=pltpu.SMEM)
    return pl.pallas_call(
        kern,
        grid_spec=pltpu.PrefetchScalarGridSpec(
            num_scalar_prefetch=3,
            grid=(n,),
            in_specs=[smem_blk, pl.BlockSpec((tt, D), lambda i, *_: (i, 0))],
            out_specs=pl.BlockSpec(memory_space=pl.ANY),
            scratch_shapes=[pltpu.VMEM((rows * SUBLANES, LANES), f32),
                            pltpu.VMEM((STAGES, tt * SUBLANES, LANES), f32),
                            pltpu.SemaphoreType.DMA((STAGES,)),
                            pltpu.SemaphoreType.DMA(())],
        ),
        out_shape=jax.ShapeDtypeStruct((n_slots * SUBLANES, LANES), f32),
        compiler_params=pltpu.CompilerParams(dimension_semantics=("arbitrary",), vmem_limit_bytes=VMEM_LIMIT),
        name="dispatch",
    )(pstart, pend, counts, dest_blocks, hn)


def _rows_from_token_major(buf, n_rows, n_tiles):
    return jnp.concatenate([buf[pl.ds(s, n_rows, stride=SUBLANES), :] for s in range(n_tiles)], axis=-1)


def _expert_kernel(beid_ref, nbu_ref, next_ref, slot_ref, xs_ref, wg_hbm, wu_hbm, wd_hbm, ys_ref,
                   wg_f, wu_f, wd_f, wsem, wg_bf, wu_bf, wd_bf, *, rows, d_model):
    b = pl.program_id(0)

    def weight_copies(e, s):
        return [pltpu.make_async_copy(src.at[e], dst.at[s], wsem.at[s])
                for src, dst in ((wg_hbm, wg_f), (wu_hbm, wu_f), (wd_hbm, wd_f))]

    s = slot_ref[b]

    @pl.when(b == 0)
    def _():
        for cp in weight_copies(beid_ref[0], s):
            cp.start()

    first_of_expert = jnp.logical_or(b == 0, beid_ref[b] != beid_ref[jnp.maximum(b - 1, 0)])

    @pl.when(jnp.logical_and(b < nbu_ref[0], first_of_expert))
    def _():
        for cp in weight_copies(beid_ref[b], s):
            cp.wait()

        @pl.when(next_ref[b] >= 0)
        def _():
            for cp in weight_copies(next_ref[b], 1 - s):
                cp.start()

        wg_bf[...] = wg_f[s].astype(bf16)
        wu_bf[...] = wu_f[s].astype(bf16)
        wd_bf[...] = wd_f[s].astype(bf16)

    n_tiles = d_model // LANES

    @pl.when(b < nbu_ref[0])
    def _():
        xb = _rows_from_token_major(xs_ref, rows, n_tiles).astype(bf16)
        hdn = jax.nn.silu(_dot(xb, wg_bf[...])) * _dot(xb, wu_bf[...])
        y = _dot(hdn.astype(bf16), wd_bf[...])
        for s in range(n_tiles):
            ys_ref[pl.ds(s, rows, stride=SUBLANES), :] = y[:, s * LANES:(s + 1) * LANES]

    @pl.when(b >= nbu_ref[0])
    def _():
        ys_ref[...] = jnp.zeros_like(ys_ref)


def _experts(block_eid, n_used, next_eid, w_slot, xs, w_gate, w_up, w_down):
    n_blocks = block_eid.shape[0]
    rows = EXPERT_ROWS
    E, D, DE = w_gate.shape
    kern = functools.partial(_expert_kernel, rows=rows, d_model=D)
    xs_idx = lambda b, eid, nbu, *_: (jnp.minimum(b, nbu[0] - 1), 0)
    hbm = pl.BlockSpec(memory_space=pl.ANY)
    return pl.pallas_call(
        kern,
        grid_spec=pltpu.PrefetchScalarGridSpec(
            num_scalar_prefetch=4,
            grid=(n_blocks,),
            in_specs=[pl.BlockSpec((rows * SUBLANES, LANES), xs_idx), hbm, hbm, hbm],
            out_specs=pl.BlockSpec((rows * SUBLANES, LANES), lambda b, *_: (b, 0)),
            scratch_shapes=[
                pltpu.VMEM((2, D, DE), f32),
                pltpu.VMEM((2, D, DE), f32),
                pltpu.VMEM((2, DE, D), f32),
                pltpu.SemaphoreType.DMA((2,)),
                pltpu.VMEM((D, DE), bf16),
                pltpu.VMEM((D, DE), bf16),
                pltpu.VMEM((DE, D), bf16),
            ],
        ),
        out_shape=jax.ShapeDtypeStruct((n_blocks * rows * SUBLANES, LANES), f32),
        compiler_params=pltpu.CompilerParams(dimension_semantics=("arbitrary",), vmem_limit_bytes=VMEM_LIMIT),
        name="experts",
    )(block_eid, n_used, next_eid, w_slot, xs, w_gate, w_up, w_down)


def _combine_kernel(dest_ref, destn_ref, x1_ref, wt_ref, ys_hbm, out_ref, buf, sem, *, te, d_model):
    i = pl.program_id(0)
    n = pl.num_programs(0)
    slot = i % 2

    def start_gather(d_ref, dst, dsem):
        for t in range(te):
            for k in range(TOP_K):
                r = k * te + t
                pltpu.make_async_copy(_row(ys_hbm, d_ref[0, r]), _row(dst, r), dsem).start(priority=k)

    @pl.when(i == 0)
    def _():
        start_gather(dest_ref, buf.at[0], sem.at[0])

    @pl.when(i + 1 < n)
    def _():
        start_gather(destn_ref, buf.at[1 - slot], sem.at[1 - slot])

    _wait_bytes_of(buf.at[slot], ys_hbm, sem.at[slot])
    n_tiles = d_model // LANES
    both = _rows_from_token_major(buf.at[slot], TOP_K * te, n_tiles)
    wt = jnp.concatenate([wt_ref[...], jnp.zeros((SUBLANES - TOP_K, te), f32)], axis=0).T
    out_ref[...] = x1_ref[...] + (wt[:, 0:1] * both[:te] + wt[:, 1:2] * both[te:])


def _combine(dest_blocks, x1, wt_rows, ys):
    N, D = x1.shape
    n, _, width = dest_blocks.shape
    te = width // TOP_K
    kern = functools.partial(_combine_kernel, te=te, d_model=D)
    cur = pl.BlockSpec((None, 1, width), lambda i: (i, 0, 0), memory_space=pltpu.SMEM)
    nxt = pl.BlockSpec((None, 1, width), lambda i: (jnp.minimum(i + 1, n - 1), 0, 0), memory_space=pltpu.SMEM)
    return pl.pallas_call(
        kern,
        grid=(n,),
        in_specs=[
            cur, nxt,
            pl.BlockSpec((te, D), lambda i: (i, 0)),
            pl.BlockSpec((TOP_K, te), lambda i: (0, i)),
            pl.BlockSpec(memory_space=pl.ANY),
        ],
        out_specs=pl.BlockSpec((te, D), lambda i: (i, 0)),
        scratch_shapes=[pltpu.VMEM((2, TOP_K * te * SUBLANES, LANES), f32), pltpu.SemaphoreType.DMA((2,))],
        out_shape=jax.ShapeDtypeStruct((N, D), f32),
        compiler_params=pltpu.CompilerParams(dimension_semantics=("arbitrary",), vmem_limit_bytes=VMEM_LIMIT),
        name="combine",
    )(dest_blocks, dest_blocks, x1, wt_rows, ys)


def _token_blocks(a, tt):
    k, n = a.shape
    return a.reshape(k, n // tt, tt).transpose(1, 0, 2).reshape(n // tt, 1, k * tt)


def _layer(x, l, norm1_g, w_in, pool_w, pool_scale, w_pool_up, q_norm_g, k_norm_g, lambda_q1, lambda_k1,
           lambda_q2, lambda_k2, subln_g, w_attn_up, w_out, norm2_g, w_router_group, b_router_group,
           w_router_expert, b_router_expert, w_expert_gate, w_expert_up, w_expert_down):
    B, S, D = x.shape
    N = B * S
    head_dim = q_norm_g.shape[0]
    lam_init = 0.8 - 0.6 * math.exp(-0.3 * l)
    reps = (N_HEADS * 2 * head_dim) // head_dim

    qg = (jnp.tile(q_norm_g, reps) * (head_dim ** -0.5 * LOG2E))[None, :]
    kg = jnp.tile(k_norm_g, reps)[None, :]
    pg, ga, qn, kn, vt = _in_proj(x, norm1_g[None, :], w_in.astype(bf16), pool_w.astype(bf16),
                                  pool_scale[None, :], w_pool_up.astype(bf16), qg, kg)

    tk = min(ATTN_TILE, S)
    slopes2 = jnp.asarray([2.0 ** (-8.0 * (h + 1) / N_HEADS) * LOG2E for h in range(N_HEADS)], f32)
    fq, fk = _alibi_features(slopes2, 2 * tk, tk)
    o = _attention(qn, kn, vt, fq, fk, slopes2, lambda_q1[None, :], lambda_k1[None, :], lambda_q2[None, :],
                   lambda_k2[None, :], subln_g[:, None], lam_init)

    wr = jnp.zeros((ROUTER_ROWS, D), f32)
    wr = wr.at[:N_GROUPS].set(w_router_group.T).at[SUBLANES:].set(w_router_expert.T).astype(bf16)
    br = jnp.zeros((ROUTER_ROWS, 1), f32)
    br = br.at[:N_GROUPS, 0].set(b_router_group).at[SUBLANES:, 0].set(b_router_expert)
    x1, hn, eid, wts, rank, cnt = _merge(pg.reshape(N, D), ga.reshape(N, D), o.reshape(N, -1),
                                           x.reshape(N, D), w_attn_up.astype(bf16), w_out.astype(bf16),
                                           norm2_g[None, :], wr, br)

    R = EXPERT_ROWS
    counts = cnt[:, 0].astype(jnp.int32)
    padded = (counts + R - 1) // R * R
    pend = jnp.cumsum(padded).astype(jnp.int32)
    pstart = pend - padded
    n_blocks = -(-(N * TOP_K) // R) + N_EXPERTS
    starts = jnp.arange(n_blocks, dtype=jnp.int32) * R
    block_eid = jnp.minimum(jnp.sum((pend[None, :] <= starts[:, None]).astype(jnp.int32), axis=1), N_EXPERTS - 1)
    n_used = pend[-1:] // R
    ids = jnp.arange(N_EXPERTS, dtype=jnp.int32)
    present = padded > 0
    later = jnp.where(jnp.logical_and(present[None, :], ids[None, :] > ids[:, None]), ids[None, :], N_EXPERTS)
    next_present = jnp.min(later, axis=1)
    next_present = jnp.where(next_present == N_EXPERTS, -1, next_present)
    order = jnp.cumsum(present.astype(jnp.int32)) - 1
    onehot = (block_eid[:, None] == ids[None, :]).astype(jnp.int32)
    next_eid = jnp.sum(onehot * next_present[None, :], axis=1)
    w_slot = jnp.sum(onehot * order[None, :], axis=1) % 2

    dest = _slots(pstart, eid, rank)
    xs = _dispatch(pstart, pend, counts, _token_blocks(dest, min(DISPATCH_TILE, N)), hn, n_blocks * R)
    ys = _experts(block_eid, n_used, next_eid, w_slot, xs, w_expert_gate, w_expert_up, w_expert_down)
    out = _combine(_token_blocks(dest, min(COMBINE_TILE, N)), x1, wts, ys)
    return out.reshape(B, S, D)


def kernel(x, norm1_g, w_in, pool_w, pool_scale, w_pool_up, q_norm_g, k_norm_g, lambda_q1, lambda_k1, lambda_q2,
           lambda_k2, subln_g, w_attn_up, w_out, norm2_g, w_router_group, b_router_group, w_router_expert,
           b_router_expert, w_expert_gate, w_expert_up, w_expert_down):
    params = (norm1_g, w_in, pool_w, pool_scale, w_pool_up, q_norm_g, k_norm_g, lambda_q1, lambda_k1, lambda_q2,
              lambda_k2, subln_g, w_attn_up, w_out, norm2_g, w_router_group, b_router_group, w_router_expert,
              b_router_expert, w_expert_gate, w_expert_up, w_expert_down)
    for l in range(norm1_g.shape[0]):
        x = _layer(x, l, *(p[l] for p in params))
    return x
```
